```python
import jax
import jax.numpy as jnp
from jax import lax
import numpy as np

D_MODEL = 1024
BATCH = 2
SEQ = 16384
DEPTH = 2
DEC_BATCH = 8
DEC_SEQ = 64
PAST_LEN = 1024

CHUNK = 64
ROPE_THETA = 10000.0
RMS_EPS = 1e-6
SB_HEADS = 8
SB_HEAD_DIM = 64
SB_BLOCK = 128
RET_HEADS = 4
RET_QK_DIM = 128
RET_V_DIM = 256
SWA_HEADS = 16
SWA_KV_HEADS = 4
SWA_HEAD_DIM = 64
WINDOW = 128
D_FF = ((8 * D_MODEL // 3 + 127) // 128) * 128

SB_W = SB_HEADS * SB_HEAD_DIM
RET_QK_W = RET_HEADS * RET_QK_DIM
RET_V_W = RET_HEADS * RET_V_DIM
EVEN_IN = 3 * SB_W + 2 * RET_QK_W + 2 * RET_V_W
EVEN_OUT = SB_W + RET_V_W
SWA_Q_W = SWA_HEADS * SWA_HEAD_DIM
SWA_KV_W = SWA_KV_HEADS * SWA_HEAD_DIM
ODD_IN = SWA_Q_W + 2 * SWA_KV_W

kernel_name = 'hybrid_streaming_encoder_step'


def rms_norm(x, g):
    xf = x.astype(jnp.float32)
    y = xf * lax.rsqrt(jnp.mean(xf * xf, axis=-1, keepdims=True) + RMS_EPS)
    return (y * g.astype(jnp.float32)).astype(x.dtype)


def rope(x, pos):
    half = x.shape[-1] // 2
    inv = jnp.power(ROPE_THETA, -jnp.arange(half, dtype=jnp.float32) / half)
    ang = pos.astype(jnp.float32)[:, None] * inv[None, :]
    cos = jnp.cos(ang)[:, None, :]
    sin = jnp.sin(ang)[:, None, :]
    xf = x.astype(jnp.float32)
    x1, x2 = xf[..., :half], xf[..., half:]
    return jnp.concatenate([x1 * cos - x2 * sin, x1 * sin + x2 * cos], axis=-1).astype(x.dtype)


def swiglu(h, wg, wu, wd):
    return (jax.nn.silu(h @ wg) * (h @ wu)) @ wd


def _sb_block(q, pos_q, k, v, pos_k):
    z = jnp.einsum('bqhd,bkhd->bhqk', q.astype(jnp.float32), k.astype(jnp.float32)) * (SB_HEAD_DIM ** -0.5)
    mask = pos_k[None, :] < pos_q[:, None]
    log_keep = jnp.where(mask, jax.nn.log_sigmoid(-z), 0.0)
    tail = lax.cumsum(log_keep, axis=3, reverse=True)
    w = jnp.exp(jnp.where(mask, z + tail, -jnp.inf))
    return jnp.einsum('bhqk,bkhd->bqhd', w, v.astype(jnp.float32)).astype(v.dtype)


def stick_breaking(q, k, v, pos_q, pos_k):
    b, nq, h, d = q.shape
    blk = min(SB_BLOCK, nq)
    nb = nq // blk
    qs = q.reshape(b, nb, blk, h, d).swapaxes(0, 1)
    ps = pos_q.reshape(nb, blk)
    out = lax.map(lambda a: _sb_block(a[0], a[1], k, v, pos_k), (qs, ps))
    return out.swapaxes(0, 1).reshape(b, nq, h, v.shape[-1])


def retention_chunk(state, q, k, v):
    n_len = q.shape[2]
    log_gamma = jnp.log1p(-jnp.exp2(-5.0 - jnp.arange(RET_HEADS, dtype=jnp.float32)))
    n = jnp.arange(n_len, dtype=jnp.float32)
    diff = n[:, None] - n[None, :]
    decay = jnp.where(diff >= 0, jnp.exp(log_gamma[:, None, None] * jnp.maximum(diff, 0.0)), 0.0)
    inner = jnp.einsum('bhqk,bhkv->bhqv', jnp.einsum('bhqd,bhkd->bhqk', q, k) * decay, v)
    cross = jnp.einsum('bhqd,bhdv->bhqv', q, state) * jnp.exp(log_gamma[:, None] * (n + 1.0))[..., None]
    k_w = k * jnp.exp(log_gamma[:, None] * (n_len - 1.0 - n))[..., None]
    new_state = jnp.exp(log_gamma * n_len)[:, None, None] * state + jnp.einsum('bhkd,bhkv->bhdv', k_w, v)
    return new_state, inner + cross


def retention_prompt(q, k, v):
    b, s = q.shape[0], q.shape[1]
    nc = s // CHUNK

    def to_chunks(t):
        return t.reshape(b, nc, CHUNK, RET_HEADS, t.shape[-1]).transpose(1, 0, 3, 2, 4)

    state0 = jnp.zeros((b, RET_HEADS, RET_QK_DIM, RET_V_DIM), jnp.float32)
    final, out = lax.scan(lambda st, c: retention_chunk(st, c[0], c[1], c[2]), state0,
                          (to_chunks(q), to_chunks(k), to_chunks(v)))
    out = out.transpose(1, 0, 3, 2, 4).reshape(b, s, RET_HEADS, RET_V_DIM)
    return out, final


def even_mixer(h, pos, past_pos, w_in, w_out, k_cache, v_cache, ret_state):
    b, n, _ = h.shape
    proj = h @ w_in
    cuts = [SB_W, 2 * SB_W, 3 * SB_W, 3 * SB_W + RET_QK_W, 3 * SB_W + 2 * RET_QK_W,
            3 * SB_W + 2 * RET_QK_W + RET_V_W]
    qa, ka, va, qr, kr, vr, gate = jnp.split(proj, cuts, axis=-1)
    qa = qa.reshape(b, n, SB_HEADS, SB_HEAD_DIM)
    ka = ka.reshape(b, n, SB_HEADS, SB_HEAD_DIM)
    va = va.reshape(b, n, SB_HEADS, SB_HEAD_DIM)
    if k_cache is None:
        k_all, v_all, pos_k = ka, va, pos
    else:
        k_all = jnp.concatenate([k_cache.astype(ka.dtype), ka], axis=1)
        v_all = jnp.concatenate([v_cache.astype(va.dtype), va], axis=1)
        pos_k = jnp.concatenate([past_pos, pos])
    o_sb = stick_breaking(qa, k_all, v_all, pos, pos_k).reshape(b, n, SB_W)

    qr = rope(qr.reshape(b, n, RET_HEADS, RET_QK_DIM), pos).astype(jnp.float32)
    kr = rope(kr.reshape(b, n, RET_HEADS, RET_QK_DIM), pos).astype(jnp.float32) * (RET_QK_DIM ** -0.5)
    vr = vr.reshape(b, n, RET_HEADS, RET_V_DIM).astype(jnp.float32)
    if ret_state is None:
        o_r, st = retention_prompt(qr, kr, vr)
    else:
        st, o_r = retention_chunk(ret_state.astype(jnp.float32), qr.transpose(0, 2, 1, 3),
                                  kr.transpose(0, 2, 1, 3), vr.transpose(0, 2, 1, 3))
        o_r = o_r.transpose(0, 2, 1, 3)
    o_r = o_r * lax.rsqrt(jnp.mean(o_r * o_r, axis=-1, keepdims=True) + RMS_EPS)
    o_r = o_r.reshape(b, n, RET_V_W).astype(h.dtype) * jax.nn.silu(gate)
    out = jnp.concatenate([o_sb.astype(h.dtype), o_r], axis=-1) @ w_out
    return out, ka, va, st


def sink_attention(q, k, v, valid, sinks):
    s = jnp.einsum('bnqhgd,bnkhd->bnhgqk', q.astype(jnp.float32), k.astype(jnp.float32)) * (SWA_HEAD_DIM ** -0.5)
    s = jnp.where(valid[None, :, None, None, None, :], s, -jnp.inf)
    sink = sinks.astype(jnp.float32).reshape(SWA_KV_HEADS, SWA_HEADS // SWA_KV_HEADS)[None, None, :, :, None, None]
    m = jnp.maximum(jnp.max(s, axis=-1, keepdims=True), sink)
    p = jnp.exp(s - m)
    denom = jnp.sum(p, axis=-1, keepdims=True) + jnp.exp(sink - m)
    o = jnp.einsum('bnhgqk,bnkhd->bnqhgd', p / denom, v.astype(jnp.float32))
    return o.astype(v.dtype)


def odd_mixer(h, pos, w_in, w_out, q_g, k_g, sinks, k_cache, v_cache):
    b, n, _ = h.shape
    grp = SWA_HEADS // SWA_KV_HEADS
    q, k, v = jnp.split(h @ w_in, [SWA_Q_W, SWA_Q_W + SWA_KV_W], axis=-1)
    q = rope(rms_norm(q.reshape(b, n, SWA_HEADS, SWA_HEAD_DIM), q_g), pos)
    k = rope(rms_norm(k.reshape(b, n, SWA_KV_HEADS, SWA_HEAD_DIM), k_g), pos)
    v = v.reshape(b, n, SWA_KV_HEADS, SWA_HEAD_DIM)
    if k_cache is None:
        nc = n // CHUNK
        nb = WINDOW // CHUNK
        pad = ((0, 0), (nb * CHUNK, 0), (0, 0), (0, 0))
        kp = jnp.pad(k, pad).reshape(b, nc + nb, CHUNK, SWA_KV_HEADS, SWA_HEAD_DIM)
        vp = jnp.pad(v, pad).reshape(b, nc + nb, CHUNK, SWA_KV_HEADS, SWA_HEAD_DIM)
        kb = jnp.concatenate([kp[:, j:j + nc] for j in range(nb + 1)], axis=2)
        vb = jnp.concatenate([vp[:, j:j + nc] for j in range(nb + 1)], axis=2)
        key_pos = (jnp.arange(nc)[:, None] - nb) * CHUNK + jnp.arange((nb + 1) * CHUNK)[None, :]
        valid = key_pos >= 0
        qb = q.reshape(b, nc, CHUNK, SWA_KV_HEADS, grp, SWA_HEAD_DIM)
        keep = min(WINDOW, n)
        k_rows, v_rows = k[:, n - keep:], v[:, n - keep:]
    else:
        kb = jnp.concatenate([k_cache.astype(k.dtype), k], axis=1)[:, None]
        vb = jnp.concatenate([v_cache.astype(v.dtype), v], axis=1)[:, None]
        valid = jnp.ones((1, kb.shape[2]), dtype=bool)
        qb = q.reshape(b, 1, n, SWA_KV_HEADS, grp, SWA_HEAD_DIM)
        k_rows, v_rows = k, v
    o = sink_attention(qb, kb, vb, valid, sinks).reshape(b, n, SWA_Q_W)
    return o @ w_out, k_rows, v_rows


def _pick(c, i):
    return None if c is None else c[i]


def trunk(x, pos, past_pos, sb_k_c, sb_v_c, ret_c, swa_k_c, swa_v_c, norm_g, ffn_w_gate, ffn_w_up,
          ffn_w_down, even_w_in, even_w_out, odd_w_in, odd_w_out, odd_q_norm, odd_k_norm, odd_sinks):
    sb_k, sb_v, ret, swa_k, swa_v = [], [], [], [], []
    for layer in range(DEPTH):
        i = layer // 2
        x = x + 0.5 * swiglu(rms_norm(x, norm_g[layer, 0]), ffn_w_gate[layer, 0], ffn_w_up[layer, 0], ffn_w_down[layer, 0])
        h = rms_norm(x, norm_g[layer, 1])
        if layer % 2 == 0:
            mix, k_rows, v_rows, st = even_mixer(h, pos, past_pos, even_w_in[i], even_w_out[i],
                                                 _pick(sb_k_c, i), _pick(sb_v_c, i), _pick(ret_c, i))
            sb_k.append(k_rows)
            sb_v.append(v_rows)
            ret.append(st)
        else:
            mix, k_rows, v_rows = odd_mixer(h, pos, odd_w_in[i], odd_w_out[i], odd_q_norm[i], odd_k_norm[i],
                                            odd_sinks[i], _pick(swa_k_c, i), _pick(swa_v_c, i))
            swa_k.append(k_rows)
            swa_v.append(v_rows)
        x = x + mix
        x = x + 0.5 * swiglu(rms_norm(x, norm_g[layer, 2]), ffn_w_gate[layer, 1], ffn_w_up[layer, 1], ffn_w_down[layer, 1])
    return x, jnp.stack(sb_k), jnp.stack(sb_v), jnp.stack(ret), jnp.stack(swa_k), jnp.stack(swa_v)


def setup_inputs(seed: int = 0) -> dict:
    key = jax.random.key(seed)
    ks = jax.random.split(key, 20)
    n_even = (DEPTH + 1) // 2
    n_odd = DEPTH // 2
    swa_keep = min(WINDOW, PAST_LEN)

    def nrm(k, shape, scale=1.0):
        return jax.random.normal(k, shape, jnp.float32) * scale

    return {
        'x_prompt': nrm(ks[0], (BATCH, SEQ, D_MODEL)),
        'x_sample': nrm(ks[1], (DEC_BATCH, DEC_SEQ, D_MODEL)),
        'cache_sb_k': nrm(ks[2], (n_even, DEC_BATCH, PAST_LEN, SB_HEADS, SB_HEAD_DIM)),
        'cache_sb_v': nrm(ks[3], (n_even, DEC_BATCH, PAST_LEN, SB_HEADS, SB_HEAD_DIM)),
        'state_ret': nrm(ks[4], (n_even, DEC_BATCH, RET_HEADS, RET_QK_DIM, RET_V_DIM)),
        'cache_swa_k': nrm(ks[5], (n_odd, DEC_BATCH, swa_keep, SWA_KV_HEADS, SWA_HEAD_DIM)),
        'cache_swa_v': nrm(ks[6], (n_odd, DEC_BATCH, swa_keep, SWA_KV_HEADS, SWA_HEAD_DIM)),
        'norm_g': 1.0 + nrm(ks[7], (DEPTH, 3, D_MODEL), 0.05),
        'ffn_w_gate': nrm(ks[8], (DEPTH, 2, D_MODEL, D_FF), D_MODEL ** -0.5),
        'ffn_w_up': nrm(ks[9], (DEPTH, 2, D_MODEL, D_FF), D_MODEL ** -0.5),
        'ffn_w_down': nrm(ks[10], (DEPTH, 2, D_FF, D_MODEL), D_FF ** -0.5),
        'even_w_in': nrm(ks[11], (n_even, D_MODEL, EVEN_IN), D_MODEL ** -0.5),
        'even_w_out': nrm(ks[12], (n_even, EVEN_OUT, D_MODEL), EVEN_OUT ** -0.5),
        'odd_w_in': nrm(ks[13], (n_odd, D_MODEL, ODD_IN), D_MODEL ** -0.5),
        'odd_w_out': nrm(ks[14], (n_odd, SWA_Q_W, D_MODEL), SWA_Q_W ** -0.5),
        'odd_q_norm': 1.0 + nrm(ks[15], (n_odd, SWA_HEAD_DIM), 0.05),
        'odd_k_norm': 1.0 + nrm(ks[16], (n_odd, SWA_HEAD_DIM), 0.05),
        'odd_sinks': nrm(ks[17], (n_odd, SWA_HEADS), 0.5),
    }


def reference(x_prompt, x_sample, cache_sb_k, cache_sb_v, state_ret, cache_swa_k, cache_swa_v, norm_g,
              ffn_w_gate, ffn_w_up, ffn_w_down, even_w_in, even_w_out, odd_w_in, odd_w_out, odd_q_norm,
              odd_k_norm, odd_sinks):
    past = cache_sb_k.shape[2]
    pos_prompt = jnp.arange(x_prompt.shape[1], dtype=jnp.int32)
    pos_sample = past + jnp.arange(x_sample.shape[1], dtype=jnp.int32)
    past_pos = jnp.arange(past, dtype=jnp.int32)
    y_prompt, sb_k_p, sb_v_p, ret_p, swa_k_p, swa_v_p = trunk(
        x_prompt, pos_prompt, None, None, None, None, None, None, norm_g, ffn_w_gate, ffn_w_up, ffn_w_down,
        even_w_in, even_w_out, odd_w_in, odd_w_out, odd_q_norm, odd_k_norm, odd_sinks)
    y_sample, sb_k_s, sb_v_s, ret_s, swa_k_s, swa_v_s = trunk(
        x_sample, pos_sample, past_pos, cache_sb_k, cache_sb_v, state_ret, cache_swa_k, cache_swa_v, norm_g,
        ffn_w_gate, ffn_w_up, ffn_w_down, even_w_in, even_w_out, odd_w_in, odd_w_out, odd_q_norm, odd_k_norm,
        odd_sinks)
    return (y_prompt, y_sample, sb_k_p, sb_v_p, ret_p, swa_k_p, swa_v_p, sb_k_s, sb_v_s, ret_s, swa_k_s, swa_v_s)
```

```python
import functools
import math

import jax
import jax.numpy as jnp
from jax import lax
from jax.experimental import pallas as pl
from jax.experimental.pallas import tpu as pltpu

F32 = jnp.float32
BF16 = jnp.bfloat16

RMS_EPS = 1e-6
ROPE_THETA = 10000.0
CHUNK = 64
SB_HEADS = 8
SB_HEAD_DIM = 64
RET_HEADS = 4
RET_QK_DIM = 128
RET_V_DIM = 256
SWA_HEADS = 16
SWA_KV_HEADS = 4
SWA_HEAD_DIM = 64
WINDOW = 128

SB_W = SB_HEADS * SB_HEAD_DIM
RET_QK_W = RET_HEADS * RET_QK_DIM
RET_V_W = RET_HEADS * RET_V_DIM
SWA_Q_W = SWA_HEADS * SWA_HEAD_DIM
SWA_KV_W = SWA_KV_HEADS * SWA_HEAD_DIM

LANES = 128
FF_CHUNK = 256
VMEM_LIMIT = 56 * 1024 * 1024
LOG2E = math.log2(math.e)
SOFTPLUS2_LINEAR = 40.0


def _dot(a, b):
    return jnp.dot(a, b, preferred_element_type=F32)


def _dot_nt(a, b):
    return lax.dot_general(a, b, (((1,), (1,)), ((), ())), preferred_element_type=F32)


def _dot_tn(a, b):
    return lax.dot_general(a, b, (((0,), (0,)), ((), ())), preferred_element_type=F32)


def _rms(x, g):
    return x * lax.rsqrt(jnp.mean(x * x, axis=-1, keepdims=True) + RMS_EPS) * g


def _full_spec(shape):
    nd = len(shape)
    return pl.BlockSpec(shape, lambda *_: (0,) * nd, pipeline_mode=pl.Buffered(1))


def _params(sem):
    return pltpu.CompilerParams(dimension_semantics=sem, vmem_limit_bytes=VMEM_LIMIT)


def _ffn_kernel(*refs, n_mix, n_chunks):
    x_ref = refs[0]
    mix = refs[1:1 + 2 * n_mix]
    g_ref, wg_ref, wu_ref, wd_ref, o_ref, h_ref, acc_ref = refs[1 + 2 * n_mix:]
    x = x_ref[...]
    for m in range(n_mix):
        x = x + _dot(mix[2 * m][...], mix[2 * m + 1][...])
    h_ref[...] = _rms(x, g_ref[...]).astype(BF16)
    acc_ref[...] = jnp.zeros_like(acc_ref)

    def body(c, carry):
        h = h_ref[...]
        gate = _dot(h, wg_ref[c])
        up = _dot(h, wu_ref[c])
        act = (gate * jax.nn.sigmoid(gate) * up).astype(BF16)
        acc_ref[...] += _dot(act, wd_ref[c])
        return carry

    lax.fori_loop(0, n_chunks, body, 0)
    o_ref[...] = x + 0.5 * acc_ref[...]


def _ffn_call(x, mix, g, wg, wu, wd, tm):
    t, d = x.shape
    n_chunks = wg.shape[0]
    row = lambda w: pl.BlockSpec((tm, w), lambda i: (i, 0))
    in_specs = [row(d)]
    args = [x]
    for a, w in mix:
        in_specs += [row(a.shape[1]), _full_spec(w.shape)]
        args += [a, w]
    in_specs += [_full_spec(g.shape), _full_spec(wg.shape), _full_spec(wu.shape), _full_spec(wd.shape)]
    args += [g, wg, wu, wd]
    return pl.pallas_call(
        functools.partial(_ffn_kernel, n_mix=len(mix), n_chunks=n_chunks),
        grid=(t // tm,),
        in_specs=in_specs,
        out_specs=row(d),
        out_shape=jax.ShapeDtypeStruct((t, d), F32),
        scratch_shapes=[pltpu.VMEM((tm, d), BF16), pltpu.VMEM((tm, d), F32)],
        compiler_params=_params(("parallel",)),
        name="ffn_mix%d" % len(mix),
    )(*args)


def _rope128(x, cos2, sin2):
    parts = []
    for hd in range(x.shape[1] // LANES):
        sl = x[:, hd * LANES:(hd + 1) * LANES]
        parts.append(sl * cos2 + pltpu.roll(sl, LANES // 2, axis=1) * sin2)
    return jnp.concatenate(parts, axis=1)


def _even_proj_kernel(x_ref, g_ref, wq_ref, wk_ref, wv_ref, wqr_ref, wkr_ref, wvr_ref, wgt_ref, cos_ref,
                      sin_ref, qa_ref, ka_ref, kab_ref, va_ref, vab_ref, qr_ref, kr_ref, vr_ref, gt_ref):
    h = _rms(x_ref[...], g_ref[...]).astype(BF16)
    qa_ref[...] = (_dot(h, wq_ref[...]) * (LOG2E * SB_HEAD_DIM ** -0.5)).astype(BF16)
    ka = _dot(h, wk_ref[...])
    ka_ref[...] = ka
    kab_ref[...] = ka.astype(BF16)
    va = _dot(h, wv_ref[...])
    va_ref[...] = va
    vab_ref[...] = va.astype(BF16)
    cos2 = cos_ref[...]
    sin2 = sin_ref[...]
    qr_ref[...] = _rope128(_dot(h, wqr_ref[...]), cos2, sin2).astype(BF16)
    kr_ref[...] = (_rope128(_dot(h, wkr_ref[...]), cos2, sin2) * (RET_QK_DIM ** -0.5)).astype(BF16)
    vr_ref[...] = _dot(h, wvr_ref[...]).astype(BF16)
    gt_ref[...] = _dot(h, wgt_ref[...])


def _even_proj_call(x, g, ws, cos2, sin2, tm):
    t, d = x.shape
    n_tab = cos2.shape[0] // tm
    row = lambda w: pl.BlockSpec((tm, w), lambda i: (i, 0))
    tab = pl.BlockSpec((tm, LANES), lambda i: (i % n_tab, 0))
    outs = [(SB_W, BF16), (SB_W, F32), (SB_W, BF16), (SB_W, F32), (SB_W, BF16),
            (RET_QK_W, BF16), (RET_QK_W, BF16), (RET_V_W, BF16), (RET_V_W, F32)]
    return pl.pallas_call(
        _even_proj_kernel,
        grid=(t // tm,),
        in_specs=[row(d), _full_spec(g.shape)] + [_full_spec(w.shape) for w in ws] + [tab, tab],
        out_specs=[row(w) for w, _ in outs],
        out_shape=[jax.ShapeDtypeStruct((t, w), dt) for w, dt in outs],
        compiler_params=_params(("parallel",)),
        name="even_proj",
    )(x, g, *ws, cos2, sin2)


def _sb_kernel(q_ref, k_ref, v_ref, negu_ref, o_ref, *, qb, kb, q_pos0):
    i = pl.program_id(2)
    qstart = q_pos0 + i * qb
    nkb = (qstart + qb - 2 + kb) // kb
    q = q_ref[...]
    lane = lax.broadcasted_iota(jnp.int32, (qb, LANES), 1)
    first = lane < SB_HEAD_DIM
    qpos = qstart + lax.broadcasted_iota(jnp.int32, (qb, kb), 0)
    kofs = lax.broadcasted_iota(jnp.int32, (qb, kb), 1)
    negu = negu_ref[...]
    zero = jnp.zeros_like(q)
    accs = []
    for qh in (jnp.where(first, q, zero), jnp.where(first, zero, q)):
        def body(t, carry, qh=qh):
            c, acc = carry
            j = nkb - 1 - t
            start = pl.multiple_of(j * kb, kb)
            ks = k_ref[pl.ds(start, kb), :]
            vs = v_ref[pl.ds(start, kb), :]
            z = _dot_nt(qh, ks)
            mask = (kofs + j * kb) < qpos
            sp = jnp.log2(1.0 + jnp.exp2(z))
            sp = jnp.where(z > SOFTPLUS2_LINEAR, z, sp)
            sp = jnp.where(mask, sp, 0.0)
            tail = _dot(sp.astype(BF16), negu)
            w = jnp.where(mask, jnp.exp2(z + tail + c), 0.0)
            acc = acc + _dot(w.astype(BF16), vs)
            return c + tail[:, :1], acc

        _, acc = lax.fori_loop(0, nkb, body, (jnp.zeros((qb, 1), F32), jnp.zeros((qb, LANES), F32)))
        accs.append(acc)
    o_ref[...] = jnp.where(first, accs[0], accs[1]).astype(o_ref.dtype)


def _sb_call(q, k, v, negu, qb, q_pos0):
    b, nq, w = q.shape
    nk = k.shape[1]
    kb = negu.shape[0]
    return pl.pallas_call(
        functools.partial(_sb_kernel, qb=qb, kb=kb, q_pos0=q_pos0),
        grid=(b, w // LANES, nq // qb),
        in_specs=[pl.BlockSpec((None, qb, LANES), lambda bi, p, i: (bi, i, p)),
                  pl.BlockSpec((None, nk, LANES), lambda bi, p, i: (bi, 0, p)),
                  pl.BlockSpec((None, nk, LANES), lambda bi, p, i: (bi, 0, p)),
                  _full_spec(negu.shape)],
        out_specs=pl.BlockSpec((None, qb, LANES), lambda bi, p, i: (bi, i, p)),
        out_shape=jax.ShapeDtypeStruct((b, nq, w), BF16),
        compiler_params=_params(("parallel", "parallel", "parallel")),
        name="stick_breaking",
    )(q, k, v, negu)


def _ret_kernel(sdec_ref, q_ref, k_ref, v_ref, gt_ref, s0_ref, dec_ref, qdec_ref, kdec_ref,
                o_ref, sfin_ref, st_ref):
    c = pl.program_id(1)

    @pl.when(c == 0)
    def _():
        st_ref[...] = s0_ref[...]

    for h in range(RET_HEADS):
        q = q_ref[:, h * RET_QK_DIM:(h + 1) * RET_QK_DIM]
        k = k_ref[:, h * RET_QK_DIM:(h + 1) * RET_QK_DIM]
        v = v_ref[:, h * RET_V_DIM:(h + 1) * RET_V_DIM]
        state = st_ref[h]
        inner = _dot((_dot_nt(q, k) * dec_ref[h]).astype(BF16), v)
        cross = _dot(q, state.astype(BF16)) * qdec_ref[h]
        kw = (k.astype(F32) * kdec_ref[h]).astype(BF16)
        st_ref[h] = sdec_ref[h] * state + _dot_tn(kw, v)
        o = inner + cross
        o = o * lax.rsqrt(jnp.mean(o * o, axis=-1, keepdims=True) + RMS_EPS)
        gate = gt_ref[:, h * RET_V_DIM:(h + 1) * RET_V_DIM]
        o_ref[:, h * RET_V_DIM:(h + 1) * RET_V_DIM] = (o * (gate * jax.nn.sigmoid(gate))).astype(o_ref.dtype)

    @pl.when(c == pl.num_programs(1) - 1)
    def _():
        sfin_ref[...] = st_ref[...]


def _ret_call(q, k, v, gate, state0, cs):
    b, n, _ = q.shape
    log_gamma = jnp.log1p(-jnp.exp2(-5.0 - jnp.arange(RET_HEADS, dtype=F32)))
    pos = jnp.arange(cs, dtype=F32)
    diff = pos[:, None] - pos[None, :]
    dec = jnp.where(diff >= 0, jnp.exp(log_gamma[:, None, None] * jnp.maximum(diff, 0.0)), 0.0)
    qdec = jnp.exp(log_gamma[:, None] * (pos + 1.0))[..., None]
    kdec = jnp.exp(log_gamma[:, None] * (cs - 1.0 - pos))[..., None]
    sdec = jnp.exp(log_gamma * cs)
    seq = lambda w: pl.BlockSpec((None, cs, w), lambda bi, c: (bi, c, 0))
    st = pl.BlockSpec((None, RET_HEADS, RET_QK_DIM, RET_V_DIM), lambda bi, c: (bi, 0, 0, 0))
    return pl.pallas_call(
        _ret_kernel,
        grid=(b, n // cs),
        in_specs=[pl.BlockSpec(memory_space=pltpu.SMEM), seq(RET_QK_W), seq(RET_QK_W), seq(RET_V_W),
                  seq(RET_V_W), st, _full_spec(dec.shape), _full_spec(qdec.shape), _full_spec(kdec.shape)],
        out_specs=[seq(RET_V_W), st],
        out_shape=[jax.ShapeDtypeStruct((b, n, RET_V_W), BF16),
                   jax.ShapeDtypeStruct((b, RET_HEADS, RET_QK_DIM, RET_V_DIM), F32)],
        scratch_shapes=[pltpu.VMEM((RET_HEADS, RET_QK_DIM, RET_V_DIM), F32)],
        compiler_params=_params(("parallel", "arbitrary")),
        name="retention",
    )(sdec, q, k, v, gate, state0, dec, qdec, kdec)


def _head_norm_rope(t, bd, gain, cos4, sin4, first_half):
    parts = []
    for c in range(t.shape[1] // LANES):
        sq = t[:, c * LANES:(c + 1) * LANES]
        sq = sq * sq
        hi = sq.astype(BF16)
        lo = (sq - hi.astype(F32)).astype(BF16)
        parts.append(_dot(hi, bd) + _dot(lo, bd))
    ms = jnp.concatenate(parts, axis=1) * (1.0 / SWA_HEAD_DIM)
    y = t * lax.rsqrt(ms + RMS_EPS) * gain
    w = t.shape[1]
    quarter = SWA_HEAD_DIM // 2
    partner = jnp.where(first_half, pltpu.roll(y, w - quarter, axis=1), pltpu.roll(y, quarter, axis=1))
    return y * cos4 + partner * sin4


def _odd_proj_kernel(x_ref, g_ref, wq_ref, wk_ref, wv_ref, bd_ref, qg_ref, kg_ref, cos_ref, sin_ref,
                     q_ref, k_ref, v_ref):
    h = _rms(x_ref[...], g_ref[...]).astype(BF16)
    bd = bd_ref[...]
    cos2 = cos_ref[...]
    sin2 = sin_ref[...]
    tm = cos2.shape[0]
    nq = SWA_Q_W // LANES
    nk = SWA_KV_W // LANES
    lane_q = lax.broadcasted_iota(jnp.int32, (tm, SWA_Q_W), 1)
    lane_k = lax.broadcasted_iota(jnp.int32, (tm, SWA_KV_W), 1)
    half = SWA_HEAD_DIM // 2
    q = _head_norm_rope(_dot(h, wq_ref[...]), bd, qg_ref[...], jnp.concatenate([cos2] * nq, axis=1),
                        jnp.concatenate([sin2] * nq, axis=1), (lane_q % SWA_HEAD_DIM) < half)
    q_ref[...] = (q * (SWA_HEAD_DIM ** -0.5)).astype(BF16)
    k_ref[...] = _head_norm_rope(_dot(h, wk_ref[...]), bd, kg_ref[...], jnp.concatenate([cos2] * nk, axis=1),
                                 jnp.concatenate([sin2] * nk, axis=1), (lane_k % SWA_HEAD_DIM) < half)
    v_ref[...] = _dot(h, wv_ref[...])


def _odd_proj_call(x, g, wq, wk, wv, qg, kg, cos2, sin2, tm):
    t, d = x.shape
    n_tab = cos2.shape[0] // tm
    head = jnp.arange(LANES) // SWA_HEAD_DIM
    bd = (head[:, None] == head[None, :]).astype(BF16)
    row = lambda w: pl.BlockSpec((tm, w), lambda i: (i, 0))
    tab = pl.BlockSpec((tm, LANES), lambda i: (i % n_tab, 0))
    ins = [g, wq, wk, wv, bd, qg, kg]
    return pl.pallas_call(
        _odd_proj_kernel,
        grid=(t // tm,),
        in_specs=[row(d)] + [_full_spec(a.shape) for a in ins] + [tab, tab],
        out_specs=[row(SWA_Q_W), row(SWA_KV_W), row(SWA_KV_W)],
        out_shape=[jax.ShapeDtypeStruct((t, SWA_Q_W), BF16), jax.ShapeDtypeStruct((t, SWA_KV_W), F32),
                   jax.ShapeDtypeStruct((t, SWA_KV_W), F32)],
        compiler_params=_params(("parallel",)),
        name="odd_proj",
    )(x, *ins, cos2, sin2)


def _swa_kernel(sink_ref, q_ref, kp_ref, kc_ref, vp_ref, vc_ref, o_ref, *, rows, first_prev_valid):
    i = pl.program_id(1)
    nkeys = WINDOW + rows
    q = q_ref[...]
    k = jnp.concatenate([kp_ref[...], kc_ref[...]], axis=0).astype(BF16)
    v = jnp.concatenate([vp_ref[...], vc_ref[...]], axis=0).astype(BF16)
    qc = lax.broadcasted_iota(jnp.int32, (rows, nkeys), 0) // CHUNK
    kc = lax.broadcasted_iota(jnp.int32, (rows, nkeys), 1) // CHUNK
    nb = WINDOW // CHUNK
    valid = (kc >= qc) & (kc <= qc + nb)
    if not first_prev_valid:
        valid = valid & ((kc >= nb) | (i > 0))
    grp = SWA_HEADS // SWA_KV_HEADS
    outs = []
    for j in range(SWA_KV_HEADS):
        kj = k[:, j * SWA_HEAD_DIM:(j + 1) * SWA_HEAD_DIM]
        vj = v[:, j * SWA_HEAD_DIM:(j + 1) * SWA_HEAD_DIM]
        for gq in range(grp):
            hd = j * grp + gq
            s = _dot_nt(q[:, hd * SWA_HEAD_DIM:(hd + 1) * SWA_HEAD_DIM], kj)
            s = jnp.where(valid, s, -jnp.inf)
            sink = sink_ref[hd]
            m = jnp.maximum(jnp.max(s, axis=-1, keepdims=True), sink)
            p = jnp.exp(s - m)
            denom = jnp.sum(p, axis=-1, keepdims=True) + jnp.exp(sink - m)
            outs.append(_dot(p.astype(BF16), vj) / denom)
    o_ref[...] = jnp.concatenate(outs, axis=1).astype(o_ref.dtype)


def _swa_call(q, k_prev, k_cur, v_prev, v_cur, sinks, rows, same_array):
    b, n, _ = q.shape
    per = rows // WINDOW if same_array else 0
    cur = lambda w: pl.BlockSpec((None, rows, w), lambda bi, i: (bi, i, 0))
    prev = pl.BlockSpec((None, WINDOW, SWA_KV_W), lambda bi, i: (bi, jnp.maximum(i * per - 1, 0), 0))
    return pl.pallas_call(
        functools.partial(_swa_kernel, rows=rows, first_prev_valid=not same_array),
        grid=(b, n // rows),
        in_specs=[pl.BlockSpec(memory_space=pltpu.SMEM), cur(SWA_Q_W), prev, cur(SWA_KV_W), prev, cur(SWA_KV_W)],
        out_specs=cur(SWA_Q_W),
        out_shape=jax.ShapeDtypeStruct((b, n, SWA_Q_W), BF16),
        compiler_params=_params(("parallel", "parallel")),
        name="swa",
    )(sinks, q, k_prev, k_cur, v_prev, v_cur)


def _rope_tables(pos, head_dim):
    half = head_dim // 2
    inv = jnp.power(ROPE_THETA, -jnp.arange(half, dtype=F32) / half)
    ang = pos.astype(F32)[:, None] * inv[None, :]
    cos, sin = jnp.cos(ang), jnp.sin(ang)
    reps = LANES // head_dim
    return (jnp.tile(jnp.concatenate([cos, cos], axis=1), (1, reps)),
            jnp.tile(jnp.concatenate([-sin, sin], axis=1), (1, reps)))


def _chunk_cols(w):
    d, f = w.shape
    return w.reshape(d, f // FF_CHUNK, FF_CHUNK).transpose(1, 0, 2).astype(BF16)


def _prep_weights(norm_g, ffn_w_gate, ffn_w_up, ffn_w_down, even_w_in, even_w_out, odd_w_in, odd_w_out,
                  odd_q_norm, odd_k_norm):
    depth = norm_g.shape[0]
    prm = {"g": norm_g[:, :, None, :], "ffn": [], "even": [], "odd": []}
    for layer in range(depth):
        prm["ffn"].append([(_chunk_cols(ffn_w_gate[layer, s]), _chunk_cols(ffn_w_up[layer, s]),
                            ffn_w_down[layer, s].reshape(-1, FF_CHUNK, ffn_w_down.shape[-1]).astype(BF16))
                           for s in range(2)])
    cuts = [0, SB_W, 2 * SB_W, 3 * SB_W, 3 * SB_W + RET_QK_W, 3 * SB_W + 2 * RET_QK_W,
            3 * SB_W + 2 * RET_QK_W + RET_V_W, 3 * SB_W + 2 * RET_QK_W + 2 * RET_V_W]
    for i in range(even_w_in.shape[0]):
        w_in = even_w_in[i].astype(BF16)
        w_out = even_w_out[i].astype(BF16)
        prm["even"].append(([w_in[:, cuts[s]:cuts[s + 1]] for s in range(7)], w_out[:SB_W], w_out[SB_W:]))
    for i in range(odd_w_in.shape[0]):
        w_in = odd_w_in[i].astype(BF16)
        prm["odd"].append((w_in[:, :SWA_Q_W], w_in[:, SWA_Q_W:SWA_Q_W + SWA_KV_W], w_in[:, SWA_Q_W + SWA_KV_W:],
                           odd_w_out[i].astype(BF16),
                           jnp.tile(odd_q_norm[i], SWA_HEADS)[None, :], jnp.tile(odd_k_norm[i], SWA_KV_HEADS)[None, :]))
    return prm


def _trunk(x, pos, prm, sinks, caches, tm, sb_qb, sb_kb, ret_cs, swa_rows):
    b, n, d = x.shape
    t = b * n
    xf = x.reshape(t, d)
    depth = len(prm["ffn"])
    tab_rows = n if n % tm == 0 else t
    tile_tab = lambda tb: tb if tab_rows == n else jnp.tile(tb, (b, 1))
    cos_r, sin_r = [tile_tab(tb) for tb in _rope_tables(pos, RET_QK_DIM)]
    cos_s, sin_s = [tile_tab(tb) for tb in _rope_tables(pos, SWA_HEAD_DIM)]
    idx = jnp.arange(sb_kb)
    negu = jnp.where(idx[:, None] >= idx[None, :], -1.0, 0.0).astype(BF16)
    sb_k, sb_v, ret, swa_k, swa_v = [], [], [], [], []
    mix = []
    for layer in range(depth):
        i = layer // 2
        g = prm["g"][layer]
        ffn = prm["ffn"][layer]
        xf = _ffn_call(xf, mix, g[0], *ffn[0], tm)
        if layer % 2 == 0:
            ws, wo_sb, wo_r = prm["even"][i]
            qa, ka, kab, va, vab, qr, kr, vr, gt = _even_proj_call(xf, g[1], ws, cos_r, sin_r, tm)
            sb_k.append(ka.reshape(b, n, SB_HEADS, SB_HEAD_DIM))
            sb_v.append(va.reshape(b, n, SB_HEADS, SB_HEAD_DIM))
            r3 = lambda a: a.reshape(b, n, a.shape[-1])
            if caches is None:
                k_all, v_all, q_pos0 = r3(kab), r3(vab), 0
                state0 = jnp.zeros((b, RET_HEADS, RET_QK_DIM, RET_V_DIM), F32)
            else:
                past = caches["sb_k"].shape[2]
                padded = -(-(past + n) // sb_kb) * sb_kb
                cat = lambda cache, new: jnp.pad(
                    jnp.concatenate([cache.reshape(b, past, SB_W).astype(BF16), r3(new)], axis=1),
                    ((0, 0), (0, padded - past - n), (0, 0)))
                k_all, v_all, q_pos0 = cat(caches["sb_k"][i], kab), cat(caches["sb_v"][i], vab), past
                state0 = caches["ret"][i]
            o_sb = _sb_call(r3(qa), k_all, v_all, negu, sb_qb, q_pos0)
            o_r, st = _ret_call(r3(qr), r3(kr), r3(vr), r3(gt), state0, ret_cs)
            ret.append(st)
            mix = [(o_sb.reshape(t, SB_W), wo_sb), (o_r.reshape(t, RET_V_W), wo_r)]
        else:
            wq, wk, wv, wo, qg, kg = prm["odd"][i]
            q, k, v = _odd_proj_call(xf, g[1], wq, wk, wv, qg, kg, cos_s, sin_s, tm)
            r3 = lambda a: a.reshape(b, n, a.shape[-1])
            k3, v3 = r3(k), r3(v)
            if caches is None:
                o = _swa_call(r3(q), k3, k3, v3, v3, sinks[i], swa_rows, True)
                keep = min(WINDOW, n)
                k_rows, v_rows = k3[:, n - keep:], v3[:, n - keep:]
            else:
                kc = caches["swa_k"][i].reshape(b, -1, SWA_KV_W)
                vc = caches["swa_v"][i].reshape(b, -1, SWA_KV_W)
                o = _swa_call(r3(q), kc, k3, vc, v3, sinks[i], swa_rows, False)
                k_rows, v_rows = k3, v3
            swa_k.append(k_rows.reshape(b, -1, SWA_KV_HEADS, SWA_HEAD_DIM))
            swa_v.append(v_rows.reshape(b, -1, SWA_KV_HEADS, SWA_HEAD_DIM))
            mix = [(o.reshape(t, SWA_Q_W), wo)]
        xf = _ffn_call(xf, mix, g[2], *ffn[1], tm)
        mix = []
    return (xf.reshape(b, n, d), jnp.stack(sb_k), jnp.stack(sb_v), jnp.stack(ret), jnp.stack(swa_k),
            jnp.stack(swa_v))


def kernel(x_prompt, x_sample, cache_sb_k, cache_sb_v, state_ret, cache_swa_k, cache_swa_v, norm_g, ffn_w_gate,
           ffn_w_up, ffn_w_down, even_w_in, even_w_out, odd_w_in, odd_w_out, odd_q_norm, odd_k_norm, odd_sinks):
    prm = _prep_weights(norm_g, ffn_w_gate, ffn_w_up, ffn_w_down, even_w_in, even_w_out, odd_w_in, odd_w_out,
                        odd_q_norm, odd_k_norm)
    past = cache_sb_k.shape[2]
    n_p = x_prompt.shape[1]
    n_s = x_sample.shape[1]
    pos_prompt = jnp.arange(n_p, dtype=jnp.int32)
    pos_sample = past + jnp.arange(n_s, dtype=jnp.int32)
    y_p, sb_k_p, sb_v_p, ret_p, swa_k_p, swa_v_p = _trunk(
        x_prompt, pos_prompt, prm, odd_sinks, None, tm=512, sb_qb=128, sb_kb=256, ret_cs=256, swa_rows=128)
    caches = {"sb_k": cache_sb_k, "sb_v": cache_sb_v, "ret": state_ret, "swa_k": cache_swa_k, "swa_v": cache_swa_v}
    y_s, sb_k_s, sb_v_s, ret_s, swa_k_s, swa_v_s = _trunk(
        x_sample, pos_sample, prm, odd_sinks, caches, tm=x_sample.shape[0] * n_s, sb_qb=n_s, sb_kb=256,
        ret_cs=n_s, swa_rows=n_s)
    return (y_p, y_s, sb_k_p, sb_v_p, ret_p, swa_k_p, swa_v_p, sb_k_s, sb_v_s, ret_s, swa_k_s, swa_v_s)
```

```python
import functools
import math

import jax
import jax.numpy as jnp
from jax import lax
from jax.experimental import pallas as pl
from jax.experimental.pallas import tpu as pltpu

F32 = jnp.float32
BF16 = jnp.bfloat16

RMS_EPS = 1e-6
ROPE_THETA = 10000.0
CHUNK = 64
SB_HEADS = 8
SB_HEAD_DIM = 64
RET_HEADS = 4
RET_QK_DIM = 128
RET_V_DIM = 256
SWA_HEADS = 16
SWA_KV_HEADS = 4
SWA_HEAD_DIM = 64
WINDOW = 128

SB_W = SB_HEADS * SB_HEAD_DIM
RET_QK_W = RET_HEADS * RET_QK_DIM
RET_V_W = RET_HEADS * RET_V_DIM
SWA_Q_W = SWA_HEADS * SWA_HEAD_DIM
SWA_KV_W = SWA_KV_HEADS * SWA_HEAD_DIM

LANES = 128
FF_CHUNK = 256
VMEM_LIMIT = 56 * 1024 * 1024
LOG2E = math.log2(math.e)
SOFTPLUS2_LINEAR = 40.0


def _dot(a, b):
    return jnp.dot(a, b, preferred_element_type=F32)


def _dot_nt(a, b):
    return lax.dot_general(a, b, (((1,), (1,)), ((), ())), preferred_element_type=F32)


def _dot_tn(a, b):
    return lax.dot_general(a, b, (((0,), (0,)), ((), ())), preferred_element_type=F32)


def _rms(x, g):
    return x * lax.rsqrt(jnp.mean(x * x, axis=-1, keepdims=True) + RMS_EPS) * g


def _full_spec(shape):
    nd = len(shape)
    return pl.BlockSpec(shape, lambda *_: (0,) * nd, pipeline_mode=pl.Buffered(1))


def _params(sem):
    return pltpu.CompilerParams(dimension_semantics=sem, vmem_limit_bytes=VMEM_LIMIT)


def _ffn_kernel(*refs, n_mix, n_chunks):
    x_ref = refs[0]
    mix = refs[1:1 + 2 * n_mix]
    g_ref, wg_ref, wu_ref, wd_ref, o_ref, h_ref, acc_ref = refs[1 + 2 * n_mix:]
    x = x_ref[...]
    for m in range(n_mix):
        x = x + _dot(mix[2 * m][...], mix[2 * m + 1][...])
    h_ref[...] = _rms(x, g_ref[...]).astype(BF16)
    acc_ref[...] = jnp.zeros_like(acc_ref)

    def body(c, carry):
        h = h_ref[...]
        gate = _dot(h, wg_ref[c])
        up = _dot(h, wu_ref[c])
        act = (gate * jax.nn.sigmoid(gate) * up).astype(BF16)
        acc_ref[...] += _dot(act, wd_ref[c])
        return carry

    lax.fori_loop(0, n_chunks, body, 0)
    o_ref[...] = x + 0.5 * acc_ref[...]


def _ffn_call(x, mix, g, wg, wu, wd, tm):
    t, d = x.shape
    n_chunks = wg.shape[0]
    row = lambda w: pl.BlockSpec((tm, w), lambda i: (i, 0))
    in_specs = [row(d)]
    args = [x]
    for a, w in mix:
        in_specs += [row(a.shape[1]), _full_spec(w.shape)]
        args += [a, w]
    in_specs += [_full_spec(g.shape), _full_spec(wg.shape), _full_spec(wu.shape), _full_spec(wd.shape)]
    args += [g, wg, wu, wd]
    return pl.pallas_call(
        functools.partial(_ffn_kernel, n_mix=len(mix), n_chunks=n_chunks),
        grid=(t // tm,),
        in_specs=in_specs,
        out_specs=row(d),
        out_shape=jax.ShapeDtypeStruct((t, d), F32),
        scratch_shapes=[pltpu.VMEM((tm, d), BF16), pltpu.VMEM((tm, d), F32)],
        compiler_params=_params(("parallel",)),
        name="ffn_mix%d" % len(mix),
    )(*args)


def _rope128(x, cos2, sin2):
    parts = []
    for hd in range(x.shape[1] // LANES):
        sl = x[:, hd * LANES:(hd + 1) * LANES]
        parts.append(sl * cos2 + pltpu.roll(sl, LANES // 2, axis=1) * sin2)
    return jnp.concatenate(parts, axis=1)


def _even_proj_kernel(x_ref, g_ref, wq_ref, wk_ref, wv_ref, wqr_ref, wkr_ref, wvr_ref, wgt_ref, cos_ref,
                      sin_ref, qa_ref, ka_ref, kab_ref, va_ref, vab_ref, qr_ref, kr_ref, vr_ref, gt_ref):
    h = _rms(x_ref[...], g_ref[...]).astype(BF16)
    qa_ref[...] = (_dot(h, wq_ref[...]) * (LOG2E * SB_HEAD_DIM ** -0.5)).astype(BF16)
    ka = _dot(h, wk_ref[...])
    ka_ref[...] = ka
    kab_ref[...] = ka.astype(BF16)
    va = _dot(h, wv_ref[...])
    va_ref[...] = va
    vab_ref[...] = va.astype(BF16)
    cos2 = cos_ref[...]
    sin2 = sin_ref[...]
    qr_ref[...] = _rope128(_dot(h, wqr_ref[...]), cos2, sin2).astype(BF16)
    kr_ref[...] = (_rope128(_dot(h, wkr_ref[...]), cos2, sin2) * (RET_QK_DIM ** -0.5)).astype(BF16)
    vr_ref[...] = _dot(h, wvr_ref[...]).astype(BF16)
    gt_ref[...] = _dot(h, wgt_ref[...])


def _even_proj_call(x, g, ws, cos2, sin2, tm):
    t, d = x.shape
    n_tab = cos2.shape[0] // tm
    row = lambda w: pl.BlockSpec((tm, w), lambda i: (i, 0))
    tab = pl.BlockSpec((tm, LANES), lambda i: (i % n_tab, 0))
    outs = [(SB_W, BF16), (SB_W, F32), (SB_W, BF16), (SB_W, F32), (SB_W, BF16),
            (RET_QK_W, BF16), (RET_QK_W, BF16), (RET_V_W, BF16), (RET_V_W, F32)]
    return pl.pallas_call(
        _even_proj_kernel,
        grid=(t // tm,),
        in_specs=[row(d), _full_spec(g.shape)] + [_full_spec(w.shape) for w in ws] + [tab, tab],
        out_specs=[row(w) for w, _ in outs],
        out_shape=[jax.ShapeDtypeStruct((t, w), dt) for w, dt in outs],
        compiler_params=_params(("parallel",)),
        name="even_proj",
    )(x, g, *ws, cos2, sin2)


def _sb_kernel(q_ref, k_ref, v_ref, negu_ref, o_ref, qh_ref, c_ref, acc_ref, *, qb, kb, q_pos0):
    i = pl.program_id(2)
    qstart = q_pos0 + i * qb
    nkb = (qstart + qb - 2 + kb) // kb
    n_clear = qstart // kb
    lane = lax.broadcasted_iota(jnp.int32, (qb, LANES), 1)
    first = lane < SB_HEAD_DIM
    q = q_ref[...]
    zero = jnp.zeros_like(q)
    qh_ref[0] = jnp.where(first, q, zero)
    qh_ref[1] = jnp.where(first, zero, q)
    c_ref[...] = jnp.zeros_like(c_ref)
    acc_ref[...] = jnp.zeros_like(acc_ref)

    def step(j, masked):
        start = pl.multiple_of(j * kb, kb)
        ks = k_ref[pl.ds(start, kb), :]
        vs = v_ref[pl.ds(start, kb), :]
        if masked:
            qpos = qstart + lax.broadcasted_iota(jnp.int32, (qb, kb), 0)
            mask = (lax.broadcasted_iota(jnp.int32, (qb, kb), 1) + j * kb) < qpos
        for hd in range(2):
            z = _dot_nt(qh_ref[hd], ks)
            sp = jnp.log2(1.0 + jnp.exp2(z))
            sp = jnp.where(z > SOFTPLUS2_LINEAR, z, sp)
            if masked:
                sp = jnp.where(mask, sp, 0.0)
            tail = _dot(sp.astype(BF16), negu_ref[...])
            w = jnp.exp2(z + tail + c_ref[hd])
            if masked:
                w = jnp.where(mask, w, 0.0)
            acc_ref[hd] += _dot(w.astype(BF16), vs)
            c_ref[hd] += tail[:, :1]

    def masked_body(t, carry):
        step(nkb - 1 - t, True)
        return carry

    def clear_body(t, carry):
        step(n_clear - 1 - t, False)
        return carry

    lax.fori_loop(0, nkb - n_clear, masked_body, 0)
    lax.fori_loop(0, n_clear, clear_body, 0)
    o_ref[...] = jnp.where(first, acc_ref[0], acc_ref[1]).astype(o_ref.dtype)


def _sb_call(q, k, v, negu, qb, q_pos0):
    b, nq, w = q.shape
    nk = k.shape[1]
    kb = negu.shape[0]
    return pl.pallas_call(
        functools.partial(_sb_kernel, qb=qb, kb=kb, q_pos0=q_pos0),
        grid=(b, w // LANES, nq // qb),
        in_specs=[pl.BlockSpec((None, qb, LANES), lambda bi, p, i: (bi, i, p)),
                  pl.BlockSpec((None, nk, LANES), lambda bi, p, i: (bi, 0, p)),
                  pl.BlockSpec((None, nk, LANES), lambda bi, p, i: (bi, 0, p)),
                  _full_spec(negu.shape)],
        out_specs=pl.BlockSpec((None, qb, LANES), lambda bi, p, i: (bi, i, p)),
        out_shape=jax.ShapeDtypeStruct((b, nq, w), BF16),
        scratch_shapes=[pltpu.VMEM((2, qb, LANES), BF16), pltpu.VMEM((2, qb, 1), F32),
                        pltpu.VMEM((2, qb, LANES), F32)],
        compiler_params=_params(("parallel", "parallel", "parallel")),
        name="stick_breaking",
    )(q, k, v, negu)


def _ret_kernel(sdec_ref, q_ref, k_ref, v_ref, gt_ref, s0_ref, dec_ref, qdec_ref, kdec_ref,
                o_ref, sfin_ref, st_ref):
    c = pl.program_id(1)

    @pl.when(c == 0)
    def _():
        st_ref[...] = s0_ref[...]

    for h in range(RET_HEADS):
        q = q_ref[:, h * RET_QK_DIM:(h + 1) * RET_QK_DIM]
        k = k_ref[:, h * RET_QK_DIM:(h + 1) * RET_QK_DIM]
        v = v_ref[:, h * RET_V_DIM:(h + 1) * RET_V_DIM]
        state = st_ref[h]
        inner = _dot((_dot_nt(q, k) * dec_ref[h]).astype(BF16), v)
        cross = _dot(q, state.astype(BF16)) * qdec_ref[h]
        kw = (k.astype(F32) * kdec_ref[h]).astype(BF16)
        st_ref[h] = sdec_ref[h] * state + _dot_tn(kw, v)
        o = inner + cross
        o = o * lax.rsqrt(jnp.mean(o * o, axis=-1, keepdims=True) + RMS_EPS)
        gate = gt_ref[:, h * RET_V_DIM:(h + 1) * RET_V_DIM]
        o_ref[:, h * RET_V_DIM:(h + 1) * RET_V_DIM] = (o * (gate * jax.nn.sigmoid(gate))).astype(o_ref.dtype)

    @pl.when(c == pl.num_programs(1) - 1)
    def _():
        sfin_ref[...] = st_ref[...]


def _ret_call(q, k, v, gate, state0, cs):
    b, n, _ = q.shape
    log_gamma = jnp.log1p(-jnp.exp2(-5.0 - jnp.arange(RET_HEADS, dtype=F32)))
    pos = jnp.arange(cs, dtype=F32)
    diff = pos[:, None] - pos[None, :]
    dec = jnp.where(diff >= 0, jnp.exp(log_gamma[:, None, None] * jnp.maximum(diff, 0.0)), 0.0)
    qdec = jnp.exp(log_gamma[:, None] * (pos + 1.0))[..., None]
    kdec = jnp.exp(log_gamma[:, None] * (cs - 1.0 - pos))[..., None]
    sdec = jnp.exp(log_gamma * cs)
    seq = lambda w: pl.BlockSpec((None, cs, w), lambda bi, c: (bi, c, 0))
    st = pl.BlockSpec((None, RET_HEADS, RET_QK_DIM, RET_V_DIM), lambda bi, c: (bi, 0, 0, 0))
    return pl.pallas_call(
        _ret_kernel,
        grid=(b, n // cs),
        in_specs=[pl.BlockSpec(memory_space=pltpu.SMEM), seq(RET_QK_W), seq(RET_QK_W), seq(RET_V_W),
                  seq(RET_V_W), st, _full_spec(dec.shape), _full_spec(qdec.shape), _full_spec(kdec.shape)],
        out_specs=[seq(RET_V_W), st],
        out_shape=[jax.ShapeDtypeStruct((b, n, RET_V_W), BF16),
                   jax.ShapeDtypeStruct((b, RET_HEADS, RET_QK_DIM, RET_V_DIM), F32)],
        scratch_shapes=[pltpu.VMEM((RET_HEADS, RET_QK_DIM, RET_V_DIM), F32)],
        compiler_params=_params(("parallel", "arbitrary")),
        name="retention",
    )(sdec, q, k, v, gate, state0, dec, qdec, kdec)


def _head_norm_rope(t, bd, gain, cos4, sin4, first_half):
    parts = []
    for c in range(t.shape[1] // LANES):
        sq = t[:, c * LANES:(c + 1) * LANES]
        sq = sq * sq
        hi = sq.astype(BF16)
        lo = (sq - hi.astype(F32)).astype(BF16)
        parts.append(_dot(hi, bd) + _dot(lo, bd))
    ms = jnp.concatenate(parts, axis=1) * (1.0 / SWA_HEAD_DIM)
    y = t * lax.rsqrt(ms + RMS_EPS) * gain
    w = t.shape[1]
    quarter = SWA_HEAD_DIM // 2
    partner = jnp.where(first_half, pltpu.roll(y, w - quarter, axis=1), pltpu.roll(y, quarter, axis=1))
    return y * cos4 + partner * sin4


def _odd_proj_kernel(x_ref, g_ref, wq_ref, wk_ref, wv_ref, bd_ref, qg_ref, kg_ref, cos_ref, sin_ref,
                     q_ref, k_ref, v_ref):
    h = _rms(x_ref[...], g_ref[...]).astype(BF16)
    bd = bd_ref[...]
    cos2 = cos_ref[...]
    sin2 = sin_ref[...]
    tm = cos2.shape[0]
    nq = SWA_Q_W // LANES
    nk = SWA_KV_W // LANES
    lane_q = lax.broadcasted_iota(jnp.int32, (tm, SWA_Q_W), 1)
    lane_k = lax.broadcasted_iota(jnp.int32, (tm, SWA_KV_W), 1)
    half = SWA_HEAD_DIM // 2
    q = _head_norm_rope(_dot(h, wq_ref[...]), bd, qg_ref[...], jnp.concatenate([cos2] * nq, axis=1),
                        jnp.concatenate([sin2] * nq, axis=1), (lane_q % SWA_HEAD_DIM) < half)
    q_ref[...] = (q * (SWA_HEAD_DIM ** -0.5)).astype(BF16)
    k_ref[...] = _head_norm_rope(_dot(h, wk_ref[...]), bd, kg_ref[...], jnp.concatenate([cos2] * nk, axis=1),
                                 jnp.concatenate([sin2] * nk, axis=1), (lane_k % SWA_HEAD_DIM) < half)
    v_ref[...] = _dot(h, wv_ref[...])


def _odd_proj_call(x, g, wq, wk, wv, qg, kg, cos2, sin2, tm):
    t, d = x.shape
    n_tab = cos2.shape[0] // tm
    head = jnp.arange(LANES) // SWA_HEAD_DIM
    bd = (head[:, None] == head[None, :]).astype(BF16)
    row = lambda w: pl.BlockSpec((tm, w), lambda i: (i, 0))
    tab = pl.BlockSpec((tm, LANES), lambda i: (i % n_tab, 0))
    ins = [g, wq, wk, wv, bd, qg, kg]
    return pl.pallas_call(
        _odd_proj_kernel,
        grid=(t // tm,),
        in_specs=[row(d)] + [_full_spec(a.shape) for a in ins] + [tab, tab],
        out_specs=[row(SWA_Q_W), row(SWA_KV_W), row(SWA_KV_W)],
        out_shape=[jax.ShapeDtypeStruct((t, SWA_Q_W), BF16), jax.ShapeDtypeStruct((t, SWA_KV_W), F32),
                   jax.ShapeDtypeStruct((t, SWA_KV_W), F32)],
        compiler_params=_params(("parallel",)),
        name="odd_proj",
    )(x, *ins, cos2, sin2)


def _swa_kernel(sink_ref, q_ref, kp_ref, kc_ref, vp_ref, vc_ref, o_ref, *, rows, first_prev_valid):
    i = pl.program_id(1)
    nkeys = WINDOW + rows
    q = q_ref[...]
    k = jnp.concatenate([kp_ref[...], kc_ref[...]], axis=0).astype(BF16)
    v = jnp.concatenate([vp_ref[...], vc_ref[...]], axis=0).astype(BF16)
    qc = lax.broadcasted_iota(jnp.int32, (rows, nkeys), 0) // CHUNK
    kc = lax.broadcasted_iota(jnp.int32, (rows, nkeys), 1) // CHUNK
    nb = WINDOW // CHUNK
    valid = (kc >= qc) & (kc <= qc + nb)
    if not first_prev_valid:
        valid = valid & ((kc >= nb) | (i > 0))
    grp = SWA_HEADS // SWA_KV_HEADS
    outs = []
    for j in range(SWA_KV_HEADS):
        kj = k[:, j * SWA_HEAD_DIM:(j + 1) * SWA_HEAD_DIM]
        vj = v[:, j * SWA_HEAD_DIM:(j + 1) * SWA_HEAD_DIM]
        for gq in range(grp):
            hd = j * grp + gq
            s = _dot_nt(q[:, hd * SWA_HEAD_DIM:(hd + 1) * SWA_HEAD_DIM], kj)
            s = jnp.where(valid, s, -jnp.inf)
            sink = sink_ref[hd]
            m = jnp.maximum(jnp.max(s, axis=-1, keepdims=True), sink)
            p = jnp.exp(s - m)
            denom = jnp.sum(p, axis=-1, keepdims=True) + jnp.exp(sink - m)
            outs.append(_dot(p.astype(BF16), vj) / denom)
    o_ref[...] = jnp.concatenate(outs, axis=1).astype(o_ref.dtype)


def _swa_call(q, k_prev, k_cur, v_prev, v_cur, sinks, rows, same_array):
    b, n, _ = q.shape
    per = rows // WINDOW if same_array else 0
    cur = lambda w: pl.BlockSpec((None, rows, w), lambda bi, i: (bi, i, 0))
    prev = pl.BlockSpec((None, WINDOW, SWA_KV_W), lambda bi, i: (bi, jnp.maximum(i * per - 1, 0), 0))
    return pl.pallas_call(
        functools.partial(_swa_kernel, rows=rows, first_prev_valid=not same_array),
        grid=(b, n // rows),
        in_specs=[pl.BlockSpec(memory_space=pltpu.SMEM), cur(SWA_Q_W), prev, cur(SWA_KV_W), prev, cur(SWA_KV_W)],
        out_specs=cur(SWA_Q_W),
        out_shape=jax.ShapeDtypeStruct((b, n, SWA_Q_W), BF16),
        compiler_params=_params(("parallel", "parallel")),
        name="swa",
    )(sinks, q, k_prev, k_cur, v_prev, v_cur)


def _rope_tables(pos, head_dim):
    half = head_dim // 2
    inv = jnp.power(ROPE_THETA, -jnp.arange(half, dtype=F32) / half)
    ang = pos.astype(F32)[:, None] * inv[None, :]
    cos, sin = jnp.cos(ang), jnp.sin(ang)
    reps = LANES // head_dim
    return (jnp.tile(jnp.concatenate([cos, cos], axis=1), (1, reps)),
            jnp.tile(jnp.concatenate([-sin, sin], axis=1), (1, reps)))


def _chunk_cols(w):
    d, f = w.shape
    return w.reshape(d, f // FF_CHUNK, FF_CHUNK).transpose(1, 0, 2).astype(BF16)


def _prep_weights(norm_g, ffn_w_gate, ffn_w_up, ffn_w_down, even_w_in, even_w_out, odd_w_in, odd_w_out,
                  odd_q_norm, odd_k_norm):
    depth = norm_g.shape[0]
    prm = {"g": norm_g[:, :, None, :], "ffn": [], "even": [], "odd": []}
    for layer in range(depth):
        prm["ffn"].append([(_chunk_cols(ffn_w_gate[layer, s]), _chunk_cols(ffn_w_up[layer, s]),
                            ffn_w_down[layer, s].reshape(-1, FF_CHUNK, ffn_w_down.shape[-1]).astype(BF16))
                           for s in range(2)])
    cuts = [0, SB_W, 2 * SB_W, 3 * SB_W, 3 * SB_W + RET_QK_W, 3 * SB_W + 2 * RET_QK_W,
            3 * SB_W + 2 * RET_QK_W + RET_V_W, 3 * SB_W + 2 * RET_QK_W + 2 * RET_V_W]
    for i in range(even_w_in.shape[0]):
        w_in = even_w_in[i].astype(BF16)
        w_out = even_w_out[i].astype(BF16)
        prm["even"].append(([w_in[:, cuts[s]:cuts[s + 1]] for s in range(7)], w_out[:SB_W], w_out[SB_W:]))
    for i in range(odd_w_in.shape[0]):
        w_in = odd_w_in[i].astype(BF16)
        prm["odd"].append((w_in[:, :SWA_Q_W], w_in[:, SWA_Q_W:SWA_Q_W + SWA_KV_W], w_in[:, SWA_Q_W + SWA_KV_W:],
                           odd_w_out[i].astype(BF16),
                           jnp.tile(odd_q_norm[i], SWA_HEADS)[None, :], jnp.tile(odd_k_norm[i], SWA_KV_HEADS)[None, :]))
    return prm


def _trunk(x, pos, prm, sinks, caches, tm, sb_qb, sb_kb, ret_cs, swa_rows):
    b, n, d = x.shape
    t = b * n
    xf = x.reshape(t, d)
    depth = len(prm["ffn"])
    tab_rows = n if n % tm == 0 else t
    tile_tab = lambda tb: tb if tab_rows == n else jnp.tile(tb, (b, 1))
    cos_r, sin_r = [tile_tab(tb) for tb in _rope_tables(pos, RET_QK_DIM)]
    cos_s, sin_s = [tile_tab(tb) for tb in _rope_tables(pos, SWA_HEAD_DIM)]
    idx = jnp.arange(sb_kb)
    negu = jnp.where(idx[:, None] >= idx[None, :], -1.0, 0.0).astype(BF16)
    sb_k, sb_v, ret, swa_k, swa_v = [], [], [], [], []
    mix = []
    for layer in range(depth):
        i = layer // 2
        g = prm["g"][layer]
        ffn = prm["ffn"][layer]
        xf = _ffn_call(xf, mix, g[0], *ffn[0], tm)
        if layer % 2 == 0:
            ws, wo_sb, wo_r = prm["even"][i]
            qa, ka, kab, va, vab, qr, kr, vr, gt = _even_proj_call(xf, g[1], ws, cos_r, sin_r, tm)
            sb_k.append(ka.reshape(b, n, SB_HEADS, SB_HEAD_DIM))
            sb_v.append(va.reshape(b, n, SB_HEADS, SB_HEAD_DIM))
            r3 = lambda a: a.reshape(b, n, a.shape[-1])
            if caches is None:
                k_all, v_all, q_pos0 = r3(kab), r3(vab), 0
                state0 = jnp.zeros((b, RET_HEADS, RET_QK_DIM, RET_V_DIM), F32)
            else:
                past = caches["sb_k"].shape[2]
                padded = -(-(past + n) // sb_kb) * sb_kb
                cat = lambda cache, new: jnp.pad(
                    jnp.concatenate([cache.reshape(b, past, SB_W).astype(BF16), r3(new)], axis=1),
                    ((0, 0), (0, padded - past - n), (0, 0)))
                k_all, v_all, q_pos0 = cat(caches["sb_k"][i], kab), cat(caches["sb_v"][i], vab), past
                state0 = caches["ret"][i]
            o_sb = _sb_call(r3(qa), k_all, v_all, negu, sb_qb, q_pos0)
            o_r, st = _ret_call(r3(qr), r3(kr), r3(vr), r3(gt), state0, ret_cs)
            ret.append(st)
            mix = [(o_sb.reshape(t, SB_W), wo_sb), (o_r.reshape(t, RET_V_W), wo_r)]
        else:
            wq, wk, wv, wo, qg, kg = prm["odd"][i]
            q, k, v = _odd_proj_call(xf, g[1], wq, wk, wv, qg, kg, cos_s, sin_s, tm)
            r3 = lambda a: a.reshape(b, n, a.shape[-1])
            k3, v3 = r3(k), r3(v)
            if caches is None:
                o = _swa_call(r3(q), k3, k3, v3, v3, sinks[i], swa_rows, True)
                keep = min(WINDOW, n)
                k_rows, v_rows = k3[:, n - keep:], v3[:, n - keep:]
            else:
                kc = caches["swa_k"][i].reshape(b, -1, SWA_KV_W)
                vc = caches["swa_v"][i].reshape(b, -1, SWA_KV_W)
                o = _swa_call(r3(q), kc, k3, vc, v3, sinks[i], swa_rows, False)
                k_rows, v_rows = k3, v3
            swa_k.append(k_rows.reshape(b, -1, SWA_KV_HEADS, SWA_HEAD_DIM))
            swa_v.append(v_rows.reshape(b, -1, SWA_KV_HEADS, SWA_HEAD_DIM))
            mix = [(o.reshape(t, SWA_Q_W), wo)]
        xf = _ffn_call(xf, mix, g[2], *ffn[1], tm)
        mix = []
    return (xf.reshape(b, n, d), jnp.stack(sb_k), jnp.stack(sb_v), jnp.stack(ret), jnp.stack(swa_k),
            jnp.stack(swa_v))


def kernel(x_prompt, x_sample, cache_sb_k, cache_sb_v, state_ret, cache_swa_k, cache_swa_v, norm_g, ffn_w_gate,
           ffn_w_up, ffn_w_down, even_w_in, even_w_out, odd_w_in, odd_w_out, odd_q_norm, odd_k_norm, odd_sinks):
    prm = _prep_weights(norm_g, ffn_w_gate, ffn_w_up, ffn_w_down, even_w_in, even_w_out, odd_w_in, odd_w_out,
                        odd_q_norm, odd_k_norm)
    past = cache_sb_k.shape[2]
    n_p = x_prompt.shape[1]
    n_s = x_sample.shape[1]
    pos_prompt = jnp.arange(n_p, dtype=jnp.int32)
    pos_sample = past + jnp.arange(n_s, dtype=jnp.int32)
    y_p, sb_k_p, sb_v_p, ret_p, swa_k_p, swa_v_p = _trunk(
        x_prompt, pos_prompt, prm, odd_sinks, None, tm=512, sb_qb=1024, sb_kb=256, ret_cs=256, swa_rows=128)
    caches = {"sb_k": cache_sb_k, "sb_v": cache_sb_v, "ret": state_ret, "swa_k": cache_swa_k, "swa_v": cache_swa_v}
    y_s, sb_k_s, sb_v_s, ret_s, swa_k_s, swa_v_s = _trunk(
        x_sample, pos_sample, prm, odd_sinks, caches, tm=x_sample.shape[0] * n_s, sb_qb=n_s, sb_kb=256,
        ret_cs=n_s, swa_rows=n_s)
    return (y_p, y_s, sb_k_p, sb_v_p, ret_p, swa_k_p, swa_v_p, sb_k_s, sb_v_s, ret_s, swa_k_s, swa_v_s)
```

```python
import functools
import math

import jax
import jax.numpy as jnp
from jax import lax
from jax.experimental import pallas as pl
from jax.experimental.pallas import tpu as pltpu

F32 = jnp.float32
BF16 = jnp.bfloat16

RMS_EPS = 1e-6
ROPE_THETA = 10000.0
CHUNK = 64
SB_HEADS = 8
SB_HEAD_DIM = 64
RET_HEADS = 4
RET_QK_DIM = 128
RET_V_DIM = 256
SWA_HEADS = 16
SWA_KV_HEADS = 4
SWA_HEAD_DIM = 64
WINDOW = 128

SB_W = SB_HEADS * SB_HEAD_DIM
RET_QK_W = RET_HEADS * RET_QK_DIM
RET_V_W = RET_HEADS * RET_V_DIM
SWA_Q_W = SWA_HEADS * SWA_HEAD_DIM
SWA_KV_W = SWA_KV_HEADS * SWA_HEAD_DIM

LANES = 128
FF_CHUNK = 256
VMEM_LIMIT = 56 * 1024 * 1024
LOG2E = math.log2(math.e)
SOFTPLUS2_LINEAR = 40.0
SB_DEAD_LOG2 = -170.0


def _dot(a, b):
    return jnp.dot(a, b, preferred_element_type=F32)


def _dot_nt(a, b):
    return lax.dot_general(a, b, (((1,), (1,)), ((), ())), preferred_element_type=F32)


def _dot_tn(a, b):
    return lax.dot_general(a, b, (((0,), (0,)), ((), ())), preferred_element_type=F32)


def _rms(x, g):
    return x * lax.rsqrt(jnp.mean(x * x, axis=-1, keepdims=True) + RMS_EPS) * g


def _full_spec(shape):
    nd = len(shape)
    return pl.BlockSpec(shape, lambda *_: (0,) * nd, pipeline_mode=pl.Buffered(1))


def _params(sem):
    return pltpu.CompilerParams(dimension_semantics=sem, vmem_limit_bytes=VMEM_LIMIT)


def _ffn_kernel(*refs, n_mix, n_chunks):
    x_ref = refs[0]
    mix = refs[1:1 + 2 * n_mix]
    g_ref, wg_ref, wu_ref, wd_ref, o_ref, h_ref, acc_ref = refs[1 + 2 * n_mix:]
    x = x_ref[...]
    for m in range(n_mix):
        x = x + _dot(mix[2 * m][...], mix[2 * m + 1][...])
    h_ref[...] = _rms(x, g_ref[...]).astype(BF16)
    acc_ref[...] = jnp.zeros_like(acc_ref)

    def body(c, carry):
        h = h_ref[...]
        gate = _dot(h, wg_ref[c])
        up = _dot(h, wu_ref[c])
        act = (gate * jax.nn.sigmoid(gate) * up).astype(BF16)
        acc_ref[...] += _dot(act, wd_ref[c])
        return carry

    lax.fori_loop(0, n_chunks, body, 0)
    o_ref[...] = x + 0.5 * acc_ref[...]


def _ffn_call(x, mix, g, wg, wu, wd, tm):
    t, d = x.shape
    n_chunks = wg.shape[0]
    row = lambda w: pl.BlockSpec((tm, w), lambda i: (i, 0))
    in_specs = [row(d)]
    args = [x]
    for a, w in mix:
        in_specs += [row(a.shape[1]), _full_spec(w.shape)]
        args += [a, w]
    in_specs += [_full_spec(g.shape), _full_spec(wg.shape), _full_spec(wu.shape), _full_spec(wd.shape)]
    args += [g, wg, wu, wd]
    return pl.pallas_call(
        functools.partial(_ffn_kernel, n_mix=len(mix), n_chunks=n_chunks),
        grid=(t // tm,),
        in_specs=in_specs,
        out_specs=row(d),
        out_shape=jax.ShapeDtypeStruct((t, d), F32),
        scratch_shapes=[pltpu.VMEM((tm, d), BF16), pltpu.VMEM((tm, d), F32)],
        compiler_params=_params(("parallel",)),
        name="ffn_mix%d" % len(mix),
    )(*args)


def _rope128(x, cos2, sin2):
    parts = []
    for hd in range(x.shape[1] // LANES):
        sl = x[:, hd * LANES:(hd + 1) * LANES]
        parts.append(sl * cos2 + pltpu.roll(sl, LANES // 2, axis=1) * sin2)
    return jnp.concatenate(parts, axis=1)


def _even_proj_kernel(x_ref, g_ref, wq_ref, wk_ref, wv_ref, wqr_ref, wkr_ref, wvr_ref, wgt_ref, cos_ref,
                      sin_ref, qa_ref, ka_ref, kab_ref, va_ref, vab_ref, qr_ref, kr_ref, vr_ref, gt_ref):
    h = _rms(x_ref[...], g_ref[...]).astype(BF16)
    qa_ref[...] = (_dot(h, wq_ref[...]) * (LOG2E * SB_HEAD_DIM ** -0.5)).astype(BF16)
    ka = _dot(h, wk_ref[...])
    ka_ref[...] = ka
    kab_ref[...] = ka.astype(BF16)
    va = _dot(h, wv_ref[...])
    va_ref[...] = va
    vab_ref[...] = va.astype(BF16)
    cos2 = cos_ref[...]
    sin2 = sin_ref[...]
    qr_ref[...] = _rope128(_dot(h, wqr_ref[...]), cos2, sin2).astype(BF16)
    kr_ref[...] = (_rope128(_dot(h, wkr_ref[...]), cos2, sin2) * (RET_QK_DIM ** -0.5)).astype(BF16)
    vr_ref[...] = _dot(h, wvr_ref[...]).astype(BF16)
    gt_ref[...] = _dot(h, wgt_ref[...])


def _even_proj_call(x, g, ws, cos2, sin2, tm):
    t, d = x.shape
    n_tab = cos2.shape[0] // tm
    row = lambda w: pl.BlockSpec((tm, w), lambda i: (i, 0))
    tab = pl.BlockSpec((tm, LANES), lambda i: (i % n_tab, 0))
    outs = [(SB_W, BF16), (SB_W, F32), (SB_W, BF16), (SB_W, F32), (SB_W, BF16),
            (RET_QK_W, BF16), (RET_QK_W, BF16), (RET_V_W, BF16), (RET_V_W, F32)]
    return pl.pallas_call(
        _even_proj_kernel,
        grid=(t // tm,),
        in_specs=[row(d), _full_spec(g.shape)] + [_full_spec(w.shape) for w in ws] + [tab, tab],
        out_specs=[row(w) for w, _ in outs],
        out_shape=[jax.ShapeDtypeStruct((t, w), dt) for w, dt in outs],
        compiler_params=_params(("parallel",)),
        name="even_proj",
    )(x, g, *ws, cos2, sin2)


def _sb_kernel(q_ref, k_ref, v_ref, negu_ref, o_ref, qh_ref, c_ref, acc_ref, *, qb, kb, q_pos0):
    i = pl.program_id(2)
    n_clear = (q_pos0 + i * qb) // kb
    sq = min(qb, kb)
    lane = lax.broadcasted_iota(jnp.int32, (qb, LANES), 1)
    first = lane < SB_HEAD_DIM
    q = q_ref[...]
    zero = jnp.zeros_like(q)
    qh_ref[0] = jnp.where(first, q, zero)
    qh_ref[1] = jnp.where(first, zero, q)
    c_ref[...] = jnp.zeros_like(c_ref)
    acc_ref[...] = jnp.zeros_like(acc_ref)
    below = lax.broadcasted_iota(jnp.int32, (sq, kb), 1) < lax.broadcasted_iota(jnp.int32, (sq, kb), 0)

    def block(start, r0, r1, mask):
        ks = k_ref[pl.ds(start, kb), :]
        vs = v_ref[pl.ds(start, kb), :]
        for hd in range(2):
            z = _dot_nt(qh_ref[hd, r0:r1], ks)
            sp = jnp.log2(1.0 + jnp.exp2(z))
            sp = jnp.where(z > SOFTPLUS2_LINEAR, z, sp)
            if mask is not None:
                sp = jnp.where(mask, sp, 0.0)
            tail = _dot(sp.astype(BF16), negu_ref[...])
            w = jnp.exp2(z + tail + c_ref[hd, r0:r1])
            if mask is not None:
                w = jnp.where(mask, w, 0.0)
            acc_ref[hd, r0:r1] += _dot(w.astype(BF16), vs)
            c_ref[hd, r0:r1] += tail[:, :1]

    for d in reversed(range(qb // sq)):
        start = pl.multiple_of((n_clear + d) * kb, kb)
        block(start, d * sq, (d + 1) * sq, below)
        if (d + 1) * sq < qb:
            block(start, (d + 1) * sq, qb, None)

    def alive():
        return (jnp.max(c_ref[...]) > SB_DEAD_LOG2).astype(jnp.int32)

    def cond(carry):
        t, live = carry
        return jnp.logical_and(t < n_clear, live > 0)

    def body(carry):
        t, _ = carry
        block(pl.multiple_of((n_clear - 1 - t) * kb, kb), 0, qb, None)
        return t + 1, alive()

    lax.while_loop(cond, body, (jnp.int32(0), alive()))
    o_ref[...] = jnp.where(first, acc_ref[0], acc_ref[1]).astype(o_ref.dtype)


def _sb_call(q, k, v, negu, qb, q_pos0):
    b, nq, w = q.shape
    nk = k.shape[1]
    kb = negu.shape[0]
    assert q_pos0 % kb == 0 and (qb % kb == 0 or nq == qb < kb) and nk % kb == 0 and nk >= q_pos0 + nq
    return pl.pallas_call(
        functools.partial(_sb_kernel, qb=qb, kb=kb, q_pos0=q_pos0),
        grid=(b, w // LANES, nq // qb),
        in_specs=[pl.BlockSpec((None, qb, LANES), lambda bi, p, i: (bi, i, p)),
                  pl.BlockSpec((None, nk, LANES), lambda bi, p, i: (bi, 0, p)),
                  pl.BlockSpec((None, nk, LANES), lambda bi, p, i: (bi, 0, p)),
                  _full_spec(negu.shape)],
        out_specs=pl.BlockSpec((None, qb, LANES), lambda bi, p, i: (bi, i, p)),
        out_shape=jax.ShapeDtypeStruct((b, nq, w), BF16),
        scratch_shapes=[pltpu.VMEM((2, qb, LANES), BF16), pltpu.VMEM((2, qb, 1), F32),
                        pltpu.VMEM((2, qb, LANES), F32)],
        compiler_params=_params(("parallel", "parallel", "parallel")),
        name="stick_breaking",
    )(q, k, v, negu)


def _ret_kernel(sdec_ref, q_ref, k_ref, v_ref, gt_ref, s0_ref, dec_ref, qdec_ref, kdec_ref,
                o_ref, sfin_ref, st_ref):
    c = pl.program_id(1)

    @pl.when(c == 0)
    def _():
        st_ref[...] = s0_ref[...]

    for h in range(RET_HEADS):
        q = q_ref[:, h * RET_QK_DIM:(h + 1) * RET_QK_DIM]
        k = k_ref[:, h * RET_QK_DIM:(h + 1) * RET_QK_DIM]
        v = v_ref[:, h * RET_V_DIM:(h + 1) * RET_V_DIM]
        state = st_ref[h]
        inner = _dot((_dot_nt(q, k) * dec_ref[h]).astype(BF16), v)
        cross = _dot(q, state.astype(BF16)) * qdec_ref[h]
        kw = (k.astype(F32) * kdec_ref[h]).astype(BF16)
        st_ref[h] = sdec_ref[h] * state + _dot_tn(kw, v)
        o = inner + cross
        o = o * lax.rsqrt(jnp.mean(o * o, axis=-1, keepdims=True) + RMS_EPS)
        gate = gt_ref[:, h * RET_V_DIM:(h + 1) * RET_V_DIM]
        o_ref[:, h * RET_V_DIM:(h + 1) * RET_V_DIM] = (o * (gate * jax.nn.sigmoid(gate))).astype(o_ref.dtype)

    @pl.when(c == pl.num_programs(1) - 1)
    def _():
        sfin_ref[...] = st_ref[...]


def _ret_call(q, k, v, gate, state0, cs):
    b, n, _ = q.shape
    log_gamma = jnp.log1p(-jnp.exp2(-5.0 - jnp.arange(RET_HEADS, dtype=F32)))
    pos = jnp.arange(cs, dtype=F32)
    diff = pos[:, None] - pos[None, :]
    dec = jnp.where(diff >= 0, jnp.exp(log_gamma[:, None, None] * jnp.maximum(diff, 0.0)), 0.0)
    qdec = jnp.exp(log_gamma[:, None] * (pos + 1.0))[..., None]
    kdec = jnp.exp(log_gamma[:, None] * (cs - 1.0 - pos))[..., None]
    sdec = jnp.exp(log_gamma * cs)
    seq = lambda w: pl.BlockSpec((None, cs, w), lambda bi, c: (bi, c, 0))
    st = pl.BlockSpec((None, RET_HEADS, RET_QK_DIM, RET_V_DIM), lambda bi, c: (bi, 0, 0, 0))
    return pl.pallas_call(
        _ret_kernel,
        grid=(b, n // cs),
        in_specs=[pl.BlockSpec(memory_space=pltpu.SMEM), seq(RET_QK_W), seq(RET_QK_W), seq(RET_V_W),
                  seq(RET_V_W), st, _full_spec(dec.shape), _full_spec(qdec.shape), _full_spec(kdec.shape)],
        out_specs=[seq(RET_V_W), st],
        out_shape=[jax.ShapeDtypeStruct((b, n, RET_V_W), BF16),
                   jax.ShapeDtypeStruct((b, RET_HEADS, RET_QK_DIM, RET_V_DIM), F32)],
        scratch_shapes=[pltpu.VMEM((RET_HEADS, RET_QK_DIM, RET_V_DIM), F32)],
        compiler_params=_params(("parallel", "arbitrary")),
        name="retention",
    )(sdec, q, k, v, gate, state0, dec, qdec, kdec)


def _head_norm_rope(t, bd, gain, cos4, sin4, first_half):
    parts = []
    for c in range(t.shape[1] // LANES):
        sq = t[:, c * LANES:(c + 1) * LANES]
        sq = sq * sq
        hi = sq.astype(BF16)
        lo = (sq - hi.astype(F32)).astype(BF16)
        parts.append(_dot(hi, bd) + _dot(lo, bd))
    ms = jnp.concatenate(parts, axis=1) * (1.0 / SWA_HEAD_DIM)
    y = t * lax.rsqrt(ms + RMS_EPS) * gain
    w = t.shape[1]
    quarter = SWA_HEAD_DIM // 2
    partner = jnp.where(first_half, pltpu.roll(y, w - quarter, axis=1), pltpu.roll(y, quarter, axis=1))
    return y * cos4 + partner * sin4


def _odd_proj_kernel(x_ref, g_ref, wq_ref, wk_ref, wv_ref, bd_ref, qg_ref, kg_ref, cos_ref, sin_ref,
                     q_ref, k_ref, v_ref):
    h = _rms(x_ref[...], g_ref[...]).astype(BF16)
    bd = bd_ref[...]
    cos2 = cos_ref[...]
    sin2 = sin_ref[...]
    tm = cos2.shape[0]
    nq = SWA_Q_W // LANES
    nk = SWA_KV_W // LANES
    lane_q = lax.broadcasted_iota(jnp.int32, (tm, SWA_Q_W), 1)
    lane_k = lax.broadcasted_iota(jnp.int32, (tm, SWA_KV_W), 1)
    half = SWA_HEAD_DIM // 2
    q = _head_norm_rope(_dot(h, wq_ref[...]), bd, qg_ref[...], jnp.concatenate([cos2] * nq, axis=1),
                        jnp.concatenate([sin2] * nq, axis=1), (lane_q % SWA_HEAD_DIM) < half)
    q_ref[...] = (q * (SWA_HEAD_DIM ** -0.5)).astype(BF16)
    k_ref[...] = _head_norm_rope(_dot(h, wk_ref[...]), bd, kg_ref[...], jnp.concatenate([cos2] * nk, axis=1),
                                 jnp.concatenate([sin2] * nk, axis=1), (lane_k % SWA_HEAD_DIM) < half)
    v_ref[...] = _dot(h, wv_ref[...])


def _odd_proj_call(x, g, wq, wk, wv, qg, kg, cos2, sin2, tm):
    t, d = x.shape
    n_tab = cos2.shape[0] // tm
    head = jnp.arange(LANES) // SWA_HEAD_DIM
    bd = (head[:, None] == head[None, :]).astype(BF16)
    row = lambda w: pl.BlockSpec((tm, w), lambda i: (i, 0))
    tab = pl.BlockSpec((tm, LANES), lambda i: (i % n_tab, 0))
    ins = [g, wq, wk, wv, bd, qg, kg]
    return pl.pallas_call(
        _odd_proj_kernel,
        grid=(t // tm,),
        in_specs=[row(d)] + [_full_spec(a.shape) for a in ins] + [tab, tab],
        out_specs=[row(SWA_Q_W), row(SWA_KV_W), row(SWA_KV_W)],
        out_shape=[jax.ShapeDtypeStruct((t, SWA_Q_W), BF16), jax.ShapeDtypeStruct((t, SWA_KV_W), F32),
                   jax.ShapeDtypeStruct((t, SWA_KV_W), F32)],
        compiler_params=_params(("parallel",)),
        name="odd_proj",
    )(x, *ins, cos2, sin2)


def _swa_kernel(sink_ref, q_ref, kp_ref, kc_ref, vp_ref, vc_ref, o_ref, *, rows, first_prev_valid):
    i = pl.program_id(1)
    nkeys = WINDOW + rows
    q = q_ref[...]
    k = jnp.concatenate([kp_ref[...], kc_ref[...]], axis=0).astype(BF16)
    v = jnp.concatenate([vp_ref[...], vc_ref[...]], axis=0).astype(BF16)
    qc = lax.broadcasted_iota(jnp.int32, (rows, nkeys), 0) // CHUNK
    kc = lax.broadcasted_iota(jnp.int32, (rows, nkeys), 1) // CHUNK
    nb = WINDOW // CHUNK
    valid = (kc >= qc) & (kc <= qc + nb)
    if not first_prev_valid:
        valid = valid & ((kc >= nb) | (i > 0))
    grp = SWA_HEADS // SWA_KV_HEADS
    outs = []
    for j in range(SWA_KV_HEADS):
        kj = k[:, j * SWA_HEAD_DIM:(j + 1) * SWA_HEAD_DIM]
        vj = v[:, j * SWA_HEAD_DIM:(j + 1) * SWA_HEAD_DIM]
        for gq in range(grp):
            hd = j * grp + gq
            s = _dot_nt(q[:, hd * SWA_HEAD_DIM:(hd + 1) * SWA_HEAD_DIM], kj)
            s = jnp.where(valid, s, -jnp.inf)
            sink = sink_ref[hd]
            m = jnp.maximum(jnp.max(s, axis=-1, keepdims=True), sink)
            p = jnp.exp(s - m)
            denom = jnp.sum(p, axis=-1, keepdims=True) + jnp.exp(sink - m)
            outs.append(_dot(p.astype(BF16), vj) / denom)
    o_ref[...] = jnp.concatenate(outs, axis=1).astype(o_ref.dtype)


def _swa_call(q, k_prev, k_cur, v_prev, v_cur, sinks, rows, same_array):
    b, n, _ = q.shape
    per = rows // WINDOW if same_array else 0
    cur = lambda w: pl.BlockSpec((None, rows, w), lambda bi, i: (bi, i, 0))
    prev = pl.BlockSpec((None, WINDOW, SWA_KV_W), lambda bi, i: (bi, jnp.maximum(i * per - 1, 0), 0))
    return pl.pallas_call(
        functools.partial(_swa_kernel, rows=rows, first_prev_valid=not same_array),
        grid=(b, n // rows),
        in_specs=[pl.BlockSpec(memory_space=pltpu.SMEM), cur(SWA_Q_W), prev, cur(SWA_KV_W), prev, cur(SWA_KV_W)],
        out_specs=cur(SWA_Q_W),
        out_shape=jax.ShapeDtypeStruct((b, n, SWA_Q_W), BF16),
        compiler_params=_params(("parallel", "parallel")),
        name="swa",
    )(sinks, q, k_prev, k_cur, v_prev, v_cur)


def _rope_tables(pos, head_dim):
    half = head_dim // 2
    inv = jnp.power(ROPE_THETA, -jnp.arange(half, dtype=F32) / half)
    ang = pos.astype(F32)[:, None] * inv[None, :]
    cos, sin = jnp.cos(ang), jnp.sin(ang)
    reps = LANES // head_dim
    return (jnp.tile(jnp.concatenate([cos, cos], axis=1), (1, reps)),
            jnp.tile(jnp.concatenate([-sin, sin], axis=1), (1, reps)))


def _chunk_cols(w):
    d, f = w.shape
    return w.reshape(d, f // FF_CHUNK, FF_CHUNK).transpose(1, 0, 2).astype(BF16)


def _prep_weights(norm_g, ffn_w_gate, ffn_w_up, ffn_w_down, even_w_in, even_w_out, odd_w_in, odd_w_out,
                  odd_q_norm, odd_k_norm):
    depth = norm_g.shape[0]
    prm = {"g": norm_g[:, :, None, :], "ffn": [], "even": [], "odd": []}
    for layer in range(depth):
        prm["ffn"].append([(_chunk_cols(ffn_w_gate[layer, s]), _chunk_cols(ffn_w_up[layer, s]),
                            ffn_w_down[layer, s].reshape(-1, FF_CHUNK, ffn_w_down.shape[-1]).astype(BF16))
                           for s in range(2)])
    cuts = [0, SB_W, 2 * SB_W, 3 * SB_W, 3 * SB_W + RET_QK_W, 3 * SB_W + 2 * RET_QK_W,
            3 * SB_W + 2 * RET_QK_W + RET_V_W, 3 * SB_W + 2 * RET_QK_W + 2 * RET_V_W]
    for i in range(even_w_in.shape[0]):
        w_in = even_w_in[i].astype(BF16)
        w_out = even_w_out[i].astype(BF16)
        prm["even"].append(([w_in[:, cuts[s]:cuts[s + 1]] for s in range(7)], w_out[:SB_W], w_out[SB_W:]))
    for i in range(odd_w_in.shape[0]):
        w_in = odd_w_in[i].astype(BF16)
        prm["odd"].append((w_in[:, :SWA_Q_W], w_in[:, SWA_Q_W:SWA_Q_W + SWA_KV_W], w_in[:, SWA_Q_W + SWA_KV_W:],
                           odd_w_out[i].astype(BF16),
                           jnp.tile(odd_q_norm[i], SWA_HEADS)[None, :], jnp.tile(odd_k_norm[i], SWA_KV_HEADS)[None, :]))
    return prm


def _trunk(x, pos, prm, sinks, caches, tm, sb_qb, sb_kb, ret_cs, swa_rows):
    b, n, d = x.shape
    t = b * n
    xf = x.reshape(t, d)
    depth = len(prm["ffn"])
    tab_rows = n if n % tm == 0 else t
    tile_tab = lambda tb: tb if tab_rows == n else jnp.tile(tb, (b, 1))
    cos_r, sin_r = [tile_tab(tb) for tb in _rope_tables(pos, RET_QK_DIM)]
    cos_s, sin_s = [tile_tab(tb) for tb in _rope_tables(pos, SWA_HEAD_DIM)]
    idx = jnp.arange(sb_kb)
    negu = jnp.where(idx[:, None] >= idx[None, :], -1.0, 0.0).astype(BF16)
    sb_k, sb_v, ret, swa_k, swa_v = [], [], [], [], []
    mix = []
    for layer in range(depth):
        i = layer // 2
        g = prm["g"][layer]
        ffn = prm["ffn"][layer]
        xf = _ffn_call(xf, mix, g[0], *ffn[0], tm)
        if layer % 2 == 0:
            ws, wo_sb, wo_r = prm["even"][i]
            qa, ka, kab, va, vab, qr, kr, vr, gt = _even_proj_call(xf, g[1], ws, cos_r, sin_r, tm)
            sb_k.append(ka.reshape(b, n, SB_HEADS, SB_HEAD_DIM))
            sb_v.append(va.reshape(b, n, SB_HEADS, SB_HEAD_DIM))
            r3 = lambda a: a.reshape(b, n, a.shape[-1])
            if caches is None:
                k_all, v_all, q_pos0 = r3(kab), r3(vab), 0
                state0 = jnp.zeros((b, RET_HEADS, RET_QK_DIM, RET_V_DIM), F32)
            else:
                past = caches["sb_k"].shape[2]
                padded = -(-(past + n) // sb_kb) * sb_kb
                cat = lambda cache, new: jnp.pad(
                    jnp.concatenate([cache.reshape(b, past, SB_W).astype(BF16), r3(new)], axis=1),
                    ((0, 0), (0, padded - past - n), (0, 0)))
                k_all, v_all, q_pos0 = cat(caches["sb_k"][i], kab), cat(caches["sb_v"][i], vab), past
                state0 = caches["ret"][i]
            o_sb = _sb_call(r3(qa), k_all, v_all, negu, sb_qb, q_pos0)
            o_r, st = _ret_call(r3(qr), r3(kr), r3(vr), r3(gt), state0, ret_cs)
            ret.append(st)
            mix = [(o_sb.reshape(t, SB_W), wo_sb), (o_r.reshape(t, RET_V_W), wo_r)]
        else:
            wq, wk, wv, wo, qg, kg = prm["odd"][i]
            q, k, v = _odd_proj_call(xf, g[1], wq, wk, wv, qg, kg, cos_s, sin_s, tm)
            r3 = lambda a: a.reshape(b, n, a.shape[-1])
            k3, v3 = r3(k), r3(v)
            if caches is None:
                o = _swa_call(r3(q), k3, k3, v3, v3, sinks[i], swa_rows, True)
                keep = min(WINDOW, n)
                k_rows, v_rows = k3[:, n - keep:], v3[:, n - keep:]
            else:
                kc = caches["swa_k"][i].reshape(b, -1, SWA_KV_W)
                vc = caches["swa_v"][i].reshape(b, -1, SWA_KV_W)
                o = _swa_call(r3(q), kc, k3, vc, v3, sinks[i], swa_rows, False)
                k_rows, v_rows = k3, v3
            swa_k.append(k_rows.reshape(b, -1, SWA_KV_HEADS, SWA_HEAD_DIM))
            swa_v.append(v_rows.reshape(b, -1, SWA_KV_HEADS, SWA_HEAD_DIM))
            mix = [(o.reshape(t, SWA_Q_W), wo)]
        xf = _ffn_call(xf, mix, g[2], *ffn[1], tm)
        mix = []
    return (xf.reshape(b, n, d), jnp.stack(sb_k), jnp.stack(sb_v), jnp.stack(ret), jnp.stack(swa_k),
            jnp.stack(swa_v))


def kernel(x_prompt, x_sample, cache_sb_k, cache_sb_v, state_ret, cache_swa_k, cache_swa_v, norm_g, ffn_w_gate,
           ffn_w_up, ffn_w_down, even_w_in, even_w_out, odd_w_in, odd_w_out, odd_q_norm, odd_k_norm, odd_sinks):
    prm = _prep_weights(norm_g, ffn_w_gate, ffn_w_up, ffn_w_down, even_w_in, even_w_out, odd_w_in, odd_w_out,
                        odd_q_norm, odd_k_norm)
    past = cache_sb_k.shape[2]
    n_p = x_prompt.shape[1]
    n_s = x_sample.shape[1]
    pos_prompt = jnp.arange(n_p, dtype=jnp.int32)
    pos_sample = past + jnp.arange(n_s, dtype=jnp.int32)
    y_p, sb_k_p, sb_v_p, ret_p, swa_k_p, swa_v_p = _trunk(
        x_prompt, pos_prompt, prm, odd_sinks, None, tm=512, sb_qb=1024, sb_kb=256, ret_cs=256, swa_rows=128)
    caches = {"sb_k": cache_sb_k, "sb_v": cache_sb_v, "ret": state_ret, "swa_k": cache_swa_k, "swa_v": cache_swa_v}
    y_s, sb_k_s, sb_v_s, ret_s, swa_k_s, swa_v_s = _trunk(
        x_sample, pos_sample, prm, odd_sinks, caches, tm=x_sample.shape[0] * n_s, sb_qb=n_s, sb_kb=256,
        ret_cs=n_s, swa_rows=n_s)
    return (y_p, y_s, sb_k_p, sb_v_p, ret_p, swa_k_p, swa_v_p, sb_k_s, sb_v_s, ret_s, swa_k_s, swa_v_s)
```

```python
import functools
import math

import jax
import jax.numpy as jnp
from jax import lax
from jax.experimental import pallas as pl
from jax.experimental.pallas import tpu as pltpu

F32 = jnp.float32
BF16 = jnp.bfloat16

RMS_EPS = 1e-6
ROPE_THETA = 10000.0
CHUNK = 64
SB_HEADS = 8
SB_HEAD_DIM = 64
RET_HEADS = 4
RET_QK_DIM = 128
RET_V_DIM = 256
SWA_HEADS = 16
SWA_KV_HEADS = 4
SWA_HEAD_DIM = 64
WINDOW = 128

SB_W = SB_HEADS * SB_HEAD_DIM
RET_QK_W = RET_HEADS * RET_QK_DIM
RET_V_W = RET_HEADS * RET_V_DIM
SWA_Q_W = SWA_HEADS * SWA_HEAD_DIM
SWA_KV_W = SWA_KV_HEADS * SWA_HEAD_DIM

LANES = 128
FF_CHUNK = 256
VMEM_LIMIT = 56 * 1024 * 1024
LOG2E = math.log2(math.e)
SOFTPLUS2_LINEAR = 40.0
SB_DEAD_LOG2 = -170.0


def _dot(a, b):
    return jnp.dot(a, b, preferred_element_type=F32)


def _dot_nt(a, b):
    return lax.dot_general(a, b, (((1,), (1,)), ((), ())), preferred_element_type=F32)


def _dot_tn(a, b):
    return lax.dot_general(a, b, (((0,), (0,)), ((), ())), preferred_element_type=F32)


def _rms(x, g):
    return x * lax.rsqrt(jnp.mean(x * x, axis=-1, keepdims=True) + RMS_EPS) * g


def _full_spec(shape):
    nd = len(shape)
    return pl.BlockSpec(shape, lambda *_: (0,) * nd, pipeline_mode=pl.Buffered(1))


def _params(sem):
    return pltpu.CompilerParams(dimension_semantics=sem, vmem_limit_bytes=VMEM_LIMIT)


def _ffn_kernel(*refs, n_mix):
    x_ref = refs[0]
    mix = refs[1:1 + 2 * n_mix]
    g_ref, wg_ref, wu_ref, wd_ref, o_ref, h_ref, act_ref = refs[1 + 2 * n_mix:]
    x = x_ref[...]
    for m in range(n_mix):
        x = x + _dot(mix[2 * m][...], mix[2 * m + 1][...])
    h_ref[...] = _rms(x, g_ref[...]).astype(BF16)
    for c in range(wg_ref.shape[1] // FF_CHUNK):
        cols = slice(c * FF_CHUNK, (c + 1) * FF_CHUNK)
        gate = _dot(h_ref[...], wg_ref[:, cols])
        up = _dot(h_ref[...], wu_ref[:, cols])
        act_ref[:, cols] = (gate * jax.nn.sigmoid(gate) * up).astype(BF16)
    o_ref[...] = x + 0.5 * _dot(act_ref[...], wd_ref[...])


def _ffn_call(x, mix, g, wg, wu, wd, tm):
    t, d = x.shape
    row = lambda w: pl.BlockSpec((tm, w), lambda i: (i, 0))
    in_specs = [row(d)]
    args = [x]
    for a, w in mix:
        in_specs += [row(a.shape[1]), _full_spec(w.shape)]
        args += [a, w]
    in_specs += [_full_spec(g.shape), _full_spec(wg.shape), _full_spec(wu.shape), _full_spec(wd.shape)]
    args += [g, wg, wu, wd]
    return pl.pallas_call(
        functools.partial(_ffn_kernel, n_mix=len(mix)),
        grid=(t // tm,),
        in_specs=in_specs,
        out_specs=row(d),
        out_shape=jax.ShapeDtypeStruct((t, d), F32),
        scratch_shapes=[pltpu.VMEM((tm, d), BF16), pltpu.VMEM((tm, wg.shape[1]), BF16)],
        compiler_params=_params(("parallel",)),
        name="ffn_mix%d" % len(mix),
    )(*args)


def _rope128(x, cos2, sin2):
    parts = []
    for hd in range(x.shape[1] // LANES):
        sl = x[:, hd * LANES:(hd + 1) * LANES]
        parts.append(sl * cos2 + pltpu.roll(sl, LANES // 2, axis=1) * sin2)
    return jnp.concatenate(parts, axis=1)


def _even_proj_kernel(x_ref, g_ref, wq_ref, wk_ref, wv_ref, wqr_ref, wkr_ref, wvr_ref, wgt_ref, cos_ref,
                      sin_ref, qa_ref, ka_ref, kab_ref, va_ref, vab_ref, qr_ref, kr_ref, vr_ref, gt_ref):
    h = _rms(x_ref[...], g_ref[...]).astype(BF16)
    qa_ref[...] = (_dot(h, wq_ref[...]) * (LOG2E * SB_HEAD_DIM ** -0.5)).astype(BF16)
    ka = _dot(h, wk_ref[...])
    ka_ref[...] = ka
    kab_ref[...] = ka.astype(BF16)
    va = _dot(h, wv_ref[...])
    va_ref[...] = va
    vab_ref[...] = va.astype(BF16)
    cos2 = cos_ref[...]
    sin2 = sin_ref[...]
    qr_ref[...] = _rope128(_dot(h, wqr_ref[...]), cos2, sin2).astype(BF16)
    kr_ref[...] = (_rope128(_dot(h, wkr_ref[...]), cos2, sin2) * (RET_QK_DIM ** -0.5)).astype(BF16)
    vr_ref[...] = _dot(h, wvr_ref[...]).astype(BF16)
    gt_ref[...] = _dot(h, wgt_ref[...])


def _even_proj_call(x, g, ws, cos2, sin2, tm):
    t, d = x.shape
    n_tab = cos2.shape[0] // tm
    row = lambda w: pl.BlockSpec((tm, w), lambda i: (i, 0))
    tab = pl.BlockSpec((tm, LANES), lambda i: (i % n_tab, 0))
    outs = [(SB_W, BF16), (SB_W, F32), (SB_W, BF16), (SB_W, F32), (SB_W, BF16),
            (RET_QK_W, BF16), (RET_QK_W, BF16), (RET_V_W, BF16), (RET_V_W, F32)]
    return pl.pallas_call(
        _even_proj_kernel,
        grid=(t // tm,),
        in_specs=[row(d), _full_spec(g.shape)] + [_full_spec(w.shape) for w in ws] + [tab, tab],
        out_specs=[row(w) for w, _ in outs],
        out_shape=[jax.ShapeDtypeStruct((t, w), dt) for w, dt in outs],
        compiler_params=_params(("parallel",)),
        name="even_proj",
    )(x, g, *ws, cos2, sin2)


def _sb_kernel(q_ref, k_ref, v_ref, negu_ref, o_ref, qh_ref, c_ref, acc_ref, *, qb, kb, q_pos0):
    i = pl.program_id(2)
    n_clear = (q_pos0 + i * qb) // kb
    sq = min(qb, kb)
    lane = lax.broadcasted_iota(jnp.int32, (qb, LANES), 1)
    first = lane < SB_HEAD_DIM
    q = q_ref[...]
    zero = jnp.zeros_like(q)
    qh_ref[0] = jnp.where(first, q, zero)
    qh_ref[1] = jnp.where(first, zero, q)
    c_ref[...] = jnp.zeros_like(c_ref)
    acc_ref[...] = jnp.zeros_like(acc_ref)
    below = lax.broadcasted_iota(jnp.int32, (sq, kb), 1) < lax.broadcasted_iota(jnp.int32, (sq, kb), 0)

    def block(start, r0, r1, mask):
        ks = k_ref[pl.ds(start, kb), :]
        vs = v_ref[pl.ds(start, kb), :]
        for hd in range(2):
            z = _dot_nt(qh_ref[hd, r0:r1], ks)
            sp = jnp.log2(1.0 + jnp.exp2(z))
            sp = jnp.where(z > SOFTPLUS2_LINEAR, z, sp)
            if mask is not None:
                sp = jnp.where(mask, sp, 0.0)
            tail = _dot(sp.astype(BF16), negu_ref[...])
            w = jnp.exp2(z + tail + c_ref[hd, r0:r1])
            if mask is not None:
                w = jnp.where(mask, w, 0.0)
            acc_ref[hd, r0:r1] += _dot(w.astype(BF16), vs)
            c_ref[hd, r0:r1] += tail[:, :1]

    for d in reversed(range(qb // sq)):
        start = pl.multiple_of((n_clear + d) * kb, kb)
        block(start, d * sq, (d + 1) * sq, below)
        if (d + 1) * sq < qb:
            block(start, (d + 1) * sq, qb, None)

    def alive():
        return (jnp.max(c_ref[...]) > SB_DEAD_LOG2).astype(jnp.int32)

    def cond(carry):
        t, live = carry
        return jnp.logical_and(t < n_clear, live > 0)

    def body(carry):
        t, _ = carry
        block(pl.multiple_of((n_clear - 1 - t) * kb, kb), 0, qb, None)
        return t + 1, alive()

    lax.while_loop(cond, body, (jnp.int32(0), alive()))
    o_ref[...] = jnp.where(first, acc_ref[0], acc_ref[1]).astype(o_ref.dtype)


def _sb_call(q, k, v, negu, qb, q_pos0):
    b, nq, w = q.shape
    nk = k.shape[1]
    kb = negu.shape[0]
    assert q_pos0 % kb == 0 and (qb % kb == 0 or nq == qb < kb) and nk % kb == 0 and nk >= q_pos0 + nq
    return pl.pallas_call(
        functools.partial(_sb_kernel, qb=qb, kb=kb, q_pos0=q_pos0),
        grid=(b, w // LANES, nq // qb),
        in_specs=[pl.BlockSpec((None, qb, LANES), lambda bi, p, i: (bi, i, p)),
                  pl.BlockSpec((None, nk, LANES), lambda bi, p, i: (bi, 0, p)),
                  pl.BlockSpec((None, nk, LANES), lambda bi, p, i: (bi, 0, p)),
                  _full_spec(negu.shape)],
        out_specs=pl.BlockSpec((None, qb, LANES), lambda bi, p, i: (bi, i, p)),
        out_shape=jax.ShapeDtypeStruct((b, nq, w), BF16),
        scratch_shapes=[pltpu.VMEM((2, qb, LANES), BF16), pltpu.VMEM((2, qb, 1), F32),
                        pltpu.VMEM((2, qb, LANES), F32)],
        compiler_params=_params(("parallel", "parallel", "parallel")),
        name="stick_breaking",
    )(q, k, v, negu)


def _ret_kernel(sdec_ref, q_ref, k_ref, v_ref, gt_ref, s0_ref, dec_ref, qdec_ref, kdec_ref,
                o_ref, sfin_ref, st_ref):
    c = pl.program_id(1)

    @pl.when(c == 0)
    def _():
        st_ref[...] = s0_ref[...]

    for h in range(RET_HEADS):
        q = q_ref[:, h * RET_QK_DIM:(h + 1) * RET_QK_DIM]
        k = k_ref[:, h * RET_QK_DIM:(h + 1) * RET_QK_DIM]
        v = v_ref[:, h * RET_V_DIM:(h + 1) * RET_V_DIM]
        state = st_ref[h]
        inner = _dot((_dot_nt(q, k) * dec_ref[h]).astype(BF16), v)
        cross = _dot(q, state.astype(BF16)) * qdec_ref[h]
        kw = (k.astype(F32) * kdec_ref[h]).astype(BF16)
        st_ref[h] = sdec_ref[h] * state + _dot_tn(kw, v)
        o = inner + cross
        o = o * lax.rsqrt(jnp.mean(o * o, axis=-1, keepdims=True) + RMS_EPS)
        gate = gt_ref[:, h * RET_V_DIM:(h + 1) * RET_V_DIM]
        o_ref[:, h * RET_V_DIM:(h + 1) * RET_V_DIM] = (o * (gate * jax.nn.sigmoid(gate))).astype(o_ref.dtype)

    @pl.when(c == pl.num_programs(1) - 1)
    def _():
        sfin_ref[...] = st_ref[...]


def _ret_call(q, k, v, gate, state0, cs):
    b, n, _ = q.shape
    log_gamma = jnp.log1p(-jnp.exp2(-5.0 - jnp.arange(RET_HEADS, dtype=F32)))
    pos = jnp.arange(cs, dtype=F32)
    diff = pos[:, None] - pos[None, :]
    dec = jnp.where(diff >= 0, jnp.exp(log_gamma[:, None, None] * jnp.maximum(diff, 0.0)), 0.0)
    qdec = jnp.exp(log_gamma[:, None] * (pos + 1.0))[..., None]
    kdec = jnp.exp(log_gamma[:, None] * (cs - 1.0 - pos))[..., None]
    sdec = jnp.exp(log_gamma * cs)
    seq = lambda w: pl.BlockSpec((None, cs, w), lambda bi, c: (bi, c, 0))
    st = pl.BlockSpec((None, RET_HEADS, RET_QK_DIM, RET_V_DIM), lambda bi, c: (bi, 0, 0, 0))
    return pl.pallas_call(
        _ret_kernel,
        grid=(b, n // cs),
        in_specs=[pl.BlockSpec(memory_space=pltpu.SMEM), seq(RET_QK_W), seq(RET_QK_W), seq(RET_V_W),
                  seq(RET_V_W), st, _full_spec(dec.shape), _full_spec(qdec.shape), _full_spec(kdec.shape)],
        out_specs=[seq(RET_V_W), st],
        out_shape=[jax.ShapeDtypeStruct((b, n, RET_V_W), BF16),
                   jax.ShapeDtypeStruct((b, RET_HEADS, RET_QK_DIM, RET_V_DIM), F32)],
        scratch_shapes=[pltpu.VMEM((RET_HEADS, RET_QK_DIM, RET_V_DIM), F32)],
        compiler_params=_params(("parallel", "arbitrary")),
        name="retention",
    )(sdec, q, k, v, gate, state0, dec, qdec, kdec)


def _head_norm_rope(t, bd, gain, cos4, sin4, first_half):
    parts = []
    for c in range(t.shape[1] // LANES):
        sq = t[:, c * LANES:(c + 1) * LANES]
        sq = sq * sq
        hi = sq.astype(BF16)
        lo = (sq - hi.astype(F32)).astype(BF16)
        parts.append(_dot(hi, bd) + _dot(lo, bd))
    ms = jnp.concatenate(parts, axis=1) * (1.0 / SWA_HEAD_DIM)
    y = t * lax.rsqrt(ms + RMS_EPS) * gain
    w = t.shape[1]
    quarter = SWA_HEAD_DIM // 2
    partner = jnp.where(first_half, pltpu.roll(y, w - quarter, axis=1), pltpu.roll(y, quarter, axis=1))
    return y * cos4 + partner * sin4


def _odd_proj_kernel(x_ref, g_ref, wq_ref, wk_ref, wv_ref, bd_ref, qg_ref, kg_ref, cos_ref, sin_ref,
                     q_ref, k_ref, v_ref):
    h = _rms(x_ref[...], g_ref[...]).astype(BF16)
    bd = bd_ref[...]
    cos2 = cos_ref[...]
    sin2 = sin_ref[...]
    tm = cos2.shape[0]
    nq = SWA_Q_W // LANES
    nk = SWA_KV_W // LANES
    lane_q = lax.broadcasted_iota(jnp.int32, (tm, SWA_Q_W), 1)
    lane_k = lax.broadcasted_iota(jnp.int32, (tm, SWA_KV_W), 1)
    half = SWA_HEAD_DIM // 2
    q = _head_norm_rope(_dot(h, wq_ref[...]), bd, qg_ref[...], jnp.concatenate([cos2] * nq, axis=1),
                        jnp.concatenate([sin2] * nq, axis=1), (lane_q % SWA_HEAD_DIM) < half)
    q_ref[...] = (q * (SWA_HEAD_DIM ** -0.5)).astype(BF16)
    k_ref[...] = _head_norm_rope(_dot(h, wk_ref[...]), bd, kg_ref[...], jnp.concatenate([cos2] * nk, axis=1),
                                 jnp.concatenate([sin2] * nk, axis=1), (lane_k % SWA_HEAD_DIM) < half)
    v_ref[...] = _dot(h, wv_ref[...])


def _odd_proj_call(x, g, wq, wk, wv, qg, kg, cos2, sin2, tm):
    t, d = x.shape
    n_tab = cos2.shape[0] // tm
    head = jnp.arange(LANES) // SWA_HEAD_DIM
    bd = (head[:, None] == head[None, :]).astype(BF16)
    row = lambda w: pl.BlockSpec((tm, w), lambda i: (i, 0))
    tab = pl.BlockSpec((tm, LANES), lambda i: (i % n_tab, 0))
    ins = [g, wq, wk, wv, bd, qg, kg]
    return pl.pallas_call(
        _odd_proj_kernel,
        grid=(t // tm,),
        in_specs=[row(d)] + [_full_spec(a.shape) for a in ins] + [tab, tab],
        out_specs=[row(SWA_Q_W), row(SWA_KV_W), row(SWA_KV_W)],
        out_shape=[jax.ShapeDtypeStruct((t, SWA_Q_W), BF16), jax.ShapeDtypeStruct((t, SWA_KV_W), F32),
                   jax.ShapeDtypeStruct((t, SWA_KV_W), F32)],
        compiler_params=_params(("parallel",)),
        name="odd_proj",
    )(x, *ins, cos2, sin2)


def _swa_kernel(sink_ref, q_ref, kp_ref, kc_ref, vp_ref, vc_ref, o_ref, *, rows, first_prev_valid):
    i = pl.program_id(1)
    nkeys = WINDOW + rows
    q = q_ref[...]
    k = jnp.concatenate([kp_ref[...], kc_ref[...]], axis=0).astype(BF16)
    v = jnp.concatenate([vp_ref[...], vc_ref[...]], axis=0).astype(BF16)
    qc = lax.broadcasted_iota(jnp.int32, (rows, nkeys), 0) // CHUNK
    kc = lax.broadcasted_iota(jnp.int32, (rows, nkeys), 1) // CHUNK
    nb = WINDOW // CHUNK
    valid = (kc >= qc) & (kc <= qc + nb)
    if not first_prev_valid:
        valid = valid & ((kc >= nb) | (i > 0))
    grp = SWA_HEADS // SWA_KV_HEADS
    outs = []
    for j in range(SWA_KV_HEADS):
        kj = k[:, j * SWA_HEAD_DIM:(j + 1) * SWA_HEAD_DIM]
        vj = v[:, j * SWA_HEAD_DIM:(j + 1) * SWA_HEAD_DIM]
        for gq in range(grp):
            hd = j * grp + gq
            s = _dot_nt(q[:, hd * SWA_HEAD_DIM:(hd + 1) * SWA_HEAD_DIM], kj)
            s = jnp.where(valid, s, -jnp.inf)
            sink = sink_ref[hd]
            m = jnp.maximum(jnp.max(s, axis=-1, keepdims=True), sink)
            p = jnp.exp(s - m)
            denom = jnp.sum(p, axis=-1, keepdims=True) + jnp.exp(sink - m)
            outs.append(_dot(p.astype(BF16), vj) / denom)
    o_ref[...] = jnp.concatenate(outs, axis=1).astype(o_ref.dtype)


def _swa_call(q, k_prev, k_cur, v_prev, v_cur, sinks, rows, same_array):
    b, n, _ = q.shape
    per = rows // WINDOW if same_array else 0
    cur = lambda w: pl.BlockSpec((None, rows, w), lambda bi, i: (bi, i, 0))
    prev = pl.BlockSpec((None, WINDOW, SWA_KV_W), lambda bi, i: (bi, jnp.maximum(i * per - 1, 0), 0))
    return pl.pallas_call(
        functools.partial(_swa_kernel, rows=rows, first_prev_valid=not same_array),
        grid=(b, n // rows),
        in_specs=[pl.BlockSpec(memory_space=pltpu.SMEM), cur(SWA_Q_W), prev, cur(SWA_KV_W), prev, cur(SWA_KV_W)],
        out_specs=cur(SWA_Q_W),
        out_shape=jax.ShapeDtypeStruct((b, n, SWA_Q_W), BF16),
        compiler_params=_params(("parallel", "parallel")),
        name="swa",
    )(sinks, q, k_prev, k_cur, v_prev, v_cur)


def _rope_tables(pos, head_dim):
    half = head_dim // 2
    inv = jnp.power(ROPE_THETA, -jnp.arange(half, dtype=F32) / half)
    ang = pos.astype(F32)[:, None] * inv[None, :]
    cos, sin = jnp.cos(ang), jnp.sin(ang)
    reps = LANES // head_dim
    return (jnp.tile(jnp.concatenate([cos, cos], axis=1), (1, reps)),
            jnp.tile(jnp.concatenate([-sin, sin], axis=1), (1, reps)))


def _prep_weights(norm_g, ffn_w_gate, ffn_w_up, ffn_w_down, even_w_in, even_w_out, odd_w_in, odd_w_out,
                  odd_q_norm, odd_k_norm):
    depth = norm_g.shape[0]
    prm = {"g": norm_g[:, :, None, :], "ffn": [], "even": [], "odd": []}
    for layer in range(depth):
        prm["ffn"].append([(ffn_w_gate[layer, s].astype(BF16), ffn_w_up[layer, s].astype(BF16),
                            ffn_w_down[layer, s].astype(BF16)) for s in range(2)])
    cuts = [0, SB_W, 2 * SB_W, 3 * SB_W, 3 * SB_W + RET_QK_W, 3 * SB_W + 2 * RET_QK_W,
            3 * SB_W + 2 * RET_QK_W + RET_V_W, 3 * SB_W + 2 * RET_QK_W + 2 * RET_V_W]
    for i in range(even_w_in.shape[0]):
        w_in = even_w_in[i].astype(BF16)
        w_out = even_w_out[i].astype(BF16)
        prm["even"].append(([w_in[:, cuts[s]:cuts[s + 1]] for s in range(7)], w_out[:SB_W], w_out[SB_W:]))
    for i in range(odd_w_in.shape[0]):
        w_in = odd_w_in[i].astype(BF16)
        prm["odd"].append((w_in[:, :SWA_Q_W], w_in[:, SWA_Q_W:SWA_Q_W + SWA_KV_W], w_in[:, SWA_Q_W + SWA_KV_W:],
                           odd_w_out[i].astype(BF16),
                           jnp.tile(odd_q_norm[i], SWA_HEADS)[None, :], jnp.tile(odd_k_norm[i], SWA_KV_HEADS)[None, :]))
    return prm


def _trunk(x, pos, prm, sinks, caches, tm, sb_qb, sb_kb, ret_cs, swa_rows):
    b, n, d = x.shape
    t = b * n
    xf = x.reshape(t, d)
    depth = len(prm["ffn"])
    tab_rows = n if n % tm == 0 else t
    tile_tab = lambda tb: tb if tab_rows == n else jnp.tile(tb, (b, 1))
    cos_r, sin_r = [tile_tab(tb) for tb in _rope_tables(pos, RET_QK_DIM)]
    cos_s, sin_s = [tile_tab(tb) for tb in _rope_tables(pos, SWA_HEAD_DIM)]
    idx = jnp.arange(sb_kb)
    negu = jnp.where(idx[:, None] >= idx[None, :], -1.0, 0.0).astype(BF16)
    sb_k, sb_v, ret, swa_k, swa_v = [], [], [], [], []
    mix = []
    for layer in range(depth):
        i = layer // 2
        g = prm["g"][layer]
        ffn = prm["ffn"][layer]
        xf = _ffn_call(xf, mix, g[0], *ffn[0], tm)
        if layer % 2 == 0:
            ws, wo_sb, wo_r = prm["even"][i]
            qa, ka, kab, va, vab, qr, kr, vr, gt = _even_proj_call(xf, g[1], ws, cos_r, sin_r, tm)
            sb_k.append(ka.reshape(b, n, SB_HEADS, SB_HEAD_DIM))
            sb_v.append(va.reshape(b, n, SB_HEADS, SB_HEAD_DIM))
            r3 = lambda a: a.reshape(b, n, a.shape[-1])
            if caches is None:
                k_all, v_all, q_pos0 = r3(kab), r3(vab), 0
                state0 = jnp.zeros((b, RET_HEADS, RET_QK_DIM, RET_V_DIM), F32)
            else:
                past = caches["sb_k"].shape[2]
                padded = -(-(past + n) // sb_kb) * sb_kb
                cat = lambda cache, new: jnp.pad(
                    jnp.concatenate([cache.reshape(b, past, SB_W).astype(BF16), r3(new)], axis=1),
                    ((0, 0), (0, padded - past - n), (0, 0)))
                k_all, v_all, q_pos0 = cat(caches["sb_k"][i], kab), cat(caches["sb_v"][i], vab), past
                state0 = caches["ret"][i]
            o_sb = _sb_call(r3(qa), k_all, v_all, negu, sb_qb, q_pos0)
            o_r, st = _ret_call(r3(qr), r3(kr), r3(vr), r3(gt), state0, ret_cs)
            ret.append(st)
            mix = [(o_sb.reshape(t, SB_W), wo_sb), (o_r.reshape(t, RET_V_W), wo_r)]
        else:
            wq, wk, wv, wo, qg, kg = prm["odd"][i]
            q, k, v = _odd_proj_call(xf, g[1], wq, wk, wv, qg, kg, cos_s, sin_s, tm)
            r3 = lambda a: a.reshape(b, n, a.shape[-1])
            k3, v3 = r3(k), r3(v)
            if caches is None:
                o = _swa_call(r3(q), k3, k3, v3, v3, sinks[i], swa_rows, True)
                keep = min(WINDOW, n)
                k_rows, v_rows = k3[:, n - keep:], v3[:, n - keep:]
            else:
                kc = caches["swa_k"][i].reshape(b, -1, SWA_KV_W)
                vc = caches["swa_v"][i].reshape(b, -1, SWA_KV_W)
                o = _swa_call(r3(q), kc, k3, vc, v3, sinks[i], swa_rows, False)
                k_rows, v_rows = k3, v3
            swa_k.append(k_rows.reshape(b, -1, SWA_KV_HEADS, SWA_HEAD_DIM))
            swa_v.append(v_rows.reshape(b, -1, SWA_KV_HEADS, SWA_HEAD_DIM))
            mix = [(o.reshape(t, SWA_Q_W), wo)]
        xf = _ffn_call(xf, mix, g[2], *ffn[1], tm)
        mix = []
    stack = lambda parts: parts[0][None] if len(parts) == 1 else jnp.stack(parts)
    return (xf.reshape(b, n, d), stack(sb_k), stack(sb_v), stack(ret), stack(swa_k), stack(swa_v))


def kernel(x_prompt, x_sample, cache_sb_k, cache_sb_v, state_ret, cache_swa_k, cache_swa_v, norm_g, ffn_w_gate,
           ffn_w_up, ffn_w_down, even_w_in, even_w_out, odd_w_in, odd_w_out, odd_q_norm, odd_k_norm, odd_sinks):
    prm = _prep_weights(norm_g, ffn_w_gate, ffn_w_up, ffn_w_down, even_w_in, even_w_out, odd_w_in, odd_w_out,
                        odd_q_norm, odd_k_norm)
    past = cache_sb_k.shape[2]
    n_p = x_prompt.shape[1]
    n_s = x_sample.shape[1]
    pos_prompt = jnp.arange(n_p, dtype=jnp.int32)
    pos_sample = past + jnp.arange(n_s, dtype=jnp.int32)
    y_p, sb_k_p, sb_v_p, ret_p, swa_k_p, swa_v_p = _trunk(
        x_prompt, pos_prompt, prm, odd_sinks, None, tm=512, sb_qb=1024, sb_kb=256, ret_cs=256, swa_rows=128)
    caches = {"sb_k": cache_sb_k, "sb_v": cache_sb_v, "ret": state_ret, "swa_k": cache_swa_k, "swa_v": cache_swa_v}
    y_s, sb_k_s, sb_v_s, ret_s, swa_k_s, swa_v_s = _trunk(
        x_sample, pos_sample, prm, odd_sinks, caches, tm=x_sample.shape[0] * n_s, sb_qb=n_s, sb_kb=256,
        ret_cs=n_s, swa_rows=n_s)
    return (y_p, y_s, sb_k_p, sb_v_p, ret_p, swa_k_p, swa_v_p, sb_k_s, sb_v_s, ret_s, swa_k_s, swa_v_s)
```

```python
import functools
import math

import jax
import jax.numpy as jnp
from jax import lax
from jax.experimental import pallas as pl
from jax.experimental.pallas import tpu as pltpu

F32 = jnp.float32
BF16 = jnp.bfloat16

RMS_EPS = 1e-6
ROPE_THETA = 10000.0
CHUNK = 64
SB_HEADS = 8
SB_HEAD_DIM = 64
RET_HEADS = 4
RET_QK_DIM = 128
RET_V_DIM = 256
SWA_HEADS = 16
SWA_KV_HEADS = 4
SWA_HEAD_DIM = 64
WINDOW = 128

SB_W = SB_HEADS * SB_HEAD_DIM
RET_QK_W = RET_HEADS * RET_QK_DIM
RET_V_W = RET_HEADS * RET_V_DIM
SWA_Q_W = SWA_HEADS * SWA_HEAD_DIM
SWA_KV_W = SWA_KV_HEADS * SWA_HEAD_DIM

LANES = 128
FF_CHUNK = 256
VMEM_LIMIT = 56 * 1024 * 1024
LOG2E = math.log2(math.e)
SOFTPLUS2_LINEAR = 40.0
SB_DEAD_LOG2 = -170.0


def _dot(a, b):
    return jnp.dot(a, b, preferred_element_type=F32)


def _dot_nt(a, b):
    return lax.dot_general(a, b, (((1,), (1,)), ((), ())), preferred_element_type=F32)


def _dot_tn(a, b):
    return lax.dot_general(a, b, (((0,), (0,)), ((), ())), preferred_element_type=F32)


def _rms(x, g):
    return x * lax.rsqrt(jnp.mean(x * x, axis=-1, keepdims=True) + RMS_EPS) * g


def _full_spec(shape):
    nd = len(shape)
    return pl.BlockSpec(shape, lambda *_: (0,) * nd, pipeline_mode=pl.Buffered(1))


def _params(sem):
    return pltpu.CompilerParams(dimension_semantics=sem, vmem_limit_bytes=VMEM_LIMIT)


def _ffn_kernel(*refs, n_mix):
    x_ref = refs[0]
    mix = refs[1:1 + 2 * n_mix]
    g_ref, wg_ref, wu_ref, wd_ref, o_ref, h_ref, act_ref = refs[1 + 2 * n_mix:]
    x = x_ref[...]
    for m in range(n_mix):
        x = x + _dot(mix[2 * m][...], mix[2 * m + 1][...])
    h_ref[...] = _rms(x, g_ref[...]).astype(BF16)
    for c in range(wg_ref.shape[1] // FF_CHUNK):
        cols = slice(c * FF_CHUNK, (c + 1) * FF_CHUNK)
        gate = _dot(h_ref[...], wg_ref[:, cols])
        up = _dot(h_ref[...], wu_ref[:, cols])
        act_ref[:, cols] = (gate * jax.nn.sigmoid(gate) * up).astype(BF16)
    o_ref[...] = x + 0.5 * _dot(act_ref[...], wd_ref[...])


def _ffn_call(x, mix, g, wg, wu, wd, tm):
    t, d = x.shape
    row = lambda w: pl.BlockSpec((tm, w), lambda i: (i, 0))
    in_specs = [row(d)]
    args = [x]
    for a, w in mix:
        in_specs += [row(a.shape[1]), _full_spec(w.shape)]
        args += [a, w]
    in_specs += [_full_spec(g.shape), _full_spec(wg.shape), _full_spec(wu.shape), _full_spec(wd.shape)]
    args += [g, wg, wu, wd]
    return pl.pallas_call(
        functools.partial(_ffn_kernel, n_mix=len(mix)),
        grid=(t // tm,),
        in_specs=in_specs,
        out_specs=row(d),
        out_shape=jax.ShapeDtypeStruct((t, d), F32),
        scratch_shapes=[pltpu.VMEM((tm, d), BF16), pltpu.VMEM((tm, wg.shape[1]), BF16)],
        compiler_params=_params(("parallel",)),
        name="ffn_mix%d" % len(mix),
    )(*args)


def _rope128(x, cos2, sin2):
    parts = []
    for hd in range(x.shape[1] // LANES):
        sl = x[:, hd * LANES:(hd + 1) * LANES]
        parts.append(sl * cos2 + pltpu.roll(sl, LANES // 2, axis=1) * sin2)
    return jnp.concatenate(parts, axis=1)


def _even_proj_kernel(x_ref, g_ref, wq_ref, wk_ref, wv_ref, wqr_ref, wkr_ref, wvr_ref, wgt_ref, cos_ref,
                      sin_ref, qa_ref, ka_ref, kab_ref, va_ref, vab_ref, qr_ref, kr_ref, vr_ref, gt_ref):
    h = _rms(x_ref[...], g_ref[...]).astype(BF16)
    qa_ref[...] = (_dot(h, wq_ref[...]) * (LOG2E * SB_HEAD_DIM ** -0.5)).astype(BF16)
    ka = _dot(h, wk_ref[...])
    ka_ref[...] = ka
    kab_ref[...] = ka.astype(BF16)
    va = _dot(h, wv_ref[...])
    va_ref[...] = va
    vab_ref[...] = va.astype(BF16)
    cos2 = cos_ref[...]
    sin2 = sin_ref[...]
    qr_ref[...] = _rope128(_dot(h, wqr_ref[...]), cos2, sin2).astype(BF16)
    kr_ref[...] = (_rope128(_dot(h, wkr_ref[...]), cos2, sin2) * (RET_QK_DIM ** -0.5)).astype(BF16)
    vr_ref[...] = _dot(h, wvr_ref[...]).astype(BF16)
    gt_ref[...] = _dot(h, wgt_ref[...])


def _even_proj_call(x, g, ws, cos2, sin2, tm):
    t, d = x.shape
    n_tab = cos2.shape[0] // tm
    row = lambda w: pl.BlockSpec((tm, w), lambda i: (i, 0))
    tab = pl.BlockSpec((tm, LANES), lambda i: (i % n_tab, 0))
    outs = [(SB_W, BF16), (SB_W, F32), (SB_W, BF16), (SB_W, F32), (SB_W, BF16),
            (RET_QK_W, BF16), (RET_QK_W, BF16), (RET_V_W, BF16), (RET_V_W, F32)]
    return pl.pallas_call(
        _even_proj_kernel,
        grid=(t // tm,),
        in_specs=[row(d), _full_spec(g.shape)] + [_full_spec(w.shape) for w in ws] + [tab, tab],
        out_specs=[row(w) for w, _ in outs],
        out_shape=[jax.ShapeDtypeStruct((t, w), dt) for w, dt in outs],
        compiler_params=_params(("parallel",)),
        name="even_proj",
    )(x, g, *ws, cos2, sin2)


def _sb_kernel(q_ref, k_ref, v_ref, negu_ref, o_ref, qh_ref, c_ref, acc_ref, *, qb, kb, q_pos0):
    i = pl.program_id(2)
    n_clear = (q_pos0 + i * qb) // kb
    sq = min(qb, kb)
    nsub = qb // sq
    rows = lambda s: slice(s * sq, (s + 1) * sq)
    kstart = lambda j: pl.multiple_of(j * kb, kb)
    q = q_ref[...]
    first = lax.broadcasted_iota(jnp.int32, (qb, LANES), 1) < SB_HEAD_DIM
    qh_ref[0] = jnp.where(first, q, jnp.zeros_like(q))
    qh_ref[1] = jnp.where(first, jnp.zeros_like(q), q)

    def scores(hd, rws, start):
        return _dot_nt(qh_ref[hd, rws], k_ref[pl.ds(start, kb), :])

    def softplus2(z):
        return jnp.where(z > SOFTPLUS2_LINEAR, z, jnp.log2(1.0 + jnp.exp2(z)))

    def v_heads(start):
        vs = v_ref[pl.ds(start, kb), :]
        vfirst = lax.broadcasted_iota(jnp.int32, (kb, LANES), 1) < SB_HEAD_DIM
        return jnp.concatenate([jnp.where(vfirst, vs, jnp.zeros_like(vs)),
                                jnp.where(vfirst, jnp.zeros_like(vs), vs)], axis=0)

    prev = lambda s: kstart(jnp.maximum(n_clear + s - 1, 0))
    row = lax.broadcasted_iota(jnp.int32, (qb, kb), 0)
    below = lax.broadcasted_iota(jnp.int32, (qb, kb), 1) < (row & (sq - 1))
    has_prev = jnp.broadcast_to(n_clear > 0, (sq, kb))

    def apply_masks(t):
        parts = [jnp.where(below, t[:qb], 0.0), jnp.where(has_prev, t[qb:qb + sq], 0.0)]
        return jnp.concatenate(parts + ([t[qb + sq:]] if nsub > 1 else []), axis=0)

    w_diag, w_prev = [], []
    for hd in range(2):
        z = jnp.concatenate([scores(hd, rows(s), kstart(n_clear + s)) for s in range(nsub)]
                            + [scores(hd, rows(s), prev(s)) for s in range(nsub)], axis=0)
        sp = apply_masks(softplus2(z))
        tail = _dot(sp.astype(BF16), negu_ref[...])
        c_diag = tail[:qb, :1]
        w = apply_masks(jnp.exp2(z + tail + jnp.concatenate([jnp.zeros_like(c_diag), c_diag], axis=0)))
        c_ref[hd] = c_diag + tail[qb:, :1]
        w = w.astype(BF16)
        w_diag.append(w[:qb])
        w_prev.append(w[qb:])
    w_diag = jnp.concatenate(w_diag, axis=1)
    w_prev = jnp.concatenate(w_prev, axis=1)
    for s in range(nsub):
        acc_ref[rows(s)] = (_dot(w_diag[rows(s)], v_heads(kstart(n_clear + s)))
                            + _dot(w_prev[rows(s)], v_heads(prev(s))))

    def block(start, r0, r1):
        ws = []
        for hd in range(2):
            z = scores(hd, slice(r0, r1), start)
            tail = _dot(softplus2(z).astype(BF16), negu_ref[...])
            ws.append(jnp.exp2(z + tail + c_ref[hd, r0:r1]).astype(BF16))
            c_ref[hd, r0:r1] += tail[:, :1]
        acc_ref[r0:r1] += _dot(jnp.concatenate(ws, axis=1), v_heads(start))

    def alive(r0, r1):
        return (jnp.max(c_ref[:, r0:r1]) > SB_DEAD_LOG2).astype(jnp.int32)

    def run_blocks(j_first, j_last, r0, r1):
        def cond(carry):
            j, live = carry
            return jnp.logical_and(j >= j_last, live > 0)

        def body(carry):
            j, _ = carry
            block(kstart(j), r0, r1)
            return j - 1, alive(r0, r1)

        lax.while_loop(cond, body, (j_first, alive(r0, r1)))

    @pl.when(alive(0, qb) > 0)
    def _():
        for s in range(2, nsub):
            run_blocks(n_clear + s - 2, n_clear, s * sq, (s + 1) * sq)
        if nsub > 1:
            run_blocks(n_clear - 1, jnp.maximum(n_clear - 1, 0), sq, qb)
        run_blocks(n_clear - 2, 0, 0, qb)

    o_ref[...] = acc_ref[...].astype(o_ref.dtype)


def _sb_call(q, k, v, negu, qb, q_pos0):
    b, nq, w = q.shape
    nk = k.shape[1]
    kb = negu.shape[0]
    assert q_pos0 % kb == 0 and (qb % kb == 0 or nq == qb < kb) and nk % kb == 0 and nk >= q_pos0 + nq
    return pl.pallas_call(
        functools.partial(_sb_kernel, qb=qb, kb=kb, q_pos0=q_pos0),
        grid=(b, w // LANES, nq // qb),
        in_specs=[pl.BlockSpec((None, qb, LANES), lambda bi, p, i: (bi, i, p)),
                  pl.BlockSpec((None, nk, LANES), lambda bi, p, i: (bi, 0, p)),
                  pl.BlockSpec((None, nk, LANES), lambda bi, p, i: (bi, 0, p)),
                  _full_spec(negu.shape)],
        out_specs=pl.BlockSpec((None, qb, LANES), lambda bi, p, i: (bi, i, p)),
        out_shape=jax.ShapeDtypeStruct((b, nq, w), BF16),
        scratch_shapes=[pltpu.VMEM((2, qb, LANES), BF16), pltpu.VMEM((2, qb, 1), F32),
                        pltpu.VMEM((qb, LANES), F32)],
        compiler_params=_params(("parallel", "parallel", "parallel")),
        name="stick_breaking",
    )(q, k, v, negu)


def _ret_kernel(sdec_ref, q_ref, k_ref, v_ref, gt_ref, s0_ref, dec_ref, qdec_ref, kdec_ref,
                o_ref, sfin_ref, st_ref):
    c = pl.program_id(1)

    @pl.when(c == 0)
    def _():
        st_ref[...] = s0_ref[...]

    for h in range(RET_HEADS):
        q = q_ref[:, h * RET_QK_DIM:(h + 1) * RET_QK_DIM]
        k = k_ref[:, h * RET_QK_DIM:(h + 1) * RET_QK_DIM]
        v = v_ref[:, h * RET_V_DIM:(h + 1) * RET_V_DIM]
        state = st_ref[h]
        inner = _dot((_dot_nt(q, k) * dec_ref[h]).astype(BF16), v)
        cross = _dot(q, state.astype(BF16)) * qdec_ref[h]
        kw = (k.astype(F32) * kdec_ref[h]).astype(BF16)
        st_ref[h] = sdec_ref[h] * state + _dot_tn(kw, v)
        o = inner + cross
        o = o * lax.rsqrt(jnp.mean(o * o, axis=-1, keepdims=True) + RMS_EPS)
        gate = gt_ref[:, h * RET_V_DIM:(h + 1) * RET_V_DIM]
        o_ref[:, h * RET_V_DIM:(h + 1) * RET_V_DIM] = (o * (gate * jax.nn.sigmoid(gate))).astype(o_ref.dtype)

    @pl.when(c == pl.num_programs(1) - 1)
    def _():
        sfin_ref[...] = st_ref[...]


def _ret_call(q, k, v, gate, state0, cs):
    b, n, _ = q.shape
    log_gamma = jnp.log1p(-jnp.exp2(-5.0 - jnp.arange(RET_HEADS, dtype=F32)))
    pos = jnp.arange(cs, dtype=F32)
    diff = pos[:, None] - pos[None, :]
    dec = jnp.where(diff >= 0, jnp.exp(log_gamma[:, None, None] * jnp.maximum(diff, 0.0)), 0.0)
    qdec = jnp.exp(log_gamma[:, None] * (pos + 1.0))[..., None]
    kdec = jnp.exp(log_gamma[:, None] * (cs - 1.0 - pos))[..., None]
    sdec = jnp.exp(log_gamma * cs)
    seq = lambda w: pl.BlockSpec((None, cs, w), lambda bi, c: (bi, c, 0))
    st = pl.BlockSpec((None, RET_HEADS, RET_QK_DIM, RET_V_DIM), lambda bi, c: (bi, 0, 0, 0))
    return pl.pallas_call(
        _ret_kernel,
        grid=(b, n // cs),
        in_specs=[pl.BlockSpec(memory_space=pltpu.SMEM), seq(RET_QK_W), seq(RET_QK_W), seq(RET_V_W),
                  seq(RET_V_W), st, _full_spec(dec.shape), _full_spec(qdec.shape), _full_spec(kdec.shape)],
        out_specs=[seq(RET_V_W), st],
        out_shape=[jax.ShapeDtypeStruct((b, n, RET_V_W), BF16),
                   jax.ShapeDtypeStruct((b, RET_HEADS, RET_QK_DIM, RET_V_DIM), F32)],
        scratch_shapes=[pltpu.VMEM((RET_HEADS, RET_QK_DIM, RET_V_DIM), F32)],
        compiler_params=_params(("parallel", "arbitrary")),
        name="retention",
    )(sdec, q, k, v, gate, state0, dec, qdec, kdec)


def _head_norm_rope(t, bd, gain, cos4, sin4, first_half):
    parts = []
    for c in range(t.shape[1] // LANES):
        sq = t[:, c * LANES:(c + 1) * LANES]
        sq = sq * sq
        hi = sq.astype(BF16)
        lo = (sq - hi.astype(F32)).astype(BF16)
        parts.append(_dot(hi, bd) + _dot(lo, bd))
    ms = jnp.concatenate(parts, axis=1) * (1.0 / SWA_HEAD_DIM)
    y = t * lax.rsqrt(ms + RMS_EPS) * gain
    w = t.shape[1]
    quarter = SWA_HEAD_DIM // 2
    partner = jnp.where(first_half, pltpu.roll(y, w - quarter, axis=1), pltpu.roll(y, quarter, axis=1))
    return y * cos4 + partner * sin4


def _odd_proj_kernel(x_ref, g_ref, wq_ref, wk_ref, wv_ref, bd_ref, qg_ref, kg_ref, cos_ref, sin_ref,
                     q_ref, k_ref, v_ref):
    h = _rms(x_ref[...], g_ref[...]).astype(BF16)
    bd = bd_ref[...]
    cos2 = cos_ref[...]
    sin2 = sin_ref[...]
    tm = cos2.shape[0]
    nq = SWA_Q_W // LANES
    nk = SWA_KV_W // LANES
    lane_q = lax.broadcasted_iota(jnp.int32, (tm, SWA_Q_W), 1)
    lane_k = lax.broadcasted_iota(jnp.int32, (tm, SWA_KV_W), 1)
    half = SWA_HEAD_DIM // 2
    q = _head_norm_rope(_dot(h, wq_ref[...]), bd, qg_ref[...], jnp.concatenate([cos2] * nq, axis=1),
                        jnp.concatenate([sin2] * nq, axis=1), (lane_q % SWA_HEAD_DIM) < half)
    q_ref[...] = (q * (SWA_HEAD_DIM ** -0.5)).astype(BF16)
    k_ref[...] = _head_norm_rope(_dot(h, wk_ref[...]), bd, kg_ref[...], jnp.concatenate([cos2] * nk, axis=1),
                                 jnp.concatenate([sin2] * nk, axis=1), (lane_k % SWA_HEAD_DIM) < half)
    v_ref[...] = _dot(h, wv_ref[...])


def _odd_proj_call(x, g, wq, wk, wv, qg, kg, cos2, sin2, tm):
    t, d = x.shape
    n_tab = cos2.shape[0] // tm
    head = jnp.arange(LANES) // SWA_HEAD_DIM
    bd = (head[:, None] == head[None, :]).astype(BF16)
    row = lambda w: pl.BlockSpec((tm, w), lambda i: (i, 0))
    tab = pl.BlockSpec((tm, LANES), lambda i: (i % n_tab, 0))
    ins = [g, wq, wk, wv, bd, qg, kg]
    return pl.pallas_call(
        _odd_proj_kernel,
        grid=(t // tm,),
        in_specs=[row(d)] + [_full_spec(a.shape) for a in ins] + [tab, tab],
        out_specs=[row(SWA_Q_W), row(SWA_KV_W), row(SWA_KV_W)],
        out_shape=[jax.ShapeDtypeStruct((t, SWA_Q_W), BF16), jax.ShapeDtypeStruct((t, SWA_KV_W), F32),
                   jax.ShapeDtypeStruct((t, SWA_KV_W), F32)],
        compiler_params=_params(("parallel",)),
        name="odd_proj",
    )(x, *ins, cos2, sin2)


def _swa_kernel(sink_ref, q_ref, kp_ref, kc_ref, vp_ref, vc_ref, o_ref, *, rows, first_prev_valid):
    i = pl.program_id(1)
    nkeys = WINDOW + rows
    q = q_ref[...]
    k = jnp.concatenate([kp_ref[...], kc_ref[...]], axis=0).astype(BF16)
    v = jnp.concatenate([vp_ref[...], vc_ref[...]], axis=0).astype(BF16)
    qc = lax.broadcasted_iota(jnp.int32, (rows, nkeys), 0) // CHUNK
    kc = lax.broadcasted_iota(jnp.int32, (rows, nkeys), 1) // CHUNK
    nb = WINDOW // CHUNK
    valid = (kc >= qc) & (kc <= qc + nb)
    if not first_prev_valid:
        valid = valid & ((kc >= nb) | (i > 0))
    grp = SWA_HEADS // SWA_KV_HEADS
    outs = []
    for j in range(SWA_KV_HEADS):
        kj = k[:, j * SWA_HEAD_DIM:(j + 1) * SWA_HEAD_DIM]
        vj = v[:, j * SWA_HEAD_DIM:(j + 1) * SWA_HEAD_DIM]
        for gq in range(grp):
            hd = j * grp + gq
            s = _dot_nt(q[:, hd * SWA_HEAD_DIM:(hd + 1) * SWA_HEAD_DIM], kj)
            s = jnp.where(valid, s, -jnp.inf)
            sink = sink_ref[hd]
            m = jnp.maximum(jnp.max(s, axis=-1, keepdims=True), sink)
            p = jnp.exp(s - m)
            denom = jnp.sum(p, axis=-1, keepdims=True) + jnp.exp(sink - m)
            outs.append(_dot(p.astype(BF16), vj) / denom)
    o_ref[...] = jnp.concatenate(outs, axis=1).astype(o_ref.dtype)


def _swa_call(q, k_prev, k_cur, v_prev, v_cur, sinks, rows, same_array):
    b, n, _ = q.shape
    per = rows // WINDOW if same_array else 0
    cur = lambda w: pl.BlockSpec((None, rows, w), lambda bi, i: (bi, i, 0))
    prev = pl.BlockSpec((None, WINDOW, SWA_KV_W), lambda bi, i: (bi, jnp.maximum(i * per - 1, 0), 0))
    return pl.pallas_call(
        functools.partial(_swa_kernel, rows=rows, first_prev_valid=not same_array),
        grid=(b, n // rows),
        in_specs=[pl.BlockSpec(memory_space=pltpu.SMEM), cur(SWA_Q_W), prev, cur(SWA_KV_W), prev, cur(SWA_KV_W)],
        out_specs=cur(SWA_Q_W),
        out_shape=jax.ShapeDtypeStruct((b, n, SWA_Q_W), BF16),
        compiler_params=_params(("parallel", "parallel")),
        name="swa",
    )(sinks, q, k_prev, k_cur, v_prev, v_cur)


def _rope_tables(pos, head_dim):
    half = head_dim // 2
    inv = jnp.power(ROPE_THETA, -jnp.arange(half, dtype=F32) / half)
    ang = pos.astype(F32)[:, None] * inv[None, :]
    cos, sin = jnp.cos(ang), jnp.sin(ang)
    reps = LANES // head_dim
    return (jnp.tile(jnp.concatenate([cos, cos], axis=1), (1, reps)),
            jnp.tile(jnp.concatenate([-sin, sin], axis=1), (1, reps)))


def _prep_weights(norm_g, ffn_w_gate, ffn_w_up, ffn_w_down, even_w_in, even_w_out, odd_w_in, odd_w_out,
                  odd_q_norm, odd_k_norm):
    depth = norm_g.shape[0]
    prm = {"g": norm_g[:, :, None, :], "ffn": [], "even": [], "odd": []}
    for layer in range(depth):
        prm["ffn"].append([(ffn_w_gate[layer, s].astype(BF16), ffn_w_up[layer, s].astype(BF16),
                            ffn_w_down[layer, s].astype(BF16)) for s in range(2)])
    cuts = [0, SB_W, 2 * SB_W, 3 * SB_W, 3 * SB_W + RET_QK_W, 3 * SB_W + 2 * RET_QK_W,
            3 * SB_W + 2 * RET_QK_W + RET_V_W, 3 * SB_W + 2 * RET_QK_W + 2 * RET_V_W]
    for i in range(even_w_in.shape[0]):
        w_in = even_w_in[i].astype(BF16)
        w_out = even_w_out[i].astype(BF16)
        prm["even"].append(([w_in[:, cuts[s]:cuts[s + 1]] for s in range(7)], w_out[:SB_W], w_out[SB_W:]))
    for i in range(odd_w_in.shape[0]):
        w_in = odd_w_in[i].astype(BF16)
        prm["odd"].append((w_in[:, :SWA_Q_W], w_in[:, SWA_Q_W:SWA_Q_W + SWA_KV_W], w_in[:, SWA_Q_W + SWA_KV_W:],
                           odd_w_out[i].astype(BF16),
                           jnp.tile(odd_q_norm[i], SWA_HEADS)[None, :], jnp.tile(odd_k_norm[i], SWA_KV_HEADS)[None, :]))
    return prm


def _trunk(x, pos, prm, sinks, caches, tm, sb_qb, sb_kb, ret_cs, swa_rows):
    b, n, d = x.shape
    t = b * n
    xf = x.reshape(t, d)
    depth = len(prm["ffn"])
    tab_rows = n if n % tm == 0 else t
    tile_tab = lambda tb: tb if tab_rows == n else jnp.tile(tb, (b, 1))
    cos_r, sin_r = [tile_tab(tb) for tb in _rope_tables(pos, RET_QK_DIM)]
    cos_s, sin_s = [tile_tab(tb) for tb in _rope_tables(pos, SWA_HEAD_DIM)]
    idx = jnp.arange(sb_kb)
    negu = jnp.where(idx[:, None] >= idx[None, :], -1.0, 0.0).astype(BF16)
    sb_k, sb_v, ret, swa_k, swa_v = [], [], [], [], []
    mix = []
    for layer in range(depth):
        i = layer // 2
        g = prm["g"][layer]
        ffn = prm["ffn"][layer]
        xf = _ffn_call(xf, mix, g[0], *ffn[0], tm)
        if layer % 2 == 0:
            ws, wo_sb, wo_r = prm["even"][i]
            qa, ka, kab, va, vab, qr, kr, vr, gt = _even_proj_call(xf, g[1], ws, cos_r, sin_r, tm)
            sb_k.append(ka.reshape(b, n, SB_HEADS, SB_HEAD_DIM))
            sb_v.append(va.reshape(b, n, SB_HEADS, SB_HEAD_DIM))
            r3 = lambda a: a.reshape(b, n, a.shape[-1])
            if caches is None:
                k_all, v_all, q_pos0 = r3(kab), r3(vab), 0
                state0 = jnp.zeros((b, RET_HEADS, RET_QK_DIM, RET_V_DIM), F32)
            else:
                past = caches["sb_k"].shape[2]
                padded = -(-(past + n) // sb_kb) * sb_kb
                cat = lambda cache, new: jnp.pad(
                    jnp.concatenate([cache.reshape(b, past, SB_W).astype(BF16), r3(new)], axis=1),
                    ((0, 0), (0, padded - past - n), (0, 0)))
                k_all, v_all, q_pos0 = cat(caches["sb_k"][i], kab), cat(caches["sb_v"][i], vab), past
                state0 = caches["ret"][i]
            o_sb = _sb_call(r3(qa), k_all, v_all, negu, sb_qb, q_pos0)
            o_r, st = _ret_call(r3(qr), r3(kr), r3(vr), r3(gt), state0, ret_cs)
            ret.append(st)
            mix = [(o_sb.reshape(t, SB_W), wo_sb), (o_r.reshape(t, RET_V_W), wo_r)]
        else:
            wq, wk, wv, wo, qg, kg = prm["odd"][i]
            q, k, v = _odd_proj_call(xf, g[1], wq, wk, wv, qg, kg, cos_s, sin_s, tm)
            r3 = lambda a: a.reshape(b, n, a.shape[-1])
            k3, v3 = r3(k), r3(v)
            if caches is None:
                o = _swa_call(r3(q), k3, k3, v3, v3, sinks[i], swa_rows, True)
                keep = min(WINDOW, n)
                k_rows, v_rows = k3[:, n - keep:], v3[:, n - keep:]
            else:
                kc = caches["swa_k"][i].reshape(b, -1, SWA_KV_W)
                vc = caches["swa_v"][i].reshape(b, -1, SWA_KV_W)
                o = _swa_call(r3(q), kc, k3, vc, v3, sinks[i], swa_rows, False)
                k_rows, v_rows = k3, v3
            swa_k.append(k_rows.reshape(b, -1, SWA_KV_HEADS, SWA_HEAD_DIM))
            swa_v.append(v_rows.reshape(b, -1, SWA_KV_HEADS, SWA_HEAD_DIM))
            mix = [(o.reshape(t, SWA_Q_W), wo)]
        xf = _ffn_call(xf, mix, g[2], *ffn[1], tm)
        mix = []
    stack = lambda parts: parts[0][None] if len(parts) == 1 else jnp.stack(parts)
    return (xf.reshape(b, n, d), stack(sb_k), stack(sb_v), stack(ret), stack(swa_k), stack(swa_v))


def kernel(x_prompt, x_sample, cache_sb_k, cache_sb_v, state_ret, cache_swa_k, cache_swa_v, norm_g, ffn_w_gate,
           ffn_w_up, ffn_w_down, even_w_in, even_w_out, odd_w_in, odd_w_out, odd_q_norm, odd_k_norm, odd_sinks):
    prm = _prep_weights(norm_g, ffn_w_gate, ffn_w_up, ffn_w_down, even_w_in, even_w_out, odd_w_in, odd_w_out,
                        odd_q_norm, odd_k_norm)
    past = cache_sb_k.shape[2]
    n_p = x_prompt.shape[1]
    n_s = x_sample.shape[1]
    pos_prompt = jnp.arange(n_p, dtype=jnp.int32)
    pos_sample = past + jnp.arange(n_s, dtype=jnp.int32)
    y_p, sb_k_p, sb_v_p, ret_p, swa_k_p, swa_v_p = _trunk(
        x_prompt, pos_prompt, prm, odd_sinks, None, tm=512, sb_qb=1024, sb_kb=256, ret_cs=256, swa_rows=128)
    caches = {"sb_k": cache_sb_k, "sb_v": cache_sb_v, "ret": state_ret, "swa_k": cache_swa_k, "swa_v": cache_swa_v}
    y_s, sb_k_s, sb_v_s, ret_s, swa_k_s, swa_v_s = _trunk(
        x_sample, pos_sample, prm, odd_sinks, caches, tm=x_sample.shape[0] * n_s, sb_qb=n_s, sb_kb=256,
        ret_cs=n_s, swa_rows=n_s)
    return (y_p, y_s, sb_k_p, sb_v_p, ret_p, swa_k_p, swa_v_p, sb_k_s, sb_v_s, ret_s, swa_k_s, swa_v_s)
```

```python
import functools
import math

import jax
import jax.numpy as jnp
from jax import lax
from jax.experimental import pallas as pl
from jax.experimental.pallas import tpu as pltpu

F32 = jnp.float32
BF16 = jnp.bfloat16

RMS_EPS = 1e-6
ROPE_THETA = 10000.0
CHUNK = 64
SB_HEADS = 8
SB_HEAD_DIM = 64
RET_HEADS = 4
RET_QK_DIM = 128
RET_V_DIM = 256
SWA_HEADS = 16
SWA_KV_HEADS = 4
SWA_HEAD_DIM = 64
WINDOW = 128

SB_W = SB_HEADS * SB_HEAD_DIM
RET_QK_W = RET_HEADS * RET_QK_DIM
RET_V_W = RET_HEADS * RET_V_DIM
SWA_Q_W = SWA_HEADS * SWA_HEAD_DIM
SWA_KV_W = SWA_KV_HEADS * SWA_HEAD_DIM

LANES = 128
FF_CHUNK = 256
VMEM_LIMIT = 56 * 1024 * 1024
LOG2E = math.log2(math.e)
SOFTPLUS2_LINEAR = 40.0
SWA_MASKED = -1e30
SB_DEAD_LOG2 = -170.0


def _dot(a, b):
    return jnp.dot(a, b, preferred_element_type=F32)


def _dot_nt(a, b):
    return lax.dot_general(a, b, (((1,), (1,)), ((), ())), preferred_element_type=F32)


def _dot_tn(a, b):
    return lax.dot_general(a, b, (((0,), (0,)), ((), ())), preferred_element_type=F32)


def _rms(x, g):
    return x * lax.rsqrt(jnp.mean(x * x, axis=-1, keepdims=True) + RMS_EPS) * g


def _full_spec(shape):
    nd = len(shape)
    return pl.BlockSpec(shape, lambda *_: (0,) * nd, pipeline_mode=pl.Buffered(1))


def _params(sem):
    return pltpu.CompilerParams(dimension_semantics=sem, vmem_limit_bytes=VMEM_LIMIT)


def _ffn_kernel(*refs, n_mix):
    x_ref = refs[0]
    mix = refs[1:1 + 2 * n_mix]
    g_ref, wg_ref, wu_ref, wd_ref, o_ref, h_ref, act_ref = refs[1 + 2 * n_mix:]
    x = x_ref[...]
    for m in range(n_mix):
        x = x + _dot(mix[2 * m][...], mix[2 * m + 1][...])
    h_ref[...] = _rms(x, g_ref[...]).astype(BF16)
    for c in range(wg_ref.shape[1] // FF_CHUNK):
        cols = slice(c * FF_CHUNK, (c + 1) * FF_CHUNK)
        gate = _dot(h_ref[...], wg_ref[:, cols])
        up = _dot(h_ref[...], wu_ref[:, cols])
        act_ref[:, cols] = (gate * jax.nn.sigmoid(gate) * up).astype(BF16)
    o_ref[...] = x + 0.5 * _dot(act_ref[...], wd_ref[...])


def _ffn_call(x, mix, g, wg, wu, wd, tm):
    t, d = x.shape
    row = lambda w: pl.BlockSpec((tm, w), lambda i: (i, 0))
    in_specs = [row(d)]
    args = [x]
    for a, w in mix:
        in_specs += [row(a.shape[1]), _full_spec(w.shape)]
        args += [a, w]
    in_specs += [_full_spec(g.shape), _full_spec(wg.shape), _full_spec(wu.shape), _full_spec(wd.shape)]
    args += [g, wg, wu, wd]
    return pl.pallas_call(
        functools.partial(_ffn_kernel, n_mix=len(mix)),
        grid=(t // tm,),
        in_specs=in_specs,
        out_specs=row(d),
        out_shape=jax.ShapeDtypeStruct((t, d), F32),
        scratch_shapes=[pltpu.VMEM((tm, d), BF16), pltpu.VMEM((tm, wg.shape[1]), BF16)],
        compiler_params=_params(("parallel",)),
        name="ffn_mix%d" % len(mix),
    )(*args)


def _rope128(x, cos2, sin2):
    parts = []
    for hd in range(x.shape[1] // LANES):
        sl = x[:, hd * LANES:(hd + 1) * LANES]
        parts.append(sl * cos2 + pltpu.roll(sl, LANES // 2, axis=1) * sin2)
    return jnp.concatenate(parts, axis=1)


def _even_proj_kernel(x_ref, g_ref, wq_ref, wk_ref, wv_ref, wqr_ref, wkr_ref, wvr_ref, wgt_ref, cos_ref,
                      sin_ref, qa_ref, ka_ref, kab_ref, va_ref, vab_ref, qr_ref, kr_ref, vr_ref, gt_ref):
    h = _rms(x_ref[...], g_ref[...]).astype(BF16)
    qa_ref[...] = (_dot(h, wq_ref[...]) * (LOG2E * SB_HEAD_DIM ** -0.5)).astype(BF16)
    ka = _dot(h, wk_ref[...])
    ka_ref[...] = ka
    kab_ref[...] = ka.astype(BF16)
    va = _dot(h, wv_ref[...])
    va_ref[...] = va
    vab_ref[...] = va.astype(BF16)
    cos2 = cos_ref[...]
    sin2 = sin_ref[...]
    qr_ref[...] = _rope128(_dot(h, wqr_ref[...]), cos2, sin2).astype(BF16)
    kr_ref[...] = (_rope128(_dot(h, wkr_ref[...]), cos2, sin2) * (RET_QK_DIM ** -0.5)).astype(BF16)
    vr_ref[...] = _dot(h, wvr_ref[...]).astype(BF16)
    gt_ref[...] = _dot(h, wgt_ref[...])


def _even_proj_call(x, g, ws, cos2, sin2, tm):
    t, d = x.shape
    n_tab = cos2.shape[0] // tm
    row = lambda w: pl.BlockSpec((tm, w), lambda i: (i, 0))
    tab = pl.BlockSpec((tm, LANES), lambda i: (i % n_tab, 0))
    outs = [(SB_W, BF16), (SB_W, F32), (SB_W, BF16), (SB_W, F32), (SB_W, BF16),
            (RET_QK_W, BF16), (RET_QK_W, BF16), (RET_V_W, BF16), (RET_V_W, F32)]
    return pl.pallas_call(
        _even_proj_kernel,
        grid=(t // tm,),
        in_specs=[row(d), _full_spec(g.shape)] + [_full_spec(w.shape) for w in ws] + [tab, tab],
        out_specs=[row(w) for w, _ in outs],
        out_shape=[jax.ShapeDtypeStruct((t, w), dt) for w, dt in outs],
        compiler_params=_params(("parallel",)),
        name="even_proj",
    )(x, g, *ws, cos2, sin2)


def _sb_kernel(q_ref, k_ref, v_ref, negu_ref, o_ref, qh_ref, c_ref, acc_ref, *, qb, kb, q_pos0):
    i = pl.program_id(2)
    n_clear = (q_pos0 + i * qb) // kb
    sq = min(qb, kb)
    nsub = qb // sq
    rows = lambda s: slice(s * sq, (s + 1) * sq)
    kstart = lambda j: pl.multiple_of(j * kb, kb)
    q = q_ref[...]
    first = lax.broadcasted_iota(jnp.int32, (qb, LANES), 1) < SB_HEAD_DIM
    qh_ref[0] = jnp.where(first, q, jnp.zeros_like(q))
    qh_ref[1] = jnp.where(first, jnp.zeros_like(q), q)

    def scores(hd, rws, start):
        return _dot_nt(qh_ref[hd, rws], k_ref[pl.ds(start, kb), :])

    def softplus2(z):
        return jnp.where(z > SOFTPLUS2_LINEAR, z, jnp.log2(1.0 + jnp.exp2(z)))

    def v_heads(start):
        vs = v_ref[pl.ds(start, kb), :]
        vfirst = lax.broadcasted_iota(jnp.int32, (kb, LANES), 1) < SB_HEAD_DIM
        return jnp.concatenate([jnp.where(vfirst, vs, jnp.zeros_like(vs)),
                                jnp.where(vfirst, jnp.zeros_like(vs), vs)], axis=0)

    prev = lambda s: kstart(jnp.maximum(n_clear + s - 1, 0))
    row = lax.broadcasted_iota(jnp.int32, (qb, kb), 0)
    below = lax.broadcasted_iota(jnp.int32, (qb, kb), 1) < (row & (sq - 1))
    has_prev = jnp.broadcast_to(n_clear > 0, (sq, kb))

    def apply_masks(t):
        parts = [jnp.where(below, t[:qb], 0.0), jnp.where(has_prev, t[qb:qb + sq], 0.0)]
        return jnp.concatenate(parts + ([t[qb + sq:]] if nsub > 1 else []), axis=0)

    w_diag, w_prev = [], []
    for hd in range(2):
        z = jnp.concatenate([scores(hd, rows(s), kstart(n_clear + s)) for s in range(nsub)]
                            + [scores(hd, rows(s), prev(s)) for s in range(nsub)], axis=0)
        sp = apply_masks(softplus2(z))
        tail = _dot(sp.astype(BF16), negu_ref[...])
        c_diag = tail[:qb, :1]
        w = apply_masks(jnp.exp2(z + tail + jnp.concatenate([jnp.zeros_like(c_diag), c_diag], axis=0)))
        c_ref[hd] = c_diag + tail[qb:, :1]
        w = w.astype(BF16)
        w_diag.append(w[:qb])
        w_prev.append(w[qb:])
    w_diag = jnp.concatenate(w_diag, axis=1)
    w_prev = jnp.concatenate(w_prev, axis=1)
    for s in range(nsub):
        acc_ref[rows(s)] = (_dot(w_diag[rows(s)], v_heads(kstart(n_clear + s)))
                            + _dot(w_prev[rows(s)], v_heads(prev(s))))

    def block(start, r0, r1):
        ws = []
        for hd in range(2):
            z = scores(hd, slice(r0, r1), start)
            tail = _dot(softplus2(z).astype(BF16), negu_ref[...])
            ws.append(jnp.exp2(z + tail + c_ref[hd, r0:r1]).astype(BF16))
            c_ref[hd, r0:r1] += tail[:, :1]
        acc_ref[r0:r1] += _dot(jnp.concatenate(ws, axis=1), v_heads(start))

    def alive(r0, r1):
        return (jnp.max(c_ref[:, r0:r1]) > SB_DEAD_LOG2).astype(jnp.int32)

    def run_blocks(j_first, j_last, r0, r1):
        def cond(carry):
            j, live = carry
            return jnp.logical_and(j >= j_last, live > 0)

        def body(carry):
            j, _ = carry
            block(kstart(j), r0, r1)
            return j - 1, alive(r0, r1)

        lax.while_loop(cond, body, (j_first, alive(r0, r1)))

    @pl.when(alive(0, qb) > 0)
    def _():
        for s in range(2, nsub):
            run_blocks(n_clear + s - 2, n_clear, s * sq, (s + 1) * sq)
        if nsub > 1:
            run_blocks(n_clear - 1, jnp.maximum(n_clear - 1, 0), sq, qb)
        run_blocks(n_clear - 2, 0, 0, qb)

    o_ref[...] = acc_ref[...].astype(o_ref.dtype)


def _sb_call(q, k, v, negu, qb, q_pos0):
    b, nq, w = q.shape
    nk = k.shape[1]
    kb = negu.shape[0]
    assert q_pos0 % kb == 0 and (qb % kb == 0 or nq == qb < kb) and nk % kb == 0 and nk >= q_pos0 + nq
    return pl.pallas_call(
        functools.partial(_sb_kernel, qb=qb, kb=kb, q_pos0=q_pos0),
        grid=(b, w // LANES, nq // qb),
        in_specs=[pl.BlockSpec((None, qb, LANES), lambda bi, p, i: (bi, i, p)),
                  pl.BlockSpec((None, nk, LANES), lambda bi, p, i: (bi, 0, p)),
                  pl.BlockSpec((None, nk, LANES), lambda bi, p, i: (bi, 0, p)),
                  _full_spec(negu.shape)],
        out_specs=pl.BlockSpec((None, qb, LANES), lambda bi, p, i: (bi, i, p)),
        out_shape=jax.ShapeDtypeStruct((b, nq, w), BF16),
        scratch_shapes=[pltpu.VMEM((2, qb, LANES), BF16), pltpu.VMEM((2, qb, 1), F32),
                        pltpu.VMEM((qb, LANES), F32)],
        compiler_params=_params(("parallel", "parallel", "parallel")),
        name="stick_breaking",
    )(q, k, v, negu)


def _ret_kernel(sdec_ref, q_ref, k_ref, v_ref, gt_ref, s0_ref, dec_ref, qdec_ref, kdec_ref,
                o_ref, sfin_ref, st_ref):
    c = pl.program_id(1)

    @pl.when(c == 0)
    def _():
        st_ref[...] = s0_ref[...]

    for h in range(RET_HEADS):
        q = q_ref[:, h * RET_QK_DIM:(h + 1) * RET_QK_DIM]
        k = k_ref[:, h * RET_QK_DIM:(h + 1) * RET_QK_DIM]
        v = v_ref[:, h * RET_V_DIM:(h + 1) * RET_V_DIM]
        state = st_ref[h]
        inner = _dot((_dot_nt(q, k) * dec_ref[h]).astype(BF16), v)
        cross = _dot(q, state.astype(BF16)) * qdec_ref[h]
        kw = (k.astype(F32) * kdec_ref[h]).astype(BF16)
        st_ref[h] = sdec_ref[h] * state + _dot_tn(kw, v)
        o = inner + cross
        o = o * lax.rsqrt(jnp.mean(o * o, axis=-1, keepdims=True) + RMS_EPS)
        gate = gt_ref[:, h * RET_V_DIM:(h + 1) * RET_V_DIM]
        o_ref[:, h * RET_V_DIM:(h + 1) * RET_V_DIM] = (o * (gate * jax.nn.sigmoid(gate))).astype(o_ref.dtype)

    @pl.when(c == pl.num_programs(1) - 1)
    def _():
        sfin_ref[...] = st_ref[...]


def _ret_call(q, k, v, gate, state0, cs):
    b, n, _ = q.shape
    log_gamma = jnp.log1p(-jnp.exp2(-5.0 - jnp.arange(RET_HEADS, dtype=F32)))
    pos = jnp.arange(cs, dtype=F32)
    diff = pos[:, None] - pos[None, :]
    dec = jnp.where(diff >= 0, jnp.exp(log_gamma[:, None, None] * jnp.maximum(diff, 0.0)), 0.0)
    qdec = jnp.exp(log_gamma[:, None] * (pos + 1.0))[..., None]
    kdec = jnp.exp(log_gamma[:, None] * (cs - 1.0 - pos))[..., None]
    sdec = jnp.exp(log_gamma * cs)
    seq = lambda w: pl.BlockSpec((None, cs, w), lambda bi, c: (bi, c, 0))
    st = pl.BlockSpec((None, RET_HEADS, RET_QK_DIM, RET_V_DIM), lambda bi, c: (bi, 0, 0, 0))
    return pl.pallas_call(
        _ret_kernel,
        grid=(b, n // cs),
        in_specs=[pl.BlockSpec(memory_space=pltpu.SMEM), seq(RET_QK_W), seq(RET_QK_W), seq(RET_V_W),
                  seq(RET_V_W), st, _full_spec(dec.shape), _full_spec(qdec.shape), _full_spec(kdec.shape)],
        out_specs=[seq(RET_V_W), st],
        out_shape=[jax.ShapeDtypeStruct((b, n, RET_V_W), BF16),
                   jax.ShapeDtypeStruct((b, RET_HEADS, RET_QK_DIM, RET_V_DIM), F32)],
        scratch_shapes=[pltpu.VMEM((RET_HEADS, RET_QK_DIM, RET_V_DIM), F32)],
        compiler_params=_params(("parallel", "arbitrary")),
        name="retention",
    )(sdec, q, k, v, gate, state0, dec, qdec, kdec)


def _head_norm_rope(t, bd, gain, cos4, sin4, first_half):
    parts = []
    for c in range(t.shape[1] // LANES):
        sq = t[:, c * LANES:(c + 1) * LANES]
        sq = sq * sq
        hi = sq.astype(BF16)
        lo = (sq - hi.astype(F32)).astype(BF16)
        parts.append(_dot(hi, bd) + _dot(lo, bd))
    ms = jnp.concatenate(parts, axis=1) * (1.0 / SWA_HEAD_DIM)
    y = t * lax.rsqrt(ms + RMS_EPS) * gain
    w = t.shape[1]
    quarter = SWA_HEAD_DIM // 2
    partner = jnp.where(first_half, pltpu.roll(y, w - quarter, axis=1), pltpu.roll(y, quarter, axis=1))
    return y * cos4 + partner * sin4


def _odd_proj_kernel(x_ref, g_ref, wq_ref, wk_ref, wv_ref, bd_ref, qg_ref, kg_ref, cos_ref, sin_ref,
                     q_ref, k_ref, v_ref):
    h = _rms(x_ref[...], g_ref[...]).astype(BF16)
    bd = bd_ref[...]
    cos2 = cos_ref[...]
    sin2 = sin_ref[...]
    tm = cos2.shape[0]
    nq = SWA_Q_W // LANES
    nk = SWA_KV_W // LANES
    lane_q = lax.broadcasted_iota(jnp.int32, (tm, SWA_Q_W), 1)
    lane_k = lax.broadcasted_iota(jnp.int32, (tm, SWA_KV_W), 1)
    half = SWA_HEAD_DIM // 2
    q = _head_norm_rope(_dot(h, wq_ref[...]), bd, qg_ref[...], jnp.concatenate([cos2] * nq, axis=1),
                        jnp.concatenate([sin2] * nq, axis=1), (lane_q % SWA_HEAD_DIM) < half)
    q_ref[...] = (q * (LOG2E * SWA_HEAD_DIM ** -0.5)).astype(BF16)
    k_ref[...] = _head_norm_rope(_dot(h, wk_ref[...]), bd, kg_ref[...], jnp.concatenate([cos2] * nk, axis=1),
                                 jnp.concatenate([sin2] * nk, axis=1), (lane_k % SWA_HEAD_DIM) < half)
    v_ref[...] = _dot(h, wv_ref[...])


def _odd_proj_call(x, g, wq, wk, wv, qg, kg, cos2, sin2, tm):
    t, d = x.shape
    n_tab = cos2.shape[0] // tm
    head = jnp.arange(LANES) // SWA_HEAD_DIM
    bd = (head[:, None] == head[None, :]).astype(BF16)
    row = lambda w: pl.BlockSpec((tm, w), lambda i: (i, 0))
    tab = pl.BlockSpec((tm, LANES), lambda i: (i % n_tab, 0))
    ins = [g, wq, wk, wv, bd, qg, kg]
    return pl.pallas_call(
        _odd_proj_kernel,
        grid=(t // tm,),
        in_specs=[row(d)] + [_full_spec(a.shape) for a in ins] + [tab, tab],
        out_specs=[row(SWA_Q_W), row(SWA_KV_W), row(SWA_KV_W)],
        out_shape=[jax.ShapeDtypeStruct((t, SWA_Q_W), BF16), jax.ShapeDtypeStruct((t, SWA_KV_W), F32),
                   jax.ShapeDtypeStruct((t, SWA_KV_W), F32)],
        compiler_params=_params(("parallel",)),
        name="odd_proj",
    )(x, *ins, cos2, sin2)


def _swa_kernel(sink_ref, q_ref, kp_ref, kc_ref, vp_ref, vc_ref, o_ref, *, rows, first_prev_valid):
    i = pl.program_id(1)
    nch = rows // CHUNK
    wk = WINDOW + CHUNK
    nk = WINDOW + rows
    grp = SWA_HEADS // SWA_KV_HEADS
    gw = grp * CHUNK
    key_low = lax.broadcasted_iota(jnp.int32, (nk, LANES), 1) < SWA_HEAD_DIM
    low = lax.broadcasted_iota(jnp.int32, (CHUNK, LANES), 1) < SWA_HEAD_DIM
    if not first_prev_valid:
        krow = lax.broadcasted_iota(jnp.int32, (nk, LANES), 0)
        klane = lax.broadcasted_iota(jnp.int32, (nk, LANES), 1)
        k_bias = jnp.where((krow < WINDOW) & (klane == 0) & (i == 0), SWA_MASKED, 0.0).astype(BF16)
        q_one = jnp.where(lax.broadcasted_iota(jnp.int32, (gw, LANES), 1) == 0, 1.0, 0.0).astype(BF16)

    s, vj_t = [], []
    for j in range(SWA_KV_HEADS):
        pair, odd = divmod(j, 2)
        lanes = slice(pair * LANES, (pair + 1) * LANES)
        k_pair = jnp.concatenate([kp_ref[:, lanes], kc_ref[:, lanes]], axis=0)
        own = jnp.where(key_low, 0.0, k_pair) if odd else jnp.where(key_low, k_pair, 0.0)
        k_both = (own + pltpu.roll(own, SWA_HEAD_DIM, axis=1)).astype(BF16)
        v_t = jnp.concatenate([vp_ref[:, lanes], vc_ref[:, lanes]], axis=0).T
        vj_t.append(v_t[odd * SWA_HEAD_DIM:(odd + 1) * SWA_HEAD_DIM].astype(BF16))
        if not first_prev_valid:
            k_both = jnp.concatenate([k_both, k_bias], axis=1)
        for c in range(nch):
            q_four = []
            for qp in (2 * j, 2 * j + 1):
                q_pair = q_ref[c * CHUNK:(c + 1) * CHUNK, qp * LANES:(qp + 1) * LANES]
                zero = jnp.zeros_like(q_pair)
                q_four += [jnp.where(low, q_pair, zero), jnp.where(low, zero, q_pair)]
            q_four = jnp.concatenate(q_four, axis=0)
            if not first_prev_valid:
                q_four = jnp.concatenate([q_four, q_one], axis=1)
            s.append(_dot_nt(k_both[c * CHUNK:c * CHUNK + wk], q_four))
    s = jnp.concatenate(s, axis=1)
    sink = jnp.concatenate([jnp.full((1, CHUNK), sink_ref[j * grp + g] * LOG2E, F32)
                            for j in range(SWA_KV_HEADS) for _ in range(nch) for g in range(grp)], axis=1)
    m = jnp.maximum(jnp.max(s, axis=0, keepdims=True), sink)
    p = jnp.exp2(s - m)
    inv = 1.0 / (jnp.sum(p, axis=0, keepdims=True) + jnp.exp2(sink - m))
    p = p.astype(BF16)
    o_t = []
    for j in range(SWA_KV_HEADS):
        for c in range(nch):
            cols = slice((j * nch + c) * gw, (j * nch + c + 1) * gw)
            o_t.append(_dot(vj_t[j][:, c * CHUNK:c * CHUNK + wk], p[:, cols]) * inv[:, cols])
    for c0 in range(0, nch, 2):
        pieces = []
        for j in range(SWA_KV_HEADS):
            for t in range(grp // 2):
                tiles = [o_t[j * nch + c][:, t * LANES:(t + 1) * LANES] for c in range(c0, min(c0 + 2, nch))]
                if len(tiles) == 2:
                    pieces += [jnp.where(low, tiles[0], pltpu.roll(tiles[1], CHUNK, axis=1)),
                               jnp.where(low, pltpu.roll(tiles[0], CHUNK, axis=1), tiles[1])]
                else:
                    pieces += [tiles[0], pltpu.roll(tiles[0], CHUNK, axis=1)]
        out = jnp.concatenate(pieces, axis=0).T
        r1 = min((c0 + 2) * CHUNK, rows)
        o_ref[c0 * CHUNK:r1] = out[:r1 - c0 * CHUNK].astype(o_ref.dtype)


def _swa_call(q, k_prev, k_cur, v_prev, v_cur, sinks, rows, same_array):
    b, n, _ = q.shape
    per = rows // WINDOW if same_array else 0
    cur = lambda w: pl.BlockSpec((None, rows, w), lambda bi, i: (bi, i, 0))
    prev = pl.BlockSpec((None, WINDOW, SWA_KV_W), lambda bi, i: (bi, jnp.maximum(i * per - 1, 0), 0))
    return pl.pallas_call(
        functools.partial(_swa_kernel, rows=rows, first_prev_valid=not same_array),
        grid=(b, n // rows),
        in_specs=[pl.BlockSpec(memory_space=pltpu.SMEM), cur(SWA_Q_W), prev, cur(SWA_KV_W), prev, cur(SWA_KV_W)],
        out_specs=cur(SWA_Q_W),
        out_shape=jax.ShapeDtypeStruct((b, n, SWA_Q_W), BF16),
        compiler_params=_params(("parallel", "parallel")),
        name="swa",
    )(sinks, q, k_prev, k_cur, v_prev, v_cur)


def _rope_tables(pos, head_dim):
    half = head_dim // 2
    inv = jnp.power(ROPE_THETA, -jnp.arange(half, dtype=F32) / half)
    ang = pos.astype(F32)[:, None] * inv[None, :]
    cos, sin = jnp.cos(ang), jnp.sin(ang)
    reps = LANES // head_dim
    return (jnp.tile(jnp.concatenate([cos, cos], axis=1), (1, reps)),
            jnp.tile(jnp.concatenate([-sin, sin], axis=1), (1, reps)))


def _prep_weights(norm_g, ffn_w_gate, ffn_w_up, ffn_w_down, even_w_in, even_w_out, odd_w_in, odd_w_out,
                  odd_q_norm, odd_k_norm):
    depth = norm_g.shape[0]
    prm = {"g": norm_g[:, :, None, :], "ffn": [], "even": [], "odd": []}
    for layer in range(depth):
        prm["ffn"].append([(ffn_w_gate[layer, s].astype(BF16), ffn_w_up[layer, s].astype(BF16),
                            ffn_w_down[layer, s].astype(BF16)) for s in range(2)])
    cuts = [0, SB_W, 2 * SB_W, 3 * SB_W, 3 * SB_W + RET_QK_W, 3 * SB_W + 2 * RET_QK_W,
            3 * SB_W + 2 * RET_QK_W + RET_V_W, 3 * SB_W + 2 * RET_QK_W + 2 * RET_V_W]
    for i in range(even_w_in.shape[0]):
        w_in = even_w_in[i].astype(BF16)
        w_out = even_w_out[i].astype(BF16)
        prm["even"].append(([w_in[:, cuts[s]:cuts[s + 1]] for s in range(7)], w_out[:SB_W], w_out[SB_W:]))
    for i in range(odd_w_in.shape[0]):
        w_in = odd_w_in[i].astype(BF16)
        prm["odd"].append((w_in[:, :SWA_Q_W], w_in[:, SWA_Q_W:SWA_Q_W + SWA_KV_W], w_in[:, SWA_Q_W + SWA_KV_W:],
                           odd_w_out[i].astype(BF16),
                           jnp.tile(odd_q_norm[i], SWA_HEADS)[None, :], jnp.tile(odd_k_norm[i], SWA_KV_HEADS)[None, :]))
    return prm


def _trunk(x, pos, prm, sinks, caches, tm, sb_qb, sb_kb, ret_cs, swa_rows):
    b, n, d = x.shape
    t = b * n
    xf = x.reshape(t, d)
    depth = len(prm["ffn"])
    tab_rows = n if n % tm == 0 else t
    tile_tab = lambda tb: tb if tab_rows == n else jnp.tile(tb, (b, 1))
    cos_r, sin_r = [tile_tab(tb) for tb in _rope_tables(pos, RET_QK_DIM)]
    cos_s, sin_s = [tile_tab(tb) for tb in _rope_tables(pos, SWA_HEAD_DIM)]
    idx = jnp.arange(sb_kb)
    negu = jnp.where(idx[:, None] >= idx[None, :], -1.0, 0.0).astype(BF16)
    sb_k, sb_v, ret, swa_k, swa_v = [], [], [], [], []
    mix = []
    for layer in range(depth):
        i = layer // 2
        g = prm["g"][layer]
        ffn = prm["ffn"][layer]
        xf = _ffn_call(xf, mix, g[0], *ffn[0], tm)
        if layer % 2 == 0:
            ws, wo_sb, wo_r = prm["even"][i]
            qa, ka, kab, va, vab, qr, kr, vr, gt = _even_proj_call(xf, g[1], ws, cos_r, sin_r, tm)
            sb_k.append(ka.reshape(b, n, SB_HEADS, SB_HEAD_DIM))
            sb_v.append(va.reshape(b, n, SB_HEADS, SB_HEAD_DIM))
            r3 = lambda a: a.reshape(b, n, a.shape[-1])
            if caches is None:
                k_all, v_all, q_pos0 = r3(kab), r3(vab), 0
                state0 = jnp.zeros((b, RET_HEADS, RET_QK_DIM, RET_V_DIM), F32)
            else:
                past = caches["sb_k"].shape[2]
                padded = -(-(past + n) // sb_kb) * sb_kb
                cat = lambda cache, new: jnp.pad(
                    jnp.concatenate([cache.reshape(b, past, SB_W).astype(BF16), r3(new)], axis=1),
                    ((0, 0), (0, padded - past - n), (0, 0)))
                k_all, v_all, q_pos0 = cat(caches["sb_k"][i], kab), cat(caches["sb_v"][i], vab), past
                state0 = caches["ret"][i]
            o_sb = _sb_call(r3(qa), k_all, v_all, negu, sb_qb, q_pos0)
            o_r, st = _ret_call(r3(qr), r3(kr), r3(vr), r3(gt), state0, ret_cs)
            ret.append(st)
            mix = [(o_sb.reshape(t, SB_W), wo_sb), (o_r.reshape(t, RET_V_W), wo_r)]
        else:
            wq, wk, wv, wo, qg, kg = prm["odd"][i]
            q, k, v = _odd_proj_call(xf, g[1], wq, wk, wv, qg, kg, cos_s, sin_s, tm)
            r3 = lambda a: a.reshape(b, n, a.shape[-1])
            k3, v3 = r3(k), r3(v)
            if caches is None:
                o = _swa_call(r3(q), k3, k3, v3, v3, sinks[i], swa_rows, True)
                keep = min(WINDOW, n)
                k_rows, v_rows = k3[:, n - keep:], v3[:, n - keep:]
            else:
                kc = caches["swa_k"][i].reshape(b, -1, SWA_KV_W)
                vc = caches["swa_v"][i].reshape(b, -1, SWA_KV_W)
                o = _swa_call(r3(q), kc, k3, vc, v3, sinks[i], swa_rows, False)
                k_rows, v_rows = k3, v3
            swa_k.append(k_rows.reshape(b, -1, SWA_KV_HEADS, SWA_HEAD_DIM))
            swa_v.append(v_rows.reshape(b, -1, SWA_KV_HEADS, SWA_HEAD_DIM))
            mix = [(o.reshape(t, SWA_Q_W), wo)]
        xf = _ffn_call(xf, mix, g[2], *ffn[1], tm)
        mix = []
    stack = lambda parts: parts[0][None] if len(parts) == 1 else jnp.stack(parts)
    return (xf.reshape(b, n, d), stack(sb_k), stack(sb_v), stack(ret), stack(swa_k), stack(swa_v))


def kernel(x_prompt, x_sample, cache_sb_k, cache_sb_v, state_ret, cache_swa_k, cache_swa_v, norm_g, ffn_w_gate,
           ffn_w_up, ffn_w_down, even_w_in, even_w_out, odd_w_in, odd_w_out, odd_q_norm, odd_k_norm, odd_sinks):
    prm = _prep_weights(norm_g, ffn_w_gate, ffn_w_up, ffn_w_down, even_w_in, even_w_out, odd_w_in, odd_w_out,
                        odd_q_norm, odd_k_norm)
    past = cache_sb_k.shape[2]
    n_p = x_prompt.shape[1]
    n_s = x_sample.shape[1]
    pos_prompt = jnp.arange(n_p, dtype=jnp.int32)
    pos_sample = past + jnp.arange(n_s, dtype=jnp.int32)
    y_p, sb_k_p, sb_v_p, ret_p, swa_k_p, swa_v_p = _trunk(
        x_prompt, pos_prompt, prm, odd_sinks, None, tm=512, sb_qb=1024, sb_kb=256, ret_cs=256, swa_rows=512)
    caches = {"sb_k": cache_sb_k, "sb_v": cache_sb_v, "ret": state_ret, "swa_k": cache_swa_k, "swa_v": cache_swa_v}
    y_s, sb_k_s, sb_v_s, ret_s, swa_k_s, swa_v_s = _trunk(
        x_sample, pos_sample, prm, odd_sinks, caches, tm=x_sample.shape[0] * n_s, sb_qb=n_s, sb_kb=256,
        ret_cs=n_s, swa_rows=n_s)
    return (y_p, y_s, sb_k_p, sb_v_p, ret_p, swa_k_p, swa_v_p, sb_k_s, sb_v_s, ret_s, swa_k_s, swa_v_s)
```

```python
import functools
import math

import jax
import jax.numpy as jnp
from jax import lax
from jax.experimental import pallas as pl
from jax.experimental.pallas import tpu as pltpu

F32 = jnp.float32
BF16 = jnp.bfloat16

RMS_EPS = 1e-6
ROPE_THETA = 10000.0
CHUNK = 64
SB_HEADS = 8
SB_HEAD_DIM = 64
RET_HEADS = 4
RET_QK_DIM = 128
RET_V_DIM = 256
SWA_HEADS = 16
SWA_KV_HEADS = 4
SWA_HEAD_DIM = 64
WINDOW = 128

SB_W = SB_HEADS * SB_HEAD_DIM
RET_QK_W = RET_HEADS * RET_QK_DIM
RET_V_W = RET_HEADS * RET_V_DIM
SWA_Q_W = SWA_HEADS * SWA_HEAD_DIM
SWA_KV_W = SWA_KV_HEADS * SWA_HEAD_DIM

LANES = 128
FF_CHUNK = 256
VMEM_LIMIT = 56 * 1024 * 1024
LOG2E = math.log2(math.e)
SOFTPLUS2_LINEAR = 40.0
SWA_MASKED = -1e30
SB_DEAD_LOG2 = -170.0


def _dot(a, b):
    return jnp.dot(a, b, preferred_element_type=F32)


def _dot_nt(a, b):
    return lax.dot_general(a, b, (((1,), (1,)), ((), ())), preferred_element_type=F32)


def _dot_tn(a, b):
    return lax.dot_general(a, b, (((0,), (0,)), ((), ())), preferred_element_type=F32)


def _rms(x, g):
    return x * lax.rsqrt(jnp.mean(x * x, axis=-1, keepdims=True) + RMS_EPS) * g


def _full_spec(shape):
    nd = len(shape)
    return pl.BlockSpec(shape, lambda *_: (0,) * nd, pipeline_mode=pl.Buffered(1))


def _params(sem):
    return pltpu.CompilerParams(dimension_semantics=sem, vmem_limit_bytes=VMEM_LIMIT)


def _ffn_kernel(*refs, n_mix):
    x_ref = refs[0]
    mix = refs[1:1 + 2 * n_mix]
    g_ref, wg_ref, wu_ref, wd_ref, o_ref, h_ref, act_ref = refs[1 + 2 * n_mix:]
    x = x_ref[...]
    for m in range(n_mix):
        x = x + _dot(mix[2 * m][...], mix[2 * m + 1][...])
    h_ref[...] = _rms(x, g_ref[...]).astype(BF16)
    for c in range(wg_ref.shape[1] // FF_CHUNK):
        cols = slice(c * FF_CHUNK, (c + 1) * FF_CHUNK)
        gate = _dot(h_ref[...], wg_ref[:, cols])
        up = _dot(h_ref[...], wu_ref[:, cols])
        act_ref[:, cols] = (gate * jax.nn.sigmoid(gate) * up).astype(BF16)
    o_ref[...] = x + 0.5 * _dot(act_ref[...], wd_ref[...])


def _ffn_call(x, mix, g, wg, wu, wd, tm):
    t, d = x.shape
    row = lambda w: pl.BlockSpec((tm, w), lambda i: (i, 0))
    in_specs = [row(d)]
    args = [x]
    for a, w in mix:
        in_specs += [row(a.shape[1]), _full_spec(w.shape)]
        args += [a, w]
    in_specs += [_full_spec(g.shape), _full_spec(wg.shape), _full_spec(wu.shape), _full_spec(wd.shape)]
    args += [g, wg, wu, wd]
    return pl.pallas_call(
        functools.partial(_ffn_kernel, n_mix=len(mix)),
        grid=(t // tm,),
        in_specs=in_specs,
        out_specs=row(d),
        out_shape=jax.ShapeDtypeStruct((t, d), F32),
        scratch_shapes=[pltpu.VMEM((tm, d), BF16), pltpu.VMEM((tm, wg.shape[1]), BF16)],
        compiler_params=_params(("parallel",)),
        name="ffn_mix%d" % len(mix),
    )(*args)


def _rope128(x, cos2, sin2):
    parts = []
    for hd in range(x.shape[1] // LANES):
        sl = x[:, hd * LANES:(hd + 1) * LANES]
        parts.append(sl * cos2 + pltpu.roll(sl, LANES // 2, axis=1) * sin2)
    return jnp.concatenate(parts, axis=1)


def _even_proj_kernel(x_ref, g_ref, wq_ref, wk_ref, wv_ref, wqr_ref, wkr_ref, wvr_ref, wgt_ref, cos_ref,
                      sin_ref, qa_ref, ka_ref, kab_ref, va_ref, vab_ref, qr_ref, kr_ref, vr_ref, gt_ref):
    h = _rms(x_ref[...], g_ref[...]).astype(BF16)
    qa_ref[...] = (_dot(h, wq_ref[...]) * (LOG2E * SB_HEAD_DIM ** -0.5)).astype(BF16)
    ka = _dot(h, wk_ref[...])
    ka_ref[...] = ka.reshape(ka_ref.shape)
    kab_ref[...] = ka.astype(BF16)
    va = _dot(h, wv_ref[...])
    va_ref[...] = va.reshape(va_ref.shape)
    vab_ref[...] = va.astype(BF16)
    cos2 = cos_ref[...]
    sin2 = sin_ref[...]
    qr_ref[...] = _rope128(_dot(h, wqr_ref[...]), cos2, sin2).astype(BF16)
    kr_ref[...] = (_rope128(_dot(h, wkr_ref[...]), cos2, sin2) * (RET_QK_DIM ** -0.5)).astype(BF16)
    vr_ref[...] = _dot(h, wvr_ref[...]).astype(BF16)
    gt_ref[...] = _dot(h, wgt_ref[...])


def _even_proj_call(x, g, ws, cos2, sin2, tm):
    t, d = x.shape
    n_tab = cos2.shape[0] // tm
    row = lambda w: pl.BlockSpec((tm, w), lambda i: (i, 0))
    tab = pl.BlockSpec((tm, LANES), lambda i: (i % n_tab, 0))
    per_head = (SB_HEADS, SB_HEAD_DIM)
    outs = [((SB_W,), BF16), (per_head, F32), ((SB_W,), BF16), (per_head, F32), ((SB_W,), BF16),
            ((RET_QK_W,), BF16), ((RET_QK_W,), BF16), ((RET_V_W,), BF16), ((RET_V_W,), F32)]
    return pl.pallas_call(
        _even_proj_kernel,
        grid=(t // tm,),
        in_specs=[row(d), _full_spec(g.shape)] + [_full_spec(w.shape) for w in ws] + [tab, tab],
        out_specs=[pl.BlockSpec((tm,) + tail, lambda i, nd=len(tail): (i,) + (0,) * nd) for tail, _ in outs],
        out_shape=[jax.ShapeDtypeStruct((t,) + tail, dt) for tail, dt in outs],
        compiler_params=_params(("parallel",)),
        name="even_proj",
    )(x, g, *ws, cos2, sin2)


def _sb_kernel(q_ref, k_ref, v_ref, negu_ref, o_ref, qh_ref, c_ref, acc_ref, *, qb, kb, q_pos0):
    i = pl.program_id(2)
    n_clear = (q_pos0 + i * qb) // kb
    sq = min(qb, kb)
    nsub = qb // sq
    rows = lambda s: slice(s * sq, (s + 1) * sq)
    kstart = lambda j: pl.multiple_of(j * kb, kb)
    q = q_ref[...]
    first = lax.broadcasted_iota(jnp.int32, (qb, LANES), 1) < SB_HEAD_DIM
    qh_ref[0] = jnp.where(first, q, jnp.zeros_like(q))
    qh_ref[1] = jnp.where(first, jnp.zeros_like(q), q)

    def scores(hd, rws, start):
        return _dot_nt(qh_ref[hd, rws], k_ref[pl.ds(start, kb), :])

    def softplus2(z):
        return jnp.where(z > SOFTPLUS2_LINEAR, z, jnp.log2(1.0 + jnp.exp2(z)))

    def v_heads(start):
        vs = v_ref[pl.ds(start, kb), :]
        vfirst = lax.broadcasted_iota(jnp.int32, (kb, LANES), 1) < SB_HEAD_DIM
        return jnp.concatenate([jnp.where(vfirst, vs, jnp.zeros_like(vs)),
                                jnp.where(vfirst, jnp.zeros_like(vs), vs)], axis=0)

    prev = lambda s: kstart(jnp.maximum(n_clear + s - 1, 0))
    row = lax.broadcasted_iota(jnp.int32, (qb, kb), 0)
    below = lax.broadcasted_iota(jnp.int32, (qb, kb), 1) < (row & (sq - 1))
    has_prev = jnp.broadcast_to(n_clear > 0, (sq, kb))

    def apply_masks(t):
        parts = [jnp.where(below, t[:qb], 0.0), jnp.where(has_prev, t[qb:qb + sq], 0.0)]
        return jnp.concatenate(parts + ([t[qb + sq:]] if nsub > 1 else []), axis=0)

    w_diag, w_prev = [], []
    for hd in range(2):
        z = jnp.concatenate([scores(hd, rows(s), kstart(n_clear + s)) for s in range(nsub)]
                            + [scores(hd, rows(s), prev(s)) for s in range(nsub)], axis=0)
        sp = apply_masks(softplus2(z))
        tail = _dot(sp.astype(BF16), negu_ref[...])
        c_diag = tail[:qb, :1]
        w = apply_masks(jnp.exp2(z + tail + jnp.concatenate([jnp.zeros_like(c_diag), c_diag], axis=0)))
        c_ref[hd] = c_diag + tail[qb:, :1]
        w = w.astype(BF16)
        w_diag.append(w[:qb])
        w_prev.append(w[qb:])
    w_diag = jnp.concatenate(w_diag, axis=1)
    w_prev = jnp.concatenate(w_prev, axis=1)
    for s in range(nsub):
        acc_ref[rows(s)] = (_dot(w_diag[rows(s)], v_heads(kstart(n_clear + s)))
                            + _dot(w_prev[rows(s)], v_heads(prev(s))))

    def block(start, r0, r1):
        ws = []
        for hd in range(2):
            z = scores(hd, slice(r0, r1), start)
            tail = _dot(softplus2(z).astype(BF16), negu_ref[...])
            ws.append(jnp.exp2(z + tail + c_ref[hd, r0:r1]).astype(BF16))
            c_ref[hd, r0:r1] += tail[:, :1]
        acc_ref[r0:r1] += _dot(jnp.concatenate(ws, axis=1), v_heads(start))

    def alive(r0, r1):
        return (jnp.max(c_ref[:, r0:r1]) > SB_DEAD_LOG2).astype(jnp.int32)

    def run_blocks(j_first, j_last, r0, r1):
        def cond(carry):
            j, live = carry
            return jnp.logical_and(j >= j_last, live > 0)

        def body(carry):
            j, _ = carry
            block(kstart(j), r0, r1)
            return j - 1, alive(r0, r1)

        lax.while_loop(cond, body, (j_first, alive(r0, r1)))

    @pl.when(alive(0, qb) > 0)
    def _():
        for s in range(2, nsub):
            run_blocks(n_clear + s - 2, n_clear, s * sq, (s + 1) * sq)
        if nsub > 1:
            run_blocks(n_clear - 1, jnp.maximum(n_clear - 1, 0), sq, qb)
        run_blocks(n_clear - 2, 0, 0, qb)

    o_ref[...] = acc_ref[...].astype(o_ref.dtype)


def _sb_call(q, k, v, negu, qb, q_pos0):
    b, nq, w = q.shape
    nk = k.shape[1]
    kb = negu.shape[0]
    assert q_pos0 % kb == 0 and (qb % kb == 0 or nq == qb < kb) and nk % kb == 0 and nk >= q_pos0 + nq
    return pl.pallas_call(
        functools.partial(_sb_kernel, qb=qb, kb=kb, q_pos0=q_pos0),
        grid=(b, w // LANES, nq // qb),
        in_specs=[pl.BlockSpec((None, qb, LANES), lambda bi, p, i: (bi, i, p)),
                  pl.BlockSpec((None, nk, LANES), lambda bi, p, i: (bi, 0, p)),
                  pl.BlockSpec((None, nk, LANES), lambda bi, p, i: (bi, 0, p)),
                  _full_spec(negu.shape)],
        out_specs=pl.BlockSpec((None, qb, LANES), lambda bi, p, i: (bi, i, p)),
        out_shape=jax.ShapeDtypeStruct((b, nq, w), BF16),
        scratch_shapes=[pltpu.VMEM((2, qb, LANES), BF16), pltpu.VMEM((2, qb, 1), F32),
                        pltpu.VMEM((qb, LANES), F32)],
        compiler_params=_params(("parallel", "parallel", "parallel")),
        name="stick_breaking",
    )(q, k, v, negu)


def _ret_kernel(sdec_ref, q_ref, k_ref, v_ref, gt_ref, s0_ref, dec_ref, qdec_ref, kdec_ref,
                o_ref, sfin_ref, st_ref):
    c = pl.program_id(1)

    @pl.when(c == 0)
    def _():
        st_ref[...] = s0_ref[...]

    for h in range(RET_HEADS):
        q = q_ref[:, h * RET_QK_DIM:(h + 1) * RET_QK_DIM]
        k = k_ref[:, h * RET_QK_DIM:(h + 1) * RET_QK_DIM]
        v = v_ref[:, h * RET_V_DIM:(h + 1) * RET_V_DIM]
        state = st_ref[h]
        inner = _dot((_dot_nt(q, k) * dec_ref[h]).astype(BF16), v)
        cross = _dot(q, state.astype(BF16)) * qdec_ref[h]
        kw = (k.astype(F32) * kdec_ref[h]).astype(BF16)
        st_ref[h] = sdec_ref[h] * state + _dot_tn(kw, v)
        o = inner + cross
        o = o * lax.rsqrt(jnp.mean(o * o, axis=-1, keepdims=True) + RMS_EPS)
        gate = gt_ref[:, h * RET_V_DIM:(h + 1) * RET_V_DIM]
        o_ref[:, h * RET_V_DIM:(h + 1) * RET_V_DIM] = (o * (gate * jax.nn.sigmoid(gate))).astype(o_ref.dtype)

    @pl.when(c == pl.num_programs(1) - 1)
    def _():
        sfin_ref[...] = st_ref[...]


def _ret_call(q, k, v, gate, state0, cs):
    b, n, _ = q.shape
    log_gamma = jnp.log1p(-jnp.exp2(-5.0 - jnp.arange(RET_HEADS, dtype=F32)))
    pos = jnp.arange(cs, dtype=F32)
    diff = pos[:, None] - pos[None, :]
    dec = jnp.where(diff >= 0, jnp.exp(log_gamma[:, None, None] * jnp.maximum(diff, 0.0)), 0.0)
    qdec = jnp.exp(log_gamma[:, None] * (pos + 1.0))[..., None]
    kdec = jnp.exp(log_gamma[:, None] * (cs - 1.0 - pos))[..., None]
    sdec = jnp.exp(log_gamma * cs)
    seq = lambda w: pl.BlockSpec((None, cs, w), lambda bi, c: (bi, c, 0))
    st = pl.BlockSpec((None, RET_HEADS, RET_QK_DIM, RET_V_DIM), lambda bi, c: (bi, 0, 0, 0))
    return pl.pallas_call(
        _ret_kernel,
        grid=(b, n // cs),
        in_specs=[pl.BlockSpec(memory_space=pltpu.SMEM), seq(RET_QK_W), seq(RET_QK_W), seq(RET_V_W),
                  seq(RET_V_W), st, _full_spec(dec.shape), _full_spec(qdec.shape), _full_spec(kdec.shape)],
        out_specs=[seq(RET_V_W), st],
        out_shape=[jax.ShapeDtypeStruct((b, n, RET_V_W), BF16),
                   jax.ShapeDtypeStruct((b, RET_HEADS, RET_QK_DIM, RET_V_DIM), F32)],
        scratch_shapes=[pltpu.VMEM((RET_HEADS, RET_QK_DIM, RET_V_DIM), F32)],
        compiler_params=_params(("parallel", "arbitrary")),
        name="retention",
    )(sdec, q, k, v, gate, state0, dec, qdec, kdec)


def _head_norm_rope(t, bd, gain, cos4, sin4, first_half):
    parts = []
    for c in range(t.shape[1] // LANES):
        sq = t[:, c * LANES:(c + 1) * LANES]
        sq = sq * sq
        hi = sq.astype(BF16)
        lo = (sq - hi.astype(F32)).astype(BF16)
        parts.append(_dot(hi, bd) + _dot(lo, bd))
    ms = jnp.concatenate(parts, axis=1) * (1.0 / SWA_HEAD_DIM)
    y = t * lax.rsqrt(ms + RMS_EPS) * gain
    w = t.shape[1]
    quarter = SWA_HEAD_DIM // 2
    partner = jnp.where(first_half, pltpu.roll(y, w - quarter, axis=1), pltpu.roll(y, quarter, axis=1))
    return y * cos4 + partner * sin4


def _odd_proj_kernel(x_ref, g_ref, wq_ref, wk_ref, wv_ref, bd_ref, qg_ref, kg_ref, cos_ref, sin_ref,
                     q_ref, k_ref, v_ref):
    h = _rms(x_ref[...], g_ref[...]).astype(BF16)
    bd = bd_ref[...]
    cos2 = cos_ref[...]
    sin2 = sin_ref[...]
    tm = cos2.shape[0]
    nq = SWA_Q_W // LANES
    nk = SWA_KV_W // LANES
    lane_q = lax.broadcasted_iota(jnp.int32, (tm, SWA_Q_W), 1)
    lane_k = lax.broadcasted_iota(jnp.int32, (tm, SWA_KV_W), 1)
    half = SWA_HEAD_DIM // 2
    q = _head_norm_rope(_dot(h, wq_ref[...]), bd, qg_ref[...], jnp.concatenate([cos2] * nq, axis=1),
                        jnp.concatenate([sin2] * nq, axis=1), (lane_q % SWA_HEAD_DIM) < half)
    q_ref[...] = (q * (LOG2E * SWA_HEAD_DIM ** -0.5)).astype(BF16)
    k_ref[...] = _head_norm_rope(_dot(h, wk_ref[...]), bd, kg_ref[...], jnp.concatenate([cos2] * nk, axis=1),
                                 jnp.concatenate([sin2] * nk, axis=1), (lane_k % SWA_HEAD_DIM) < half)
    v_ref[...] = _dot(h, wv_ref[...])


def _odd_proj_call(x, g, wq, wk, wv, qg, kg, cos2, sin2, tm):
    t, d = x.shape
    n_tab = cos2.shape[0] // tm
    head = jnp.arange(LANES) // SWA_HEAD_DIM
    bd = (head[:, None] == head[None, :]).astype(BF16)
    row = lambda w: pl.BlockSpec((tm, w), lambda i: (i, 0))
    tab = pl.BlockSpec((tm, LANES), lambda i: (i % n_tab, 0))
    ins = [g, wq, wk, wv, bd, qg, kg]
    return pl.pallas_call(
        _odd_proj_kernel,
        grid=(t // tm,),
        in_specs=[row(d)] + [_full_spec(a.shape) for a in ins] + [tab, tab],
        out_specs=[row(SWA_Q_W), row(SWA_KV_W), row(SWA_KV_W)],
        out_shape=[jax.ShapeDtypeStruct((t, SWA_Q_W), BF16), jax.ShapeDtypeStruct((t, SWA_KV_W), F32),
                   jax.ShapeDtypeStruct((t, SWA_KV_W), F32)],
        compiler_params=_params(("parallel",)),
        name="odd_proj",
    )(x, *ins, cos2, sin2)


def _swa_kernel(sink_ref, q_ref, kp_ref, kc_ref, vp_ref, vc_ref, o_ref, *, rows, first_prev_valid):
    i = pl.program_id(1)
    nch = rows // CHUNK
    wk = WINDOW + CHUNK
    nk = WINDOW + rows
    grp = SWA_HEADS // SWA_KV_HEADS
    gw = grp * CHUNK
    key_low = lax.broadcasted_iota(jnp.int32, (nk, LANES), 1) < SWA_HEAD_DIM
    low = lax.broadcasted_iota(jnp.int32, (CHUNK, LANES), 1) < SWA_HEAD_DIM
    if not first_prev_valid:
        krow = lax.broadcasted_iota(jnp.int32, (nk, LANES), 0)
        klane = lax.broadcasted_iota(jnp.int32, (nk, LANES), 1)
        k_bias = jnp.where((krow < WINDOW) & (klane == 0) & (i == 0), SWA_MASKED, 0.0).astype(BF16)
        q_one = jnp.where(lax.broadcasted_iota(jnp.int32, (gw, LANES), 1) == 0, 1.0, 0.0).astype(BF16)

    s, vj_t = [], []
    for j in range(SWA_KV_HEADS):
        pair, odd = divmod(j, 2)
        lanes = slice(pair * LANES, (pair + 1) * LANES)
        k_pair = jnp.concatenate([kp_ref[:, lanes], kc_ref[:, lanes]], axis=0)
        own = jnp.where(key_low, 0.0, k_pair) if odd else jnp.where(key_low, k_pair, 0.0)
        k_both = (own + pltpu.roll(own, SWA_HEAD_DIM, axis=1)).astype(BF16)
        v_t = jnp.concatenate([vp_ref[:, lanes], vc_ref[:, lanes]], axis=0).T
        vj_t.append(v_t[odd * SWA_HEAD_DIM:(odd + 1) * SWA_HEAD_DIM].astype(BF16))
        if not first_prev_valid:
            k_both = jnp.concatenate([k_both, k_bias], axis=1)
        for c in range(nch):
            q_four = []
            for qp in (2 * j, 2 * j + 1):
                q_pair = q_ref[c * CHUNK:(c + 1) * CHUNK, qp * LANES:(qp + 1) * LANES]
                zero = jnp.zeros_like(q_pair)
                q_four += [jnp.where(low, q_pair, zero), jnp.where(low, zero, q_pair)]
            q_four = jnp.concatenate(q_four, axis=0)
            if not first_prev_valid:
                q_four = jnp.concatenate([q_four, q_one], axis=1)
            s.append(_dot_nt(k_both[c * CHUNK:c * CHUNK + wk], q_four))
    s = jnp.concatenate(s, axis=1)
    sink = jnp.concatenate([jnp.full((1, CHUNK), sink_ref[j * grp + g] * LOG2E, F32)
                            for j in range(SWA_KV_HEADS) for _ in range(nch) for g in range(grp)], axis=1)
    m = jnp.maximum(jnp.max(s, axis=0, keepdims=True), sink)
    p = jnp.exp2(s - m)
    inv = 1.0 / (jnp.sum(p, axis=0, keepdims=True) + jnp.exp2(sink - m))
    p = p.astype(BF16)
    o_t = []
    for j in range(SWA_KV_HEADS):
        for c in range(nch):
            cols = slice((j * nch + c) * gw, (j * nch + c + 1) * gw)
            o_t.append(_dot(vj_t[j][:, c * CHUNK:c * CHUNK + wk], p[:, cols]) * inv[:, cols])
    for c0 in range(0, nch, 2):
        pieces = []
        for j in range(SWA_KV_HEADS):
            for t in range(grp // 2):
                tiles = [o_t[j * nch + c][:, t * LANES:(t + 1) * LANES] for c in range(c0, min(c0 + 2, nch))]
                if len(tiles) == 2:
                    pieces += [jnp.where(low, tiles[0], pltpu.roll(tiles[1], CHUNK, axis=1)),
                               jnp.where(low, pltpu.roll(tiles[0], CHUNK, axis=1), tiles[1])]
                else:
                    pieces += [tiles[0], pltpu.roll(tiles[0], CHUNK, axis=1)]
        out = jnp.concatenate(pieces, axis=0).T
        r1 = min((c0 + 2) * CHUNK, rows)
        o_ref[c0 * CHUNK:r1] = out[:r1 - c0 * CHUNK].astype(o_ref.dtype)


def _swa_call(q, k_prev, k_cur, v_prev, v_cur, sinks, rows, same_array):
    b, n, _ = q.shape
    per = rows // WINDOW if same_array else 0
    cur = lambda w: pl.BlockSpec((None, rows, w), lambda bi, i: (bi, i, 0))
    prev = pl.BlockSpec((None, WINDOW, SWA_KV_W), lambda bi, i: (bi, jnp.maximum(i * per - 1, 0), 0))
    return pl.pallas_call(
        functools.partial(_swa_kernel, rows=rows, first_prev_valid=not same_array),
        grid=(b, n // rows),
        in_specs=[pl.BlockSpec(memory_space=pltpu.SMEM), cur(SWA_Q_W), prev, cur(SWA_KV_W), prev, cur(SWA_KV_W)],
        out_specs=cur(SWA_Q_W),
        out_shape=jax.ShapeDtypeStruct((b, n, SWA_Q_W), BF16),
        compiler_params=_params(("parallel", "parallel")),
        name="swa",
    )(sinks, q, k_prev, k_cur, v_prev, v_cur)


def _rope_tables(pos, head_dim):
    half = head_dim // 2
    inv = jnp.power(ROPE_THETA, -jnp.arange(half, dtype=F32) / half)
    ang = pos.astype(F32)[:, None] * inv[None, :]
    cos, sin = jnp.cos(ang), jnp.sin(ang)
    reps = LANES // head_dim
    return (jnp.tile(jnp.concatenate([cos, cos], axis=1), (1, reps)),
            jnp.tile(jnp.concatenate([-sin, sin], axis=1), (1, reps)))


def _prep_weights(norm_g, ffn_w_gate, ffn_w_up, ffn_w_down, even_w_in, even_w_out, odd_w_in, odd_w_out,
                  odd_q_norm, odd_k_norm):
    depth = norm_g.shape[0]
    prm = {"g": norm_g[:, :, None, :], "ffn": [], "even": [], "odd": []}
    for layer in range(depth):
        prm["ffn"].append([(ffn_w_gate[layer, s].astype(BF16), ffn_w_up[layer, s].astype(BF16),
                            ffn_w_down[layer, s].astype(BF16)) for s in range(2)])
    cuts = [0, SB_W, 2 * SB_W, 3 * SB_W, 3 * SB_W + RET_QK_W, 3 * SB_W + 2 * RET_QK_W,
            3 * SB_W + 2 * RET_QK_W + RET_V_W, 3 * SB_W + 2 * RET_QK_W + 2 * RET_V_W]
    for i in range(even_w_in.shape[0]):
        w_in = even_w_in[i].astype(BF16)
        w_out = even_w_out[i].astype(BF16)
        prm["even"].append(([w_in[:, cuts[s]:cuts[s + 1]] for s in range(7)], w_out[:SB_W], w_out[SB_W:]))
    for i in range(odd_w_in.shape[0]):
        w_in = odd_w_in[i].astype(BF16)
        prm["odd"].append((w_in[:, :SWA_Q_W], w_in[:, SWA_Q_W:SWA_Q_W + SWA_KV_W], w_in[:, SWA_Q_W + SWA_KV_W:],
                           odd_w_out[i].astype(BF16),
                           jnp.tile(odd_q_norm[i], SWA_HEADS)[None, :], jnp.tile(odd_k_norm[i], SWA_KV_HEADS)[None, :]))
    return prm


def _trunk(x, pos, prm, sinks, caches, tm, sb_qb, sb_kb, ret_cs, swa_rows):
    b, n, d = x.shape
    t = b * n
    xf = x.reshape(t, d)
    depth = len(prm["ffn"])
    tab_rows = n if n % tm == 0 else t
    tile_tab = lambda tb: tb if tab_rows == n else jnp.tile(tb, (b, 1))
    cos_r, sin_r = [tile_tab(tb) for tb in _rope_tables(pos, RET_QK_DIM)]
    cos_s, sin_s = [tile_tab(tb) for tb in _rope_tables(pos, SWA_HEAD_DIM)]
    idx = jnp.arange(sb_kb)
    negu = jnp.where(idx[:, None] >= idx[None, :], -1.0, 0.0).astype(BF16)
    sb_k, sb_v, ret, swa_k, swa_v = [], [], [], [], []
    mix = []
    for layer in range(depth):
        i = layer // 2
        g = prm["g"][layer]
        ffn = prm["ffn"][layer]
        xf = _ffn_call(xf, mix, g[0], *ffn[0], tm)
        if layer % 2 == 0:
            ws, wo_sb, wo_r = prm["even"][i]
            qa, ka, kab, va, vab, qr, kr, vr, gt = _even_proj_call(xf, g[1], ws, cos_r, sin_r, tm)
            sb_k.append(ka.reshape(b, n, SB_HEADS, SB_HEAD_DIM))
            sb_v.append(va.reshape(b, n, SB_HEADS, SB_HEAD_DIM))
            r3 = lambda a: a.reshape(b, n, a.shape[-1])
            if caches is None:
                k_all, v_all, q_pos0 = r3(kab), r3(vab), 0
                state0 = jnp.zeros((b, RET_HEADS, RET_QK_DIM, RET_V_DIM), F32)
            else:
                past = caches["sb_k"].shape[2]
                padded = -(-(past + n) // sb_kb) * sb_kb
                cat = lambda cache, new: jnp.pad(
                    jnp.concatenate([cache.reshape(b, past, SB_W).astype(BF16), r3(new)], axis=1),
                    ((0, 0), (0, padded - past - n), (0, 0)))
                k_all, v_all, q_pos0 = cat(caches["sb_k"][i], kab), cat(caches["sb_v"][i], vab), past
                state0 = caches["ret"][i]
            o_sb = _sb_call(r3(qa), k_all, v_all, negu, sb_qb, q_pos0)
            o_r, st = _ret_call(r3(qr), r3(kr), r3(vr), r3(gt), state0, ret_cs)
            ret.append(st)
            mix = [(o_sb.reshape(t, SB_W), wo_sb), (o_r.reshape(t, RET_V_W), wo_r)]
        else:
            wq, wk, wv, wo, qg, kg = prm["odd"][i]
            q, k, v = _odd_proj_call(xf, g[1], wq, wk, wv, qg, kg, cos_s, sin_s, tm)
            r3 = lambda a: a.reshape(b, n, a.shape[-1])
            k3, v3 = r3(k), r3(v)
            if caches is None:
                o = _swa_call(r3(q), k3, k3, v3, v3, sinks[i], swa_rows, True)
                keep = min(WINDOW, n)
                k_rows, v_rows = k3[:, n - keep:], v3[:, n - keep:]
            else:
                kc = caches["swa_k"][i].reshape(b, -1, SWA_KV_W)
                vc = caches["swa_v"][i].reshape(b, -1, SWA_KV_W)
                o = _swa_call(r3(q), kc, k3, vc, v3, sinks[i], swa_rows, False)
                k_rows, v_rows = k3, v3
            swa_k.append(k_rows.reshape(b, -1, SWA_KV_HEADS, SWA_HEAD_DIM))
            swa_v.append(v_rows.reshape(b, -1, SWA_KV_HEADS, SWA_HEAD_DIM))
            mix = [(o.reshape(t, SWA_Q_W), wo)]
        xf = _ffn_call(xf, mix, g[2], *ffn[1], tm)
        mix = []
    stack = lambda parts: parts[0][None] if len(parts) == 1 else jnp.stack(parts)
    return (xf.reshape(b, n, d), stack(sb_k), stack(sb_v), stack(ret), stack(swa_k), stack(swa_v))


def kernel(x_prompt, x_sample, cache_sb_k, cache_sb_v, state_ret, cache_swa_k, cache_swa_v, norm_g, ffn_w_gate,
           ffn_w_up, ffn_w_down, even_w_in, even_w_out, odd_w_in, odd_w_out, odd_q_norm, odd_k_norm, odd_sinks):
    prm = _prep_weights(norm_g, ffn_w_gate, ffn_w_up, ffn_w_down, even_w_in, even_w_out, odd_w_in, odd_w_out,
                        odd_q_norm, odd_k_norm)
    past = cache_sb_k.shape[2]
    n_p = x_prompt.shape[1]
    n_s = x_sample.shape[1]
    pos_prompt = jnp.arange(n_p, dtype=jnp.int32)
    pos_sample = past + jnp.arange(n_s, dtype=jnp.int32)
    y_p, sb_k_p, sb_v_p, ret_p, swa_k_p, swa_v_p = _trunk(
        x_prompt, pos_prompt, prm, odd_sinks, None, tm=512, sb_qb=1024, sb_kb=256, ret_cs=256, swa_rows=512)
    caches = {"sb_k": cache_sb_k, "sb_v": cache_sb_v, "ret": state_ret, "swa_k": cache_swa_k, "swa_v": cache_swa_v}
    y_s, sb_k_s, sb_v_s, ret_s, swa_k_s, swa_v_s = _trunk(
        x_sample, pos_sample, prm, odd_sinks, caches, tm=x_sample.shape[0] * n_s, sb_qb=n_s, sb_kb=256,
        ret_cs=n_s, swa_rows=n_s)
    return (y_p, y_s, sb_k_p, sb_v_p, ret_p, swa_k_p, swa_v_p, sb_k_s, sb_v_s, ret_s, swa_k_s, swa_v_s)
```

```python
import functools
import math

import jax
import jax.numpy as jnp
from jax import lax
from jax.experimental import pallas as pl
from jax.experimental.pallas import tpu as pltpu

F32 = jnp.float32
BF16 = jnp.bfloat16

RMS_EPS = 1e-6
ROPE_THETA = 10000.0
CHUNK = 64
SB_HEADS = 8
SB_HEAD_DIM = 64
RET_HEADS = 4
RET_QK_DIM = 128
RET_V_DIM = 256
SWA_HEADS = 16
SWA_KV_HEADS = 4
SWA_HEAD_DIM = 64
WINDOW = 128

SB_W = SB_HEADS * SB_HEAD_DIM
RET_QK_W = RET_HEADS * RET_QK_DIM
RET_V_W = RET_HEADS * RET_V_DIM
SWA_Q_W = SWA_HEADS * SWA_HEAD_DIM
SWA_KV_W = SWA_KV_HEADS * SWA_HEAD_DIM
EVEN_CUTS = (0, SB_W, 2 * SB_W, 3 * SB_W, 3 * SB_W + RET_QK_W, 3 * SB_W + 2 * RET_QK_W,
             3 * SB_W + 2 * RET_QK_W + RET_V_W, 3 * SB_W + 2 * RET_QK_W + 2 * RET_V_W)

LANES = 128
FF_CHUNK = 256
VMEM_LIMIT = 56 * 1024 * 1024
LOG2E = math.log2(math.e)
SOFTPLUS2_LINEAR = 40.0
SWA_MASKED = -1e30
SB_DEAD_LOG2 = -170.0


def _dot(a, b):
    return jnp.dot(a, b, preferred_element_type=F32)


def _dot_nt(a, b):
    return lax.dot_general(a, b, (((1,), (1,)), ((), ())), preferred_element_type=F32)


def _dot_tn(a, b):
    return lax.dot_general(a, b, (((0,), (0,)), ((), ())), preferred_element_type=F32)


def _rms(x, g):
    return x * lax.rsqrt(jnp.mean(x * x, axis=-1, keepdims=True) + RMS_EPS) * g


def _full_spec(shape):
    nd = len(shape)
    return pl.BlockSpec(shape, lambda *_: (0,) * nd, pipeline_mode=pl.Buffered(1))


def _params(sem):
    return pltpu.CompilerParams(dimension_semantics=sem, vmem_limit_bytes=VMEM_LIMIT)


def _ffn_kernel(*refs, n_mix):
    x_ref = refs[0]
    mix = refs[1:1 + n_mix]
    wmix_ref = refs[1 + n_mix] if n_mix else None
    g_ref, wg_ref, wu_ref, wd_ref, o_ref, h_ref, act_ref = refs[1 + n_mix + bool(n_mix):]
    x = x_ref[...]
    r0 = 0
    for a_ref in mix:
        x = x + _dot(a_ref[...], wmix_ref[r0:r0 + a_ref.shape[1]])
        r0 += a_ref.shape[1]
    h_ref[...] = _rms(x, g_ref[...]).astype(BF16)
    for c in range(wg_ref.shape[1] // FF_CHUNK):
        cols = slice(c * FF_CHUNK, (c + 1) * FF_CHUNK)
        gate = _dot(h_ref[...], wg_ref[:, cols])
        up = _dot(h_ref[...], wu_ref[:, cols])
        act_ref[:, cols] = (gate * jax.nn.sigmoid(gate) * up).astype(BF16)
    o_ref[...] = x + 0.5 * _dot(act_ref[...], wd_ref[...])


def _ffn_call(x, mix, w_mix, g, wg, wu, wd, tm):
    t, d = x.shape
    row = lambda w: pl.BlockSpec((tm, w), lambda i: (i, 0))
    in_specs = [row(d)]
    args = [x]
    for a in mix:
        in_specs.append(row(a.shape[1]))
        args.append(a)
    if mix:
        in_specs.append(_full_spec(w_mix.shape))
        args.append(w_mix)
    in_specs += [_full_spec(g.shape), _full_spec(wg.shape), _full_spec(wu.shape), _full_spec(wd.shape)]
    args += [g, wg, wu, wd]
    return pl.pallas_call(
        functools.partial(_ffn_kernel, n_mix=len(mix)),
        grid=(t // tm,),
        in_specs=in_specs,
        out_specs=row(d),
        out_shape=jax.ShapeDtypeStruct((t, d), F32),
        scratch_shapes=[pltpu.VMEM((tm, d), BF16), pltpu.VMEM((tm, wg.shape[1]), BF16)],
        compiler_params=_params(("parallel",)),
        name="ffn_mix%d" % len(mix),
    )(*args)


def _rope128(x, cos2, sin2):
    parts = []
    for hd in range(x.shape[1] // LANES):
        sl = x[:, hd * LANES:(hd + 1) * LANES]
        parts.append(sl * cos2 + pltpu.roll(sl, LANES // 2, axis=1) * sin2)
    return jnp.concatenate(parts, axis=1)


def _even_proj_kernel(x_ref, g_ref, w_ref, cos_ref, sin_ref,
                      qa_ref, ka_ref, kab_ref, va_ref, vab_ref, qr_ref, kr_ref, vr_ref, gt_ref):
    h = _rms(x_ref[...], g_ref[...]).astype(BF16)
    proj = lambda s: _dot(h, w_ref[:, EVEN_CUTS[s]:EVEN_CUTS[s + 1]])
    qa_ref[...] = (proj(0) * (LOG2E * SB_HEAD_DIM ** -0.5)).astype(BF16)
    ka = proj(1)
    ka_ref[...] = ka.reshape(ka_ref.shape)
    kab_ref[...] = ka.astype(BF16)
    va = proj(2)
    va_ref[...] = va.reshape(va_ref.shape)
    vab_ref[...] = va.astype(BF16)
    cos2 = cos_ref[...]
    sin2 = sin_ref[...]
    qr_ref[...] = _rope128(proj(3), cos2, sin2).astype(BF16)
    kr_ref[...] = (_rope128(proj(4), cos2, sin2) * (RET_QK_DIM ** -0.5)).astype(BF16)
    vr_ref[...] = proj(5).astype(BF16)
    gate = proj(6)
    gt_ref[...] = gate * jax.nn.sigmoid(gate)


def _even_proj_call(x, g, w_in, cos2, sin2, tm):
    t, d = x.shape
    n_tab = cos2.shape[0] // tm
    row = lambda w: pl.BlockSpec((tm, w), lambda i: (i, 0))
    tab = pl.BlockSpec((tm, LANES), lambda i: (i % n_tab, 0))
    per_head = (SB_HEADS, SB_HEAD_DIM)
    outs = [((SB_W,), BF16), (per_head, F32), ((SB_W,), BF16), (per_head, F32), ((SB_W,), BF16),
            ((RET_QK_W,), BF16), ((RET_QK_W,), BF16), ((RET_V_W,), BF16), ((RET_V_W,), F32)]
    return pl.pallas_call(
        _even_proj_kernel,
        grid=(t // tm,),
        in_specs=[row(d), _full_spec(g.shape), _full_spec(w_in.shape), tab, tab],
        out_specs=[pl.BlockSpec((tm,) + tail, lambda i, nd=len(tail): (i,) + (0,) * nd) for tail, _ in outs],
        out_shape=[jax.ShapeDtypeStruct((t,) + tail, dt) for tail, dt in outs],
        compiler_params=_params(("parallel",)),
        name="even_proj",
    )(x, g, w_in, cos2, sin2)


def _sb_kernel(q_ref, k_ref, v_ref, negu_ref, o_ref, qh_ref, c_ref, acc_ref, *, qb, kb, q_pos0):
    i = pl.program_id(2)
    n_clear = (q_pos0 + i * qb) // kb
    sq = min(qb, kb)
    nsub = qb // sq
    rows = lambda s: slice(s * sq, (s + 1) * sq)
    kstart = lambda j: pl.multiple_of(j * kb, kb)
    q = q_ref[...]
    first = lax.broadcasted_iota(jnp.int32, (qb, LANES), 1) < SB_HEAD_DIM
    qh_ref[0] = jnp.where(first, q, jnp.zeros_like(q))
    qh_ref[1] = jnp.where(first, jnp.zeros_like(q), q)

    def scores(hd, rws, start):
        return _dot_nt(qh_ref[hd, rws], k_ref[pl.ds(start, kb), :])

    def softplus2(z):
        return jnp.where(z > SOFTPLUS2_LINEAR, z, jnp.log2(1.0 + jnp.exp2(z)))

    def v_heads(start):
        vs = v_ref[pl.ds(start, kb), :]
        vfirst = lax.broadcasted_iota(jnp.int32, (kb, LANES), 1) < SB_HEAD_DIM
        return jnp.concatenate([jnp.where(vfirst, vs, jnp.zeros_like(vs)),
                                jnp.where(vfirst, jnp.zeros_like(vs), vs)], axis=0)

    prev = lambda s: kstart(jnp.maximum(n_clear + s - 1, 0))
    row = lax.broadcasted_iota(jnp.int32, (qb, kb), 0)
    below = lax.broadcasted_iota(jnp.int32, (qb, kb), 1) < (row & (sq - 1))
    has_prev = jnp.broadcast_to(n_clear > 0, (sq, kb))

    def apply_masks(t):
        parts = [jnp.where(below, t[:qb], 0.0), jnp.where(has_prev, t[qb:qb + sq], 0.0)]
        return jnp.concatenate(parts + ([t[qb + sq:]] if nsub > 1 else []), axis=0)

    w_diag, w_prev = [], []
    for hd in range(2):
        z = jnp.concatenate([scores(hd, rows(s), kstart(n_clear + s)) for s in range(nsub)]
                            + [scores(hd, rows(s), prev(s)) for s in range(nsub)], axis=0)
        sp = apply_masks(softplus2(z))
        tail = _dot(sp.astype(BF16), negu_ref[...])
        c_diag = tail[:qb, :1]
        w = apply_masks(jnp.exp2(z + tail + jnp.concatenate([jnp.zeros_like(c_diag), c_diag], axis=0)))
        c_ref[hd] = c_diag + tail[qb:, :1]
        w = w.astype(BF16)
        w_diag.append(w[:qb])
        w_prev.append(w[qb:])
    w_diag = jnp.concatenate(w_diag, axis=1)
    w_prev = jnp.concatenate(w_prev, axis=1)
    for s in range(nsub):
        acc_ref[rows(s)] = (_dot(w_diag[rows(s)], v_heads(kstart(n_clear + s)))
                            + _dot(w_prev[rows(s)], v_heads(prev(s))))

    def block(start, r0, r1):
        ws = []
        for hd in range(2):
            z = scores(hd, slice(r0, r1), start)
            tail = _dot(softplus2(z).astype(BF16), negu_ref[...])
            ws.append(jnp.exp2(z + tail + c_ref[hd, r0:r1]).astype(BF16))
            c_ref[hd, r0:r1] += tail[:, :1]
        acc_ref[r0:r1] += _dot(jnp.concatenate(ws, axis=1), v_heads(start))

    def alive(r0, r1):
        return (jnp.max(c_ref[:, r0:r1]) > SB_DEAD_LOG2).astype(jnp.int32)

    def run_blocks(j_first, j_last, r0, r1):
        def cond(carry):
            j, live = carry
            return jnp.logical_and(j >= j_last, live > 0)

        def body(carry):
            j, _ = carry
            block(kstart(j), r0, r1)
            return j - 1, alive(r0, r1)

        lax.while_loop(cond, body, (j_first, alive(r0, r1)))

    @pl.when(alive(0, qb) > 0)
    def _():
        for s in range(2, nsub):
            run_blocks(n_clear + s - 2, n_clear, s * sq, (s + 1) * sq)
        if nsub > 1:
            run_blocks(n_clear - 1, jnp.maximum(n_clear - 1, 0), sq, qb)
        run_blocks(n_clear - 2, 0, 0, qb)

    o_ref[...] = acc_ref[...].astype(o_ref.dtype)


def _sb_call(q, k, v, negu, qb, q_pos0):
    b, nq, w = q.shape
    nk = k.shape[1]
    kb = negu.shape[0]
    assert q_pos0 % kb == 0 and (qb % kb == 0 or nq == qb < kb) and nk % kb == 0 and nk >= q_pos0 + nq
    return pl.pallas_call(
        functools.partial(_sb_kernel, qb=qb, kb=kb, q_pos0=q_pos0),
        grid=(b, w // LANES, nq // qb),
        in_specs=[pl.BlockSpec((None, qb, LANES), lambda bi, p, i: (bi, i, p)),
                  pl.BlockSpec((None, nk, LANES), lambda bi, p, i: (bi, 0, p)),
                  pl.BlockSpec((None, nk, LANES), lambda bi, p, i: (bi, 0, p)),
                  _full_spec(negu.shape)],
        out_specs=pl.BlockSpec((None, qb, LANES), lambda bi, p, i: (bi, i, p)),
        out_shape=jax.ShapeDtypeStruct((b, nq, w), BF16),
        scratch_shapes=[pltpu.VMEM((2, qb, LANES), BF16), pltpu.VMEM((2, qb, 1), F32),
                        pltpu.VMEM((qb, LANES), F32)],
        compiler_params=_params(("parallel", "parallel", "parallel")),
        name="stick_breaking",
    )(q, k, v, negu)


def _ret_kernel(sdec_ref, q_ref, k_ref, v_ref, gt_ref, s0_ref, dec_ref, qdec_ref, kdec_ref,
                o_ref, sfin_ref, st_ref):
    c = pl.program_id(0)

    @pl.when(c == 0)
    def _():
        st_ref[...] = s0_ref[...]

    units = [(b, h) for b in range(q_ref.shape[0]) for h in range(RET_HEADS)]
    qk = lambda h: slice(h * RET_QK_DIM, (h + 1) * RET_QK_DIM)
    vw = lambda h: slice(h * RET_V_DIM, (h + 1) * RET_V_DIM)
    scores = [_dot_nt(q_ref[b, :, qk(h)], k_ref[b, :, qk(h)]) for b, h in units]
    cross = [_dot(q_ref[b, :, qk(h)], st_ref[b, h].astype(BF16)) for b, h in units]
    kw = [(k_ref[b, :, qk(h)].astype(F32) * kdec_ref[h]).astype(BF16) for b, h in units]
    grow = [_dot_tn(kw[u], v_ref[b, :, vw(h)]) for u, (b, h) in enumerate(units)]
    decayed = [(scores[u] * dec_ref[h]).astype(BF16) for u, (b, h) in enumerate(units)]
    inner = [_dot(decayed[u], v_ref[b, :, vw(h)]) for u, (b, h) in enumerate(units)]
    for u, (b, h) in enumerate(units):
        st_ref[b, h] = sdec_ref[h] * st_ref[b, h] + grow[u]
        o = inner[u] + cross[u] * qdec_ref[h]
        o = o * lax.rsqrt(jnp.mean(o * o, axis=-1, keepdims=True) + RMS_EPS)
        o_ref[b, :, vw(h)] = (o * gt_ref[b, :, vw(h)]).astype(o_ref.dtype)

    @pl.when(c == pl.num_programs(0) - 1)
    def _():
        sfin_ref[...] = st_ref[...]


def _ret_call(q, k, v, gate, state0, cs):
    b, n, _ = q.shape
    log_gamma = jnp.log1p(-jnp.exp2(-5.0 - jnp.arange(RET_HEADS, dtype=F32)))
    pos = jnp.arange(cs, dtype=F32)
    diff = pos[:, None] - pos[None, :]
    dec = jnp.where(diff >= 0, jnp.exp(log_gamma[:, None, None] * jnp.maximum(diff, 0.0)), 0.0)
    qdec = jnp.exp(log_gamma[:, None] * (pos + 1.0))[..., None]
    kdec = jnp.exp(log_gamma[:, None] * (cs - 1.0 - pos))[..., None]
    sdec = jnp.exp(log_gamma * cs)
    seq = lambda w: pl.BlockSpec((b, cs, w), lambda c: (0, c, 0))
    st = pl.BlockSpec((b, RET_HEADS, RET_QK_DIM, RET_V_DIM), lambda c: (0, 0, 0, 0))
    return pl.pallas_call(
        _ret_kernel,
        grid=(n // cs,),
        in_specs=[pl.BlockSpec(memory_space=pltpu.SMEM), seq(RET_QK_W), seq(RET_QK_W), seq(RET_V_W),
                  seq(RET_V_W), st, _full_spec(dec.shape), _full_spec(qdec.shape), _full_spec(kdec.shape)],
        out_specs=[seq(RET_V_W), st],
        out_shape=[jax.ShapeDtypeStruct((b, n, RET_V_W), BF16),
                   jax.ShapeDtypeStruct((b, RET_HEADS, RET_QK_DIM, RET_V_DIM), F32)],
        scratch_shapes=[pltpu.VMEM((b, RET_HEADS, RET_QK_DIM, RET_V_DIM), F32)],
        compiler_params=_params(("arbitrary",)),
        name="retention",
    )(sdec, q, k, v, gate, state0, dec, qdec, kdec)


def _head_norm_rope(t, bd, gain, cos4, sin4, first_half):
    parts = []
    for c in range(t.shape[1] // LANES):
        sq = t[:, c * LANES:(c + 1) * LANES]
        sq = sq * sq
        hi = sq.astype(BF16)
        lo = (sq - hi.astype(F32)).astype(BF16)
        parts.append(_dot(hi, bd) + _dot(lo, bd))
    ms = jnp.concatenate(parts, axis=1) * (1.0 / SWA_HEAD_DIM)
    y = t * lax.rsqrt(ms + RMS_EPS) * gain
    w = t.shape[1]
    quarter = SWA_HEAD_DIM // 2
    partner = jnp.where(first_half, pltpu.roll(y, w - quarter, axis=1), pltpu.roll(y, quarter, axis=1))
    return y * cos4 + partner * sin4


def _odd_proj_kernel(x_ref, g_ref, w_ref, bd_ref, qg_ref, kg_ref, cos_ref, sin_ref,
                     q_ref, k_ref, v_ref):
    h = _rms(x_ref[...], g_ref[...]).astype(BF16)
    bd = bd_ref[...]
    cos2 = cos_ref[...]
    sin2 = sin_ref[...]
    tm = cos2.shape[0]
    nq = SWA_Q_W // LANES
    nk = SWA_KV_W // LANES
    lane_q = lax.broadcasted_iota(jnp.int32, (tm, SWA_Q_W), 1)
    lane_k = lax.broadcasted_iota(jnp.int32, (tm, SWA_KV_W), 1)
    half = SWA_HEAD_DIM // 2
    q = _head_norm_rope(_dot(h, w_ref[:, :SWA_Q_W]), bd, qg_ref[...], jnp.concatenate([cos2] * nq, axis=1),
                        jnp.concatenate([sin2] * nq, axis=1), (lane_q % SWA_HEAD_DIM) < half)
    q_ref[...] = (q * (LOG2E * SWA_HEAD_DIM ** -0.5)).astype(BF16)
    k_ref[...] = _head_norm_rope(_dot(h, w_ref[:, SWA_Q_W:SWA_Q_W + SWA_KV_W]), bd, kg_ref[...], jnp.concatenate([cos2] * nk, axis=1),
                                 jnp.concatenate([sin2] * nk, axis=1), (lane_k % SWA_HEAD_DIM) < half)
    v_ref[...] = _dot(h, w_ref[:, SWA_Q_W + SWA_KV_W:])


def _odd_proj_call(x, g, w_in, qg, kg, cos2, sin2, tm):
    t, d = x.shape
    n_tab = cos2.shape[0] // tm
    head = jnp.arange(LANES) // SWA_HEAD_DIM
    bd = (head[:, None] == head[None, :]).astype(BF16)
    row = lambda w: pl.BlockSpec((tm, w), lambda i: (i, 0))
    tab = pl.BlockSpec((tm, LANES), lambda i: (i % n_tab, 0))
    ins = [g, w_in, bd, qg, kg]
    return pl.pallas_call(
        _odd_proj_kernel,
        grid=(t // tm,),
        in_specs=[row(d)] + [_full_spec(a.shape) for a in ins] + [tab, tab],
        out_specs=[row(SWA_Q_W), row(SWA_KV_W), row(SWA_KV_W)],
        out_shape=[jax.ShapeDtypeStruct((t, SWA_Q_W), BF16), jax.ShapeDtypeStruct((t, SWA_KV_W), F32),
                   jax.ShapeDtypeStruct((t, SWA_KV_W), F32)],
        compiler_params=_params(("parallel",)),
        name="odd_proj",
    )(x, *ins, cos2, sin2)


def _swa_kernel(sink_ref, q_ref, kp_ref, kc_ref, vp_ref, vc_ref, o_ref, *, rows, first_prev_valid):
    i = pl.program_id(1)
    nch = rows // CHUNK
    wk = WINDOW + CHUNK
    nk = WINDOW + rows
    grp = SWA_HEADS // SWA_KV_HEADS
    gw = grp * CHUNK
    key_low = lax.broadcasted_iota(jnp.int32, (nk, LANES), 1) < SWA_HEAD_DIM
    low = lax.broadcasted_iota(jnp.int32, (CHUNK, LANES), 1) < SWA_HEAD_DIM
    if not first_prev_valid:
        krow = lax.broadcasted_iota(jnp.int32, (nk, LANES), 0)
        klane = lax.broadcasted_iota(jnp.int32, (nk, LANES), 1)
        k_bias = jnp.where((krow < WINDOW) & (klane == 0) & (i == 0), SWA_MASKED, 0.0).astype(BF16)
        q_one = jnp.where(lax.broadcasted_iota(jnp.int32, (gw, LANES), 1) == 0, 1.0, 0.0).astype(BF16)

    s, vj_t = [], []
    for j in range(SWA_KV_HEADS):
        pair, odd = divmod(j, 2)
        lanes = slice(pair * LANES, (pair + 1) * LANES)
        k_pair = jnp.concatenate([kp_ref[:, lanes], kc_ref[:, lanes]], axis=0)
        own = jnp.where(key_low, 0.0, k_pair) if odd else jnp.where(key_low, k_pair, 0.0)
        k_both = (own + pltpu.roll(own, SWA_HEAD_DIM, axis=1)).astype(BF16)
        v_t = jnp.concatenate([vp_ref[:, lanes], vc_ref[:, lanes]], axis=0).T
        vj_t.append(v_t[odd * SWA_HEAD_DIM:(odd + 1) * SWA_HEAD_DIM].astype(BF16))
        if not first_prev_valid:
            k_both = jnp.concatenate([k_both, k_bias], axis=1)
        for c in range(nch):
            q_four = []
            for qp in (2 * j, 2 * j + 1):
                q_pair = q_ref[c * CHUNK:(c + 1) * CHUNK, qp * LANES:(qp + 1) * LANES]
                zero = jnp.zeros_like(q_pair)
                q_four += [jnp.where(low, q_pair, zero), jnp.where(low, zero, q_pair)]
            q_four = jnp.concatenate(q_four, axis=0)
            if not first_prev_valid:
                q_four = jnp.concatenate([q_four, q_one], axis=1)
            s.append(_dot_nt(k_both[c * CHUNK:c * CHUNK + wk], q_four))
    s = jnp.concatenate(s, axis=1)
    sink = jnp.concatenate([jnp.full((1, CHUNK), sink_ref[j * grp + g] * LOG2E, F32)
                            for j in range(SWA_KV_HEADS) for _ in range(nch) for g in range(grp)], axis=1)
    m = jnp.maximum(jnp.max(s, axis=0, keepdims=True), sink)
    p = jnp.exp2(s - m)
    inv = 1.0 / (jnp.sum(p, axis=0, keepdims=True) + jnp.exp2(sink - m))
    p = p.astype(BF16)
    o_t = []
    for j in range(SWA_KV_HEADS):
        for c in range(nch):
            cols = slice((j * nch + c) * gw, (j * nch + c + 1) * gw)
            o_t.append(_dot(vj_t[j][:, c * CHUNK:c * CHUNK + wk], p[:, cols]) * inv[:, cols])
    for c0 in range(0, nch, 2):
        pieces = []
        for j in range(SWA_KV_HEADS):
            for t in range(grp // 2):
                tiles = [o_t[j * nch + c][:, t * LANES:(t + 1) * LANES] for c in range(c0, min(c0 + 2, nch))]
                if len(tiles) == 2:
                    pieces += [jnp.where(low, tiles[0], pltpu.roll(tiles[1], CHUNK, axis=1)),
                               jnp.where(low, pltpu.roll(tiles[0], CHUNK, axis=1), tiles[1])]
                else:
                    pieces += [tiles[0], pltpu.roll(tiles[0], CHUNK, axis=1)]
        out = jnp.concatenate(pieces, axis=0).T
        r1 = min((c0 + 2) * CHUNK, rows)
        o_ref[c0 * CHUNK:r1] = out[:r1 - c0 * CHUNK].astype(o_ref.dtype)


def _swa_call(q, k_prev, k_cur, v_prev, v_cur, sinks, rows, same_array):
    b, n, _ = q.shape
    per = rows // WINDOW if same_array else 0
    cur = lambda w: pl.BlockSpec((None, rows, w), lambda bi, i: (bi, i, 0))
    prev = pl.BlockSpec((None, WINDOW, SWA_KV_W), lambda bi, i: (bi, jnp.maximum(i * per - 1, 0), 0))
    return pl.pallas_call(
        functools.partial(_swa_kernel, rows=rows, first_prev_valid=not same_array),
        grid=(b, n // rows),
        in_specs=[pl.BlockSpec(memory_space=pltpu.SMEM), cur(SWA_Q_W), prev, cur(SWA_KV_W), prev, cur(SWA_KV_W)],
        out_specs=cur(SWA_Q_W),
        out_shape=jax.ShapeDtypeStruct((b, n, SWA_Q_W), BF16),
        compiler_params=_params(("parallel", "parallel")),
        name="swa",
    )(sinks, q, k_prev, k_cur, v_prev, v_cur)


def _rope_tables(pos, head_dim):
    half = head_dim // 2
    inv = jnp.power(ROPE_THETA, -jnp.arange(half, dtype=F32) / half)
    ang = pos.astype(F32)[:, None] * inv[None, :]
    cos, sin = jnp.cos(ang), jnp.sin(ang)
    reps = LANES // head_dim
    return (jnp.tile(jnp.concatenate([cos, cos], axis=1), (1, reps)),
            jnp.tile(jnp.concatenate([-sin, sin], axis=1), (1, reps)))


def _prep_weights(norm_g, ffn_w_gate, ffn_w_up, ffn_w_down, even_w_in, even_w_out, odd_w_in, odd_w_out,
                  odd_q_norm, odd_k_norm):
    depth = norm_g.shape[0]
    prm = {"g": norm_g[:, :, None, :], "ffn": [], "even": [], "odd": []}
    for layer in range(depth):
        prm["ffn"].append([(ffn_w_gate[layer, s].astype(BF16), ffn_w_up[layer, s].astype(BF16),
                            ffn_w_down[layer, s].astype(BF16)) for s in range(2)])
    for i in range(even_w_in.shape[0]):
        prm["even"].append((even_w_in[i].astype(BF16), even_w_out[i].astype(BF16)))
    for i in range(odd_w_in.shape[0]):
        prm["odd"].append((odd_w_in[i].astype(BF16), odd_w_out[i].astype(BF16),
                           jnp.tile(odd_q_norm[i], SWA_HEADS)[None, :], jnp.tile(odd_k_norm[i], SWA_KV_HEADS)[None, :]))
    return prm


def _trunk(x, pos, prm, sinks, caches, tm, sb_qb, sb_kb, ret_cs, swa_rows):
    b, n, d = x.shape
    t = b * n
    xf = x.reshape(t, d)
    depth = len(prm["ffn"])
    tab_rows = n if n % tm == 0 else t
    tile_tab = lambda tb: tb if tab_rows == n else jnp.tile(tb, (b, 1))
    cos_r, sin_r = [tile_tab(tb) for tb in _rope_tables(pos, RET_QK_DIM)]
    cos_s, sin_s = [tile_tab(tb) for tb in _rope_tables(pos, SWA_HEAD_DIM)]
    idx = jnp.arange(sb_kb)
    negu = jnp.where(idx[:, None] >= idx[None, :], -1.0, 0.0).astype(BF16)
    sb_k, sb_v, ret, swa_k, swa_v = [], [], [], [], []
    mix, w_mix = [], None
    for layer in range(depth):
        i = layer // 2
        g = prm["g"][layer]
        ffn = prm["ffn"][layer]
        xf = _ffn_call(xf, mix, w_mix, g[0], *ffn[0], tm)
        if layer % 2 == 0:
            w_in, w_mix = prm["even"][i]
            qa, ka, kab, va, vab, qr, kr, vr, gt = _even_proj_call(xf, g[1], w_in, cos_r, sin_r, tm)
            sb_k.append(ka.reshape(b, n, SB_HEADS, SB_HEAD_DIM))
            sb_v.append(va.reshape(b, n, SB_HEADS, SB_HEAD_DIM))
            r3 = lambda a: a.reshape(b, n, a.shape[-1])
            if caches is None:
                k_all, v_all, q_pos0 = r3(kab), r3(vab), 0
                state0 = jnp.zeros((b, RET_HEADS, RET_QK_DIM, RET_V_DIM), F32)
            else:
                past = caches["sb_k"].shape[2]
                padded = -(-(past + n) // sb_kb) * sb_kb
                cat = lambda cache, new: jnp.pad(
                    jnp.concatenate([cache.reshape(b, past, SB_W).astype(BF16), r3(new)], axis=1),
                    ((0, 0), (0, padded - past - n), (0, 0)))
                k_all, v_all, q_pos0 = cat(caches["sb_k"][i], kab), cat(caches["sb_v"][i], vab), past
                state0 = caches["ret"][i]
            o_sb = _sb_call(r3(qa), k_all, v_all, negu, sb_qb, q_pos0)
            o_r, st = _ret_call(r3(qr), r3(kr), r3(vr), r3(gt), state0, ret_cs)
            ret.append(st)
            mix = [o_sb.reshape(t, SB_W), o_r.reshape(t, RET_V_W)]
        else:
            w_in, w_mix, qg, kg = prm["odd"][i]
            q, k, v = _odd_proj_call(xf, g[1], w_in, qg, kg, cos_s, sin_s, tm)
            r3 = lambda a: a.reshape(b, n, a.shape[-1])
            k3, v3 = r3(k), r3(v)
            if caches is None:
                o = _swa_call(r3(q), k3, k3, v3, v3, sinks[i], swa_rows, True)
                keep = min(WINDOW, n)
                k_rows, v_rows = k3[:, n - keep:], v3[:, n - keep:]
            else:
                kc = caches["swa_k"][i].reshape(b, -1, SWA_KV_W)
                vc = caches["swa_v"][i].reshape(b, -1, SWA_KV_W)
                o = _swa_call(r3(q), kc, k3, vc, v3, sinks[i], swa_rows, False)
                k_rows, v_rows = k3, v3
            swa_k.append(k_rows.reshape(b, -1, SWA_KV_HEADS, SWA_HEAD_DIM))
            swa_v.append(v_rows.reshape(b, -1, SWA_KV_HEADS, SWA_HEAD_DIM))
            mix = [o.reshape(t, SWA_Q_W)]
        xf = _ffn_call(xf, mix, w_mix, g[2], *ffn[1], tm)
        mix, w_mix = [], None
    stack = lambda parts: parts[0][None] if len(parts) == 1 else jnp.stack(parts)
    return (xf.reshape(b, n, d), stack(sb_k), stack(sb_v), stack(ret), stack(swa_k), stack(swa_v))


def kernel(x_prompt, x_sample, cache_sb_k, cache_sb_v, state_ret, cache_swa_k, cache_swa_v, norm_g, ffn_w_gate,
           ffn_w_up, ffn_w_down, even_w_in, even_w_out, odd_w_in, odd_w_out, odd_q_norm, odd_k_norm, odd_sinks):
    prm = _prep_weights(norm_g, ffn_w_gate, ffn_w_up, ffn_w_down, even_w_in, even_w_out, odd_w_in, odd_w_out,
                        odd_q_norm, odd_k_norm)
    past = cache_sb_k.shape[2]
    n_p = x_prompt.shape[1]
    n_s = x_sample.shape[1]
    pos_prompt = jnp.arange(n_p, dtype=jnp.int32)
    pos_sample = past + jnp.arange(n_s, dtype=jnp.int32)
    y_p, sb_k_p, sb_v_p, ret_p, swa_k_p, swa_v_p = _trunk(
        x_prompt, pos_prompt, prm, odd_sinks, None, tm=512, sb_qb=1024, sb_kb=256, ret_cs=256, swa_rows=512)
    caches = {"sb_k": cache_sb_k, "sb_v": cache_sb_v, "ret": state_ret, "swa_k": cache_swa_k, "swa_v": cache_swa_v}
    y_s, sb_k_s, sb_v_s, ret_s, swa_k_s, swa_v_s = _trunk(
        x_sample, pos_sample, prm, odd_sinks, caches, tm=x_sample.shape[0] * n_s, sb_qb=n_s, sb_kb=256,
        ret_cs=n_s, swa_rows=n_s)
    return (y_p, y_s, sb_k_p, sb_v_p, ret_p, swa_k_p, swa_v_p, sb_k_s, sb_v_s, ret_s, swa_k_s, swa_v_s)
```

```python
import functools
import math

import jax
import jax.numpy as jnp
from jax import lax
from jax.experimental import pallas as pl
from jax.experimental.pallas import tpu as pltpu

F32 = jnp.float32
BF16 = jnp.bfloat16

RMS_EPS = 1e-6
ROPE_THETA = 10000.0
CHUNK = 64
SB_HEADS = 8
SB_HEAD_DIM = 64
RET_HEADS = 4
RET_QK_DIM = 128
RET_V_DIM = 256
SWA_HEADS = 16
SWA_KV_HEADS = 4
SWA_HEAD_DIM = 64
WINDOW = 128

SB_W = SB_HEADS * SB_HEAD_DIM
RET_QK_W = RET_HEADS * RET_QK_DIM
RET_V_W = RET_HEADS * RET_V_DIM
SWA_Q_W = SWA_HEADS * SWA_HEAD_DIM
SWA_KV_W = SWA_KV_HEADS * SWA_HEAD_DIM
EVEN_CUTS = (0, SB_W, 2 * SB_W, 3 * SB_W, 3 * SB_W + RET_QK_W, 3 * SB_W + 2 * RET_QK_W,
             3 * SB_W + 2 * RET_QK_W + RET_V_W, 3 * SB_W + 2 * RET_QK_W + 2 * RET_V_W)

LANES = 128
FF_CHUNK = 256
VMEM_LIMIT = 56 * 1024 * 1024
LOG2E = math.log2(math.e)
SOFTPLUS2_LINEAR = 40.0
SWA_MASKED = -1e30
SB_DEAD_LOG2 = -170.0


def _dot(a, b):
    return jnp.dot(a, b, preferred_element_type=F32)


def _dot_nt(a, b):
    return lax.dot_general(a, b, (((1,), (1,)), ((), ())), preferred_element_type=F32)


def _dot_tn(a, b):
    return lax.dot_general(a, b, (((0,), (0,)), ((), ())), preferred_element_type=F32)


def _rms(x, g):
    return x * lax.rsqrt(jnp.mean(x * x, axis=-1, keepdims=True) + RMS_EPS) * g


def _full_spec(shape):
    nd = len(shape)
    return pl.BlockSpec(shape, lambda *_: (0,) * nd, pipeline_mode=pl.Buffered(1))


def _params(sem):
    return pltpu.CompilerParams(dimension_semantics=sem, vmem_limit_bytes=VMEM_LIMIT)


def _ffn_kernel(*refs, n_mix):
    x_ref = refs[0]
    mix = refs[1:1 + n_mix]
    wmix_ref = refs[1 + n_mix] if n_mix else None
    g_ref, wg_ref, wu_ref, wd_ref, o_ref, h_ref, act_ref = refs[1 + n_mix + bool(n_mix):]
    x = x_ref[...]
    r0 = 0
    for a_ref in mix:
        x = x + _dot(a_ref[...], wmix_ref[r0:r0 + a_ref.shape[1]])
        r0 += a_ref.shape[1]
    h_ref[...] = _rms(x, g_ref[...]).astype(BF16)
    for c in range(wg_ref.shape[1] // FF_CHUNK):
        cols = slice(c * FF_CHUNK, (c + 1) * FF_CHUNK)
        gate = _dot(h_ref[...], wg_ref[:, cols])
        up = _dot(h_ref[...], wu_ref[:, cols])
        act_ref[:, cols] = (gate * jax.nn.sigmoid(gate) * up).astype(BF16)
    o_ref[...] = x + 0.5 * _dot(act_ref[...], wd_ref[...])


def _ffn_call(x, mix, w_mix, g, ffn_w, layer, slot, tm):
    wg, wu, wd = ffn_w
    stack_spec = lambda w: pl.BlockSpec((None, None) + w.shape[2:], lambda i: (layer, slot, 0, 0),
                                        pipeline_mode=pl.Buffered(1))
    t, d = x.shape
    row = lambda w: pl.BlockSpec((tm, w), lambda i: (i, 0))
    in_specs = [row(d)]
    args = [x]
    for a in mix:
        in_specs.append(row(a.shape[1]))
        args.append(a)
    if mix:
        in_specs.append(_full_spec(w_mix.shape))
        args.append(w_mix)
    in_specs += [_full_spec(g.shape), stack_spec(wg), stack_spec(wu), stack_spec(wd)]
    args += [g, wg, wu, wd]
    return pl.pallas_call(
        functools.partial(_ffn_kernel, n_mix=len(mix)),
        grid=(t // tm,),
        in_specs=in_specs,
        out_specs=row(d),
        out_shape=jax.ShapeDtypeStruct((t, d), F32),
        scratch_shapes=[pltpu.VMEM((tm, d), BF16), pltpu.VMEM((tm, wg.shape[-1]), BF16)],
        compiler_params=_params(("parallel",)),
        name="ffn_mix%d" % len(mix),
    )(*args)


def _rope128(x, cos2, sin2):
    parts = []
    for hd in range(x.shape[1] // LANES):
        sl = x[:, hd * LANES:(hd + 1) * LANES]
        parts.append(sl * cos2 + pltpu.roll(sl, LANES // 2, axis=1) * sin2)
    return jnp.concatenate(parts, axis=1)


def _even_proj_kernel(x_ref, g_ref, w_ref, cos_ref, sin_ref,
                      qa_ref, ka_ref, kab_ref, va_ref, vab_ref, qr_ref, kr_ref, vr_ref, gt_ref):
    h = _rms(x_ref[...], g_ref[...]).astype(BF16)
    proj = lambda s: _dot(h, w_ref[:, EVEN_CUTS[s]:EVEN_CUTS[s + 1]])
    qa_ref[...] = (proj(0) * (LOG2E * SB_HEAD_DIM ** -0.5)).astype(BF16)
    ka = proj(1)
    ka_ref[...] = ka.reshape(ka_ref.shape)
    kab_ref[...] = ka.astype(BF16)
    va = proj(2)
    va_ref[...] = va.reshape(va_ref.shape)
    vab_ref[...] = va.astype(BF16)
    cos2 = cos_ref[...]
    sin2 = sin_ref[...]
    qr_ref[...] = _rope128(proj(3), cos2, sin2).astype(BF16)
    kr_ref[...] = (_rope128(proj(4), cos2, sin2) * (RET_QK_DIM ** -0.5)).astype(BF16)
    vr_ref[...] = proj(5).astype(BF16)
    gate = proj(6)
    gt_ref[...] = gate * jax.nn.sigmoid(gate)


def _even_proj_call(x, g, w_in, cos2, sin2, tm):
    t, d = x.shape
    n_tab = cos2.shape[0] // tm
    row = lambda w: pl.BlockSpec((tm, w), lambda i: (i, 0))
    tab = pl.BlockSpec((tm, LANES), lambda i: (i % n_tab, 0))
    per_head = (SB_HEADS, SB_HEAD_DIM)
    outs = [((SB_W,), BF16), (per_head, F32), ((SB_W,), BF16), (per_head, F32), ((SB_W,), BF16),
            ((RET_QK_W,), BF16), ((RET_QK_W,), BF16), ((RET_V_W,), BF16), ((RET_V_W,), F32)]
    return pl.pallas_call(
        _even_proj_kernel,
        grid=(t // tm,),
        in_specs=[row(d), _full_spec(g.shape), _full_spec(w_in.shape), tab, tab],
        out_specs=[pl.BlockSpec((tm,) + tail, lambda i, nd=len(tail): (i,) + (0,) * nd) for tail, _ in outs],
        out_shape=[jax.ShapeDtypeStruct((t,) + tail, dt) for tail, dt in outs],
        compiler_params=_params(("parallel",)),
        name="even_proj",
    )(x, g, w_in, cos2, sin2)


def _sb_kernel(q_ref, k_ref, v_ref, negu_ref, o_ref, qh_ref, c_ref, acc_ref, *, qb, kb, q_pos0):
    i = pl.program_id(2)
    n_clear = (q_pos0 + i * qb) // kb
    sq = min(qb, kb)
    nsub = qb // sq
    rows = lambda s: slice(s * sq, (s + 1) * sq)
    kstart = lambda j: pl.multiple_of(j * kb, kb)
    q = q_ref[...]
    first = lax.broadcasted_iota(jnp.int32, (qb, LANES), 1) < SB_HEAD_DIM
    qh_ref[0] = jnp.where(first, q, jnp.zeros_like(q))
    qh_ref[1] = jnp.where(first, jnp.zeros_like(q), q)

    def scores(hd, rws, start):
        return _dot_nt(qh_ref[hd, rws], k_ref[pl.ds(start, kb), :])

    def softplus2(z):
        return jnp.where(z > SOFTPLUS2_LINEAR, z, jnp.log2(1.0 + jnp.exp2(z)))

    def v_heads(start):
        vs = v_ref[pl.ds(start, kb), :]
        vfirst = lax.broadcasted_iota(jnp.int32, (kb, LANES), 1) < SB_HEAD_DIM
        return jnp.concatenate([jnp.where(vfirst, vs, jnp.zeros_like(vs)),
                                jnp.where(vfirst, jnp.zeros_like(vs), vs)], axis=0)

    prev = lambda s: kstart(jnp.maximum(n_clear + s - 1, 0))
    row = lax.broadcasted_iota(jnp.int32, (qb, kb), 0)
    below = lax.broadcasted_iota(jnp.int32, (qb, kb), 1) < (row & (sq - 1))
    has_prev = jnp.broadcast_to(n_clear > 0, (sq, kb))

    def apply_masks(t):
        parts = [jnp.where(below, t[:qb], 0.0), jnp.where(has_prev, t[qb:qb + sq], 0.0)]
        return jnp.concatenate(parts + ([t[qb + sq:]] if nsub > 1 else []), axis=0)

    w_diag, w_prev = [], []
    for hd in range(2):
        z = jnp.concatenate([scores(hd, rows(s), kstart(n_clear + s)) for s in range(nsub)]
                            + [scores(hd, rows(s), prev(s)) for s in range(nsub)], axis=0)
        sp = apply_masks(softplus2(z))
        tail = _dot(sp.astype(BF16), negu_ref[...])
        c_diag = tail[:qb, :1]
        w = apply_masks(jnp.exp2(z + tail + jnp.concatenate([jnp.zeros_like(c_diag), c_diag], axis=0)))
        c_ref[hd] = c_diag + tail[qb:, :1]
        w = w.astype(BF16)
        w_diag.append(w[:qb])
        w_prev.append(w[qb:])
    w_diag = jnp.concatenate(w_diag, axis=1)
    w_prev = jnp.concatenate(w_prev, axis=1)
    for s in range(nsub):
        acc_ref[rows(s)] = (_dot(w_diag[rows(s)], v_heads(kstart(n_clear + s)))
                            + _dot(w_prev[rows(s)], v_heads(prev(s))))

    def block(start, r0, r1):
        ws = []
        for hd in range(2):
            z = scores(hd, slice(r0, r1), start)
            tail = _dot(softplus2(z).astype(BF16), negu_ref[...])
            ws.append(jnp.exp2(z + tail + c_ref[hd, r0:r1]).astype(BF16))
            c_ref[hd, r0:r1] += tail[:, :1]
        acc_ref[r0:r1] += _dot(jnp.concatenate(ws, axis=1), v_heads(start))

    def alive(r0, r1):
        return (jnp.max(c_ref[:, r0:r1]) > SB_DEAD_LOG2).astype(jnp.int32)

    def run_blocks(j_first, j_last, r0, r1):
        def cond(carry):
            j, live = carry
            return jnp.logical_and(j >= j_last, live > 0)

        def body(carry):
            j, _ = carry
            block(kstart(j), r0, r1)
            return j - 1, alive(r0, r1)

        lax.while_loop(cond, body, (j_first, alive(r0, r1)))

    @pl.when(alive(0, qb) > 0)
    def _():
        for s in range(2, nsub):
            run_blocks(n_clear + s - 2, n_clear, s * sq, (s + 1) * sq)
        if nsub > 1:
            run_blocks(n_clear - 1, jnp.maximum(n_clear - 1, 0), sq, qb)
        run_blocks(n_clear - 2, 0, 0, qb)

    o_ref[...] = acc_ref[...].astype(o_ref.dtype)


def _sb_call(q, k, v, negu, qb, q_pos0):
    b, nq, w = q.shape
    nk = k.shape[1]
    kb = negu.shape[0]
    assert q_pos0 % kb == 0 and (qb % kb == 0 or nq == qb < kb) and nk % kb == 0 and nk >= q_pos0 + nq
    return pl.pallas_call(
        functools.partial(_sb_kernel, qb=qb, kb=kb, q_pos0=q_pos0),
        grid=(b, w // LANES, nq // qb),
        in_specs=[pl.BlockSpec((None, qb, LANES), lambda bi, p, i: (bi, i, p)),
                  pl.BlockSpec((None, nk, LANES), lambda bi, p, i: (bi, 0, p)),
                  pl.BlockSpec((None, nk, LANES), lambda bi, p, i: (bi, 0, p)),
                  _full_spec(negu.shape)],
        out_specs=pl.BlockSpec((None, qb, LANES), lambda bi, p, i: (bi, i, p)),
        out_shape=jax.ShapeDtypeStruct((b, nq, w), BF16),
        scratch_shapes=[pltpu.VMEM((2, qb, LANES), BF16), pltpu.VMEM((2, qb, 1), F32),
                        pltpu.VMEM((qb, LANES), F32)],
        compiler_params=_params(("parallel", "parallel", "parallel")),
        name="stick_breaking",
    )(q, k, v, negu)


def _ret_kernel(sdec_ref, q_ref, k_ref, v_ref, gt_ref, s0_ref, dec_ref, qdec_ref, kdec_ref,
                o_ref, sfin_ref, st_ref):
    c = pl.program_id(0)

    @pl.when(c == 0)
    def _():
        st_ref[...] = s0_ref[...]

    units = [(b, h) for b in range(q_ref.shape[0]) for h in range(RET_HEADS)]
    qk = lambda h: slice(h * RET_QK_DIM, (h + 1) * RET_QK_DIM)
    vw = lambda h: slice(h * RET_V_DIM, (h + 1) * RET_V_DIM)
    scores = [_dot_nt(q_ref[b, :, qk(h)], k_ref[b, :, qk(h)]) for b, h in units]
    cross = [_dot(q_ref[b, :, qk(h)], st_ref[b, h].astype(BF16)) for b, h in units]
    kw = [(k_ref[b, :, qk(h)].astype(F32) * kdec_ref[h]).astype(BF16) for b, h in units]
    grow = [_dot_tn(kw[u], v_ref[b, :, vw(h)]) for u, (b, h) in enumerate(units)]
    decayed = [(scores[u] * dec_ref[h]).astype(BF16) for u, (b, h) in enumerate(units)]
    inner = [_dot(decayed[u], v_ref[b, :, vw(h)]) for u, (b, h) in enumerate(units)]
    for u, (b, h) in enumerate(units):
        st_ref[b, h] = sdec_ref[h] * st_ref[b, h] + grow[u]
        o = inner[u] + cross[u] * qdec_ref[h]
        o = o * lax.rsqrt(jnp.mean(o * o, axis=-1, keepdims=True) + RMS_EPS)
        o_ref[b, :, vw(h)] = (o * gt_ref[b, :, vw(h)]).astype(o_ref.dtype)

    @pl.when(c == pl.num_programs(0) - 1)
    def _():
        sfin_ref[...] = st_ref[...]


def _ret_call(q, k, v, gate, state0, cs):
    b, n, _ = q.shape
    log_gamma = jnp.log1p(-jnp.exp2(-5.0 - jnp.arange(RET_HEADS, dtype=F32)))
    pos = jnp.arange(cs, dtype=F32)
    diff = pos[:, None] - pos[None, :]
    dec = jnp.where(diff >= 0, jnp.exp(log_gamma[:, None, None] * jnp.maximum(diff, 0.0)), 0.0)
    qdec = jnp.exp(log_gamma[:, None] * (pos + 1.0))[..., None]
    kdec = jnp.exp(log_gamma[:, None] * (cs - 1.0 - pos))[..., None]
    sdec = jnp.exp(log_gamma * cs)
    seq = lambda w: pl.BlockSpec((b, cs, w), lambda c: (0, c, 0))
    st = pl.BlockSpec((b, RET_HEADS, RET_QK_DIM, RET_V_DIM), lambda c: (0, 0, 0, 0))
    return pl.pallas_call(
        _ret_kernel,
        grid=(n // cs,),
        in_specs=[pl.BlockSpec(memory_space=pltpu.SMEM), seq(RET_QK_W), seq(RET_QK_W), seq(RET_V_W),
                  seq(RET_V_W), st, _full_spec(dec.shape), _full_spec(qdec.shape), _full_spec(kdec.shape)],
        out_specs=[seq(RET_V_W), st],
        out_shape=[jax.ShapeDtypeStruct((b, n, RET_V_W), BF16),
                   jax.ShapeDtypeStruct((b, RET_HEADS, RET_QK_DIM, RET_V_DIM), F32)],
        scratch_shapes=[pltpu.VMEM((b, RET_HEADS, RET_QK_DIM, RET_V_DIM), F32)],
        compiler_params=_params(("arbitrary",)),
        name="retention",
    )(sdec, q, k, v, gate, state0, dec, qdec, kdec)


def _head_norm_rope(t, bd, gain, cos4, sin4, first_half):
    parts = []
    for c in range(t.shape[1] // LANES):
        sq = t[:, c * LANES:(c + 1) * LANES]
        sq = sq * sq
        hi = sq.astype(BF16)
        lo = (sq - hi.astype(F32)).astype(BF16)
        parts.append(_dot(hi, bd) + _dot(lo, bd))
    ms = jnp.concatenate(parts, axis=1) * (1.0 / SWA_HEAD_DIM)
    y = t * lax.rsqrt(ms + RMS_EPS) * gain
    w = t.shape[1]
    quarter = SWA_HEAD_DIM // 2
    partner = jnp.where(first_half, pltpu.roll(y, w - quarter, axis=1), pltpu.roll(y, quarter, axis=1))
    return y * cos4 + partner * sin4


def _odd_proj_kernel(x_ref, g_ref, w_ref, bd_ref, qg_ref, kg_ref, cos_ref, sin_ref,
                     q_ref, k_ref, v_ref):
    h = _rms(x_ref[...], g_ref[...]).astype(BF16)
    bd = bd_ref[...]
    cos2 = cos_ref[...]
    sin2 = sin_ref[...]
    tm = cos2.shape[0]
    nq = SWA_Q_W // LANES
    nk = SWA_KV_W // LANES
    lane_q = lax.broadcasted_iota(jnp.int32, (tm, SWA_Q_W), 1)
    lane_k = lax.broadcasted_iota(jnp.int32, (tm, SWA_KV_W), 1)
    half = SWA_HEAD_DIM // 2
    q = _head_norm_rope(_dot(h, w_ref[:, :SWA_Q_W]), bd, qg_ref[...], jnp.concatenate([cos2] * nq, axis=1),
                        jnp.concatenate([sin2] * nq, axis=1), (lane_q % SWA_HEAD_DIM) < half)
    q_ref[...] = (q * (LOG2E * SWA_HEAD_DIM ** -0.5)).astype(BF16)
    k_ref[...] = _head_norm_rope(_dot(h, w_ref[:, SWA_Q_W:SWA_Q_W + SWA_KV_W]), bd, kg_ref[...], jnp.concatenate([cos2] * nk, axis=1),
                                 jnp.concatenate([sin2] * nk, axis=1), (lane_k % SWA_HEAD_DIM) < half)
    v_ref[...] = _dot(h, w_ref[:, SWA_Q_W + SWA_KV_W:])


def _odd_proj_call(x, g, w_in, qg, kg, cos2, sin2, tm):
    t, d = x.shape
    n_tab = cos2.shape[0] // tm
    head = jnp.arange(LANES) // SWA_HEAD_DIM
    bd = (head[:, None] == head[None, :]).astype(BF16)
    row = lambda w: pl.BlockSpec((tm, w), lambda i: (i, 0))
    tab = pl.BlockSpec((tm, LANES), lambda i: (i % n_tab, 0))
    ins = [g, w_in, bd, qg, kg]
    return pl.pallas_call(
        _odd_proj_kernel,
        grid=(t // tm,),
        in_specs=[row(d)] + [_full_spec(a.shape) for a in ins] + [tab, tab],
        out_specs=[row(SWA_Q_W), row(SWA_KV_W), row(SWA_KV_W)],
        out_shape=[jax.ShapeDtypeStruct((t, SWA_Q_W), BF16), jax.ShapeDtypeStruct((t, SWA_KV_W), F32),
                   jax.ShapeDtypeStruct((t, SWA_KV_W), F32)],
        compiler_params=_params(("parallel",)),
        name="odd_proj",
    )(x, *ins, cos2, sin2)


def _swa_kernel(sink_ref, q_ref, kp_ref, kc_ref, vp_ref, vc_ref, o_ref, *, rows, first_prev_valid):
    i = pl.program_id(1)
    nch = rows // CHUNK
    wk = WINDOW + CHUNK
    nk = WINDOW + rows
    grp = SWA_HEADS // SWA_KV_HEADS
    gw = grp * CHUNK
    key_low = lax.broadcasted_iota(jnp.int32, (nk, LANES), 1) < SWA_HEAD_DIM
    low = lax.broadcasted_iota(jnp.int32, (CHUNK, LANES), 1) < SWA_HEAD_DIM
    if not first_prev_valid:
        krow = lax.broadcasted_iota(jnp.int32, (nk, LANES), 0)
        klane = lax.broadcasted_iota(jnp.int32, (nk, LANES), 1)
        k_bias = jnp.where((krow < WINDOW) & (klane == 0) & (i == 0), SWA_MASKED, 0.0).astype(BF16)
        q_one = jnp.where(lax.broadcasted_iota(jnp.int32, (gw, LANES), 1) == 0, 1.0, 0.0).astype(BF16)

    s, vj_t = [], []
    for j in range(SWA_KV_HEADS):
        pair, odd = divmod(j, 2)
        lanes = slice(pair * LANES, (pair + 1) * LANES)
        k_pair = jnp.concatenate([kp_ref[:, lanes], kc_ref[:, lanes]], axis=0)
        own = jnp.where(key_low, 0.0, k_pair) if odd else jnp.where(key_low, k_pair, 0.0)
        k_both = (own + pltpu.roll(own, SWA_HEAD_DIM, axis=1)).astype(BF16)
        v_t = jnp.concatenate([vp_ref[:, lanes], vc_ref[:, lanes]], axis=0).T
        vj_t.append(v_t[odd * SWA_HEAD_DIM:(odd + 1) * SWA_HEAD_DIM].astype(BF16))
        if not first_prev_valid:
            k_both = jnp.concatenate([k_both, k_bias], axis=1)
        for c in range(nch):
            q_four = []
            for qp in (2 * j, 2 * j + 1):
                q_pair = q_ref[c * CHUNK:(c + 1) * CHUNK, qp * LANES:(qp + 1) * LANES]
                zero = jnp.zeros_like(q_pair)
                q_four += [jnp.where(low, q_pair, zero), jnp.where(low, zero, q_pair)]
            q_four = jnp.concatenate(q_four, axis=0)
            if not first_prev_valid:
                q_four = jnp.concatenate([q_four, q_one], axis=1)
            s.append(_dot_nt(k_both[c * CHUNK:c * CHUNK + wk], q_four))
    s = jnp.concatenate(s, axis=1)
    sink = jnp.concatenate([jnp.full((1, CHUNK), sink_ref[j * grp + g] * LOG2E, F32)
                            for j in range(SWA_KV_HEADS) for _ in range(nch) for g in range(grp)], axis=1)
    m = jnp.maximum(jnp.max(s, axis=0, keepdims=True), sink)
    p = jnp.exp2(s - m)
    inv = 1.0 / (jnp.sum(p, axis=0, keepdims=True) + jnp.exp2(sink - m))
    p = p.astype(BF16)
    o_t = []
    for j in range(SWA_KV_HEADS):
        for c in range(nch):
            cols = slice((j * nch + c) * gw, (j * nch + c + 1) * gw)
            o_t.append(_dot(vj_t[j][:, c * CHUNK:c * CHUNK + wk], p[:, cols]) * inv[:, cols])
    for c0 in range(0, nch, 2):
        pieces = []
        for j in range(SWA_KV_HEADS):
            for t in range(grp // 2):
                tiles = [o_t[j * nch + c][:, t * LANES:(t + 1) * LANES] for c in range(c0, min(c0 + 2, nch))]
                if len(tiles) == 2:
                    pieces += [jnp.where(low, tiles[0], pltpu.roll(tiles[1], CHUNK, axis=1)),
                               jnp.where(low, pltpu.roll(tiles[0], CHUNK, axis=1), tiles[1])]
                else:
                    pieces += [tiles[0], pltpu.roll(tiles[0], CHUNK, axis=1)]
        out = jnp.concatenate(pieces, axis=0).T
        r1 = min((c0 + 2) * CHUNK, rows)
        o_ref[c0 * CHUNK:r1] = out[:r1 - c0 * CHUNK].astype(o_ref.dtype)


def _swa_call(q, k_prev, k_cur, v_prev, v_cur, sinks, rows, same_array):
    b, n, _ = q.shape
    per = rows // WINDOW if same_array else 0
    cur = lambda w: pl.BlockSpec((None, rows, w), lambda bi, i: (bi, i, 0))
    prev = pl.BlockSpec((None, WINDOW, SWA_KV_W), lambda bi, i: (bi, jnp.maximum(i * per - 1, 0), 0))
    return pl.pallas_call(
        functools.partial(_swa_kernel, rows=rows, first_prev_valid=not same_array),
        grid=(b, n // rows),
        in_specs=[pl.BlockSpec(memory_space=pltpu.SMEM), cur(SWA_Q_W), prev, cur(SWA_KV_W), prev, cur(SWA_KV_W)],
        out_specs=cur(SWA_Q_W),
        out_shape=jax.ShapeDtypeStruct((b, n, SWA_Q_W), BF16),
        compiler_params=_params(("parallel", "parallel")),
        name="swa",
    )(sinks, q, k_prev, k_cur, v_prev, v_cur)


def _rope_tables(pos0, n, head_dim):
    half = head_dim // 2
    lane = jnp.arange(LANES)
    inv = jnp.power(ROPE_THETA, -(lane % half).astype(F32) / half)
    sign = jnp.where(lane % head_dim < half, -1.0, 1.0)
    fine = min(n, LANES)
    ang_c = (pos0 + fine * jnp.arange(n // fine)).astype(F32)[:, None] * inv[None, :]
    ang_f = jnp.arange(fine, dtype=F32)[:, None] * inv[None, :]
    cos_c, sin_c = jnp.cos(ang_c)[:, None, :], jnp.sin(ang_c)[:, None, :]
    cos_f, sin_f = jnp.cos(ang_f)[None, :, :], jnp.sin(ang_f)[None, :, :]
    return ((cos_c * cos_f - sin_c * sin_f).reshape(n, LANES),
            ((sin_c * cos_f + cos_c * sin_f) * sign).reshape(n, LANES))


def _prep_weights(norm_g, ffn_w_gate, ffn_w_up, ffn_w_down, even_w_in, even_w_out, odd_w_in, odd_w_out,
                  odd_q_norm, odd_k_norm):
    depth = norm_g.shape[0]
    prm = {"g": norm_g[:, :, None, :], "even": [], "odd": [], "depth": depth,
           "ffn": (ffn_w_gate.astype(BF16), ffn_w_up.astype(BF16), ffn_w_down.astype(BF16))}
    for i in range(even_w_in.shape[0]):
        prm["even"].append((even_w_in[i].astype(BF16), even_w_out[i].astype(BF16)))
    for i in range(odd_w_in.shape[0]):
        prm["odd"].append((odd_w_in[i].astype(BF16), odd_w_out[i].astype(BF16),
                           jnp.tile(odd_q_norm[i], SWA_HEADS)[None, :], jnp.tile(odd_k_norm[i], SWA_KV_HEADS)[None, :]))
    return prm


def _trunk(x, pos0, prm, sinks, caches, tm, sb_qb, sb_kb, ret_cs, swa_rows):
    b, n, d = x.shape
    t = b * n
    xf = x.reshape(t, d)
    depth = prm["depth"]
    tab_rows = n if n % tm == 0 else t
    tile_tab = lambda tb: tb if tab_rows == n else jnp.tile(tb, (b, 1))
    cos_r, sin_r = [tile_tab(tb) for tb in _rope_tables(pos0, n, RET_QK_DIM)]
    cos_s, sin_s = [tile_tab(tb) for tb in _rope_tables(pos0, n, SWA_HEAD_DIM)]
    idx = jnp.arange(sb_kb)
    negu = jnp.where(idx[:, None] >= idx[None, :], -1.0, 0.0).astype(BF16)
    sb_k, sb_v, ret, swa_k, swa_v = [], [], [], [], []
    mix, w_mix = [], None
    for layer in range(depth):
        i = layer // 2
        g = prm["g"][layer]
        xf = _ffn_call(xf, mix, w_mix, g[0], prm["ffn"], layer, 0, tm)
        if layer % 2 == 0:
            w_in, w_mix = prm["even"][i]
            qa, ka, kab, va, vab, qr, kr, vr, gt = _even_proj_call(xf, g[1], w_in, cos_r, sin_r, tm)
            sb_k.append(ka.reshape(b, n, SB_HEADS, SB_HEAD_DIM))
            sb_v.append(va.reshape(b, n, SB_HEADS, SB_HEAD_DIM))
            r3 = lambda a: a.reshape(b, n, a.shape[-1])
            if caches is None:
                k_all, v_all, q_pos0 = r3(kab), r3(vab), 0
                state0 = jnp.zeros((b, RET_HEADS, RET_QK_DIM, RET_V_DIM), F32)
            else:
                past = caches["sb_k"].shape[2]
                padded = -(-(past + n) // sb_kb) * sb_kb
                cat = lambda cache, new: jnp.pad(
                    jnp.concatenate([cache.reshape(b, past, SB_W).astype(BF16), r3(new)], axis=1),
                    ((0, 0), (0, padded - past - n), (0, 0)))
                k_all, v_all, q_pos0 = cat(caches["sb_k"][i], kab), cat(caches["sb_v"][i], vab), past
                state0 = caches["ret"][i]
            o_sb = _sb_call(r3(qa), k_all, v_all, negu, sb_qb, q_pos0)
            o_r, st = _ret_call(r3(qr), r3(kr), r3(vr), r3(gt), state0, ret_cs)
            ret.append(st)
            mix = [o_sb.reshape(t, SB_W), o_r.reshape(t, RET_V_W)]
        else:
            w_in, w_mix, qg, kg = prm["odd"][i]
            q, k, v = _odd_proj_call(xf, g[1], w_in, qg, kg, cos_s, sin_s, tm)
            r3 = lambda a: a.reshape(b, n, a.shape[-1])
            k3, v3 = r3(k), r3(v)
            if caches is None:
                o = _swa_call(r3(q), k3, k3, v3, v3, sinks[i], swa_rows, True)
                keep = min(WINDOW, n)
                k_rows, v_rows = k3[:, n - keep:], v3[:, n - keep:]
            else:
                kc = caches["swa_k"][i].reshape(b, -1, SWA_KV_W)
                vc = caches["swa_v"][i].reshape(b, -1, SWA_KV_W)
                o = _swa_call(r3(q), kc, k3, vc, v3, sinks[i], swa_rows, False)
                k_rows, v_rows = k3, v3
            swa_k.append(k_rows.reshape(b, -1, SWA_KV_HEADS, SWA_HEAD_DIM))
            swa_v.append(v_rows.reshape(b, -1, SWA_KV_HEADS, SWA_HEAD_DIM))
            mix = [o.reshape(t, SWA_Q_W)]
        xf = _ffn_call(xf, mix, w_mix, g[2], prm["ffn"], layer, 1, tm)
        mix, w_mix = [], None
    stack = lambda parts: parts[0][None] if len(parts) == 1 else jnp.stack(parts)
    return (xf.reshape(b, n, d), stack(sb_k), stack(sb_v), stack(ret), stack(swa_k), stack(swa_v))


def kernel(x_prompt, x_sample, cache_sb_k, cache_sb_v, state_ret, cache_swa_k, cache_swa_v, norm_g, ffn_w_gate,
           ffn_w_up, ffn_w_down, even_w_in, even_w_out, odd_w_in, odd_w_out, odd_q_norm, odd_k_norm, odd_sinks):
    prm = _prep_weights(norm_g, ffn_w_gate, ffn_w_up, ffn_w_down, even_w_in, even_w_out, odd_w_in, odd_w_out,
                        odd_q_norm, odd_k_norm)
    past = cache_sb_k.shape[2]
    n_s = x_sample.shape[1]
    y_p, sb_k_p, sb_v_p, ret_p, swa_k_p, swa_v_p = _trunk(
        x_prompt, 0, prm, odd_sinks, None, tm=512, sb_qb=1024, sb_kb=256, ret_cs=256, swa_rows=512)
    caches = {"sb_k": cache_sb_k, "sb_v": cache_sb_v, "ret": state_ret, "swa_k": cache_swa_k, "swa_v": cache_swa_v}
    y_s, sb_k_s, sb_v_s, ret_s, swa_k_s, swa_v_s = _trunk(
        x_sample, past, prm, odd_sinks, caches, tm=x_sample.shape[0] * n_s, sb_qb=n_s, sb_kb=256,
        ret_cs=n_s, swa_rows=n_s)
    return (y_p, y_s, sb_k_p, sb_v_p, ret_p, swa_k_p, swa_v_p, sb_k_s, sb_v_s, ret_s, swa_k_s, swa_v_s)
```

```python
import functools
import math

import jax
import jax.numpy as jnp
from jax import lax
from jax.experimental import pallas as pl
from jax.experimental.pallas import tpu as pltpu

F32 = jnp.float32
BF16 = jnp.bfloat16

RMS_EPS = 1e-6
ROPE_THETA = 10000.0
CHUNK = 64
SB_HEADS = 8
SB_HEAD_DIM = 64
RET_HEADS = 4
RET_QK_DIM = 128
RET_V_DIM = 256
SWA_HEADS = 16
SWA_KV_HEADS = 4
SWA_HEAD_DIM = 64
WINDOW = 128

SB_W = SB_HEADS * SB_HEAD_DIM
RET_QK_W = RET_HEADS * RET_QK_DIM
RET_V_W = RET_HEADS * RET_V_DIM
SWA_Q_W = SWA_HEADS * SWA_HEAD_DIM
SWA_KV_W = SWA_KV_HEADS * SWA_HEAD_DIM
EVEN_CUTS = (0, SB_W, 2 * SB_W, 3 * SB_W, 3 * SB_W + RET_QK_W, 3 * SB_W + 2 * RET_QK_W,
             3 * SB_W + 2 * RET_QK_W + RET_V_W, 3 * SB_W + 2 * RET_QK_W + 2 * RET_V_W)

LANES = 128
MXU_TILE = 256
FF_CHUNK = MXU_TILE
VMEM_LIMIT = 56 * 1024 * 1024
LOG2E = math.log2(math.e)
SOFTPLUS2_LINEAR = 40.0
SWA_MASKED = -1e30
SB_DEAD_LOG2 = -170.0


def _dot(a, b):
    return jnp.dot(a, b, preferred_element_type=F32)


def _dot_nt(a, b):
    return lax.dot_general(a, b, (((1,), (1,)), ((), ())), preferred_element_type=F32)


def _dot_tn(a, b):
    return lax.dot_general(a, b, (((0,), (0,)), ((), ())), preferred_element_type=F32)


def _rms(x, g):
    return x * lax.rsqrt(jnp.mean(x * x, axis=-1, keepdims=True) + RMS_EPS) * g


def _full_spec(shape):
    nd = len(shape)
    return pl.BlockSpec(shape, lambda *_: (0,) * nd, pipeline_mode=pl.Buffered(1))


def _params(sem):
    return pltpu.CompilerParams(dimension_semantics=sem, vmem_limit_bytes=VMEM_LIMIT)


def _ffn_kernel(*refs, n_mix):
    x_ref = refs[0]
    mix = refs[1:1 + n_mix]
    wmix_ref = refs[1 + n_mix] if n_mix else None
    g_ref, wg_ref, wu_ref, wd_ref, o_ref, h_ref, act_ref = refs[1 + n_mix + bool(n_mix):]
    x = x_ref[...]
    r0 = 0
    for a_ref in mix:
        x = x + _dot(a_ref[...], wmix_ref[r0:r0 + a_ref.shape[1]])
        r0 += a_ref.shape[1]
    h_ref[...] = _rms(x, g_ref[...]).astype(BF16)
    for c in range(wg_ref.shape[1] // FF_CHUNK):
        cols = slice(c * FF_CHUNK, (c + 1) * FF_CHUNK)
        gate = _dot(h_ref[...], wg_ref[:, cols])
        up = _dot(h_ref[...], wu_ref[:, cols])
        act_ref[:, cols] = (gate * jax.nn.sigmoid(gate) * up).astype(BF16)
    o_ref[...] = x + 0.5 * _dot(act_ref[...], wd_ref[...])


def _ffn_call(x, mix, w_mix, g, ffn_w, layer, slot, tm):
    wg, wu, wd = ffn_w
    stack_spec = lambda w: pl.BlockSpec((None, None) + w.shape[2:], lambda i: (layer, slot, 0, 0),
                                        pipeline_mode=pl.Buffered(1))
    t, d = x.shape
    row = lambda w: pl.BlockSpec((tm, w), lambda i: (i, 0))
    in_specs = [row(d)]
    args = [x]
    for a in mix:
        in_specs.append(row(a.shape[1]))
        args.append(a)
    if mix:
        in_specs.append(_full_spec(w_mix.shape))
        args.append(w_mix)
    in_specs += [_full_spec(g.shape), stack_spec(wg), stack_spec(wu), stack_spec(wd)]
    args += [g, wg, wu, wd]
    return pl.pallas_call(
        functools.partial(_ffn_kernel, n_mix=len(mix)),
        grid=(t // tm,),
        in_specs=in_specs,
        out_specs=row(d),
        out_shape=jax.ShapeDtypeStruct((t, d), F32),
        scratch_shapes=[pltpu.VMEM((tm, d), BF16), pltpu.VMEM((tm, wg.shape[-1]), BF16)],
        compiler_params=_params(("parallel",)),
        name="ffn_mix%d" % len(mix),
    )(*args)


def _rope128(x, cos2, sin2):
    parts = []
    for hd in range(x.shape[1] // LANES):
        sl = x[:, hd * LANES:(hd + 1) * LANES]
        parts.append(sl * cos2 + pltpu.roll(sl, LANES // 2, axis=1) * sin2)
    return jnp.concatenate(parts, axis=1)


def _even_proj_kernel(x_ref, g_ref, w_ref, cos_ref, sin_ref,
                      qa_ref, ka_ref, kab_ref, va_ref, vab_ref, qr_ref, kr_ref, vr_ref, gt_ref):
    h = _rms(x_ref[...], g_ref[...]).astype(BF16)
    proj = lambda s: _dot(h, w_ref[:, EVEN_CUTS[s]:EVEN_CUTS[s + 1]])
    qa_ref[...] = (proj(0) * (LOG2E * SB_HEAD_DIM ** -0.5)).astype(BF16)
    ka = proj(1)
    ka_ref[...] = ka.reshape(ka_ref.shape)
    kab_ref[...] = ka.astype(BF16)
    va = proj(2)
    va_ref[...] = va.reshape(va_ref.shape)
    vab_ref[...] = va.astype(BF16)
    cos2 = cos_ref[...]
    sin2 = sin_ref[...]
    qr_ref[...] = _rope128(proj(3), cos2, sin2).astype(BF16)
    kr_ref[...] = (_rope128(proj(4), cos2, sin2) * (RET_QK_DIM ** -0.5)).astype(BF16)
    vr_ref[...] = proj(5).astype(BF16)
    gate = proj(6)
    gt_ref[...] = gate * jax.nn.sigmoid(gate)


def _even_proj_call(x, g, w_in, cos2, sin2, tm):
    t, d = x.shape
    n_tab = cos2.shape[0] // tm
    row = lambda w: pl.BlockSpec((tm, w), lambda i: (i, 0))
    tab = pl.BlockSpec((tm, LANES), lambda i: (i % n_tab, 0))
    per_head = (SB_HEADS, SB_HEAD_DIM)
    outs = [((SB_W,), BF16), (per_head, F32), ((SB_W,), BF16), (per_head, F32), ((SB_W,), BF16),
            ((RET_QK_W,), BF16), ((RET_QK_W,), BF16), ((RET_V_W,), BF16), ((RET_V_W,), F32)]
    return pl.pallas_call(
        _even_proj_kernel,
        grid=(t // tm,),
        in_specs=[row(d), _full_spec(g.shape), _full_spec(w_in.shape), tab, tab],
        out_specs=[pl.BlockSpec((tm,) + tail, lambda i, nd=len(tail): (i,) + (0,) * nd) for tail, _ in outs],
        out_shape=[jax.ShapeDtypeStruct((t,) + tail, dt) for tail, dt in outs],
        compiler_params=_params(("parallel",)),
        name="even_proj",
    )(x, g, w_in, cos2, sin2)


def _sb_kernel(q_ref, k_ref, v_ref, negu_ref, o_ref, qh_ref, c_ref, acc_ref, *, qb, kb, q_pos0):
    i = pl.program_id(2)
    n_clear = (q_pos0 + i * qb) // kb
    sq = min(qb, kb)
    nsub = qb // sq
    rows = lambda s: slice(s * sq, (s + 1) * sq)
    kstart = lambda j: pl.multiple_of(j * kb, kb)
    q = q_ref[...]
    first = lax.broadcasted_iota(jnp.int32, (qb, LANES), 1) < SB_HEAD_DIM
    qh_ref[0] = jnp.where(first, q, jnp.zeros_like(q))
    qh_ref[1] = jnp.where(first, jnp.zeros_like(q), q)

    def scores(hd, rws, start):
        return _dot_nt(qh_ref[hd, rws], k_ref[pl.ds(start, kb), :])

    def softplus2(z):
        return jnp.where(z > SOFTPLUS2_LINEAR, z, jnp.log2(1.0 + jnp.exp2(z)))

    def v_heads(start):
        vs = v_ref[pl.ds(start, kb), :]
        vfirst = lax.broadcasted_iota(jnp.int32, (kb, LANES), 1) < SB_HEAD_DIM
        return jnp.concatenate([jnp.where(vfirst, vs, jnp.zeros_like(vs)),
                                jnp.where(vfirst, jnp.zeros_like(vs), vs)], axis=0)

    prev = lambda s: kstart(jnp.maximum(n_clear + s - 1, 0))
    row = lax.broadcasted_iota(jnp.int32, (qb, kb), 0)
    below = lax.broadcasted_iota(jnp.int32, (qb, kb), 1) < (row & (sq - 1))
    has_prev = jnp.broadcast_to(n_clear > 0, (sq, kb))

    def apply_masks(t):
        parts = [jnp.where(below, t[:qb], 0.0), jnp.where(has_prev, t[qb:qb + sq], 0.0)]
        return jnp.concatenate(parts + ([t[qb + sq:]] if nsub > 1 else []), axis=0)

    w_diag, w_prev = [], []
    for hd in range(2):
        z = jnp.concatenate([scores(hd, rows(s), kstart(n_clear + s)) for s in range(nsub)]
                            + [scores(hd, rows(s), prev(s)) for s in range(nsub)], axis=0)
        sp = apply_masks(softplus2(z))
        tail = _dot(sp.astype(BF16), negu_ref[...])
        c_diag = tail[:qb, :1]
        w = apply_masks(jnp.exp2(z + tail + jnp.concatenate([jnp.zeros_like(c_diag), c_diag], axis=0)))
        c_ref[hd] = c_diag + tail[qb:, :1]
        w = w.astype(BF16)
        w_diag.append(w[:qb])
        w_prev.append(w[qb:])
    w_diag = jnp.concatenate(w_diag, axis=1)
    w_prev = jnp.concatenate(w_prev, axis=1)
    for s in range(nsub):
        acc_ref[rows(s)] = (_dot(w_diag[rows(s)], v_heads(kstart(n_clear + s)))
                            + _dot(w_prev[rows(s)], v_heads(prev(s))))

    def block(start, r0, r1):
        ws = []
        for hd in range(2):
            z = scores(hd, slice(r0, r1), start)
            tail = _dot(softplus2(z).astype(BF16), negu_ref[...])
            ws.append(jnp.exp2(z + tail + c_ref[hd, r0:r1]).astype(BF16))
            c_ref[hd, r0:r1] += tail[:, :1]
        acc_ref[r0:r1] += _dot(jnp.concatenate(ws, axis=1), v_heads(start))

    def alive(r0, r1):
        return (jnp.max(c_ref[:, r0:r1]) > SB_DEAD_LOG2).astype(jnp.int32)

    def run_blocks(j_first, j_last, r0, r1):
        def cond(carry):
            j, live = carry
            return jnp.logical_and(j >= j_last, live > 0)

        def body(carry):
            j, _ = carry
            block(kstart(j), r0, r1)
            return j - 1, alive(r0, r1)

        lax.while_loop(cond, body, (j_first, alive(r0, r1)))

    @pl.when(alive(0, qb) > 0)
    def _():
        for s in range(2, nsub):
            run_blocks(n_clear + s - 2, n_clear, s * sq, (s + 1) * sq)
        if nsub > 1:
            run_blocks(n_clear - 1, jnp.maximum(n_clear - 1, 0), sq, qb)
        run_blocks(n_clear - 2, 0, 0, qb)

    o_ref[...] = acc_ref[...].astype(o_ref.dtype)


def _sb_call(q, k, v, negu, qb, q_pos0):
    b, nq, w = q.shape
    nk = k.shape[1]
    kb = negu.shape[0]
    assert q_pos0 % kb == 0 and (qb % kb == 0 or nq == qb < kb) and nk % kb == 0 and nk >= q_pos0 + nq
    return pl.pallas_call(
        functools.partial(_sb_kernel, qb=qb, kb=kb, q_pos0=q_pos0),
        grid=(b, w // LANES, nq // qb),
        in_specs=[pl.BlockSpec((None, qb, LANES), lambda bi, p, i: (bi, i, p)),
                  pl.BlockSpec((None, nk, LANES), lambda bi, p, i: (bi, 0, p)),
                  pl.BlockSpec((None, nk, LANES), lambda bi, p, i: (bi, 0, p)),
                  _full_spec(negu.shape)],
        out_specs=pl.BlockSpec((None, qb, LANES), lambda bi, p, i: (bi, i, p)),
        out_shape=jax.ShapeDtypeStruct((b, nq, w), BF16),
        scratch_shapes=[pltpu.VMEM((2, qb, LANES), BF16), pltpu.VMEM((2, qb, 1), F32),
                        pltpu.VMEM((qb, LANES), F32)],
        compiler_params=_params(("parallel", "parallel", "parallel")),
        name="stick_breaking",
    )(q, k, v, negu)


def _ret_kernel(sdec_ref, q_ref, k_ref, v_ref, gt_ref, s0_ref, dec_ref, qdec_ref, kdec_ref,
                o_ref, sfin_ref, st_ref):
    c = pl.program_id(0)

    @pl.when(c == 0)
    def _():
        st_ref[...] = s0_ref[...]

    units = [(b, h) for b in range(q_ref.shape[0]) for h in range(RET_HEADS)]
    qk = lambda h: slice(h * RET_QK_DIM, (h + 1) * RET_QK_DIM)
    vw = lambda h: slice(h * RET_V_DIM, (h + 1) * RET_V_DIM)
    scores = [_dot_nt(q_ref[b, :, qk(h)], k_ref[b, :, qk(h)]) for b, h in units]
    cross = [_dot(q_ref[b, :, qk(h)], st_ref[b, h].astype(BF16)) for b, h in units]
    kw = [(k_ref[b, :, qk(h)].astype(F32) * kdec_ref[h]).astype(BF16) for b, h in units]
    grow = [_dot_tn(kw[u], v_ref[b, :, vw(h)]) for u, (b, h) in enumerate(units)]
    decayed = [(scores[u] * dec_ref[h]).astype(BF16) for u, (b, h) in enumerate(units)]
    inner = [_dot(decayed[u], v_ref[b, :, vw(h)]) for u, (b, h) in enumerate(units)]
    for u, (b, h) in enumerate(units):
        st_ref[b, h] = sdec_ref[h] * st_ref[b, h] + grow[u]
        o = inner[u] + cross[u] * qdec_ref[h]
        o = o * lax.rsqrt(jnp.mean(o * o, axis=-1, keepdims=True) + RMS_EPS)
        o_ref[b, :, vw(h)] = (o * gt_ref[b, :, vw(h)]).astype(o_ref.dtype)

    @pl.when(c == pl.num_programs(0) - 1)
    def _():
        sfin_ref[...] = st_ref[...]


def _ret_call(q, k, v, gate, state0, cs):
    b, n, _ = q.shape
    log_gamma = jnp.log1p(-jnp.exp2(-5.0 - jnp.arange(RET_HEADS, dtype=F32)))
    pos = jnp.arange(cs, dtype=F32)
    diff = pos[:, None] - pos[None, :]
    dec = jnp.where(diff >= 0, jnp.exp(log_gamma[:, None, None] * jnp.maximum(diff, 0.0)), 0.0)
    qdec = jnp.exp(log_gamma[:, None] * (pos + 1.0))[..., None]
    kdec = jnp.exp(log_gamma[:, None] * (cs - 1.0 - pos))[..., None]
    sdec = jnp.exp(log_gamma * cs)
    seq = lambda w: pl.BlockSpec((b, cs, w), lambda c: (0, c, 0))
    st = pl.BlockSpec((b, RET_HEADS, RET_QK_DIM, RET_V_DIM), lambda c: (0, 0, 0, 0))
    return pl.pallas_call(
        _ret_kernel,
        grid=(n // cs,),
        in_specs=[pl.BlockSpec(memory_space=pltpu.SMEM), seq(RET_QK_W), seq(RET_QK_W), seq(RET_V_W),
                  seq(RET_V_W), st, _full_spec(dec.shape), _full_spec(qdec.shape), _full_spec(kdec.shape)],
        out_specs=[seq(RET_V_W), st],
        out_shape=[jax.ShapeDtypeStruct((b, n, RET_V_W), BF16),
                   jax.ShapeDtypeStruct((b, RET_HEADS, RET_QK_DIM, RET_V_DIM), F32)],
        scratch_shapes=[pltpu.VMEM((b, RET_HEADS, RET_QK_DIM, RET_V_DIM), F32)],
        compiler_params=_params(("arbitrary",)),
        name="retention",
    )(sdec, q, k, v, gate, state0, dec, qdec, kdec)


def _head_norm_rope(t, bd, gain, cos2, sin2):
    wide = bd.shape[0]
    parts = []
    for c in range(t.shape[1] // wide):
        sq = t[:, c * wide:(c + 1) * wide]
        sq = sq * sq
        hi = sq.astype(BF16)
        lo = (sq - hi.astype(F32)).astype(BF16)
        parts.append(_dot(hi, bd) + _dot(lo, bd))
    ms = jnp.concatenate(parts, axis=1) * (1.0 / SWA_HEAD_DIM)
    y = t * lax.rsqrt(ms + RMS_EPS) * gain
    w = t.shape[1]
    reps = w // LANES
    quarter = SWA_HEAD_DIM // 2
    first_half = (lax.broadcasted_iota(jnp.int32, t.shape, 1) % SWA_HEAD_DIM) < quarter
    partner = jnp.where(first_half, pltpu.roll(y, w - quarter, axis=1), pltpu.roll(y, quarter, axis=1))
    return y * jnp.concatenate([cos2] * reps, axis=1) + partner * jnp.concatenate([sin2] * reps, axis=1)


def _odd_proj_kernel(x_ref, g_ref, w_ref, bd_ref, gain_ref, cos_ref, sin_ref, q_ref, k_ref, v_ref):
    h = _rms(x_ref[...], g_ref[...]).astype(BF16)
    qk_w = SWA_Q_W + SWA_KV_W
    qk = _dot(h, w_ref[:, :qk_w])
    v_ref[...] = _dot(h, w_ref[:, qk_w:])
    qk = _head_norm_rope(qk, bd_ref[...], gain_ref[...], cos_ref[...], sin_ref[...])
    q_ref[...] = (qk[:, :SWA_Q_W] * (LOG2E * SWA_HEAD_DIM ** -0.5)).astype(BF16)
    k_ref[...] = qk[:, SWA_Q_W:]


def _odd_proj_call(x, g, w_in, gain, cos2, sin2, tm):
    t, d = x.shape
    n_tab = cos2.shape[0] // tm
    head = jnp.arange(MXU_TILE) // SWA_HEAD_DIM
    bd = (head[:, None] == head[None, :]).astype(BF16)
    row = lambda w: pl.BlockSpec((tm, w), lambda i: (i, 0))
    tab = pl.BlockSpec((tm, LANES), lambda i: (i % n_tab, 0))
    ins = [g, w_in, bd, gain]
    return pl.pallas_call(
        _odd_proj_kernel,
        grid=(t // tm,),
        in_specs=[row(d)] + [_full_spec(a.shape) for a in ins] + [tab, tab],
        out_specs=[row(SWA_Q_W), row(SWA_KV_W), row(SWA_KV_W)],
        out_shape=[jax.ShapeDtypeStruct((t, SWA_Q_W), BF16), jax.ShapeDtypeStruct((t, SWA_KV_W), F32),
                   jax.ShapeDtypeStruct((t, SWA_KV_W), F32)],
        compiler_params=_params(("parallel",)),
        name="odd_proj",
    )(x, *ins, cos2, sin2)


def _swa_kernel(sink_ref, q_ref, kp_ref, kc_ref, vp_ref, vc_ref, o_ref, *, rows, first_prev_valid):
    i = pl.program_id(1)
    nch = rows // CHUNK
    wk = WINDOW + CHUNK
    nk = WINDOW + rows
    grp = SWA_HEADS // SWA_KV_HEADS
    gw = grp * CHUNK
    key_low = lax.broadcasted_iota(jnp.int32, (nk, LANES), 1) < SWA_HEAD_DIM
    low = lax.broadcasted_iota(jnp.int32, (CHUNK, LANES), 1) < SWA_HEAD_DIM
    if not first_prev_valid:
        krow = lax.broadcasted_iota(jnp.int32, (nk, LANES), 0)
        klane = lax.broadcasted_iota(jnp.int32, (nk, LANES), 1)
        k_bias = jnp.where((krow < WINDOW) & (klane == 0) & (i == 0), SWA_MASKED, 0.0).astype(BF16)
        q_one = jnp.where(lax.broadcasted_iota(jnp.int32, (gw, LANES), 1) == 0, 1.0, 0.0).astype(BF16)

    s, vj_t = [], []
    for j in range(SWA_KV_HEADS):
        pair, odd = divmod(j, 2)
        lanes = slice(pair * LANES, (pair + 1) * LANES)
        k_pair = jnp.concatenate([kp_ref[:, lanes], kc_ref[:, lanes]], axis=0)
        own = jnp.where(key_low, 0.0, k_pair) if odd else jnp.where(key_low, k_pair, 0.0)
        k_both = (own + pltpu.roll(own, SWA_HEAD_DIM, axis=1)).astype(BF16)
        v_t = jnp.concatenate([vp_ref[:, lanes], vc_ref[:, lanes]], axis=0).T
        vj_t.append(v_t[odd * SWA_HEAD_DIM:(odd + 1) * SWA_HEAD_DIM].astype(BF16))
        if not first_prev_valid:
            k_both = jnp.concatenate([k_both, k_bias], axis=1)
        for c in range(nch):
            q_four = []
            for qp in (2 * j, 2 * j + 1):
                q_pair = q_ref[c * CHUNK:(c + 1) * CHUNK, qp * LANES:(qp + 1) * LANES]
                zero = jnp.zeros_like(q_pair)
                q_four += [jnp.where(low, q_pair, zero), jnp.where(low, zero, q_pair)]
            q_four = jnp.concatenate(q_four, axis=0)
            if not first_prev_valid:
                q_four = jnp.concatenate([q_four, q_one], axis=1)
            s.append(_dot_nt(k_both[c * CHUNK:c * CHUNK + wk], q_four))
    s = jnp.concatenate(s, axis=1)
    sink = jnp.concatenate([jnp.full((1, CHUNK), sink_ref[j * grp + g] * LOG2E, F32)
                            for j in range(SWA_KV_HEADS) for _ in range(nch) for g in range(grp)], axis=1)
    m = jnp.maximum(jnp.max(s, axis=0, keepdims=True), sink)
    p = jnp.exp2(s - m)
    inv = 1.0 / (jnp.sum(p, axis=0, keepdims=True) + jnp.exp2(sink - m))
    p = p.astype(BF16)
    o_t = []
    for j in range(SWA_KV_HEADS):
        for c in range(nch):
            cols = slice((j * nch + c) * gw, (j * nch + c + 1) * gw)
            o_t.append(_dot(vj_t[j][:, c * CHUNK:c * CHUNK + wk], p[:, cols]) * inv[:, cols])
    for c0 in range(0, nch, 2):
        pieces = []
        for j in range(SWA_KV_HEADS):
            for t in range(grp // 2):
                tiles = [o_t[j * nch + c][:, t * LANES:(t + 1) * LANES] for c in range(c0, min(c0 + 2, nch))]
                if len(tiles) == 2:
                    pieces += [jnp.where(low, tiles[0], pltpu.roll(tiles[1], CHUNK, axis=1)),
                               jnp.where(low, pltpu.roll(tiles[0], CHUNK, axis=1), tiles[1])]
                else:
                    pieces += [tiles[0], pltpu.roll(tiles[0], CHUNK, axis=1)]
        out = jnp.concatenate(pieces, axis=0).T
        r1 = min((c0 + 2) * CHUNK, rows)
        o_ref[c0 * CHUNK:r1] = out[:r1 - c0 * CHUNK].astype(o_ref.dtype)


def _swa_call(q, k_prev, k_cur, v_prev, v_cur, sinks, rows, same_array):
    b, n, _ = q.shape
    per = rows // WINDOW if same_array else 0
    cur = lambda w: pl.BlockSpec((None, rows, w), lambda bi, i: (bi, i, 0))
    prev = pl.BlockSpec((None, WINDOW, SWA_KV_W), lambda bi, i: (bi, jnp.maximum(i * per - 1, 0), 0))
    return pl.pallas_call(
        functools.partial(_swa_kernel, rows=rows, first_prev_valid=not same_array),
        grid=(b, n // rows),
        in_specs=[pl.BlockSpec(memory_space=pltpu.SMEM), cur(SWA_Q_W), prev, cur(SWA_KV_W), prev, cur(SWA_KV_W)],
        out_specs=cur(SWA_Q_W),
        out_shape=jax.ShapeDtypeStruct((b, n, SWA_Q_W), BF16),
        compiler_params=_params(("parallel", "parallel")),
        name="swa",
    )(sinks, q, k_prev, k_cur, v_prev, v_cur)


def _rope_tables(pos0, n, head_dim):
    half = head_dim // 2
    lane = jnp.arange(LANES)
    inv = jnp.power(ROPE_THETA, -(lane % half).astype(F32) / half)
    sign = jnp.where(lane % head_dim < half, -1.0, 1.0)
    fine = min(n, LANES)
    ang_c = (pos0 + fine * jnp.arange(n // fine)).astype(F32)[:, None] * inv[None, :]
    ang_f = jnp.arange(fine, dtype=F32)[:, None] * inv[None, :]
    cos_c, sin_c = jnp.cos(ang_c)[:, None, :], jnp.sin(ang_c)[:, None, :]
    cos_f, sin_f = jnp.cos(ang_f)[None, :, :], jnp.sin(ang_f)[None, :, :]
    return ((cos_c * cos_f - sin_c * sin_f).reshape(n, LANES),
            ((sin_c * cos_f + cos_c * sin_f) * sign).reshape(n, LANES))


def _prep_weights(norm_g, ffn_w_gate, ffn_w_up, ffn_w_down, even_w_in, even_w_out, odd_w_in, odd_w_out,
                  odd_q_norm, odd_k_norm):
    depth = norm_g.shape[0]
    prm = {"g": norm_g[:, :, None, :], "even": [], "odd": [], "depth": depth,
           "ffn": (ffn_w_gate.astype(BF16), ffn_w_up.astype(BF16), ffn_w_down.astype(BF16))}
    for i in range(even_w_in.shape[0]):
        prm["even"].append((even_w_in[i].astype(BF16), even_w_out[i].astype(BF16)))
    for i in range(odd_w_in.shape[0]):
        gain = jnp.concatenate([jnp.tile(odd_q_norm[i], SWA_HEADS), jnp.tile(odd_k_norm[i], SWA_KV_HEADS)])
        prm["odd"].append((odd_w_in[i].astype(BF16), odd_w_out[i].astype(BF16), gain[None, :]))
    return prm


def _trunk(x, pos0, prm, sinks, caches, tm, sb_qb, sb_kb, ret_cs, swa_rows):
    b, n, d = x.shape
    t = b * n
    xf = x.reshape(t, d)
    depth = prm["depth"]
    tab_rows = n if n % tm == 0 else t
    tile_tab = lambda tb: tb if tab_rows == n else jnp.tile(tb, (b, 1))
    cos_r, sin_r = [tile_tab(tb) for tb in _rope_tables(pos0, n, RET_QK_DIM)]
    cos_s, sin_s = [tile_tab(tb) for tb in _rope_tables(pos0, n, SWA_HEAD_DIM)]
    idx = jnp.arange(sb_kb)
    negu = jnp.where(idx[:, None] >= idx[None, :], -1.0, 0.0).astype(BF16)
    sb_k, sb_v, ret, swa_k, swa_v = [], [], [], [], []
    mix, w_mix = [], None
    for layer in range(depth):
        i = layer // 2
        g = prm["g"][layer]
        xf = _ffn_call(xf, mix, w_mix, g[0], prm["ffn"], layer, 0, tm)
        if layer % 2 == 0:
            w_in, w_mix = prm["even"][i]
            qa, ka, kab, va, vab, qr, kr, vr, gt = _even_proj_call(xf, g[1], w_in, cos_r, sin_r, tm)
            sb_k.append(ka.reshape(b, n, SB_HEADS, SB_HEAD_DIM))
            sb_v.append(va.reshape(b, n, SB_HEADS, SB_HEAD_DIM))
            r3 = lambda a: a.reshape(b, n, a.shape[-1])
            if caches is None:
                k_all, v_all, q_pos0 = r3(kab), r3(vab), 0
                state0 = jnp.zeros((b, RET_HEADS, RET_QK_DIM, RET_V_DIM), F32)
            else:
                past = caches["sb_k"].shape[2]
                padded = -(-(past + n) // sb_kb) * sb_kb
                cat = lambda cache, new: jnp.pad(
                    jnp.concatenate([cache.reshape(b, past, SB_W).astype(BF16), r3(new)], axis=1),
                    ((0, 0), (0, padded - past - n), (0, 0)))
                k_all, v_all, q_pos0 = cat(caches["sb_k"][i], kab), cat(caches["sb_v"][i], vab), past
                state0 = caches["ret"][i]
            o_sb = _sb_call(r3(qa), k_all, v_all, negu, sb_qb, q_pos0)
            o_r, st = _ret_call(r3(qr), r3(kr), r3(vr), r3(gt), state0, ret_cs)
            ret.append(st)
            mix = [o_sb.reshape(t, SB_W), o_r.reshape(t, RET_V_W)]
        else:
            w_in, w_mix, gain = prm["odd"][i]
            q, k, v = _odd_proj_call(xf, g[1], w_in, gain, cos_s, sin_s, tm)
            r3 = lambda a: a.reshape(b, n, a.shape[-1])
            k3, v3 = r3(k), r3(v)
            if caches is None:
                o = _swa_call(r3(q), k3, k3, v3, v3, sinks[i], swa_rows, True)
                keep = min(WINDOW, n)
                k_rows, v_rows = k3[:, n - keep:], v3[:, n - keep:]
            else:
                kc = caches["swa_k"][i].reshape(b, -1, SWA_KV_W)
                vc = caches["swa_v"][i].reshape(b, -1, SWA_KV_W)
                o = _swa_call(r3(q), kc, k3, vc, v3, sinks[i], swa_rows, False)
                k_rows, v_rows = k3, v3
            swa_k.append(k_rows.reshape(b, -1, SWA_KV_HEADS, SWA_HEAD_DIM))
            swa_v.append(v_rows.reshape(b, -1, SWA_KV_HEADS, SWA_HEAD_DIM))
            mix = [o.reshape(t, SWA_Q_W)]
        xf = _ffn_call(xf, mix, w_mix, g[2], prm["ffn"], layer, 1, tm)
        mix, w_mix = [], None
    stack = lambda parts: parts[0][None] if len(parts) == 1 else jnp.stack(parts)
    return (xf.reshape(b, n, d), stack(sb_k), stack(sb_v), stack(ret), stack(swa_k), stack(swa_v))


def kernel(x_prompt, x_sample, cache_sb_k, cache_sb_v, state_ret, cache_swa_k, cache_swa_v, norm_g, ffn_w_gate,
           ffn_w_up, ffn_w_down, even_w_in, even_w_out, odd_w_in, odd_w_out, odd_q_norm, odd_k_norm, odd_sinks):
    prm = _prep_weights(norm_g, ffn_w_gate, ffn_w_up, ffn_w_down, even_w_in, even_w_out, odd_w_in, odd_w_out,
                        odd_q_norm, odd_k_norm)
    past = cache_sb_k.shape[2]
    n_s = x_sample.shape[1]
    y_p, sb_k_p, sb_v_p, ret_p, swa_k_p, swa_v_p = _trunk(
        x_prompt, 0, prm, odd_sinks, None, tm=512, sb_qb=1024, sb_kb=256, ret_cs=256, swa_rows=512)
    caches = {"sb_k": cache_sb_k, "sb_v": cache_sb_v, "ret": state_ret, "swa_k": cache_swa_k, "swa_v": cache_swa_v}
    y_s, sb_k_s, sb_v_s, ret_s, swa_k_s, swa_v_s = _trunk(
        x_sample, past, prm, odd_sinks, caches, tm=x_sample.shape[0] * n_s, sb_qb=n_s, sb_kb=256,
        ret_cs=n_s, swa_rows=n_s)
    return (y_p, y_s, sb_k_p, sb_v_p, ret_p, swa_k_p, swa_v_p, sb_k_s, sb_v_s, ret_s, swa_k_s, swa_v_s)
```

```python
import functools
import math

import jax
import jax.numpy as jnp
from jax import lax
from jax.experimental import pallas as pl
from jax.experimental.pallas import tpu as pltpu

F32 = jnp.float32
BF16 = jnp.bfloat16

RMS_EPS = 1e-6
ROPE_THETA = 10000.0
CHUNK = 64
SB_HEADS = 8
SB_HEAD_DIM = 64
RET_HEADS = 4
RET_QK_DIM = 128
RET_V_DIM = 256
SWA_HEADS = 16
SWA_KV_HEADS = 4
SWA_HEAD_DIM = 64
WINDOW = 128

SB_W = SB_HEADS * SB_HEAD_DIM
RET_QK_W = RET_HEADS * RET_QK_DIM
RET_V_W = RET_HEADS * RET_V_DIM
SWA_Q_W = SWA_HEADS * SWA_HEAD_DIM
SWA_KV_W = SWA_KV_HEADS * SWA_HEAD_DIM
EVEN_CUTS = (0, SB_W, 2 * SB_W, 3 * SB_W, 3 * SB_W + RET_QK_W, 3 * SB_W + 2 * RET_QK_W,
             3 * SB_W + 2 * RET_QK_W + RET_V_W, 3 * SB_W + 2 * RET_QK_W + 2 * RET_V_W)

LANES = 128
MXU_TILE = 256
FF_CHUNK = MXU_TILE
VMEM_LIMIT = 56 * 1024 * 1024
LOG2E = math.log2(math.e)
SOFTPLUS2_LINEAR = 40.0
SWA_MASKED = -1e30
SB_DEAD_LOG2 = -170.0


def _dot(a, b):
    return jnp.dot(a, b, preferred_element_type=F32)


def _dot_nt(a, b):
    return lax.dot_general(a, b, (((1,), (1,)), ((), ())), preferred_element_type=F32)


def _dot_tn(a, b):
    return lax.dot_general(a, b, (((0,), (0,)), ((), ())), preferred_element_type=F32)


def _rms(x, g):
    return x * lax.rsqrt(jnp.mean(x * x, axis=-1, keepdims=True) + RMS_EPS) * g


def _full_spec(shape):
    nd = len(shape)
    return pl.BlockSpec(shape, lambda *_: (0,) * nd, pipeline_mode=pl.Buffered(1))


def _params(sem):
    return pltpu.CompilerParams(dimension_semantics=sem, vmem_limit_bytes=VMEM_LIMIT)


def _ffn_kernel(*refs, n_mix):
    x_ref = refs[0]
    mix = refs[1:1 + n_mix]
    wmix_ref = refs[1 + n_mix] if n_mix else None
    g_ref, wg_ref, wu_ref, wd_ref, o_ref, h_ref, act_ref = refs[1 + n_mix + bool(n_mix):]
    x = x_ref[...]
    r0 = 0
    for a_ref in mix:
        x = x + _dot(a_ref[...], wmix_ref[r0:r0 + a_ref.shape[1]])
        r0 += a_ref.shape[1]
    h_ref[...] = _rms(x, g_ref[...]).astype(BF16)
    for c in range(wg_ref.shape[1] // FF_CHUNK):
        cols = slice(c * FF_CHUNK, (c + 1) * FF_CHUNK)
        gate = _dot(h_ref[...], wg_ref[:, cols])
        up = _dot(h_ref[...], wu_ref[:, cols])
        act_ref[:, cols] = (gate * jax.nn.sigmoid(gate) * up).astype(BF16)
    o_ref[...] = x + 0.5 * _dot(act_ref[...], wd_ref[...])


def _ffn_call(x, mix, w_mix, g, ffn_w, layer, slot, tm):
    wg, wu, wd = ffn_w
    stack_spec = lambda w: pl.BlockSpec((None, None) + w.shape[2:], lambda i: (layer, slot, 0, 0),
                                        pipeline_mode=pl.Buffered(1))
    t, d = x.shape
    row = lambda w: pl.BlockSpec((tm, w), lambda i: (i, 0))
    in_specs = [row(d)]
    args = [x]
    for a in mix:
        in_specs.append(row(a.shape[1]))
        args.append(a)
    if mix:
        in_specs.append(_full_spec(w_mix.shape))
        args.append(w_mix)
    in_specs += [_full_spec(g.shape), stack_spec(wg), stack_spec(wu), stack_spec(wd)]
    args += [g, wg, wu, wd]
    return pl.pallas_call(
        functools.partial(_ffn_kernel, n_mix=len(mix)),
        grid=(t // tm,),
        in_specs=in_specs,
        out_specs=row(d),
        out_shape=jax.ShapeDtypeStruct((t, d), F32),
        scratch_shapes=[pltpu.VMEM((tm, d), BF16), pltpu.VMEM((tm, wg.shape[-1]), BF16)],
        compiler_params=_params(("parallel",)),
        name="ffn_mix%d" % len(mix),
    )(*args)


def _rope128(x, cos2, sin2):
    parts = []
    for hd in range(x.shape[1] // LANES):
        sl = x[:, hd * LANES:(hd + 1) * LANES]
        parts.append(sl * cos2 + pltpu.roll(sl, LANES // 2, axis=1) * sin2)
    return jnp.concatenate(parts, axis=1)


def _even_proj_kernel(x_ref, g_ref, w_ref, cos_ref, sin_ref,
                      qa_ref, ka_ref, kab_ref, va_ref, vab_ref, qr_ref, kr_ref, vr_ref, gt_ref):
    h = _rms(x_ref[...], g_ref[...]).astype(BF16)
    proj = lambda s: _dot(h, w_ref[:, EVEN_CUTS[s]:EVEN_CUTS[s + 1]])
    gate = proj(6)
    gt_ref[...] = gate * jax.nn.sigmoid(gate)
    qa_ref[...] = (proj(0) * (LOG2E * SB_HEAD_DIM ** -0.5)).astype(BF16)
    ka = proj(1)
    ka_ref[...] = ka.reshape(ka_ref.shape)
    kab_ref[...] = ka.astype(BF16)
    va = proj(2)
    va_ref[...] = va.reshape(va_ref.shape)
    vab_ref[...] = va.astype(BF16)
    cos2 = cos_ref[...]
    sin2 = sin_ref[...]
    qr_ref[...] = _rope128(proj(3), cos2, sin2).astype(BF16)
    kr_ref[...] = (_rope128(proj(4), cos2, sin2) * (RET_QK_DIM ** -0.5)).astype(BF16)
    vr_ref[...] = proj(5).astype(BF16)


def _even_proj_call(x, g, w_in, cos2, sin2, tm):
    t, d = x.shape
    n_tab = cos2.shape[0] // tm
    row = lambda w: pl.BlockSpec((tm, w), lambda i: (i, 0))
    tab = pl.BlockSpec((tm, LANES), lambda i: (i % n_tab, 0))
    per_head = (SB_HEADS, SB_HEAD_DIM)
    outs = [((SB_W,), BF16), (per_head, F32), ((SB_W,), BF16), (per_head, F32), ((SB_W,), BF16),
            ((RET_QK_W,), BF16), ((RET_QK_W,), BF16), ((RET_V_W,), BF16), ((RET_V_W,), F32)]
    return pl.pallas_call(
        _even_proj_kernel,
        grid=(t // tm,),
        in_specs=[row(d), _full_spec(g.shape), _full_spec(w_in.shape), tab, tab],
        out_specs=[pl.BlockSpec((tm,) + tail, lambda i, nd=len(tail): (i,) + (0,) * nd) for tail, _ in outs],
        out_shape=[jax.ShapeDtypeStruct((t,) + tail, dt) for tail, dt in outs],
        compiler_params=_params(("parallel",)),
        name="even_proj",
    )(x, g, w_in, cos2, sin2)


def _sb_kernel(q_ref, k_ref, v_ref, negu_ref, o_ref, qh_ref, c_ref, acc_ref, *, qb, kb, q_pos0):
    i = pl.program_id(2)
    n_clear = (q_pos0 + i * qb) // kb
    sq = min(qb, kb)
    nsub = qb // sq
    rows = lambda s: slice(s * sq, (s + 1) * sq)
    kstart = lambda j: pl.multiple_of(j * kb, kb)
    q = q_ref[...]
    first = lax.broadcasted_iota(jnp.int32, (qb, LANES), 1) < SB_HEAD_DIM
    qh_ref[0] = jnp.where(first, q, jnp.zeros_like(q))
    qh_ref[1] = jnp.where(first, jnp.zeros_like(q), q)

    def scores(hd, rws, start):
        return _dot_nt(qh_ref[hd, rws], k_ref[pl.ds(start, kb), :])

    def softplus2(z):
        return jnp.where(z > SOFTPLUS2_LINEAR, z, jnp.log2(1.0 + jnp.exp2(z)))

    def v_heads(start):
        vs = v_ref[pl.ds(start, kb), :]
        vfirst = lax.broadcasted_iota(jnp.int32, (kb, LANES), 1) < SB_HEAD_DIM
        return jnp.concatenate([jnp.where(vfirst, vs, jnp.zeros_like(vs)),
                                jnp.where(vfirst, jnp.zeros_like(vs), vs)], axis=0)

    prev = lambda s: kstart(jnp.maximum(n_clear + s - 1, 0))
    row = lax.broadcasted_iota(jnp.int32, (qb, kb), 0)
    below = lax.broadcasted_iota(jnp.int32, (qb, kb), 1) < (row & (sq - 1))
    has_prev = jnp.broadcast_to(n_clear > 0, (sq, kb))

    def apply_masks(t):
        parts = [jnp.where(below, t[:qb], 0.0), jnp.where(has_prev, t[qb:qb + sq], 0.0)]
        return jnp.concatenate(parts + ([t[qb + sq:]] if nsub > 1 else []), axis=0)

    w_diag, w_prev = [], []
    for hd in range(2):
        z = jnp.concatenate([scores(hd, rows(s), kstart(n_clear + s)) for s in range(nsub)]
                            + [scores(hd, rows(s), prev(s)) for s in range(nsub)], axis=0)
        sp = apply_masks(softplus2(z))
        tail = _dot(sp.astype(BF16), negu_ref[...])
        c_diag = tail[:qb, :1]
        w = apply_masks(jnp.exp2(z + tail + jnp.concatenate([jnp.zeros_like(c_diag), c_diag], axis=0)))
        c_ref[hd] = c_diag + tail[qb:, :1]
        w = w.astype(BF16)
        w_diag.append(w[:qb])
        w_prev.append(w[qb:])
    w_diag = jnp.concatenate(w_diag, axis=1)
    w_prev = jnp.concatenate(w_prev, axis=1)
    for s in range(nsub):
        acc_ref[rows(s)] = (_dot(w_diag[rows(s)], v_heads(kstart(n_clear + s)))
                            + _dot(w_prev[rows(s)], v_heads(prev(s))))

    def block(start, r0, r1):
        ws = []
        for hd in range(2):
            z = scores(hd, slice(r0, r1), start)
            tail = _dot(softplus2(z).astype(BF16), negu_ref[...])
            ws.append(jnp.exp2(z + tail + c_ref[hd, r0:r1]).astype(BF16))
            c_ref[hd, r0:r1] += tail[:, :1]
        acc_ref[r0:r1] += _dot(jnp.concatenate(ws, axis=1), v_heads(start))

    def alive(r0, r1):
        return (jnp.max(c_ref[:, r0:r1]) > SB_DEAD_LOG2).astype(jnp.int32)

    def run_blocks(j_first, j_last, r0, r1):
        def cond(carry):
            j, live = carry
            return jnp.logical_and(j >= j_last, live > 0)

        def body(carry):
            j, _ = carry
            block(kstart(j), r0, r1)
            return j - 1, alive(r0, r1)

        lax.while_loop(cond, body, (j_first, alive(r0, r1)))

    @pl.when(alive(0, qb) > 0)
    def _():
        for s in range(2, nsub):
            run_blocks(n_clear + s - 2, n_clear, s * sq, (s + 1) * sq)
        if nsub > 1:
            run_blocks(n_clear - 1, jnp.maximum(n_clear - 1, 0), sq, qb)
        run_blocks(n_clear - 2, 0, 0, qb)

    o_ref[...] = acc_ref[...].astype(o_ref.dtype)


def _sb_call(q, k, v, negu, qb, q_pos0):
    b, nq, w = q.shape
    nk = k.shape[1]
    kb = negu.shape[0]
    assert q_pos0 % kb == 0 and (qb % kb == 0 or nq == qb < kb) and nk % kb == 0 and nk >= q_pos0 + nq
    return pl.pallas_call(
        functools.partial(_sb_kernel, qb=qb, kb=kb, q_pos0=q_pos0),
        grid=(b, w // LANES, nq // qb),
        in_specs=[pl.BlockSpec((None, qb, LANES), lambda bi, p, i: (bi, i, p)),
                  pl.BlockSpec((None, nk, LANES), lambda bi, p, i: (bi, 0, p)),
                  pl.BlockSpec((None, nk, LANES), lambda bi, p, i: (bi, 0, p)),
                  _full_spec(negu.shape)],
        out_specs=pl.BlockSpec((None, qb, LANES), lambda bi, p, i: (bi, i, p)),
        out_shape=jax.ShapeDtypeStruct((b, nq, w), BF16),
        scratch_shapes=[pltpu.VMEM((2, qb, LANES), BF16), pltpu.VMEM((2, qb, 1), F32),
                        pltpu.VMEM((qb, LANES), F32)],
        compiler_params=_params(("parallel", "parallel", "parallel")),
        name="stick_breaking",
    )(q, k, v, negu)


def _ret_kernel(sdec_ref, q_ref, k_ref, v_ref, gt_ref, s0_ref, dec_ref, qdec_ref, kdec_ref,
                o_ref, sfin_ref, st_ref):
    c = pl.program_id(0)

    @pl.when(c == 0)
    def _():
        st_ref[...] = s0_ref[...]

    units = [(b, h) for b in range(q_ref.shape[0]) for h in range(RET_HEADS)]
    qk = lambda h: slice(h * RET_QK_DIM, (h + 1) * RET_QK_DIM)
    vw = lambda h: slice(h * RET_V_DIM, (h + 1) * RET_V_DIM)
    scores = [_dot_nt(q_ref[b, :, qk(h)], k_ref[b, :, qk(h)]) for b, h in units]
    cross = [_dot(q_ref[b, :, qk(h)], st_ref[b, h].astype(BF16)) for b, h in units]
    kw = [(k_ref[b, :, qk(h)].astype(F32) * kdec_ref[h]).astype(BF16) for b, h in units]
    grow = [_dot_tn(kw[u], v_ref[b, :, vw(h)]) for u, (b, h) in enumerate(units)]
    decayed = [(scores[u] * dec_ref[h]).astype(BF16) for u, (b, h) in enumerate(units)]
    inner = [_dot(decayed[u], v_ref[b, :, vw(h)]) for u, (b, h) in enumerate(units)]
    for u, (b, h) in enumerate(units):
        st_ref[b, h] = sdec_ref[h] * st_ref[b, h] + grow[u]
        o = inner[u] + cross[u] * qdec_ref[h]
        o = o * lax.rsqrt(jnp.mean(o * o, axis=-1, keepdims=True) + RMS_EPS)
        o_ref[b, :, vw(h)] = (o * gt_ref[b, :, vw(h)]).astype(o_ref.dtype)

    @pl.when(c == pl.num_programs(0) - 1)
    def _():
        sfin_ref[...] = st_ref[...]


def _ret_call(q, k, v, gate, state0, cs):
    b, n, _ = q.shape
    log_gamma = jnp.log1p(-jnp.exp2(-5.0 - jnp.arange(RET_HEADS, dtype=F32)))
    pos = jnp.arange(cs, dtype=F32)
    diff = pos[:, None] - pos[None, :]
    dec = jnp.where(diff >= 0, jnp.exp(log_gamma[:, None, None] * jnp.maximum(diff, 0.0)), 0.0)
    qdec = jnp.exp(log_gamma[:, None] * (pos + 1.0))[..., None]
    kdec = jnp.exp(log_gamma[:, None] * (cs - 1.0 - pos))[..., None]
    sdec = jnp.exp(log_gamma * cs)
    seq = lambda w: pl.BlockSpec((b, cs, w), lambda c: (0, c, 0))
    st = pl.BlockSpec((b, RET_HEADS, RET_QK_DIM, RET_V_DIM), lambda c: (0, 0, 0, 0))
    return pl.pallas_call(
        _ret_kernel,
        grid=(n // cs,),
        in_specs=[pl.BlockSpec(memory_space=pltpu.SMEM), seq(RET_QK_W), seq(RET_QK_W), seq(RET_V_W),
                  seq(RET_V_W), st, _full_spec(dec.shape), _full_spec(qdec.shape), _full_spec(kdec.shape)],
        out_specs=[seq(RET_V_W), st],
        out_shape=[jax.ShapeDtypeStruct((b, n, RET_V_W), BF16),
                   jax.ShapeDtypeStruct((b, RET_HEADS, RET_QK_DIM, RET_V_DIM), F32)],
        scratch_shapes=[pltpu.VMEM((b, RET_HEADS, RET_QK_DIM, RET_V_DIM), F32)],
        compiler_params=_params(("arbitrary",)),
        name="retention",
    )(sdec, q, k, v, gate, state0, dec, qdec, kdec)


def _head_norm_rope(t, bd, gain, cos2, sin2):
    wide = bd.shape[0]
    parts = []
    for c in range(t.shape[1] // wide):
        sq = t[:, c * wide:(c + 1) * wide]
        sq = sq * sq
        hi = sq.astype(BF16)
        lo = (sq - hi.astype(F32)).astype(BF16)
        parts.append(_dot(hi, bd) + _dot(lo, bd))
    ms = jnp.concatenate(parts, axis=1) * (1.0 / SWA_HEAD_DIM)
    y = t * lax.rsqrt(ms + RMS_EPS) * gain
    w = t.shape[1]
    reps = w // LANES
    quarter = SWA_HEAD_DIM // 2
    first_half = (lax.broadcasted_iota(jnp.int32, t.shape, 1) % SWA_HEAD_DIM) < quarter
    partner = jnp.where(first_half, pltpu.roll(y, w - quarter, axis=1), pltpu.roll(y, quarter, axis=1))
    return y * jnp.concatenate([cos2] * reps, axis=1) + partner * jnp.concatenate([sin2] * reps, axis=1)


def _odd_proj_kernel(x_ref, g_ref, w_ref, bd_ref, gain_ref, cos_ref, sin_ref, q_ref, k_ref, v_ref):
    h = _rms(x_ref[...], g_ref[...]).astype(BF16)
    qk_w = SWA_Q_W + SWA_KV_W
    qk = _dot(h, w_ref[:, :qk_w])
    v_ref[...] = _dot(h, w_ref[:, qk_w:])
    qk = _head_norm_rope(qk, bd_ref[...], gain_ref[...], cos_ref[...], sin_ref[...])
    q_ref[...] = (qk[:, :SWA_Q_W] * (LOG2E * SWA_HEAD_DIM ** -0.5)).astype(BF16)
    k_ref[...] = qk[:, SWA_Q_W:]


def _odd_proj_call(x, g, w_in, gain, cos2, sin2, tm):
    t, d = x.shape
    n_tab = cos2.shape[0] // tm
    head = jnp.arange(MXU_TILE) // SWA_HEAD_DIM
    bd = (head[:, None] == head[None, :]).astype(BF16)
    row = lambda w: pl.BlockSpec((tm, w), lambda i: (i, 0))
    tab = pl.BlockSpec((tm, LANES), lambda i: (i % n_tab, 0))
    ins = [g, w_in, bd, gain]
    return pl.pallas_call(
        _odd_proj_kernel,
        grid=(t // tm,),
        in_specs=[row(d)] + [_full_spec(a.shape) for a in ins] + [tab, tab],
        out_specs=[row(SWA_Q_W), row(SWA_KV_W), row(SWA_KV_W)],
        out_shape=[jax.ShapeDtypeStruct((t, SWA_Q_W), BF16), jax.ShapeDtypeStruct((t, SWA_KV_W), F32),
                   jax.ShapeDtypeStruct((t, SWA_KV_W), F32)],
        compiler_params=_params(("parallel",)),
        name="odd_proj",
    )(x, *ins, cos2, sin2)


def _swa_kernel(sink_ref, q_ref, kp_ref, kc_ref, vp_ref, vc_ref, o_ref, *, rows, first_prev_valid):
    i = pl.program_id(1)
    nch = rows // CHUNK
    wk = WINDOW + CHUNK
    nk = WINDOW + rows
    grp = SWA_HEADS // SWA_KV_HEADS
    gw = grp * CHUNK
    key_low = lax.broadcasted_iota(jnp.int32, (nk, LANES), 1) < SWA_HEAD_DIM
    low = lax.broadcasted_iota(jnp.int32, (CHUNK, LANES), 1) < SWA_HEAD_DIM
    if not first_prev_valid:
        krow = lax.broadcasted_iota(jnp.int32, (nk, LANES), 0)
        klane = lax.broadcasted_iota(jnp.int32, (nk, LANES), 1)
        k_bias = jnp.where((krow < WINDOW) & (klane == 0) & (i == 0), SWA_MASKED, 0.0).astype(BF16)
        q_one = jnp.where(lax.broadcasted_iota(jnp.int32, (gw, LANES), 1) == 0, 1.0, 0.0).astype(BF16)

    s, vj_t = [], []
    for j in range(SWA_KV_HEADS):
        pair, odd = divmod(j, 2)
        lanes = slice(pair * LANES, (pair + 1) * LANES)
        k_pair = jnp.concatenate([kp_ref[:, lanes], kc_ref[:, lanes]], axis=0)
        own = jnp.where(key_low, 0.0, k_pair) if odd else jnp.where(key_low, k_pair, 0.0)
        k_both = (own + pltpu.roll(own, SWA_HEAD_DIM, axis=1)).astype(BF16)
        v_t = jnp.concatenate([vp_ref[:, lanes], vc_ref[:, lanes]], axis=0).T
        vj_t.append(v_t[odd * SWA_HEAD_DIM:(odd + 1) * SWA_HEAD_DIM].astype(BF16))
        if not first_prev_valid:
            k_both = jnp.concatenate([k_both, k_bias], axis=1)
        for c in range(nch):
            q_four = []
            for qp in (2 * j, 2 * j + 1):
                q_pair = q_ref[c * CHUNK:(c + 1) * CHUNK, qp * LANES:(qp + 1) * LANES]
                zero = jnp.zeros_like(q_pair)
                q_four += [jnp.where(low, q_pair, zero), jnp.where(low, zero, q_pair)]
            q_four = jnp.concatenate(q_four, axis=0)
            if not first_prev_valid:
                q_four = jnp.concatenate([q_four, q_one], axis=1)
            s.append(_dot_nt(k_both[c * CHUNK:c * CHUNK + wk], q_four))
    s = jnp.concatenate(s, axis=1)
    sink = jnp.concatenate([jnp.full((1, CHUNK), sink_ref[j * grp + g] * LOG2E, F32)
                            for j in range(SWA_KV_HEADS) for _ in range(nch) for g in range(grp)], axis=1)
    m = jnp.maximum(jnp.max(s, axis=0, keepdims=True), sink)
    p = jnp.exp2(s - m)
    inv = 1.0 / (jnp.sum(p, axis=0, keepdims=True) + jnp.exp2(sink - m))
    p = p.astype(BF16)
    o_t = []
    for j in range(SWA_KV_HEADS):
        for c in range(nch):
            cols = slice((j * nch + c) * gw, (j * nch + c + 1) * gw)
            o_t.append(_dot(vj_t[j][:, c * CHUNK:c * CHUNK + wk], p[:, cols]) * inv[:, cols])
    for c0 in range(0, nch, 2):
        pieces = []
        for j in range(SWA_KV_HEADS):
            for t in range(grp // 2):
                tiles = [o_t[j * nch + c][:, t * LANES:(t + 1) * LANES] for c in range(c0, min(c0 + 2, nch))]
                if len(tiles) == 2:
                    pieces += [jnp.where(low, tiles[0], pltpu.roll(tiles[1], CHUNK, axis=1)),
                               jnp.where(low, pltpu.roll(tiles[0], CHUNK, axis=1), tiles[1])]
                else:
                    pieces += [tiles[0], pltpu.roll(tiles[0], CHUNK, axis=1)]
        out = jnp.concatenate(pieces, axis=0).T
        r1 = min((c0 + 2) * CHUNK, rows)
        o_ref[c0 * CHUNK:r1] = out[:r1 - c0 * CHUNK].astype(o_ref.dtype)


def _swa_call(q, k_prev, k_cur, v_prev, v_cur, sinks, rows, same_array):
    b, n, _ = q.shape
    per = rows // WINDOW if same_array else 0
    cur = lambda w: pl.BlockSpec((None, rows, w), lambda bi, i: (bi, i, 0))
    prev = pl.BlockSpec((None, WINDOW, SWA_KV_W), lambda bi, i: (bi, jnp.maximum(i * per - 1, 0), 0))
    return pl.pallas_call(
        functools.partial(_swa_kernel, rows=rows, first_prev_valid=not same_array),
        grid=(b, n // rows),
        in_specs=[pl.BlockSpec(memory_space=pltpu.SMEM), cur(SWA_Q_W), prev, cur(SWA_KV_W), prev, cur(SWA_KV_W)],
        out_specs=cur(SWA_Q_W),
        out_shape=jax.ShapeDtypeStruct((b, n, SWA_Q_W), BF16),
        compiler_params=_params(("parallel", "parallel")),
        name="swa",
    )(sinks, q, k_prev, k_cur, v_prev, v_cur)


def _rope_tables(pos0, n, head_dim):
    half = head_dim // 2
    lane = jnp.arange(LANES)
    inv = jnp.power(ROPE_THETA, -(lane % half).astype(F32) / half)
    sign = jnp.where(lane % head_dim < half, -1.0, 1.0)
    fine = min(n, LANES)
    ang_c = (pos0 + fine * jnp.arange(n // fine)).astype(F32)[:, None] * inv[None, :]
    ang_f = jnp.arange(fine, dtype=F32)[:, None] * inv[None, :]
    cos_c, sin_c = jnp.cos(ang_c)[:, None, :], jnp.sin(ang_c)[:, None, :]
    cos_f, sin_f = jnp.cos(ang_f)[None, :, :], jnp.sin(ang_f)[None, :, :]
    return ((cos_c * cos_f - sin_c * sin_f).reshape(n, LANES),
            ((sin_c * cos_f + cos_c * sin_f) * sign).reshape(n, LANES))


def _prep_weights(norm_g, ffn_w_gate, ffn_w_up, ffn_w_down, even_w_in, even_w_out, odd_w_in, odd_w_out,
                  odd_q_norm, odd_k_norm):
    depth = norm_g.shape[0]
    prm = {"g": norm_g[:, :, None, :], "even": [], "odd": [], "depth": depth,
           "ffn": (ffn_w_gate.astype(BF16), ffn_w_up.astype(BF16), ffn_w_down.astype(BF16))}
    for i in range(even_w_in.shape[0]):
        prm["even"].append((even_w_in[i].astype(BF16), even_w_out[i].astype(BF16)))
    for i in range(odd_w_in.shape[0]):
        gain = jnp.concatenate([jnp.tile(odd_q_norm[i], SWA_HEADS), jnp.tile(odd_k_norm[i], SWA_KV_HEADS)])
        prm["odd"].append((odd_w_in[i].astype(BF16), odd_w_out[i].astype(BF16), gain[None, :]))
    return prm


def _trunk(x, pos0, prm, sinks, caches, tm, sb_qb, sb_kb, ret_cs, swa_rows):
    b, n, d = x.shape
    t = b * n
    xf = x.reshape(t, d)
    depth = prm["depth"]
    tab_rows = n if n % tm == 0 else t
    tile_tab = lambda tb: tb if tab_rows == n else jnp.tile(tb, (b, 1))
    cos_r, sin_r = [tile_tab(tb) for tb in _rope_tables(pos0, n, RET_QK_DIM)]
    cos_s, sin_s = [tile_tab(tb) for tb in _rope_tables(pos0, n, SWA_HEAD_DIM)]
    idx = jnp.arange(sb_kb)
    negu = jnp.where(idx[:, None] >= idx[None, :], -1.0, 0.0).astype(BF16)
    sb_k, sb_v, ret, swa_k, swa_v = [], [], [], [], []
    mix, w_mix = [], None
    for layer in range(depth):
        i = layer // 2
        g = prm["g"][layer]
        xf = _ffn_call(xf, mix, w_mix, g[0], prm["ffn"], layer, 0, tm)
        if layer % 2 == 0:
            w_in, w_mix = prm["even"][i]
            qa, ka, kab, va, vab, qr, kr, vr, gt = _even_proj_call(xf, g[1], w_in, cos_r, sin_r, tm)
            sb_k.append(ka.reshape(b, n, SB_HEADS, SB_HEAD_DIM))
            sb_v.append(va.reshape(b, n, SB_HEADS, SB_HEAD_DIM))
            r3 = lambda a: a.reshape(b, n, a.shape[-1])
            if caches is None:
                k_all, v_all, q_pos0 = r3(kab), r3(vab), 0
                state0 = jnp.zeros((b, RET_HEADS, RET_QK_DIM, RET_V_DIM), F32)
            else:
                past = caches["sb_k"].shape[2]
                padded = -(-(past + n) // sb_kb) * sb_kb
                cat = lambda cache, new: jnp.pad(
                    jnp.concatenate([cache.reshape(b, past, SB_W).astype(BF16), r3(new)], axis=1),
                    ((0, 0), (0, padded - past - n), (0, 0)))
                k_all, v_all, q_pos0 = cat(caches["sb_k"][i], kab), cat(caches["sb_v"][i], vab), past
                state0 = caches["ret"][i]
            o_sb = _sb_call(r3(qa), k_all, v_all, negu, sb_qb, q_pos0)
            o_r, st = _ret_call(r3(qr), r3(kr), r3(vr), r3(gt), state0, ret_cs)
            ret.append(st)
            mix = [o_sb.reshape(t, SB_W), o_r.reshape(t, RET_V_W)]
        else:
            w_in, w_mix, gain = prm["odd"][i]
            q, k, v = _odd_proj_call(xf, g[1], w_in, gain, cos_s, sin_s, tm)
            r3 = lambda a: a.reshape(b, n, a.shape[-1])
            k3, v3 = r3(k), r3(v)
            if caches is None:
                o = _swa_call(r3(q), k3, k3, v3, v3, sinks[i], swa_rows, True)
                keep = min(WINDOW, n)
                k_rows, v_rows = k3[:, n - keep:], v3[:, n - keep:]
            else:
                kc = caches["swa_k"][i].reshape(b, -1, SWA_KV_W)
                vc = caches["swa_v"][i].reshape(b, -1, SWA_KV_W)
                o = _swa_call(r3(q), kc, k3, vc, v3, sinks[i], swa_rows, False)
                k_rows, v_rows = k3, v3
            swa_k.append(k_rows.reshape(b, -1, SWA_KV_HEADS, SWA_HEAD_DIM))
            swa_v.append(v_rows.reshape(b, -1, SWA_KV_HEADS, SWA_HEAD_DIM))
            mix = [o.reshape(t, SWA_Q_W)]
        xf = _ffn_call(xf, mix, w_mix, g[2], prm["ffn"], layer, 1, tm)
        mix, w_mix = [], None
    stack = lambda parts: parts[0][None] if len(parts) == 1 else jnp.stack(parts)
    return (xf.reshape(b, n, d), stack(sb_k), stack(sb_v), stack(ret), stack(swa_k), stack(swa_v))


def kernel(x_prompt, x_sample, cache_sb_k, cache_sb_v, state_ret, cache_swa_k, cache_swa_v, norm_g, ffn_w_gate,
           ffn_w_up, ffn_w_down, even_w_in, even_w_out, odd_w_in, odd_w_out, odd_q_norm, odd_k_norm, odd_sinks):
    prm = _prep_weights(norm_g, ffn_w_gate, ffn_w_up, ffn_w_down, even_w_in, even_w_out, odd_w_in, odd_w_out,
                        odd_q_norm, odd_k_norm)
    past = cache_sb_k.shape[2]
    n_s = x_sample.shape[1]
    y_p, sb_k_p, sb_v_p, ret_p, swa_k_p, swa_v_p = _trunk(
        x_prompt, 0, prm, odd_sinks, None, tm=512, sb_qb=2048, sb_kb=256, ret_cs=256, swa_rows=512)
    caches = {"sb_k": cache_sb_k, "sb_v": cache_sb_v, "ret": state_ret, "swa_k": cache_swa_k, "swa_v": cache_swa_v}
    y_s, sb_k_s, sb_v_s, ret_s, swa_k_s, swa_v_s = _trunk(
        x_sample, past, prm, odd_sinks, caches, tm=x_sample.shape[0] * n_s, sb_qb=n_s, sb_kb=256,
        ret_cs=n_s, swa_rows=n_s)
    return (y_p, y_s, sb_k_p, sb_v_p, ret_p, swa_k_p, swa_v_p, sb_k_s, sb_v_s, ret_s, swa_k_s, swa_v_s)
```

```python
import functools
import math

import jax
import jax.numpy as jnp
from jax import lax
from jax.experimental import pallas as pl
from jax.experimental.pallas import tpu as pltpu

F32 = jnp.float32
BF16 = jnp.bfloat16

RMS_EPS = 1e-6
ROPE_THETA = 10000.0
CHUNK = 64
SB_HEADS = 8
SB_HEAD_DIM = 64
RET_HEADS = 4
RET_QK_DIM = 128
RET_V_DIM = 256
SWA_HEADS = 16
SWA_KV_HEADS = 4
SWA_HEAD_DIM = 64
WINDOW = 128

SB_W = SB_HEADS * SB_HEAD_DIM
RET_QK_W = RET_HEADS * RET_QK_DIM
RET_V_W = RET_HEADS * RET_V_DIM
SWA_Q_W = SWA_HEADS * SWA_HEAD_DIM
SWA_KV_W = SWA_KV_HEADS * SWA_HEAD_DIM
EVEN_CUTS = (0, SB_W, 2 * SB_W, 3 * SB_W, 3 * SB_W + RET_QK_W, 3 * SB_W + 2 * RET_QK_W,
             3 * SB_W + 2 * RET_QK_W + RET_V_W, 3 * SB_W + 2 * RET_QK_W + 2 * RET_V_W)

LANES = 128
MXU_TILE = 256
FF_CHUNK = MXU_TILE
VMEM_LIMIT = 56 * 1024 * 1024
LOG2E = math.log2(math.e)
SOFTPLUS2_LINEAR = 40.0
SWA_MASKED = -1e30
SB_DEAD_LOG2 = -170.0


def _dot(a, b):
    return jnp.dot(a, b, preferred_element_type=F32)


def _dot_nt(a, b):
    return lax.dot_general(a, b, (((1,), (1,)), ((), ())), preferred_element_type=F32)


def _dot_tn(a, b):
    return lax.dot_general(a, b, (((0,), (0,)), ((), ())), preferred_element_type=F32)


def _rms(x, g):
    return x * lax.rsqrt(jnp.mean(x * x, axis=-1, keepdims=True) + RMS_EPS) * g


def _full_spec(shape):
    nd = len(shape)
    return pl.BlockSpec(shape, lambda *_: (0,) * nd, pipeline_mode=pl.Buffered(1))


def _params(sem):
    return pltpu.CompilerParams(dimension_semantics=sem, vmem_limit_bytes=VMEM_LIMIT)


def _ffn_kernel(*refs, n_mix):
    n_in = 1 + n_mix
    cur, nxt = refs[:n_in], refs[n_in:2 * n_in]
    wmix_ref = refs[2 * n_in] if n_mix else None
    g_ref, wg_ref, wu_ref, wd_ref, o_ref, ha_ref, hb_ref, ra_ref, rb_ref, act_ref = refs[2 * n_in + bool(n_mix):]
    half = o_ref.shape[0] // 2
    lo, hi = slice(0, half), slice(half, 2 * half)

    def prepare(src, rws, h_ref, res_ref):
        x = src[0][rws]
        r0 = 0
        for a_ref in src[1:]:
            x = x + _dot(a_ref[rws], wmix_ref[r0:r0 + a_ref.shape[1]])
            r0 += a_ref.shape[1]
        res_ref[...] = x
        h_ref[...] = _rms(x, g_ref[...]).astype(BF16)

    def main(h_ref, res_ref, rws):
        for c in range(wg_ref.shape[1] // FF_CHUNK):
            cols = slice(c * FF_CHUNK, (c + 1) * FF_CHUNK)
            gate = _dot(h_ref[...], wg_ref[:, cols])
            up = _dot(h_ref[...], wu_ref[:, cols])
            act_ref[rws, cols] = (gate * jax.nn.sigmoid(gate) * up).astype(BF16)
        o_ref[rws] = res_ref[...] + 0.5 * _dot(act_ref[rws], wd_ref[...])

    @pl.when(pl.program_id(0) == 0)
    def _():
        prepare(cur, lo, ha_ref, ra_ref)

    prepare(cur, hi, hb_ref, rb_ref)
    main(ha_ref, ra_ref, lo)
    prepare(nxt, lo, ha_ref, ra_ref)
    main(hb_ref, rb_ref, hi)


def _ffn_call(x, mix, w_mix, g, ffn_w, layer, slot, tm):
    wg, wu, wd = ffn_w
    stack_spec = lambda w: pl.BlockSpec((None, None) + w.shape[2:], lambda i: (layer, slot, 0, 0),
                                        pipeline_mode=pl.Buffered(1))
    t, d = x.shape
    half = tm // 2
    last = t // tm - 1
    row = lambda w: pl.BlockSpec((tm, w), lambda i: (i, 0))
    nxt = lambda w: pl.BlockSpec((half, w), lambda i: (2 * jnp.minimum(i + 1, last), 0))
    acts = [x] + list(mix)
    in_specs = [row(a.shape[1]) for a in acts] + [nxt(a.shape[1]) for a in acts]
    args = acts + acts
    if mix:
        in_specs.append(_full_spec(w_mix.shape))
        args.append(w_mix)
    in_specs += [_full_spec(g.shape), stack_spec(wg), stack_spec(wu), stack_spec(wd)]
    args += [g, wg, wu, wd]
    return pl.pallas_call(
        functools.partial(_ffn_kernel, n_mix=len(mix)),
        grid=(t // tm,),
        in_specs=in_specs,
        out_specs=row(d),
        out_shape=jax.ShapeDtypeStruct((t, d), F32),
        scratch_shapes=[pltpu.VMEM((half, d), BF16), pltpu.VMEM((half, d), BF16), pltpu.VMEM((half, d), F32),
                        pltpu.VMEM((half, d), F32), pltpu.VMEM((tm, wg.shape[-1]), BF16)],
        compiler_params=_params(("arbitrary",)),
        name="ffn_mix%d" % len(mix),
    )(*args)


def _rope128(x, cos2, sin2):
    parts = []
    for hd in range(x.shape[1] // LANES):
        sl = x[:, hd * LANES:(hd + 1) * LANES]
        parts.append(sl * cos2 + pltpu.roll(sl, LANES // 2, axis=1) * sin2)
    return jnp.concatenate(parts, axis=1)


def _even_proj_kernel(x_ref, g_ref, w_ref, cos_ref, sin_ref,
                      qa_ref, ka_ref, kab_ref, va_ref, vab_ref, qr_ref, kr_ref, vr_ref, gt_ref):
    h = _rms(x_ref[...], g_ref[...]).astype(BF16)
    proj = lambda s: _dot(h, w_ref[:, EVEN_CUTS[s]:EVEN_CUTS[s + 1]])
    qa_ref[...] = (proj(0) * (LOG2E * SB_HEAD_DIM ** -0.5)).astype(BF16)
    ka = proj(1)
    ka_ref[...] = ka.reshape(ka_ref.shape)
    kab_ref[...] = ka.astype(BF16)
    va = proj(2)
    va_ref[...] = va.reshape(va_ref.shape)
    vab_ref[...] = va.astype(BF16)
    cos2 = cos_ref[...]
    sin2 = sin_ref[...]
    qr_ref[...] = _rope128(proj(3), cos2, sin2).astype(BF16)
    kr_ref[...] = (_rope128(proj(4), cos2, sin2) * (RET_QK_DIM ** -0.5)).astype(BF16)
    vr_ref[...] = proj(5).astype(BF16)
    gate = proj(6)
    gt_ref[...] = gate * jax.nn.sigmoid(gate)


def _even_proj_call(x, g, w_in, cos2, sin2, tm):
    t, d = x.shape
    n_tab = cos2.shape[0] // tm
    row = lambda w: pl.BlockSpec((tm, w), lambda i: (i, 0))
    tab = pl.BlockSpec((tm, LANES), lambda i: (i % n_tab, 0))
    per_head = (SB_HEADS, SB_HEAD_DIM)
    outs = [((SB_W,), BF16), (per_head, F32), ((SB_W,), BF16), (per_head, F32), ((SB_W,), BF16),
            ((RET_QK_W,), BF16), ((RET_QK_W,), BF16), ((RET_V_W,), BF16), ((RET_V_W,), F32)]
    return pl.pallas_call(
        _even_proj_kernel,
        grid=(t // tm,),
        in_specs=[row(d), _full_spec(g.shape), _full_spec(w_in.shape), tab, tab],
        out_specs=[pl.BlockSpec((tm,) + tail, lambda i, nd=len(tail): (i,) + (0,) * nd) for tail, _ in outs],
        out_shape=[jax.ShapeDtypeStruct((t,) + tail, dt) for tail, dt in outs],
        compiler_params=_params(("parallel",)),
        name="even_proj",
    )(x, g, w_in, cos2, sin2)


def _sb_kernel(q_ref, k_ref, v_ref, negu_ref, o_ref, qh_ref, c_ref, acc_ref, *, qb, kb, q_pos0):
    i = pl.program_id(2)
    n_clear = (q_pos0 + i * qb) // kb
    sq = min(qb, kb)
    nsub = qb // sq
    rows = lambda s: slice(s * sq, (s + 1) * sq)
    kstart = lambda j: pl.multiple_of(j * kb, kb)
    q = q_ref[...]
    first = lax.broadcasted_iota(jnp.int32, (qb, LANES), 1) < SB_HEAD_DIM
    qh_ref[0] = jnp.where(first, q, jnp.zeros_like(q))
    qh_ref[1] = jnp.where(first, jnp.zeros_like(q), q)

    def scores(hd, rws, start):
        return _dot_nt(qh_ref[hd, rws], k_ref[pl.ds(start, kb), :])

    def softplus2(z):
        return jnp.where(z > SOFTPLUS2_LINEAR, z, jnp.log2(1.0 + jnp.exp2(z)))

    def v_heads(start):
        vs = v_ref[pl.ds(start, kb), :]
        vfirst = lax.broadcasted_iota(jnp.int32, (kb, LANES), 1) < SB_HEAD_DIM
        return jnp.concatenate([jnp.where(vfirst, vs, jnp.zeros_like(vs)),
                                jnp.where(vfirst, jnp.zeros_like(vs), vs)], axis=0)

    prev = lambda s: kstart(jnp.maximum(n_clear + s - 1, 0))
    row = lax.broadcasted_iota(jnp.int32, (qb, kb), 0)
    below = lax.broadcasted_iota(jnp.int32, (qb, kb), 1) < (row & (sq - 1))
    has_prev = jnp.broadcast_to(n_clear > 0, (sq, kb))

    def apply_masks(t):
        parts = [jnp.where(below, t[:qb], 0.0), jnp.where(has_prev, t[qb:qb + sq], 0.0)]
        return jnp.concatenate(parts + ([t[qb + sq:]] if nsub > 1 else []), axis=0)

    w_diag, w_prev = [], []
    for hd in range(2):
        z = jnp.concatenate([scores(hd, rows(s), kstart(n_clear + s)) for s in range(nsub)]
                            + [scores(hd, rows(s), prev(s)) for s in range(nsub)], axis=0)
        sp = apply_masks(softplus2(z))
        tail = _dot(sp.astype(BF16), negu_ref[...])
        c_diag = tail[:qb, :1]
        w = apply_masks(jnp.exp2(z + tail + jnp.concatenate([jnp.zeros_like(c_diag), c_diag], axis=0)))
        c_ref[hd] = c_diag + tail[qb:, :1]
        w = w.astype(BF16)
        w_diag.append(w[:qb])
        w_prev.append(w[qb:])
    w_diag = jnp.concatenate(w_diag, axis=1)
    w_prev = jnp.concatenate(w_prev, axis=1)
    for s in range(nsub):
        acc_ref[rows(s)] = (_dot(w_diag[rows(s)], v_heads(kstart(n_clear + s)))
                            + _dot(w_prev[rows(s)], v_heads(prev(s))))

    def block(start, r0, r1):
        ws = []
        for hd in range(2):
            z = scores(hd, slice(r0, r1), start)
            tail = _dot(softplus2(z).astype(BF16), negu_ref[...])
            ws.append(jnp.exp2(z + tail + c_ref[hd, r0:r1]).astype(BF16))
            c_ref[hd, r0:r1] += tail[:, :1]
        acc_ref[r0:r1] += _dot(jnp.concatenate(ws, axis=1), v_heads(start))

    def alive(r0, r1):
        return (jnp.max(c_ref[:, r0:r1]) > SB_DEAD_LOG2).astype(jnp.int32)

    def run_blocks(j_first, j_last, r0, r1):
        def cond(carry):
            j, live = carry
            return jnp.logical_and(j >= j_last, live > 0)

        def body(carry):
            j, _ = carry
            block(kstart(j), r0, r1)
            return j - 1, alive(r0, r1)

        lax.while_loop(cond, body, (j_first, alive(r0, r1)))

    @pl.when(alive(0, qb) > 0)
    def _():
        for s in range(2, nsub):
            run_blocks(n_clear + s - 2, n_clear, s * sq, (s + 1) * sq)
        if nsub > 1:
            run_blocks(n_clear - 1, jnp.maximum(n_clear - 1, 0), sq, qb)
        run_blocks(n_clear - 2, 0, 0, qb)

    o_ref[...] = acc_ref[...].astype(o_ref.dtype)


def _sb_call(q, k, v, negu, qb, q_pos0):
    b, nq, w = q.shape
    nk = k.shape[1]
    kb = negu.shape[0]
    assert q_pos0 % kb == 0 and (qb % kb == 0 or nq == qb < kb) and nk % kb == 0 and nk >= q_pos0 + nq
    return pl.pallas_call(
        functools.partial(_sb_kernel, qb=qb, kb=kb, q_pos0=q_pos0),
        grid=(b, w // LANES, nq // qb),
        in_specs=[pl.BlockSpec((None, qb, LANES), lambda bi, p, i: (bi, i, p)),
                  pl.BlockSpec((None, nk, LANES), lambda bi, p, i: (bi, 0, p)),
                  pl.BlockSpec((None, nk, LANES), lambda bi, p, i: (bi, 0, p)),
                  _full_spec(negu.shape)],
        out_specs=pl.BlockSpec((None, qb, LANES), lambda bi, p, i: (bi, i, p)),
        out_shape=jax.ShapeDtypeStruct((b, nq, w), BF16),
        scratch_shapes=[pltpu.VMEM((2, qb, LANES), BF16), pltpu.VMEM((2, qb, 1), F32),
                        pltpu.VMEM((qb, LANES), F32)],
        compiler_params=_params(("parallel", "parallel", "parallel")),
        name="stick_breaking",
    )(q, k, v, negu)


def _ret_kernel(sdec_ref, q_ref, k_ref, v_ref, gt_ref, s0_ref, dec_ref, qdec_ref, kdec_ref,
                o_ref, sfin_ref, st_ref):
    c = pl.program_id(0)

    @pl.when(c == 0)
    def _():
        st_ref[...] = s0_ref[...]

    units = [(b, h) for b in range(q_ref.shape[0]) for h in range(RET_HEADS)]
    qk = lambda h: slice(h * RET_QK_DIM, (h + 1) * RET_QK_DIM)
    vw = lambda h: slice(h * RET_V_DIM, (h + 1) * RET_V_DIM)
    scores = [_dot_nt(q_ref[b, :, qk(h)], k_ref[b, :, qk(h)]) for b, h in units]
    cross = [_dot(q_ref[b, :, qk(h)], st_ref[b, h].astype(BF16)) for b, h in units]
    kw = [(k_ref[b, :, qk(h)].astype(F32) * kdec_ref[h]).astype(BF16) for b, h in units]
    grow = [_dot_tn(kw[u], v_ref[b, :, vw(h)]) for u, (b, h) in enumerate(units)]
    decayed = [(scores[u] * dec_ref[h]).astype(BF16) for u, (b, h) in enumerate(units)]
    inner = [_dot(decayed[u], v_ref[b, :, vw(h)]) for u, (b, h) in enumerate(units)]
    for u, (b, h) in enumerate(units):
        st_ref[b, h] = sdec_ref[h] * st_ref[b, h] + grow[u]
        o = inner[u] + cross[u] * qdec_ref[h]
        o = o * lax.rsqrt(jnp.mean(o * o, axis=-1, keepdims=True) + RMS_EPS)
        o_ref[b, :, vw(h)] = (o * gt_ref[b, :, vw(h)]).astype(o_ref.dtype)

    @pl.when(c == pl.num_programs(0) - 1)
    def _():
        sfin_ref[...] = st_ref[...]


def _ret_call(q, k, v, gate, state0, cs):
    b, n, _ = q.shape
    log_gamma = jnp.log1p(-jnp.exp2(-5.0 - jnp.arange(RET_HEADS, dtype=F32)))
    pos = jnp.arange(cs, dtype=F32)
    diff = pos[:, None] - pos[None, :]
    dec = jnp.where(diff >= 0, jnp.exp(log_gamma[:, None, None] * jnp.maximum(diff, 0.0)), 0.0)
    qdec = jnp.exp(log_gamma[:, None] * (pos + 1.0))[..., None]
    kdec = jnp.exp(log_gamma[:, None] * (cs - 1.0 - pos))[..., None]
    sdec = jnp.exp(log_gamma * cs)
    seq = lambda w: pl.BlockSpec((b, cs, w), lambda c: (0, c, 0))
    st = pl.BlockSpec((b, RET_HEADS, RET_QK_DIM, RET_V_DIM), lambda c: (0, 0, 0, 0))
    return pl.pallas_call(
        _ret_kernel,
        grid=(n // cs,),
        in_specs=[pl.BlockSpec(memory_space=pltpu.SMEM), seq(RET_QK_W), seq(RET_QK_W), seq(RET_V_W),
                  seq(RET_V_W), st, _full_spec(dec.shape), _full_spec(qdec.shape), _full_spec(kdec.shape)],
        out_specs=[seq(RET_V_W), st],
        out_shape=[jax.ShapeDtypeStruct((b, n, RET_V_W), BF16),
                   jax.ShapeDtypeStruct((b, RET_HEADS, RET_QK_DIM, RET_V_DIM), F32)],
        scratch_shapes=[pltpu.VMEM((b, RET_HEADS, RET_QK_DIM, RET_V_DIM), F32)],
        compiler_params=_params(("arbitrary",)),
        name="retention",
    )(sdec, q, k, v, gate, state0, dec, qdec, kdec)


def _head_norm_rope(t, bd, gain, cos2, sin2):
    wide = bd.shape[0]
    parts = []
    for c in range(t.shape[1] // wide):
        sq = t[:, c * wide:(c + 1) * wide]
        sq = sq * sq
        hi = sq.astype(BF16)
        lo = (sq - hi.astype(F32)).astype(BF16)
        parts.append(_dot(hi, bd) + _dot(lo, bd))
    ms = jnp.concatenate(parts, axis=1) * (1.0 / SWA_HEAD_DIM)
    y = t * lax.rsqrt(ms + RMS_EPS) * gain
    w = t.shape[1]
    reps = w // LANES
    quarter = SWA_HEAD_DIM // 2
    first_half = (lax.broadcasted_iota(jnp.int32, t.shape, 1) % SWA_HEAD_DIM) < quarter
    partner = jnp.where(first_half, pltpu.roll(y, w - quarter, axis=1), pltpu.roll(y, quarter, axis=1))
    return y * jnp.concatenate([cos2] * reps, axis=1) + partner * jnp.concatenate([sin2] * reps, axis=1)


def _odd_proj_kernel(x_ref, g_ref, w_ref, bd_ref, gain_ref, cos_ref, sin_ref, q_ref, k_ref, v_ref):
    h = _rms(x_ref[...], g_ref[...]).astype(BF16)
    qk_w = SWA_Q_W + SWA_KV_W
    qk = _dot(h, w_ref[:, :qk_w])
    v_ref[...] = _dot(h, w_ref[:, qk_w:])
    qk = _head_norm_rope(qk, bd_ref[...], gain_ref[...], cos_ref[...], sin_ref[...])
    q_ref[...] = (qk[:, :SWA_Q_W] * (LOG2E * SWA_HEAD_DIM ** -0.5)).astype(BF16)
    k_ref[...] = qk[:, SWA_Q_W:]


def _odd_proj_call(x, g, w_in, gain, cos2, sin2, tm):
    t, d = x.shape
    n_tab = cos2.shape[0] // tm
    head = jnp.arange(MXU_TILE) // SWA_HEAD_DIM
    bd = (head[:, None] == head[None, :]).astype(BF16)
    row = lambda w: pl.BlockSpec((tm, w), lambda i: (i, 0))
    tab = pl.BlockSpec((tm, LANES), lambda i: (i % n_tab, 0))
    ins = [g, w_in, bd, gain]
    return pl.pallas_call(
        _odd_proj_kernel,
        grid=(t // tm,),
        in_specs=[row(d)] + [_full_spec(a.shape) for a in ins] + [tab, tab],
        out_specs=[row(SWA_Q_W), row(SWA_KV_W), row(SWA_KV_W)],
        out_shape=[jax.ShapeDtypeStruct((t, SWA_Q_W), BF16), jax.ShapeDtypeStruct((t, SWA_KV_W), F32),
                   jax.ShapeDtypeStruct((t, SWA_KV_W), F32)],
        compiler_params=_params(("parallel",)),
        name="odd_proj",
    )(x, *ins, cos2, sin2)


def _swa_kernel(sink_ref, q_ref, kp_ref, kc_ref, vp_ref, vc_ref, o_ref, *, rows, first_prev_valid):
    i = pl.program_id(1)
    nch = rows // CHUNK
    wk = WINDOW + CHUNK
    nk = WINDOW + rows
    grp = SWA_HEADS // SWA_KV_HEADS
    gw = grp * CHUNK
    key_low = lax.broadcasted_iota(jnp.int32, (nk, LANES), 1) < SWA_HEAD_DIM
    low = lax.broadcasted_iota(jnp.int32, (CHUNK, LANES), 1) < SWA_HEAD_DIM
    if not first_prev_valid:
        krow = lax.broadcasted_iota(jnp.int32, (nk, LANES), 0)
        klane = lax.broadcasted_iota(jnp.int32, (nk, LANES), 1)
        k_bias = jnp.where((krow < WINDOW) & (klane == 0) & (i == 0), SWA_MASKED, 0.0).astype(BF16)
        q_one = jnp.where(lax.broadcasted_iota(jnp.int32, (gw, LANES), 1) == 0, 1.0, 0.0).astype(BF16)

    s, vj_t = [], []
    for j in range(SWA_KV_HEADS):
        pair, odd = divmod(j, 2)
        lanes = slice(pair * LANES, (pair + 1) * LANES)
        k_pair = jnp.concatenate([kp_ref[:, lanes], kc_ref[:, lanes]], axis=0)
        own = jnp.where(key_low, 0.0, k_pair) if odd else jnp.where(key_low, k_pair, 0.0)
        k_both = (own + pltpu.roll(own, SWA_HEAD_DIM, axis=1)).astype(BF16)
        v_t = jnp.concatenate([vp_ref[:, lanes], vc_ref[:, lanes]], axis=0).T
        vj_t.append(v_t[odd * SWA_HEAD_DIM:(odd + 1) * SWA_HEAD_DIM].astype(BF16))
        if not first_prev_valid:
            k_both = jnp.concatenate([k_both, k_bias], axis=1)
        for c in range(nch):
            q_four = []
            for qp in (2 * j, 2 * j + 1):
                q_pair = q_ref[c * CHUNK:(c + 1) * CHUNK, qp * LANES:(qp + 1) * LANES]
                zero = jnp.zeros_like(q_pair)
                q_four += [jnp.where(low, q_pair, zero), jnp.where(low, zero, q_pair)]
            q_four = jnp.concatenate(q_four, axis=0)
            if not first_prev_valid:
                q_four = jnp.concatenate([q_four, q_one], axis=1)
            s.append(_dot_nt(k_both[c * CHUNK:c * CHUNK + wk], q_four))
    s = jnp.concatenate(s, axis=1)
    sink = jnp.concatenate([jnp.full((1, CHUNK), sink_ref[j * grp + g] * LOG2E, F32)
                            for j in range(SWA_KV_HEADS) for _ in range(nch) for g in range(grp)], axis=1)
    m = jnp.maximum(jnp.max(s, axis=0, keepdims=True), sink)
    p = jnp.exp2(s - m)
    inv = 1.0 / (jnp.sum(p, axis=0, keepdims=True) + jnp.exp2(sink - m))
    p = p.astype(BF16)
    o_t = []
    for j in range(SWA_KV_HEADS):
        for c in range(nch):
            cols = slice((j * nch + c) * gw, (j * nch + c + 1) * gw)
            o_t.append(_dot(vj_t[j][:, c * CHUNK:c * CHUNK + wk], p[:, cols]) * inv[:, cols])
    for c0 in range(0, nch, 2):
        pieces = []
        for j in range(SWA_KV_HEADS):
            for t in range(grp // 2):
                tiles = [o_t[j * nch + c][:, t * LANES:(t + 1) * LANES] for c in range(c0, min(c0 + 2, nch))]
                if len(tiles) == 2:
                    pieces += [jnp.where(low, tiles[0], pltpu.roll(tiles[1], CHUNK, axis=1)),
                               jnp.where(low, pltpu.roll(tiles[0], CHUNK, axis=1), tiles[1])]
                else:
                    pieces += [tiles[0], pltpu.roll(tiles[0], CHUNK, axis=1)]
        out = jnp.concatenate(pieces, axis=0).T
        r1 = min((c0 + 2) * CHUNK, rows)
        o_ref[c0 * CHUNK:r1] = out[:r1 - c0 * CHUNK].astype(o_ref.dtype)


def _swa_call(q, k_prev, k_cur, v_prev, v_cur, sinks, rows, same_array):
    b, n, _ = q.shape
    per = rows // WINDOW if same_array else 0
    cur = lambda w: pl.BlockSpec((None, rows, w), lambda bi, i: (bi, i, 0))
    prev = pl.BlockSpec((None, WINDOW, SWA_KV_W), lambda bi, i: (bi, jnp.maximum(i * per - 1, 0), 0))
    return pl.pallas_call(
        functools.partial(_swa_kernel, rows=rows, first_prev_valid=not same_array),
        grid=(b, n // rows),
        in_specs=[pl.BlockSpec(memory_space=pltpu.SMEM), cur(SWA_Q_W), prev, cur(SWA_KV_W), prev, cur(SWA_KV_W)],
        out_specs=cur(SWA_Q_W),
        out_shape=jax.ShapeDtypeStruct((b, n, SWA_Q_W), BF16),
        compiler_params=_params(("parallel", "parallel")),
        name="swa",
    )(sinks, q, k_prev, k_cur, v_prev, v_cur)


def _rope_tables(pos0, n, head_dim):
    half = head_dim // 2
    lane = jnp.arange(LANES)
    inv = jnp.power(ROPE_THETA, -(lane % half).astype(F32) / half)
    sign = jnp.where(lane % head_dim < half, -1.0, 1.0)
    fine = min(n, LANES)
    ang_c = (pos0 + fine * jnp.arange(n // fine)).astype(F32)[:, None] * inv[None, :]
    ang_f = jnp.arange(fine, dtype=F32)[:, None] * inv[None, :]
    cos_c, sin_c = jnp.cos(ang_c)[:, None, :], jnp.sin(ang_c)[:, None, :]
    cos_f, sin_f = jnp.cos(ang_f)[None, :, :], jnp.sin(ang_f)[None, :, :]
    return ((cos_c * cos_f - sin_c * sin_f).reshape(n, LANES),
            ((sin_c * cos_f + cos_c * sin_f) * sign).reshape(n, LANES))


def _prep_weights(norm_g, ffn_w_gate, ffn_w_up, ffn_w_down, even_w_in, even_w_out, odd_w_in, odd_w_out,
                  odd_q_norm, odd_k_norm):
    depth = norm_g.shape[0]
    prm = {"g": norm_g[:, :, None, :], "even": [], "odd": [], "depth": depth,
           "ffn": (ffn_w_gate.astype(BF16), ffn_w_up.astype(BF16), ffn_w_down.astype(BF16))}
    for i in range(even_w_in.shape[0]):
        prm["even"].append((even_w_in[i].astype(BF16), even_w_out[i].astype(BF16)))
    for i in range(odd_w_in.shape[0]):
        gain = jnp.concatenate([jnp.tile(odd_q_norm[i], SWA_HEADS), jnp.tile(odd_k_norm[i], SWA_KV_HEADS)])
        prm["odd"].append((odd_w_in[i].astype(BF16), odd_w_out[i].astype(BF16), gain[None, :]))
    return prm


def _trunk(x, pos0, prm, sinks, caches, tm, sb_qb, sb_kb, ret_cs, swa_rows):
    b, n, d = x.shape
    t = b * n
    xf = x.reshape(t, d)
    depth = prm["depth"]
    tab_rows = n if n % tm == 0 else t
    tile_tab = lambda tb: tb if tab_rows == n else jnp.tile(tb, (b, 1))
    cos_r, sin_r = [tile_tab(tb) for tb in _rope_tables(pos0, n, RET_QK_DIM)]
    cos_s, sin_s = [tile_tab(tb) for tb in _rope_tables(pos0, n, SWA_HEAD_DIM)]
    idx = jnp.arange(sb_kb)
    negu = jnp.where(idx[:, None] >= idx[None, :], -1.0, 0.0).astype(BF16)
    sb_k, sb_v, ret, swa_k, swa_v = [], [], [], [], []
    mix, w_mix = [], None
    for layer in range(depth):
        i = layer // 2
        g = prm["g"][layer]
        xf = _ffn_call(xf, mix, w_mix, g[0], prm["ffn"], layer, 0, tm)
        if layer % 2 == 0:
            w_in, w_mix = prm["even"][i]
            qa, ka, kab, va, vab, qr, kr, vr, gt = _even_proj_call(xf, g[1], w_in, cos_r, sin_r, tm)
            sb_k.append(ka.reshape(b, n, SB_HEADS, SB_HEAD_DIM))
            sb_v.append(va.reshape(b, n, SB_HEADS, SB_HEAD_DIM))
            r3 = lambda a: a.reshape(b, n, a.shape[-1])
            if caches is None:
                k_all, v_all, q_pos0 = r3(kab), r3(vab), 0
                state0 = jnp.zeros((b, RET_HEADS, RET_QK_DIM, RET_V_DIM), F32)
            else:
                past = caches["sb_k"].shape[2]
                padded = -(-(past + n) // sb_kb) * sb_kb
                cat = lambda cache, new: jnp.pad(
                    jnp.concatenate([cache.reshape(b, past, SB_W).astype(BF16), r3(new)], axis=1),
                    ((0, 0), (0, padded - past - n), (0, 0)))
                k_all, v_all, q_pos0 = cat(caches["sb_k"][i], kab), cat(caches["sb_v"][i], vab), past
                state0 = caches["ret"][i]
            o_sb = _sb_call(r3(qa), k_all, v_all, negu, sb_qb, q_pos0)
            o_r, st = _ret_call(r3(qr), r3(kr), r3(vr), r3(gt), state0, ret_cs)
            ret.append(st)
            mix = [o_sb.reshape(t, SB_W), o_r.reshape(t, RET_V_W)]
        else:
            w_in, w_mix, gain = prm["odd"][i]
            q, k, v = _odd_proj_call(xf, g[1], w_in, gain, cos_s, sin_s, tm)
            r3 = lambda a: a.reshape(b, n, a.shape[-1])
            k3, v3 = r3(k), r3(v)
            if caches is None:
                o = _swa_call(r3(q), k3, k3, v3, v3, sinks[i], swa_rows, True)
                keep = min(WINDOW, n)
                k_rows, v_rows = k3[:, n - keep:], v3[:, n - keep:]
            else:
                kc = caches["swa_k"][i].reshape(b, -1, SWA_KV_W)
                vc = caches["swa_v"][i].reshape(b, -1, SWA_KV_W)
                o = _swa_call(r3(q), kc, k3, vc, v3, sinks[i], swa_rows, False)
                k_rows, v_rows = k3, v3
            swa_k.append(k_rows.reshape(b, -1, SWA_KV_HEADS, SWA_HEAD_DIM))
            swa_v.append(v_rows.reshape(b, -1, SWA_KV_HEADS, SWA_HEAD_DIM))
            mix = [o.reshape(t, SWA_Q_W)]
        xf = _ffn_call(xf, mix, w_mix, g[2], prm["ffn"], layer, 1, tm)
        mix, w_mix = [], None
    stack = lambda parts: parts[0][None] if len(parts) == 1 else jnp.stack(parts)
    return (xf.reshape(b, n, d), stack(sb_k), stack(sb_v), stack(ret), stack(swa_k), stack(swa_v))


def kernel(x_prompt, x_sample, cache_sb_k, cache_sb_v, state_ret, cache_swa_k, cache_swa_v, norm_g, ffn_w_gate,
           ffn_w_up, ffn_w_down, even_w_in, even_w_out, odd_w_in, odd_w_out, odd_q_norm, odd_k_norm, odd_sinks):
    prm = _prep_weights(norm_g, ffn_w_gate, ffn_w_up, ffn_w_down, even_w_in, even_w_out, odd_w_in, odd_w_out,
                        odd_q_norm, odd_k_norm)
    past = cache_sb_k.shape[2]
    n_s = x_sample.shape[1]
    y_p, sb_k_p, sb_v_p, ret_p, swa_k_p, swa_v_p = _trunk(
        x_prompt, 0, prm, odd_sinks, None, tm=512, sb_qb=1024, sb_kb=256, ret_cs=256, swa_rows=512)
    caches = {"sb_k": cache_sb_k, "sb_v": cache_sb_v, "ret": state_ret, "swa_k": cache_swa_k, "swa_v": cache_swa_v}
    y_s, sb_k_s, sb_v_s, ret_s, swa_k_s, swa_v_s = _trunk(
        x_sample, past, prm, odd_sinks, caches, tm=x_sample.shape[0] * n_s, sb_qb=n_s, sb_kb=256,
        ret_cs=n_s, swa_rows=n_s)
    return (y_p, y_s, sb_k_p, sb_v_p, ret_p, swa_k_p, swa_v_p, sb_k_s, sb_v_s, ret_s, swa_k_s, swa_v_s)
```

```python
import functools
import math

import jax
import jax.numpy as jnp
from jax import lax
from jax.experimental import pallas as pl
from jax.experimental.pallas import tpu as pltpu

F32 = jnp.float32
BF16 = jnp.bfloat16

RMS_EPS = 1e-6
ROPE_THETA = 10000.0
CHUNK = 64
SB_HEADS = 8
SB_HEAD_DIM = 64
RET_HEADS = 4
RET_QK_DIM = 128
RET_V_DIM = 256
SWA_HEADS = 16
SWA_KV_HEADS = 4
SWA_HEAD_DIM = 64
WINDOW = 128

SB_W = SB_HEADS * SB_HEAD_DIM
RET_QK_W = RET_HEADS * RET_QK_DIM
RET_V_W = RET_HEADS * RET_V_DIM
SWA_Q_W = SWA_HEADS * SWA_HEAD_DIM
SWA_KV_W = SWA_KV_HEADS * SWA_HEAD_DIM
EVEN_CUTS = (0, SB_W, 2 * SB_W, 3 * SB_W, 3 * SB_W + RET_QK_W, 3 * SB_W + 2 * RET_QK_W,
             3 * SB_W + 2 * RET_QK_W + RET_V_W, 3 * SB_W + 2 * RET_QK_W + 2 * RET_V_W)

LANES = 128
MXU_TILE = 256
FF_CHUNK = MXU_TILE
VMEM_LIMIT = 56 * 1024 * 1024
LOG2E = math.log2(math.e)
SOFTPLUS2_LINEAR = 40.0
SWA_MASKED = -1e30
SB_STATIC_BLOCKS = 3
SB_DEAD_LOG2 = -170.0


def _dot(a, b):
    return jnp.dot(a, b, preferred_element_type=F32)


def _dot_nt(a, b):
    return lax.dot_general(a, b, (((1,), (1,)), ((), ())), preferred_element_type=F32)


def _dot_tn(a, b):
    return lax.dot_general(a, b, (((0,), (0,)), ((), ())), preferred_element_type=F32)


def _rms(x, g):
    return x * lax.rsqrt(jnp.mean(x * x, axis=-1, keepdims=True) + RMS_EPS) * g


def _full_spec(shape):
    nd = len(shape)
    return pl.BlockSpec(shape, lambda *_: (0,) * nd, pipeline_mode=pl.Buffered(1))


def _params(sem):
    return pltpu.CompilerParams(dimension_semantics=sem, vmem_limit_bytes=VMEM_LIMIT)


def _ffn_kernel(*refs, n_mix):
    n_in = 1 + n_mix
    cur, nxt = refs[:n_in], refs[n_in:2 * n_in]
    wmix_ref = refs[2 * n_in] if n_mix else None
    g_ref, wg_ref, wu_ref, wd_ref, o_ref, ha_ref, hb_ref, ra_ref, rb_ref, act_ref = refs[2 * n_in + bool(n_mix):]
    half = o_ref.shape[0] // 2
    lo, hi = slice(0, half), slice(half, 2 * half)

    def prepare(src, rws, h_ref, res_ref):
        x = src[0][rws]
        r0 = 0
        for a_ref in src[1:]:
            x = x + _dot(a_ref[rws], wmix_ref[r0:r0 + a_ref.shape[1]])
            r0 += a_ref.shape[1]
        res_ref[...] = x
        h_ref[...] = _rms(x, g_ref[...]).astype(BF16)

    def main(h_ref, res_ref, rws):
        for c in range(wg_ref.shape[1] // FF_CHUNK):
            cols = slice(c * FF_CHUNK, (c + 1) * FF_CHUNK)
            gate = _dot(h_ref[...], wg_ref[:, cols])
            up = _dot(h_ref[...], wu_ref[:, cols])
            act_ref[rws, cols] = (gate * jax.nn.sigmoid(gate) * up).astype(BF16)
        o_ref[rws] = res_ref[...] + 0.5 * _dot(act_ref[rws], wd_ref[...])

    @pl.when(pl.program_id(0) == 0)
    def _():
        prepare(cur, lo, ha_ref, ra_ref)

    prepare(cur, hi, hb_ref, rb_ref)
    main(ha_ref, ra_ref, lo)
    prepare(nxt, lo, ha_ref, ra_ref)
    main(hb_ref, rb_ref, hi)


def _ffn_call(x, mix, w_mix, g, ffn_w, layer, slot, tm):
    wg, wu, wd = ffn_w
    stack_spec = lambda w: pl.BlockSpec((None, None) + w.shape[2:], lambda i: (layer, slot, 0, 0),
                                        pipeline_mode=pl.Buffered(1))
    t, d = x.shape
    half = tm // 2
    last = t // tm - 1
    row = lambda w: pl.BlockSpec((tm, w), lambda i: (i, 0))
    nxt = lambda w: pl.BlockSpec((half, w), lambda i: (2 * jnp.minimum(i + 1, last), 0))
    acts = [x] + list(mix)
    in_specs = [row(a.shape[1]) for a in acts] + [nxt(a.shape[1]) for a in acts]
    args = acts + acts
    if mix:
        in_specs.append(_full_spec(w_mix.shape))
        args.append(w_mix)
    in_specs += [_full_spec(g.shape), stack_spec(wg), stack_spec(wu), stack_spec(wd)]
    args += [g, wg, wu, wd]
    return pl.pallas_call(
        functools.partial(_ffn_kernel, n_mix=len(mix)),
        grid=(t // tm,),
        in_specs=in_specs,
        out_specs=row(d),
        out_shape=jax.ShapeDtypeStruct((t, d), F32),
        scratch_shapes=[pltpu.VMEM((half, d), BF16), pltpu.VMEM((half, d), BF16), pltpu.VMEM((half, d), F32),
                        pltpu.VMEM((half, d), F32), pltpu.VMEM((tm, wg.shape[-1]), BF16)],
        compiler_params=_params(("arbitrary",)),
        name="ffn_mix%d" % len(mix),
    )(*args)


def _rope128(x, cos2, sin2):
    parts = []
    for hd in range(x.shape[1] // LANES):
        sl = x[:, hd * LANES:(hd + 1) * LANES]
        parts.append(sl * cos2 + pltpu.roll(sl, LANES // 2, axis=1) * sin2)
    return jnp.concatenate(parts, axis=1)


def _even_proj_kernel(x_ref, g_ref, w_ref, cos_ref, sin_ref,
                      qa_ref, ka_ref, kab_ref, va_ref, vab_ref, qr_ref, kr_ref, vr_ref, gt_ref):
    h = _rms(x_ref[...], g_ref[...]).astype(BF16)
    proj = lambda s: _dot(h, w_ref[:, EVEN_CUTS[s]:EVEN_CUTS[s + 1]])
    qa_ref[...] = (proj(0) * (LOG2E * SB_HEAD_DIM ** -0.5)).astype(BF16)
    ka = proj(1)
    ka_ref[...] = ka.reshape(ka_ref.shape)
    kab_ref[...] = ka.astype(BF16)
    va = proj(2)
    va_ref[...] = va.reshape(va_ref.shape)
    vab_ref[...] = va.astype(BF16)
    cos2 = cos_ref[...]
    sin2 = sin_ref[...]
    qr_ref[...] = _rope128(proj(3), cos2, sin2).astype(BF16)
    kr_ref[...] = (_rope128(proj(4), cos2, sin2) * (RET_QK_DIM ** -0.5)).astype(BF16)
    vr_ref[...] = proj(5).astype(BF16)
    gate = proj(6)
    gt_ref[...] = gate * jax.nn.sigmoid(gate)


def _even_proj_call(x, g, w_in, cos2, sin2, tm):
    t, d = x.shape
    n_tab = cos2.shape[0] // tm
    row = lambda w: pl.BlockSpec((tm, w), lambda i: (i, 0))
    tab = pl.BlockSpec((tm, LANES), lambda i: (i % n_tab, 0))
    per_head = (SB_HEADS, SB_HEAD_DIM)
    outs = [((SB_W,), BF16), (per_head, F32), ((SB_W,), BF16), (per_head, F32), ((SB_W,), BF16),
            ((RET_QK_W,), BF16), ((RET_QK_W,), BF16), ((RET_V_W,), BF16), ((RET_V_W,), F32)]
    return pl.pallas_call(
        _even_proj_kernel,
        grid=(t // tm,),
        in_specs=[row(d), _full_spec(g.shape), _full_spec(w_in.shape), tab, tab],
        out_specs=[pl.BlockSpec((tm,) + tail, lambda i, nd=len(tail): (i,) + (0,) * nd) for tail, _ in outs],
        out_shape=[jax.ShapeDtypeStruct((t,) + tail, dt) for tail, dt in outs],
        compiler_params=_params(("parallel",)),
        name="even_proj",
    )(x, g, w_in, cos2, sin2)


def _sb_kernel(q_ref, k_ref, v_ref, negu_ref, o_ref, qh_ref, c_ref, acc_ref, *, qb, kb, q_pos0):
    i = pl.program_id(2)
    n_clear = (q_pos0 + i * qb) // kb
    sq = min(qb, kb)
    nsub = qb // sq
    rows = lambda s: slice(s * sq, (s + 1) * sq)
    kstart = lambda j: pl.multiple_of(j * kb, kb)
    q = q_ref[...]
    first = lax.broadcasted_iota(jnp.int32, (qb, LANES), 1) < SB_HEAD_DIM
    qh_ref[0] = jnp.where(first, q, jnp.zeros_like(q))
    qh_ref[1] = jnp.where(first, jnp.zeros_like(q), q)

    def scores(hd, rws, start):
        return _dot_nt(qh_ref[hd, rws], k_ref[pl.ds(start, kb), :])

    def softplus2(z):
        return jnp.where(z > SOFTPLUS2_LINEAR, z, jnp.log2(1.0 + jnp.exp2(z)))

    def v_heads(start):
        vs = v_ref[pl.ds(start, kb), :]
        vfirst = lax.broadcasted_iota(jnp.int32, (kb, LANES), 1) < SB_HEAD_DIM
        return jnp.concatenate([jnp.where(vfirst, vs, jnp.zeros_like(vs)),
                                jnp.where(vfirst, jnp.zeros_like(vs), vs)], axis=0)

    depth = SB_STATIC_BLOCKS
    row = lax.broadcasted_iota(jnp.int32, (qb, kb), 0)
    below = lax.broadcasted_iota(jnp.int32, (qb, kb), 1) < (row & (sq - 1))
    kblock = lambda s, d: kstart(jnp.maximum(n_clear + s - d, 0))

    def apply_masks(t):
        parts = [jnp.where(below, t[:qb], 0.0)]
        for d in range(1, depth):
            for s in range(min(d, nsub)):
                exists = jnp.broadcast_to(n_clear + s - d >= 0, (sq, kb))
                parts.append(jnp.where(exists, t[d * qb + s * sq:d * qb + (s + 1) * sq], 0.0))
            if nsub > d:
                parts.append(t[d * qb + d * sq:(d + 1) * qb])
        return jnp.concatenate(parts, axis=0)

    ws = []
    for hd in range(2):
        z = jnp.concatenate([scores(hd, rows(s), kblock(s, d)) for d in range(depth) for s in range(nsub)], axis=0)
        sp = apply_masks(softplus2(z))
        tail = _dot(sp.astype(BF16), negu_ref[...])
        seen = [jnp.zeros((qb, 1), F32)]
        for d in range(depth):
            seen.append(seen[-1] + tail[d * qb:(d + 1) * qb, :1])
        c_ref[hd] = seen[depth]
        ws.append(apply_masks(jnp.exp2(z + tail + jnp.concatenate(seen[:depth], axis=0))).astype(BF16))
    w = jnp.concatenate(ws, axis=1)
    for s in range(nsub):
        acc = None
        for d in range(depth):
            part = _dot(w[d * qb + s * sq:d * qb + (s + 1) * sq], v_heads(kblock(s, d)))
            acc = part if acc is None else acc + part
        acc_ref[rows(s)] = acc

    def block(start, r0, r1):
        ws = []
        for hd in range(2):
            z = scores(hd, slice(r0, r1), start)
            tail = _dot(softplus2(z).astype(BF16), negu_ref[...])
            ws.append(jnp.exp2(z + tail + c_ref[hd, r0:r1]).astype(BF16))
            c_ref[hd, r0:r1] += tail[:, :1]
        acc_ref[r0:r1] += _dot(jnp.concatenate(ws, axis=1), v_heads(start))

    def alive(r0, r1):
        return (jnp.max(c_ref[:, r0:r1]) > SB_DEAD_LOG2).astype(jnp.int32)

    def run_blocks(j_first, j_last, r0, r1):
        def cond(carry):
            j, live = carry
            return jnp.logical_and(j >= j_last, live > 0)

        def body(carry):
            j, _ = carry
            block(kstart(j), r0, r1)
            return j - 1, alive(r0, r1)

        lax.while_loop(cond, body, (j_first, alive(r0, r1)))

    @pl.when(alive(0, qb) > 0)
    def _():
        for s in range(1, nsub):
            run_blocks(n_clear + s - depth, jnp.maximum(n_clear - depth + 1, 0), s * sq, (s + 1) * sq)
        run_blocks(n_clear - depth, 0, 0, qb)

    o_ref[...] = acc_ref[...].astype(o_ref.dtype)


def _sb_call(q, k, v, negu, qb, q_pos0):
    b, nq, w = q.shape
    nk = k.shape[1]
    kb = negu.shape[0]
    assert q_pos0 % kb == 0 and (qb % kb == 0 or nq == qb < kb) and nk % kb == 0 and nk >= q_pos0 + nq
    return pl.pallas_call(
        functools.partial(_sb_kernel, qb=qb, kb=kb, q_pos0=q_pos0),
        grid=(b, w // LANES, nq // qb),
        in_specs=[pl.BlockSpec((None, qb, LANES), lambda bi, p, i: (bi, i, p)),
                  pl.BlockSpec((None, nk, LANES), lambda bi, p, i: (bi, 0, p)),
                  pl.BlockSpec((None, nk, LANES), lambda bi, p, i: (bi, 0, p)),
                  _full_spec(negu.shape)],
        out_specs=pl.BlockSpec((None, qb, LANES), lambda bi, p, i: (bi, i, p)),
        out_shape=jax.ShapeDtypeStruct((b, nq, w), BF16),
        scratch_shapes=[pltpu.VMEM((2, qb, LANES), BF16), pltpu.VMEM((2, qb, 1), F32),
                        pltpu.VMEM((qb, LANES), F32)],
        compiler_params=_params(("parallel", "parallel", "parallel")),
        name="stick_breaking",
    )(q, k, v, negu)


def _ret_kernel(sdec_ref, q_ref, k_ref, v_ref, gt_ref, s0_ref, dec_ref, qdec_ref, kdec_ref,
                o_ref, sfin_ref, st_ref):
    c = pl.program_id(0)

    @pl.when(c == 0)
    def _():
        st_ref[...] = s0_ref[...]

    units = [(b, h) for b in range(q_ref.shape[0]) for h in range(RET_HEADS)]
    qk = lambda h: slice(h * RET_QK_DIM, (h + 1) * RET_QK_DIM)
    vw = lambda h: slice(h * RET_V_DIM, (h + 1) * RET_V_DIM)
    scores = [_dot_nt(q_ref[b, :, qk(h)], k_ref[b, :, qk(h)]) for b, h in units]
    cross = [_dot(q_ref[b, :, qk(h)], st_ref[b, h].astype(BF16)) for b, h in units]
    kw = [(k_ref[b, :, qk(h)].astype(F32) * kdec_ref[h]).astype(BF16) for b, h in units]
    grow = [_dot_tn(kw[u], v_ref[b, :, vw(h)]) for u, (b, h) in enumerate(units)]
    decayed = [(scores[u] * dec_ref[h]).astype(BF16) for u, (b, h) in enumerate(units)]
    inner = [_dot(decayed[u], v_ref[b, :, vw(h)]) for u, (b, h) in enumerate(units)]
    for u, (b, h) in enumerate(units):
        st_ref[b, h] = sdec_ref[h] * st_ref[b, h] + grow[u]
        o = inner[u] + cross[u] * qdec_ref[h]
        o = o * lax.rsqrt(jnp.mean(o * o, axis=-1, keepdims=True) + RMS_EPS)
        o_ref[b, :, vw(h)] = (o * gt_ref[b, :, vw(h)]).astype(o_ref.dtype)

    @pl.when(c == pl.num_programs(0) - 1)
    def _():
        sfin_ref[...] = st_ref[...]


def _ret_call(q, k, v, gate, state0, cs):
    b, n, _ = q.shape
    log_gamma = jnp.log1p(-jnp.exp2(-5.0 - jnp.arange(RET_HEADS, dtype=F32)))
    pos = jnp.arange(cs, dtype=F32)
    diff = pos[:, None] - pos[None, :]
    dec = jnp.where(diff >= 0, jnp.exp(log_gamma[:, None, None] * jnp.maximum(diff, 0.0)), 0.0)
    qdec = jnp.exp(log_gamma[:, None] * (pos + 1.0))[..., None]
    kdec = jnp.exp(log_gamma[:, None] * (cs - 1.0 - pos))[..., None]
    sdec = jnp.exp(log_gamma * cs)
    seq = lambda w: pl.BlockSpec((b, cs, w), lambda c: (0, c, 0))
    st = pl.BlockSpec((b, RET_HEADS, RET_QK_DIM, RET_V_DIM), lambda c: (0, 0, 0, 0))
    return pl.pallas_call(
        _ret_kernel,
        grid=(n // cs,),
        in_specs=[pl.BlockSpec(memory_space=pltpu.SMEM), seq(RET_QK_W), seq(RET_QK_W), seq(RET_V_W),
                  seq(RET_V_W), st, _full_spec(dec.shape), _full_spec(qdec.shape), _full_spec(kdec.shape)],
        out_specs=[seq(RET_V_W), st],
        out_shape=[jax.ShapeDtypeStruct((b, n, RET_V_W), BF16),
                   jax.ShapeDtypeStruct((b, RET_HEADS, RET_QK_DIM, RET_V_DIM), F32)],
        scratch_shapes=[pltpu.VMEM((b, RET_HEADS, RET_QK_DIM, RET_V_DIM), F32)],
        compiler_params=_params(("arbitrary",)),
        name="retention",
    )(sdec, q, k, v, gate, state0, dec, qdec, kdec)


def _head_norm_rope(t, bd, gain, cos2, sin2):
    wide = bd.shape[0]
    parts = []
    for c in range(t.shape[1] // wide):
        sq = t[:, c * wide:(c + 1) * wide]
        sq = sq * sq
        hi = sq.astype(BF16)
        lo = (sq - hi.astype(F32)).astype(BF16)
        parts.append(_dot(hi, bd) + _dot(lo, bd))
    ms = jnp.concatenate(parts, axis=1) * (1.0 / SWA_HEAD_DIM)
    y = t * lax.rsqrt(ms + RMS_EPS) * gain
    w = t.shape[1]
    reps = w // LANES
    quarter = SWA_HEAD_DIM // 2
    first_half = (lax.broadcasted_iota(jnp.int32, t.shape, 1) % SWA_HEAD_DIM) < quarter
    partner = jnp.where(first_half, pltpu.roll(y, w - quarter, axis=1), pltpu.roll(y, quarter, axis=1))
    return y * jnp.concatenate([cos2] * reps, axis=1) + partner * jnp.concatenate([sin2] * reps, axis=1)


def _odd_proj_kernel(x_ref, g_ref, w_ref, bd_ref, gain_ref, cos_ref, sin_ref, q_ref, k_ref, v_ref):
    h = _rms(x_ref[...], g_ref[...]).astype(BF16)
    qk_w = SWA_Q_W + SWA_KV_W
    qk = _dot(h, w_ref[:, :qk_w])
    v_ref[...] = _dot(h, w_ref[:, qk_w:])
    qk = _head_norm_rope(qk, bd_ref[...], gain_ref[...], cos_ref[...], sin_ref[...])
    q_ref[...] = (qk[:, :SWA_Q_W] * (LOG2E * SWA_HEAD_DIM ** -0.5)).astype(BF16)
    k_ref[...] = qk[:, SWA_Q_W:]


def _odd_proj_call(x, g, w_in, gain, cos2, sin2, tm):
    t, d = x.shape
    n_tab = cos2.shape[0] // tm
    head = jnp.arange(MXU_TILE) // SWA_HEAD_DIM
    bd = (head[:, None] == head[None, :]).astype(BF16)
    row = lambda w: pl.BlockSpec((tm, w), lambda i: (i, 0))
    tab = pl.BlockSpec((tm, LANES), lambda i: (i % n_tab, 0))
    ins = [g, w_in, bd, gain]
    return pl.pallas_call(
        _odd_proj_kernel,
        grid=(t // tm,),
        in_specs=[row(d)] + [_full_spec(a.shape) for a in ins] + [tab, tab],
        out_specs=[row(SWA_Q_W), row(SWA_KV_W), row(SWA_KV_W)],
        out_shape=[jax.ShapeDtypeStruct((t, SWA_Q_W), BF16), jax.ShapeDtypeStruct((t, SWA_KV_W), F32),
                   jax.ShapeDtypeStruct((t, SWA_KV_W), F32)],
        compiler_params=_params(("parallel",)),
        name="odd_proj",
    )(x, *ins, cos2, sin2)


def _swa_kernel(sink_ref, q_ref, kp_ref, kc_ref, vp_ref, vc_ref, o_ref, *, rows, first_prev_valid):
    i = pl.program_id(1)
    nch = rows // CHUNK
    wk = WINDOW + CHUNK
    nk = WINDOW + rows
    grp = SWA_HEADS // SWA_KV_HEADS
    gw = grp * CHUNK
    key_low = lax.broadcasted_iota(jnp.int32, (nk, LANES), 1) < SWA_HEAD_DIM
    low = lax.broadcasted_iota(jnp.int32, (CHUNK, LANES), 1) < SWA_HEAD_DIM
    if not first_prev_valid:
        krow = lax.broadcasted_iota(jnp.int32, (nk, LANES), 0)
        klane = lax.broadcasted_iota(jnp.int32, (nk, LANES), 1)
        k_bias = jnp.where((krow < WINDOW) & (klane == 0) & (i == 0), SWA_MASKED, 0.0).astype(BF16)
        q_one = jnp.where(lax.broadcasted_iota(jnp.int32, (gw, LANES), 1) == 0, 1.0, 0.0).astype(BF16)

    s, vj_t = [], []
    for j in range(SWA_KV_HEADS):
        pair, odd = divmod(j, 2)
        lanes = slice(pair * LANES, (pair + 1) * LANES)
        k_pair = jnp.concatenate([kp_ref[:, lanes], kc_ref[:, lanes]], axis=0)
        own = jnp.where(key_low, 0.0, k_pair) if odd else jnp.where(key_low, k_pair, 0.0)
        k_both = (own + pltpu.roll(own, SWA_HEAD_DIM, axis=1)).astype(BF16)
        v_t = jnp.concatenate([vp_ref[:, lanes], vc_ref[:, lanes]], axis=0).T
        vj_t.append(v_t[odd * SWA_HEAD_DIM:(odd + 1) * SWA_HEAD_DIM].astype(BF16))
        if not first_prev_valid:
            k_both = jnp.concatenate([k_both, k_bias], axis=1)
        for c in range(nch):
            q_four = []
            for qp in (2 * j, 2 * j + 1):
                q_pair = q_ref[c * CHUNK:(c + 1) * CHUNK, qp * LANES:(qp + 1) * LANES]
                zero = jnp.zeros_like(q_pair)
                q_four += [jnp.where(low, q_pair, zero), jnp.where(low, zero, q_pair)]
            q_four = jnp.concatenate(q_four, axis=0)
            if not first_prev_valid:
                q_four = jnp.concatenate([q_four, q_one], axis=1)
            s.append(_dot_nt(k_both[c * CHUNK:c * CHUNK + wk], q_four))
    s = jnp.concatenate(s, axis=1)
    sink = jnp.concatenate([jnp.full((1, CHUNK), sink_ref[j * grp + g] * LOG2E, F32)
                            for j in range(SWA_KV_HEADS) for _ in range(nch) for g in range(grp)], axis=1)
    m = jnp.maximum(jnp.max(s, axis=0, keepdims=True), sink)
    p = jnp.exp2(s - m)
    inv = 1.0 / (jnp.sum(p, axis=0, keepdims=True) + jnp.exp2(sink - m))
    p = p.astype(BF16)
    o_t = []
    for j in range(SWA_KV_HEADS):
        for c in range(nch):
            cols = slice((j * nch + c) * gw, (j * nch + c + 1) * gw)
            o_t.append(_dot(vj_t[j][:, c * CHUNK:c * CHUNK + wk], p[:, cols]) * inv[:, cols])
    for c0 in range(0, nch, 2):
        pieces = []
        for j in range(SWA_KV_HEADS):
            for t in range(grp // 2):
                tiles = [o_t[j * nch + c][:, t * LANES:(t + 1) * LANES] for c in range(c0, min(c0 + 2, nch))]
                if len(tiles) == 2:
                    pieces += [jnp.where(low, tiles[0], pltpu.roll(tiles[1], CHUNK, axis=1)),
                               jnp.where(low, pltpu.roll(tiles[0], CHUNK, axis=1), tiles[1])]
                else:
                    pieces += [tiles[0], pltpu.roll(tiles[0], CHUNK, axis=1)]
        out = jnp.concatenate(pieces, axis=0).T
        r1 = min((c0 + 2) * CHUNK, rows)
        o_ref[c0 * CHUNK:r1] = out[:r1 - c0 * CHUNK].astype(o_ref.dtype)


def _swa_call(q, k_prev, k_cur, v_prev, v_cur, sinks, rows, same_array):
    b, n, _ = q.shape
    per = rows // WINDOW if same_array else 0
    cur = lambda w: pl.BlockSpec((None, rows, w), lambda bi, i: (bi, i, 0))
    prev = pl.BlockSpec((None, WINDOW, SWA_KV_W), lambda bi, i: (bi, jnp.maximum(i * per - 1, 0), 0))
    return pl.pallas_call(
        functools.partial(_swa_kernel, rows=rows, first_prev_valid=not same_array),
        grid=(b, n // rows),
        in_specs=[pl.BlockSpec(memory_space=pltpu.SMEM), cur(SWA_Q_W), prev, cur(SWA_KV_W), prev, cur(SWA_KV_W)],
        out_specs=cur(SWA_Q_W),
        out_shape=jax.ShapeDtypeStruct((b, n, SWA_Q_W), BF16),
        compiler_params=_params(("parallel", "parallel")),
        name="swa",
    )(sinks, q, k_prev, k_cur, v_prev, v_cur)


def _rope_tables(pos0, n, head_dim):
    half = head_dim // 2
    lane = jnp.arange(LANES)
    inv = jnp.power(ROPE_THETA, -(lane % half).astype(F32) / half)
    sign = jnp.where(lane % head_dim < half, -1.0, 1.0)
    fine = min(n, LANES)
    ang_c = (pos0 + fine * jnp.arange(n // fine)).astype(F32)[:, None] * inv[None, :]
    ang_f = jnp.arange(fine, dtype=F32)[:, None] * inv[None, :]
    cos_c, sin_c = jnp.cos(ang_c)[:, None, :], jnp.sin(ang_c)[:, None, :]
    cos_f, sin_f = jnp.cos(ang_f)[None, :, :], jnp.sin(ang_f)[None, :, :]
    return ((cos_c * cos_f - sin_c * sin_f).reshape(n, LANES),
            ((sin_c * cos_f + cos_c * sin_f) * sign).reshape(n, LANES))


def _prep_weights(norm_g, ffn_w_gate, ffn_w_up, ffn_w_down, even_w_in, even_w_out, odd_w_in, odd_w_out,
                  odd_q_norm, odd_k_norm):
    depth = norm_g.shape[0]
    prm = {"g": norm_g[:, :, None, :], "even": [], "odd": [], "depth": depth,
           "ffn": (ffn_w_gate.astype(BF16), ffn_w_up.astype(BF16), ffn_w_down.astype(BF16))}
    for i in range(even_w_in.shape[0]):
        prm["even"].append((even_w_in[i].astype(BF16), even_w_out[i].astype(BF16)))
    for i in range(odd_w_in.shape[0]):
        gain = jnp.concatenate([jnp.tile(odd_q_norm[i], SWA_HEADS), jnp.tile(odd_k_norm[i], SWA_KV_HEADS)])
        prm["odd"].append((odd_w_in[i].astype(BF16), odd_w_out[i].astype(BF16), gain[None, :]))
    return prm


def _trunk(x, pos0, prm, sinks, caches, tm, sb_qb, sb_kb, ret_cs, swa_rows):
    b, n, d = x.shape
    t = b * n
    xf = x.reshape(t, d)
    depth = prm["depth"]
    tab_rows = n if n % tm == 0 else t
    tile_tab = lambda tb: tb if tab_rows == n else jnp.tile(tb, (b, 1))
    cos_r, sin_r = [tile_tab(tb) for tb in _rope_tables(pos0, n, RET_QK_DIM)]
    cos_s, sin_s = [tile_tab(tb) for tb in _rope_tables(pos0, n, SWA_HEAD_DIM)]
    idx = jnp.arange(sb_kb)
    negu = jnp.where(idx[:, None] >= idx[None, :], -1.0, 0.0).astype(BF16)
    sb_k, sb_v, ret, swa_k, swa_v = [], [], [], [], []
    mix, w_mix = [], None
    for layer in range(depth):
        i = layer // 2
        g = prm["g"][layer]
        xf = _ffn_call(xf, mix, w_mix, g[0], prm["ffn"], layer, 0, tm)
        if layer % 2 == 0:
            w_in, w_mix = prm["even"][i]
            qa, ka, kab, va, vab, qr, kr, vr, gt = _even_proj_call(xf, g[1], w_in, cos_r, sin_r, tm)
            sb_k.append(ka.reshape(b, n, SB_HEADS, SB_HEAD_DIM))
            sb_v.append(va.reshape(b, n, SB_HEADS, SB_HEAD_DIM))
            r3 = lambda a: a.reshape(b, n, a.shape[-1])
            if caches is None:
                k_all, v_all, q_pos0 = r3(kab), r3(vab), 0
                state0 = jnp.zeros((b, RET_HEADS, RET_QK_DIM, RET_V_DIM), F32)
            else:
                past = caches["sb_k"].shape[2]
                padded = -(-(past + n) // sb_kb) * sb_kb
                cat = lambda cache, new: jnp.pad(
                    jnp.concatenate([cache.reshape(b, past, SB_W).astype(BF16), r3(new)], axis=1),
                    ((0, 0), (0, padded - past - n), (0, 0)))
                k_all, v_all, q_pos0 = cat(caches["sb_k"][i], kab), cat(caches["sb_v"][i], vab), past
                state0 = caches["ret"][i]
            o_sb = _sb_call(r3(qa), k_all, v_all, negu, sb_qb, q_pos0)
            o_r, st = _ret_call(r3(qr), r3(kr), r3(vr), r3(gt), state0, ret_cs)
            ret.append(st)
            mix = [o_sb.reshape(t, SB_W), o_r.reshape(t, RET_V_W)]
        else:
            w_in, w_mix, gain = prm["odd"][i]
            q, k, v = _odd_proj_call(xf, g[1], w_in, gain, cos_s, sin_s, tm)
            r3 = lambda a: a.reshape(b, n, a.shape[-1])
            k3, v3 = r3(k), r3(v)
            if caches is None:
                o = _swa_call(r3(q), k3, k3, v3, v3, sinks[i], swa_rows, True)
                keep = min(WINDOW, n)
                k_rows, v_rows = k3[:, n - keep:], v3[:, n - keep:]
            else:
                kc = caches["swa_k"][i].reshape(b, -1, SWA_KV_W)
                vc = caches["swa_v"][i].reshape(b, -1, SWA_KV_W)
                o = _swa_call(r3(q), kc, k3, vc, v3, sinks[i], swa_rows, False)
                k_rows, v_rows = k3, v3
            swa_k.append(k_rows.reshape(b, -1, SWA_KV_HEADS, SWA_HEAD_DIM))
            swa_v.append(v_rows.reshape(b, -1, SWA_KV_HEADS, SWA_HEAD_DIM))
            mix = [o.reshape(t, SWA_Q_W)]
        xf = _ffn_call(xf, mix, w_mix, g[2], prm["ffn"], layer, 1, tm)
        mix, w_mix = [], None
    stack = lambda parts: parts[0][None] if len(parts) == 1 else jnp.stack(parts)
    return (xf.reshape(b, n, d), stack(sb_k), stack(sb_v), stack(ret), stack(swa_k), stack(swa_v))


def kernel(x_prompt, x_sample, cache_sb_k, cache_sb_v, state_ret, cache_swa_k, cache_swa_v, norm_g, ffn_w_gate,
           ffn_w_up, ffn_w_down, even_w_in, even_w_out, odd_w_in, odd_w_out, odd_q_norm, odd_k_norm, odd_sinks):
    prm = _prep_weights(norm_g, ffn_w_gate, ffn_w_up, ffn_w_down, even_w_in, even_w_out, odd_w_in, odd_w_out,
                        odd_q_norm, odd_k_norm)
    past = cache_sb_k.shape[2]
    n_s = x_sample.shape[1]
    y_p, sb_k_p, sb_v_p, ret_p, swa_k_p, swa_v_p = _trunk(
        x_prompt, 0, prm, odd_sinks, None, tm=512, sb_qb=1024, sb_kb=128, ret_cs=256, swa_rows=512)
    caches = {"sb_k": cache_sb_k, "sb_v": cache_sb_v, "ret": state_ret, "swa_k": cache_swa_k, "swa_v": cache_swa_v}
    y_s, sb_k_s, sb_v_s, ret_s, swa_k_s, swa_v_s = _trunk(
        x_sample, past, prm, odd_sinks, caches, tm=x_sample.shape[0] * n_s, sb_qb=n_s, sb_kb=128,
        ret_cs=n_s, swa_rows=n_s)
    return (y_p, y_s, sb_k_p, sb_v_p, ret_p, swa_k_p, swa_v_p, sb_k_s, sb_v_s, ret_s, swa_k_s, swa_v_s)
```

```python
import functools
import math

import jax
import jax.numpy as jnp
from jax import lax
from jax.experimental import pallas as pl
from jax.experimental.pallas import tpu as pltpu

F32 = jnp.float32
BF16 = jnp.bfloat16

RMS_EPS = 1e-6
ROPE_THETA = 10000.0
CHUNK = 64
SB_HEADS = 8
SB_HEAD_DIM = 64
RET_HEADS = 4
RET_QK_DIM = 128
RET_V_DIM = 256
SWA_HEADS = 16
SWA_KV_HEADS = 4
SWA_HEAD_DIM = 64
WINDOW = 128

SB_W = SB_HEADS * SB_HEAD_DIM
RET_QK_W = RET_HEADS * RET_QK_DIM
RET_V_W = RET_HEADS * RET_V_DIM
SWA_Q_W = SWA_HEADS * SWA_HEAD_DIM
SWA_KV_W = SWA_KV_HEADS * SWA_HEAD_DIM
EVEN_CUTS = (0, SB_W, 2 * SB_W, 3 * SB_W, 3 * SB_W + RET_QK_W, 3 * SB_W + 2 * RET_QK_W,
             3 * SB_W + 2 * RET_QK_W + RET_V_W, 3 * SB_W + 2 * RET_QK_W + 2 * RET_V_W)

LANES = 128
MXU_TILE = 256
FF_CHUNK = MXU_TILE
VMEM_LIMIT = 56 * 1024 * 1024
LOG2E = math.log2(math.e)
SOFTPLUS2_LINEAR = 40.0
SWA_MASKED = -1e30
SB_DEAD_LOG2 = -170.0


def _dot(a, b):
    return jnp.dot(a, b, preferred_element_type=F32)


def _dot_nt(a, b):
    return lax.dot_general(a, b, (((1,), (1,)), ((), ())), preferred_element_type=F32)


def _dot_tn(a, b):
    return lax.dot_general(a, b, (((0,), (0,)), ((), ())), preferred_element_type=F32)


def _rms(x, g):
    return x * lax.rsqrt(jnp.mean(x * x, axis=-1, keepdims=True) + RMS_EPS) * g


def _full_spec(shape):
    nd = len(shape)
    return pl.BlockSpec(shape, lambda *_: (0,) * nd, pipeline_mode=pl.Buffered(1))


def _params(sem):
    return pltpu.CompilerParams(dimension_semantics=sem, vmem_limit_bytes=VMEM_LIMIT)


def _ffn_kernel(*refs, n_mix):
    n_in = 1 + n_mix
    cur, nxt = refs[:n_in], refs[n_in:2 * n_in]
    wmix_ref = refs[2 * n_in] if n_mix else None
    g_ref, wg_ref, wu_ref, wd_ref, o_ref, ha_ref, hb_ref, ra_ref, rb_ref, act_ref = refs[2 * n_in + bool(n_mix):]
    half = o_ref.shape[0] // 2
    lo, hi = slice(0, half), slice(half, 2 * half)

    def prepare(src, rws, h_ref, res_ref):
        x = src[0][rws]
        r0 = 0
        for a_ref in src[1:]:
            x = x + _dot(a_ref[rws], wmix_ref[r0:r0 + a_ref.shape[1]])
            r0 += a_ref.shape[1]
        res_ref[...] = x
        h_ref[...] = _rms(x, g_ref[...]).astype(BF16)

    def main(h_ref, res_ref, rws):
        for c in range(wg_ref.shape[1] // FF_CHUNK):
            cols = slice(c * FF_CHUNK, (c + 1) * FF_CHUNK)
            gate = _dot(h_ref[...], wg_ref[:, cols])
            up = _dot(h_ref[...], wu_ref[:, cols])
            act_ref[rws, cols] = (gate * jax.nn.sigmoid(gate) * up).astype(BF16)
        o_ref[rws] = res_ref[...] + 0.5 * _dot(act_ref[rws], wd_ref[...])

    @pl.when(pl.program_id(0) == 0)
    def _():
        prepare(cur, lo, ha_ref, ra_ref)

    prepare(cur, hi, hb_ref, rb_ref)
    main(ha_ref, ra_ref, lo)
    prepare(nxt, lo, ha_ref, ra_ref)
    main(hb_ref, rb_ref, hi)


def _ffn_call(x, mix, w_mix, g, ffn_w, layer, slot, tm):
    wg, wu, wd = ffn_w
    stack_spec = lambda w: pl.BlockSpec((None, None) + w.shape[2:], lambda i: (layer, slot, 0, 0),
                                        pipeline_mode=pl.Buffered(1))
    t, d = x.shape
    half = tm // 2
    last = t // tm - 1
    row = lambda w: pl.BlockSpec((tm, w), lambda i: (i, 0))
    nxt = lambda w: pl.BlockSpec((half, w), lambda i: (2 * jnp.minimum(i + 1, last), 0))
    acts = [x] + list(mix)
    in_specs = [row(a.shape[1]) for a in acts] + [nxt(a.shape[1]) for a in acts]
    args = acts + acts
    if mix:
        in_specs.append(_full_spec(w_mix.shape))
        args.append(w_mix)
    in_specs += [_full_spec(g.shape), stack_spec(wg), stack_spec(wu), stack_spec(wd)]
    args += [g, wg, wu, wd]
    return pl.pallas_call(
        functools.partial(_ffn_kernel, n_mix=len(mix)),
        grid=(t // tm,),
        in_specs=in_specs,
        out_specs=row(d),
        out_shape=jax.ShapeDtypeStruct((t, d), F32),
        scratch_shapes=[pltpu.VMEM((half, d), BF16), pltpu.VMEM((half, d), BF16), pltpu.VMEM((half, d), F32),
                        pltpu.VMEM((half, d), F32), pltpu.VMEM((tm, wg.shape[-1]), BF16)],
        compiler_params=_params(("arbitrary",)),
        name="ffn_mix%d" % len(mix),
    )(*args)


def _rope128(x, cos2, sin2):
    parts = []
    for hd in range(x.shape[1] // LANES):
        sl = x[:, hd * LANES:(hd + 1) * LANES]
        parts.append(sl * cos2 + pltpu.roll(sl, LANES // 2, axis=1) * sin2)
    return jnp.concatenate(parts, axis=1)


def _even_proj_kernel(x_ref, g_ref, w_ref, cos_ref, sin_ref,
                      qa_ref, ka_ref, kab_ref, va_ref, vab_ref, qr_ref, kr_ref, vr_ref, gt_ref):
    h = _rms(x_ref[...], g_ref[...]).astype(BF16)
    proj = lambda s: _dot(h, w_ref[:, EVEN_CUTS[s]:EVEN_CUTS[s + 1]])
    qa_ref[...] = (proj(0) * (LOG2E * SB_HEAD_DIM ** -0.5)).astype(BF16)
    ka = proj(1)
    ka_ref[...] = ka.reshape(ka_ref.shape)
    kab_ref[...] = ka.astype(BF16)
    va = proj(2)
    va_ref[...] = va.reshape(va_ref.shape)
    vab_ref[...] = va.astype(BF16)
    cos2 = cos_ref[...]
    sin2 = sin_ref[...]
    qr_ref[...] = _rope128(proj(3), cos2, sin2).astype(BF16)
    kr_ref[...] = (_rope128(proj(4), cos2, sin2) * (RET_QK_DIM ** -0.5)).astype(BF16)
    vr_ref[...] = proj(5).astype(BF16)
    gate = proj(6)
    gt_ref[...] = gate * jax.nn.sigmoid(gate)


def _even_proj_call(x, g, w_in, cos2, sin2, tm):
    t, d = x.shape
    n_tab = cos2.shape[0] // tm
    row = lambda w: pl.BlockSpec((tm, w), lambda i: (i, 0))
    tab = pl.BlockSpec((tm, LANES), lambda i: (i % n_tab, 0))
    per_head = (SB_HEADS, SB_HEAD_DIM)
    outs = [((SB_W,), BF16), (per_head, F32), ((SB_W,), BF16), (per_head, F32), ((SB_W,), BF16),
            ((RET_QK_W,), BF16), ((RET_QK_W,), BF16), ((RET_V_W,), BF16), ((RET_V_W,), F32)]
    return pl.pallas_call(
        _even_proj_kernel,
        grid=(t // tm,),
        in_specs=[row(d), _full_spec(g.shape), _full_spec(w_in.shape), tab, tab],
        out_specs=[pl.BlockSpec((tm,) + tail, lambda i, nd=len(tail): (i,) + (0,) * nd) for tail, _ in outs],
        out_shape=[jax.ShapeDtypeStruct((t,) + tail, dt) for tail, dt in outs],
        compiler_params=_params(("parallel",)),
        name="even_proj",
    )(x, g, w_in, cos2, sin2)


def _sb_kernel(q_ref, k_ref, v_ref, negu_ref, o_ref, qh_ref, c_ref, acc_ref, *, qb, kb, q_pos0):
    i = pl.program_id(2)
    n_clear = (q_pos0 + i * qb) // kb
    sq = min(qb, kb)
    nsub = qb // sq
    rows = lambda s: slice(s * sq, (s + 1) * sq)
    kstart = lambda j: pl.multiple_of(j * kb, kb)
    q = q_ref[...]
    first = lax.broadcasted_iota(jnp.int32, (qb, LANES), 1) < SB_HEAD_DIM
    qh_ref[0] = jnp.where(first, q, jnp.zeros_like(q))
    qh_ref[1] = jnp.where(first, jnp.zeros_like(q), q)

    def scores(hd, rws, start):
        return _dot_nt(qh_ref[hd, rws], k_ref[pl.ds(start, kb), :])

    def softplus2(z):
        return jnp.where(z > SOFTPLUS2_LINEAR, z, jnp.log2(1.0 + jnp.exp2(z)))

    def v_heads(start):
        vs = v_ref[pl.ds(start, kb), :]
        vfirst = lax.broadcasted_iota(jnp.int32, (kb, LANES), 1) < SB_HEAD_DIM
        return jnp.concatenate([jnp.where(vfirst, vs, jnp.zeros_like(vs)),
                                jnp.where(vfirst, jnp.zeros_like(vs), vs)], axis=0)

    prev = lambda s: kstart(jnp.maximum(n_clear + s - 1, 0))
    row = lax.broadcasted_iota(jnp.int32, (qb, kb), 0)
    below = lax.broadcasted_iota(jnp.int32, (qb, kb), 1) < (row & (sq - 1))
    has_prev = jnp.broadcast_to(n_clear > 0, (sq, kb))

    def apply_masks(t):
        parts = [jnp.where(below, t[:qb], 0.0), jnp.where(has_prev, t[qb:qb + sq], 0.0)]
        return jnp.concatenate(parts + ([t[qb + sq:]] if nsub > 1 else []), axis=0)

    w_diag, w_prev = [], []
    for hd in range(2):
        z = jnp.concatenate([scores(hd, rows(s), kstart(n_clear + s)) for s in range(nsub)]
                            + [scores(hd, rows(s), prev(s)) for s in range(nsub)], axis=0)
        sp = apply_masks(softplus2(z))
        tail = _dot(sp.astype(BF16), negu_ref[...])
        c_diag = tail[:qb, :1]
        w = apply_masks(jnp.exp2(z + tail + jnp.concatenate([jnp.zeros_like(c_diag), c_diag], axis=0)))
        c_ref[hd] = c_diag + tail[qb:, :1]
        w = w.astype(BF16)
        w_diag.append(w[:qb])
        w_prev.append(w[qb:])
    w_diag = jnp.concatenate(w_diag, axis=1)
    w_prev = jnp.concatenate(w_prev, axis=1)
    for s in range(nsub):
        acc_ref[rows(s)] = (_dot(w_diag[rows(s)], v_heads(kstart(n_clear + s)))
                            + _dot(w_prev[rows(s)], v_heads(prev(s))))

    def block(start, r0, r1):
        ws = []
        for hd in range(2):
            z = scores(hd, slice(r0, r1), start)
            tail = _dot(softplus2(z).astype(BF16), negu_ref[...])
            ws.append(jnp.exp2(z + tail + c_ref[hd, r0:r1]).astype(BF16))
            c_ref[hd, r0:r1] += tail[:, :1]
        acc_ref[r0:r1] += _dot(jnp.concatenate(ws, axis=1), v_heads(start))

    def alive(r0, r1):
        return (jnp.max(c_ref[:, r0:r1]) > SB_DEAD_LOG2).astype(jnp.int32)

    def run_blocks(j_first, j_last, r0, r1):
        def cond(carry):
            j, live = carry
            return jnp.logical_and(j >= j_last, live > 0)

        def body(carry):
            j, _ = carry
            block(kstart(j), r0, r1)
            return j - 1, alive(r0, r1)

        lax.while_loop(cond, body, (j_first, alive(r0, r1)))

    @pl.when(alive(0, qb) > 0)
    def _():
        for s in range(2, nsub):
            run_blocks(n_clear + s - 2, n_clear, s * sq, (s + 1) * sq)
        if nsub > 1:
            run_blocks(n_clear - 1, jnp.maximum(n_clear - 1, 0), sq, qb)
        run_blocks(n_clear - 2, 0, 0, qb)

    o_ref[...] = acc_ref[...].astype(o_ref.dtype)


def _sb_call(q, k, v, negu, qb, q_pos0):
    b, nq, w = q.shape
    nk = k.shape[1]
    kb = negu.shape[0]
    assert q_pos0 % kb == 0 and (qb % kb == 0 or nq == qb < kb) and nk % kb == 0 and nk >= q_pos0 + nq
    return pl.pallas_call(
        functools.partial(_sb_kernel, qb=qb, kb=kb, q_pos0=q_pos0),
        grid=(b, w // LANES, nq // qb),
        in_specs=[pl.BlockSpec((None, qb, LANES), lambda bi, p, i: (bi, i, p)),
                  pl.BlockSpec((None, nk, LANES), lambda bi, p, i: (bi, 0, p)),
                  pl.BlockSpec((None, nk, LANES), lambda bi, p, i: (bi, 0, p)),
                  _full_spec(negu.shape)],
        out_specs=pl.BlockSpec((None, qb, LANES), lambda bi, p, i: (bi, i, p)),
        out_shape=jax.ShapeDtypeStruct((b, nq, w), BF16),
        scratch_shapes=[pltpu.VMEM((2, qb, LANES), BF16), pltpu.VMEM((2, qb, 1), F32),
                        pltpu.VMEM((qb, LANES), F32)],
        compiler_params=_params(("parallel", "parallel", "parallel")),
        name="stick_breaking",
    )(q, k, v, negu)


def _ret_kernel(sdec_ref, q_ref, k_ref, v_ref, gt_ref, s0_ref, dec_ref, qdec_ref, kdec_ref,
                o_ref, sfin_ref, st_ref):
    c = pl.program_id(0)

    @pl.when(c == 0)
    def _():
        st_ref[...] = s0_ref[...]

    units = [(b, h) for b in range(q_ref.shape[0]) for h in range(RET_HEADS)]
    qk = lambda h: slice(h * RET_QK_DIM, (h + 1) * RET_QK_DIM)
    vw = lambda h: slice(h * RET_V_DIM, (h + 1) * RET_V_DIM)
    scores = [_dot_nt(q_ref[b, :, qk(h)], k_ref[b, :, qk(h)]) for b, h in units]
    cross = [_dot(q_ref[b, :, qk(h)], st_ref[b, h].astype(BF16)) for b, h in units]
    kw = [(k_ref[b, :, qk(h)].astype(F32) * kdec_ref[h]).astype(BF16) for b, h in units]
    grow = [_dot_tn(kw[u], v_ref[b, :, vw(h)]) for u, (b, h) in enumerate(units)]
    decayed = [(scores[u] * dec_ref[h]).astype(BF16) for u, (b, h) in enumerate(units)]
    inner = [_dot(decayed[u], v_ref[b, :, vw(h)]) for u, (b, h) in enumerate(units)]
    for u, (b, h) in enumerate(units):
        st_ref[b, h] = sdec_ref[h] * st_ref[b, h] + grow[u]
        o = inner[u] + cross[u] * qdec_ref[h]
        o = o * lax.rsqrt(jnp.mean(o * o, axis=-1, keepdims=True) + RMS_EPS)
        o_ref[b, :, vw(h)] = (o * gt_ref[b, :, vw(h)]).astype(o_ref.dtype)

    @pl.when(c == pl.num_programs(0) - 1)
    def _():
        sfin_ref[...] = st_ref[...]


def _ret_call(q, k, v, gate, state0, cs):
    b, n, _ = q.shape
    log_gamma = jnp.log1p(-jnp.exp2(-5.0 - jnp.arange(RET_HEADS, dtype=F32)))
    pos = jnp.arange(cs, dtype=F32)
    diff = pos[:, None] - pos[None, :]
    dec = jnp.where(diff >= 0, jnp.exp(log_gamma[:, None, None] * jnp.maximum(diff, 0.0)), 0.0)
    qdec = jnp.exp(log_gamma[:, None] * (pos + 1.0))[..., None]
    kdec = jnp.exp(log_gamma[:, None] * (cs - 1.0 - pos))[..., None]
    sdec = jnp.exp(log_gamma * cs)
    seq = lambda w: pl.BlockSpec((b, cs, w), lambda c: (0, c, 0))
    st = pl.BlockSpec((b, RET_HEADS, RET_QK_DIM, RET_V_DIM), lambda c: (0, 0, 0, 0))
    return pl.pallas_call(
        _ret_kernel,
        grid=(n // cs,),
        in_specs=[pl.BlockSpec(memory_space=pltpu.SMEM), seq(RET_QK_W), seq(RET_QK_W), seq(RET_V_W),
                  seq(RET_V_W), st, _full_spec(dec.shape), _full_spec(qdec.shape), _full_spec(kdec.shape)],
        out_specs=[seq(RET_V_W), st],
        out_shape=[jax.ShapeDtypeStruct((b, n, RET_V_W), BF16),
                   jax.ShapeDtypeStruct((b, RET_HEADS, RET_QK_DIM, RET_V_DIM), F32)],
        scratch_shapes=[pltpu.VMEM((b, RET_HEADS, RET_QK_DIM, RET_V_DIM), F32)],
        compiler_params=_params(("arbitrary",)),
        name="retention",
    )(sdec, q, k, v, gate, state0, dec, qdec, kdec)


def _head_norm_rope(t, bd, gain, cos2, sin2):
    wide = bd.shape[0]
    parts = []
    for c in range(t.shape[1] // wide):
        sq = t[:, c * wide:(c + 1) * wide]
        sq = sq * sq
        hi = sq.astype(BF16)
        lo = (sq - hi.astype(F32)).astype(BF16)
        parts.append(_dot(hi, bd) + _dot(lo, bd))
    ms = jnp.concatenate(parts, axis=1) * (1.0 / SWA_HEAD_DIM)
    y = t * lax.rsqrt(ms + RMS_EPS) * gain
    w = t.shape[1]
    reps = w // LANES
    quarter = SWA_HEAD_DIM // 2
    first_half = (lax.broadcasted_iota(jnp.int32, t.shape, 1) % SWA_HEAD_DIM) < quarter
    partner = jnp.where(first_half, pltpu.roll(y, w - quarter, axis=1), pltpu.roll(y, quarter, axis=1))
    return y * jnp.concatenate([cos2] * reps, axis=1) + partner * jnp.concatenate([sin2] * reps, axis=1)


def _odd_proj_kernel(x_ref, g_ref, w_ref, bd_ref, gain_ref, cos_ref, sin_ref, q_ref, k_ref, v_ref):
    h = _rms(x_ref[...], g_ref[...]).astype(BF16)
    qk_w = SWA_Q_W + SWA_KV_W
    qk = _dot(h, w_ref[:, :qk_w])
    v_ref[...] = _dot(h, w_ref[:, qk_w:])
    qk = _head_norm_rope(qk, bd_ref[...], gain_ref[...], cos_ref[...], sin_ref[...])
    q_ref[...] = (qk[:, :SWA_Q_W] * (LOG2E * SWA_HEAD_DIM ** -0.5)).astype(BF16)
    k_ref[...] = qk[:, SWA_Q_W:]


def _odd_proj_call(x, g, w_in, gain, cos2, sin2, tm):
    t, d = x.shape
    n_tab = cos2.shape[0] // tm
    head = jnp.arange(MXU_TILE) // SWA_HEAD_DIM
    bd = (head[:, None] == head[None, :]).astype(BF16)
    row = lambda w: pl.BlockSpec((tm, w), lambda i: (i, 0))
    tab = pl.BlockSpec((tm, LANES), lambda i: (i % n_tab, 0))
    ins = [g, w_in, bd, gain]
    return pl.pallas_call(
        _odd_proj_kernel,
        grid=(t // tm,),
        in_specs=[row(d)] + [_full_spec(a.shape) for a in ins] + [tab, tab],
        out_specs=[row(SWA_Q_W), row(SWA_KV_W), row(SWA_KV_W)],
        out_shape=[jax.ShapeDtypeStruct((t, SWA_Q_W), BF16), jax.ShapeDtypeStruct((t, SWA_KV_W), F32),
                   jax.ShapeDtypeStruct((t, SWA_KV_W), F32)],
        compiler_params=_params(("parallel",)),
        name="odd_proj",
    )(x, *ins, cos2, sin2)


def _swa_kernel(sink_ref, q_ref, kp_ref, kc_ref, vp_ref, vc_ref, o_ref, *, rows, first_prev_valid):
    i = pl.program_id(1)
    nch = rows // CHUNK
    wk = WINDOW + CHUNK
    nk = WINDOW + rows
    grp = SWA_HEADS // SWA_KV_HEADS
    gw = grp * CHUNK
    key_low = lax.broadcasted_iota(jnp.int32, (nk, LANES), 1) < SWA_HEAD_DIM
    low = lax.broadcasted_iota(jnp.int32, (CHUNK, LANES), 1) < SWA_HEAD_DIM
    if not first_prev_valid:
        krow = lax.broadcasted_iota(jnp.int32, (nk, LANES), 0)
        klane = lax.broadcasted_iota(jnp.int32, (nk, LANES), 1)
        k_bias = jnp.where((krow < WINDOW) & (klane == 0) & (i == 0), SWA_MASKED, 0.0).astype(BF16)
        q_one = jnp.where(lax.broadcasted_iota(jnp.int32, (gw, LANES), 1) == 0, 1.0, 0.0).astype(BF16)

    s, vj_t = [], []
    for j in range(SWA_KV_HEADS):
        pair, odd = divmod(j, 2)
        lanes = slice(pair * LANES, (pair + 1) * LANES)
        k_pair = jnp.concatenate([kp_ref[:, lanes], kc_ref[:, lanes]], axis=0)
        own = jnp.where(key_low, 0.0, k_pair) if odd else jnp.where(key_low, k_pair, 0.0)
        k_both = (own + pltpu.roll(own, SWA_HEAD_DIM, axis=1)).astype(BF16)
        v_t = jnp.concatenate([vp_ref[:, lanes], vc_ref[:, lanes]], axis=0).T
        vj_t.append(v_t[odd * SWA_HEAD_DIM:(odd + 1) * SWA_HEAD_DIM].astype(BF16))
        if not first_prev_valid:
            k_both = jnp.concatenate([k_both, k_bias], axis=1)
        for c in range(nch):
            q_four = []
            for qp in (2 * j, 2 * j + 1):
                q_pair = q_ref[c * CHUNK:(c + 1) * CHUNK, qp * LANES:(qp + 1) * LANES]
                zero = jnp.zeros_like(q_pair)
                q_four += [jnp.where(low, q_pair, zero), jnp.where(low, zero, q_pair)]
            q_four = jnp.concatenate(q_four, axis=0)
            if not first_prev_valid:
                q_four = jnp.concatenate([q_four, q_one], axis=1)
            s.append(_dot_nt(k_both[c * CHUNK:c * CHUNK + wk], q_four))
    s = jnp.concatenate(s, axis=1)
    sink = jnp.concatenate([jnp.full((1, CHUNK), sink_ref[j * grp + g] * LOG2E, F32)
                            for j in range(SWA_KV_HEADS) for _ in range(nch) for g in range(grp)], axis=1)
    m = jnp.maximum(jnp.max(s, axis=0, keepdims=True), sink)
    p = jnp.exp2(s - m)
    inv = 1.0 / (jnp.sum(p, axis=0, keepdims=True) + jnp.exp2(sink - m))
    p = p.astype(BF16)
    o_t = []
    for j in range(SWA_KV_HEADS):
        for c in range(nch):
            cols = slice((j * nch + c) * gw, (j * nch + c + 1) * gw)
            o_t.append(_dot(vj_t[j][:, c * CHUNK:c * CHUNK + wk], p[:, cols]) * inv[:, cols])
    for c0 in range(0, nch, 2):
        pieces = []
        for j in range(SWA_KV_HEADS):
            for t in range(grp // 2):
                tiles = [o_t[j * nch + c][:, t * LANES:(t + 1) * LANES] for c in range(c0, min(c0 + 2, nch))]
                if len(tiles) == 2:
                    pieces += [jnp.where(low, tiles[0], pltpu.roll(tiles[1], CHUNK, axis=1)),
                               jnp.where(low, pltpu.roll(tiles[0], CHUNK, axis=1), tiles[1])]
                else:
                    pieces += [tiles[0], pltpu.roll(tiles[0], CHUNK, axis=1)]
        out = jnp.concatenate(pieces, axis=0).T
        r1 = min((c0 + 2) * CHUNK, rows)
        o_ref[c0 * CHUNK:r1] = out[:r1 - c0 * CHUNK].astype(o_ref.dtype)


def _swa_call(q, k_prev, k_cur, v_prev, v_cur, sinks, rows, same_array):
    b, n, _ = q.shape
    per = rows // WINDOW if same_array else 0
    cur = lambda w: pl.BlockSpec((None, rows, w), lambda bi, i: (bi, i, 0))
    prev = pl.BlockSpec((None, WINDOW, SWA_KV_W), lambda bi, i: (bi, jnp.maximum(i * per - 1, 0), 0))
    return pl.pallas_call(
        functools.partial(_swa_kernel, rows=rows, first_prev_valid=not same_array),
        grid=(b, n // rows),
        in_specs=[pl.BlockSpec(memory_space=pltpu.SMEM), cur(SWA_Q_W), prev, cur(SWA_KV_W), prev, cur(SWA_KV_W)],
        out_specs=cur(SWA_Q_W),
        out_shape=jax.ShapeDtypeStruct((b, n, SWA_Q_W), BF16),
        compiler_params=_params(("parallel", "parallel")),
        name="swa",
    )(sinks, q, k_prev, k_cur, v_prev, v_cur)


def _rope_tables(pos0, n, head_dim):
    half = head_dim // 2
    lane = jnp.arange(LANES)
    inv = jnp.power(ROPE_THETA, -(lane % half).astype(F32) / half)
    sign = jnp.where(lane % head_dim < half, -1.0, 1.0)
    fine = min(n, LANES)
    ang_c = (pos0 + fine * jnp.arange(n // fine)).astype(F32)[:, None] * inv[None, :]
    ang_f = jnp.arange(fine, dtype=F32)[:, None] * inv[None, :]
    cos_c, sin_c = jnp.cos(ang_c)[:, None, :], jnp.sin(ang_c)[:, None, :]
    cos_f, sin_f = jnp.cos(ang_f)[None, :, :], jnp.sin(ang_f)[None, :, :]
    return ((cos_c * cos_f - sin_c * sin_f).reshape(n, LANES),
            ((sin_c * cos_f + cos_c * sin_f) * sign).reshape(n, LANES))


def _prep_weights(norm_g, ffn_w_gate, ffn_w_up, ffn_w_down, even_w_in, even_w_out, odd_w_in, odd_w_out,
                  odd_q_norm, odd_k_norm):
    depth = norm_g.shape[0]
    prm = {"g": norm_g[:, :, None, :], "even": [], "odd": [], "depth": depth,
           "ffn": (ffn_w_gate.astype(BF16), ffn_w_up.astype(BF16), ffn_w_down.astype(BF16))}
    for i in range(even_w_in.shape[0]):
        prm["even"].append((even_w_in[i].astype(BF16), even_w_out[i].astype(BF16)))
    for i in range(odd_w_in.shape[0]):
        gain = jnp.concatenate([jnp.tile(odd_q_norm[i], SWA_HEADS), jnp.tile(odd_k_norm[i], SWA_KV_HEADS)])
        prm["odd"].append((odd_w_in[i].astype(BF16), odd_w_out[i].astype(BF16), gain[None, :]))
    return prm


def _trunk(x, pos0, prm, sinks, caches, tm, sb_qb, sb_kb, ret_cs, swa_rows):
    b, n, d = x.shape
    t = b * n
    xf = x.reshape(t, d)
    depth = prm["depth"]
    tab_rows = n if n % tm == 0 else t
    tile_tab = lambda tb: tb if tab_rows == n else jnp.tile(tb, (b, 1))
    cos_r, sin_r = [tile_tab(tb) for tb in _rope_tables(pos0, n, RET_QK_DIM)]
    cos_s, sin_s = [tile_tab(tb) for tb in _rope_tables(pos0, n, SWA_HEAD_DIM)]
    idx = jnp.arange(sb_kb)
    negu = jnp.where(idx[:, None] >= idx[None, :], -1.0, 0.0).astype(BF16)
    sb_k, sb_v, ret, swa_k, swa_v = [], [], [], [], []
    mix, w_mix = [], None
    for layer in range(depth):
        i = layer // 2
        g = prm["g"][layer]
        xf = _ffn_call(xf, mix, w_mix, g[0], prm["ffn"], layer, 0, tm)
        if layer % 2 == 0:
            w_in, w_mix = prm["even"][i]
            qa, ka, kab, va, vab, qr, kr, vr, gt = _even_proj_call(xf, g[1], w_in, cos_r, sin_r, tm)
            sb_k.append(ka.reshape(b, n, SB_HEADS, SB_HEAD_DIM))
            sb_v.append(va.reshape(b, n, SB_HEADS, SB_HEAD_DIM))
            r3 = lambda a: a.reshape(b, n, a.shape[-1])
            if caches is None:
                k_all, v_all, q_pos0 = r3(kab), r3(vab), 0
                state0 = jnp.zeros((b, RET_HEADS, RET_QK_DIM, RET_V_DIM), F32)
            else:
                past = caches["sb_k"].shape[2]
                padded = -(-(past + n) // sb_kb) * sb_kb
                cat = lambda cache, new: jnp.pad(
                    jnp.concatenate([cache.reshape(b, past, SB_W).astype(BF16), r3(new)], axis=1),
                    ((0, 0), (0, padded - past - n), (0, 0)))
                k_all, v_all, q_pos0 = cat(caches["sb_k"][i], kab), cat(caches["sb_v"][i], vab), past
                state0 = caches["ret"][i]
            o_sb = _sb_call(r3(qa), k_all, v_all, negu, sb_qb, q_pos0)
            o_r, st = _ret_call(r3(qr), r3(kr), r3(vr), r3(gt), state0, ret_cs)
            ret.append(st)
            mix = [o_sb.reshape(t, SB_W), o_r.reshape(t, RET_V_W)]
        else:
            w_in, w_mix, gain = prm["odd"][i]
            q, k, v = _odd_proj_call(xf, g[1], w_in, gain, cos_s, sin_s, tm)
            r3 = lambda a: a.reshape(b, n, a.shape[-1])
            k3, v3 = r3(k), r3(v)
            if caches is None:
                o = _swa_call(r3(q), k3, k3, v3, v3, sinks[i], swa_rows, True)
                keep = min(WINDOW, n)
                k_rows, v_rows = k3[:, n - keep:], v3[:, n - keep:]
            else:
                kc = caches["swa_k"][i].reshape(b, -1, SWA_KV_W)
                vc = caches["swa_v"][i].reshape(b, -1, SWA_KV_W)
                o = _swa_call(r3(q), kc, k3, vc, v3, sinks[i], swa_rows, False)
                k_rows, v_rows = k3, v3
            swa_k.append(k_rows.reshape(b, -1, SWA_KV_HEADS, SWA_HEAD_DIM))
            swa_v.append(v_rows.reshape(b, -1, SWA_KV_HEADS, SWA_HEAD_DIM))
            mix = [o.reshape(t, SWA_Q_W)]
        xf = _ffn_call(xf, mix, w_mix, g[2], prm["ffn"], layer, 1, tm)
        mix, w_mix = [], None
    stack = lambda parts: parts[0][None] if len(parts) == 1 else jnp.stack(parts)
    return (xf.reshape(b, n, d), stack(sb_k), stack(sb_v), stack(ret), stack(swa_k), stack(swa_v))


def kernel(x_prompt, x_sample, cache_sb_k, cache_sb_v, state_ret, cache_swa_k, cache_swa_v, norm_g, ffn_w_gate,
           ffn_w_up, ffn_w_down, even_w_in, even_w_out, odd_w_in, odd_w_out, odd_q_norm, odd_k_norm, odd_sinks):
    prm = _prep_weights(norm_g, ffn_w_gate, ffn_w_up, ffn_w_down, even_w_in, even_w_out, odd_w_in, odd_w_out,
                        odd_q_norm, odd_k_norm)
    past = cache_sb_k.shape[2]
    n_s = x_sample.shape[1]
    y_p, sb_k_p, sb_v_p, ret_p, swa_k_p, swa_v_p = _trunk(
        x_prompt, 0, prm, odd_sinks, None, tm=512, sb_qb=2048, sb_kb=256, ret_cs=256, swa_rows=512)
    caches = {"sb_k": cache_sb_k, "sb_v": cache_sb_v, "ret": state_ret, "swa_k": cache_swa_k, "swa_v": cache_swa_v}
    y_s, sb_k_s, sb_v_s, ret_s, swa_k_s, swa_v_s = _trunk(
        x_sample, past, prm, odd_sinks, caches, tm=x_sample.shape[0] * n_s, sb_qb=n_s, sb_kb=256,
        ret_cs=n_s, swa_rows=n_s)
    return (y_p, y_s, sb_k_p, sb_v_p, ret_p, swa_k_p, swa_v_p, sb_k_s, sb_v_s, ret_s, swa_k_s, swa_v_s)
```

```python
import functools
import math

import jax
import jax.numpy as jnp
from jax import lax
from jax.experimental import pallas as pl
from jax.experimental.pallas import tpu as pltpu

F32 = jnp.float32
BF16 = jnp.bfloat16

RMS_EPS = 1e-6
ROPE_THETA = 10000.0
CHUNK = 64
SB_HEADS = 8
SB_HEAD_DIM = 64
RET_HEADS = 4
RET_QK_DIM = 128
RET_V_DIM = 256
SWA_HEADS = 16
SWA_KV_HEADS = 4
SWA_HEAD_DIM = 64
WINDOW = 128

SB_W = SB_HEADS * SB_HEAD_DIM
RET_QK_W = RET_HEADS * RET_QK_DIM
RET_V_W = RET_HEADS * RET_V_DIM
SWA_Q_W = SWA_HEADS * SWA_HEAD_DIM
SWA_KV_W = SWA_KV_HEADS * SWA_HEAD_DIM
EVEN_CUTS = (0, SB_W, 2 * SB_W, 3 * SB_W, 3 * SB_W + RET_QK_W, 3 * SB_W + 2 * RET_QK_W,
             3 * SB_W + 2 * RET_QK_W + RET_V_W, 3 * SB_W + 2 * RET_QK_W + 2 * RET_V_W)

LANES = 128
MXU_TILE = 256
FF_CHUNK = MXU_TILE
VMEM_LIMIT = 56 * 1024 * 1024
LOG2E = math.log2(math.e)
SOFTPLUS2_LINEAR = 40.0
SWA_MASKED = -1e30
SB_STATIC_BLOCKS = 3
SB_DEAD_LOG2 = -170.0


def _dot(a, b):
    return jnp.dot(a, b, preferred_element_type=F32)


def _dot_nt(a, b):
    return lax.dot_general(a, b, (((1,), (1,)), ((), ())), preferred_element_type=F32)


def _dot_tn(a, b):
    return lax.dot_general(a, b, (((0,), (0,)), ((), ())), preferred_element_type=F32)


def _rms(x, g):
    return x * lax.rsqrt(jnp.mean(x * x, axis=-1, keepdims=True) + RMS_EPS) * g


def _full_spec(shape):
    nd = len(shape)
    return pl.BlockSpec(shape, lambda *_: (0,) * nd, pipeline_mode=pl.Buffered(1))


def _params(sem):
    return pltpu.CompilerParams(dimension_semantics=sem, vmem_limit_bytes=VMEM_LIMIT)


def _ffn_kernel(*refs, n_mix):
    n_in = 1 + n_mix
    cur, nxt = refs[:n_in], refs[n_in:2 * n_in]
    wmix_ref = refs[2 * n_in] if n_mix else None
    g_ref, wg_ref, wu_ref, wd_ref, o_ref, ha_ref, hb_ref, ra_ref, rb_ref, act_ref = refs[2 * n_in + bool(n_mix):]
    half = o_ref.shape[0] // 2
    lo, hi = slice(0, half), slice(half, 2 * half)

    def prepare(src, rws, h_ref, res_ref):
        x = src[0][rws]
        r0 = 0
        for a_ref in src[1:]:
            x = x + _dot(a_ref[rws], wmix_ref[r0:r0 + a_ref.shape[1]])
            r0 += a_ref.shape[1]
        res_ref[...] = x
        h_ref[...] = _rms(x, g_ref[...]).astype(BF16)

    def main(h_ref, res_ref, rws):
        for c in range(wg_ref.shape[1] // FF_CHUNK):
            cols = slice(c * FF_CHUNK, (c + 1) * FF_CHUNK)
            gate = _dot(h_ref[...], wg_ref[:, cols])
            up = _dot(h_ref[...], wu_ref[:, cols])
            act_ref[rws, cols] = (gate * jax.nn.sigmoid(gate) * up).astype(BF16)
        o_ref[rws] = res_ref[...] + 0.5 * _dot(act_ref[rws], wd_ref[...])

    @pl.when(pl.program_id(0) == 0)
    def _():
        prepare(cur, lo, ha_ref, ra_ref)

    prepare(cur, hi, hb_ref, rb_ref)
    main(ha_ref, ra_ref, lo)
    prepare(nxt, lo, ha_ref, ra_ref)
    main(hb_ref, rb_ref, hi)


def _ffn_call(x, mix, w_mix, g, ffn_w, layer, slot, tm):
    wg, wu, wd = ffn_w
    stack_spec = lambda w: pl.BlockSpec((None, None) + w.shape[2:], lambda i: (layer, slot, 0, 0),
                                        pipeline_mode=pl.Buffered(1))
    t, d = x.shape
    half = tm // 2
    last = t // tm - 1
    row = lambda w: pl.BlockSpec((tm, w), lambda i: (i, 0))
    nxt = lambda w: pl.BlockSpec((half, w), lambda i: (2 * jnp.minimum(i + 1, last), 0))
    acts = [x] + list(mix)
    in_specs = [row(a.shape[1]) for a in acts] + [nxt(a.shape[1]) for a in acts]
    args = acts + acts
    if mix:
        in_specs.append(_full_spec(w_mix.shape))
        args.append(w_mix)
    in_specs += [_full_spec(g.shape), stack_spec(wg), stack_spec(wu), stack_spec(wd)]
    args += [g, wg, wu, wd]
    return pl.pallas_call(
        functools.partial(_ffn_kernel, n_mix=len(mix)),
        grid=(t // tm,),
        in_specs=in_specs,
        out_specs=row(d),
        out_shape=jax.ShapeDtypeStruct((t, d), F32),
        scratch_shapes=[pltpu.VMEM((half, d), BF16), pltpu.VMEM((half, d), BF16), pltpu.VMEM((half, d), F32),
                        pltpu.VMEM((half, d), F32), pltpu.VMEM((tm, wg.shape[-1]), BF16)],
        compiler_params=_params(("arbitrary",)),
        name="ffn_mix%d" % len(mix),
    )(*args)


def _rope128(x, cos2, sin2):
    parts = []
    for hd in range(x.shape[1] // LANES):
        sl = x[:, hd * LANES:(hd + 1) * LANES]
        parts.append(sl * cos2 + pltpu.roll(sl, LANES // 2, axis=1) * sin2)
    return jnp.concatenate(parts, axis=1)


def _even_proj_kernel(x_ref, g_ref, w_ref, cos_ref, sin_ref,
                      qa_ref, ka_ref, kab_ref, va_ref, vab_ref, qr_ref, kr_ref, vr_ref, gt_ref):
    h = _rms(x_ref[...], g_ref[...]).astype(BF16)
    proj = lambda s: _dot(h, w_ref[:, EVEN_CUTS[s]:EVEN_CUTS[s + 1]])
    qa_ref[...] = (proj(0) * (LOG2E * SB_HEAD_DIM ** -0.5)).astype(BF16)
    ka = proj(1)
    ka_ref[...] = ka.reshape(ka_ref.shape)
    kab_ref[...] = ka.astype(BF16)
    va = proj(2)
    va_ref[...] = va.reshape(va_ref.shape)
    vab_ref[...] = va.astype(BF16)
    cos2 = cos_ref[...]
    sin2 = sin_ref[...]
    qr_ref[...] = _rope128(proj(3), cos2, sin2).astype(BF16)
    kr_ref[...] = (_rope128(proj(4), cos2, sin2) * (RET_QK_DIM ** -0.5)).astype(BF16)
    vr_ref[...] = proj(5).astype(BF16)
    gate = proj(6)
    gt_ref[...] = gate * jax.nn.sigmoid(gate)


def _even_proj_call(x, g, w_in, cos2, sin2, tm):
    t, d = x.shape
    n_tab = cos2.shape[0] // tm
    row = lambda w: pl.BlockSpec((tm, w), lambda i: (i, 0))
    tab = pl.BlockSpec((tm, LANES), lambda i: (i % n_tab, 0))
    per_head = (SB_HEADS, SB_HEAD_DIM)
    outs = [((SB_W,), BF16), (per_head, F32), ((SB_W,), BF16), (per_head, F32), ((SB_W,), BF16),
            ((RET_QK_W,), BF16), ((RET_QK_W,), BF16), ((RET_V_W,), BF16), ((RET_V_W,), F32)]
    return pl.pallas_call(
        _even_proj_kernel,
        grid=(t // tm,),
        in_specs=[row(d), _full_spec(g.shape), _full_spec(w_in.shape), tab, tab],
        out_specs=[pl.BlockSpec((tm,) + tail, lambda i, nd=len(tail): (i,) + (0,) * nd) for tail, _ in outs],
        out_shape=[jax.ShapeDtypeStruct((t,) + tail, dt) for tail, dt in outs],
        compiler_params=_params(("parallel",)),
        name="even_proj",
    )(x, g, w_in, cos2, sin2)


def _sb_kernel(q_ref, k_ref, v_ref, negu_ref, o_ref, qh_ref, c_ref, acc_ref, *, qb, kb, q_pos0):
    i = pl.program_id(2)
    n_clear = (q_pos0 + i * qb) // kb
    sq = min(qb, kb)
    nsub = qb // sq
    rows = lambda s: slice(s * sq, (s + 1) * sq)
    kstart = lambda j: pl.multiple_of(j * kb, kb)
    q = q_ref[...]
    first = lax.broadcasted_iota(jnp.int32, (qb, LANES), 1) < SB_HEAD_DIM
    qh_ref[0] = jnp.where(first, q, jnp.zeros_like(q))
    qh_ref[1] = jnp.where(first, jnp.zeros_like(q), q)

    def scores(hd, rws, start):
        return _dot_nt(qh_ref[hd, rws], k_ref[pl.ds(start, kb), :])

    def softplus2(z):
        return jnp.where(z > SOFTPLUS2_LINEAR, z, jnp.log2(1.0 + jnp.exp2(z)))

    def v_heads(start):
        vs = v_ref[pl.ds(start, kb), :]
        vfirst = lax.broadcasted_iota(jnp.int32, (kb, LANES), 1) < SB_HEAD_DIM
        return jnp.concatenate([jnp.where(vfirst, vs, jnp.zeros_like(vs)),
                                jnp.where(vfirst, jnp.zeros_like(vs), vs)], axis=0)

    depth = SB_STATIC_BLOCKS
    row = lax.broadcasted_iota(jnp.int32, (qb, kb), 0)
    below = lax.broadcasted_iota(jnp.int32, (qb, kb), 1) < (row & (sq - 1))
    kblock = lambda s, d: kstart(jnp.maximum(n_clear + s - d, 0))

    def apply_masks(t):
        parts = [jnp.where(below, t[:qb], 0.0)]
        for d in range(1, depth):
            for s in range(min(d, nsub)):
                exists = jnp.broadcast_to(n_clear + s - d >= 0, (sq, kb))
                parts.append(jnp.where(exists, t[d * qb + s * sq:d * qb + (s + 1) * sq], 0.0))
            if nsub > d:
                parts.append(t[d * qb + d * sq:(d + 1) * qb])
        return jnp.concatenate(parts, axis=0)

    ws = []
    for hd in range(2):
        z = jnp.concatenate([scores(hd, rows(s), kblock(s, d)) for d in range(depth) for s in range(nsub)], axis=0)
        sp = apply_masks(softplus2(z))
        tail = _dot(sp.astype(BF16), negu_ref[...])
        seen = [jnp.zeros((qb, 1), F32)]
        for d in range(depth):
            seen.append(seen[-1] + tail[d * qb:(d + 1) * qb, :1])
        c_ref[hd] = seen[depth]
        ws.append(apply_masks(jnp.exp2(z + tail + jnp.concatenate(seen[:depth], axis=0))).astype(BF16))
    w = jnp.concatenate(ws, axis=1)
    for s in range(nsub):
        acc = None
        for d in range(depth):
            part = _dot(w[d * qb + s * sq:d * qb + (s + 1) * sq], v_heads(kblock(s, d)))
            acc = part if acc is None else acc + part
        acc_ref[rows(s)] = acc

    def block(start, r0, r1):
        ws = []
        for hd in range(2):
            z = scores(hd, slice(r0, r1), start)
            tail = _dot(softplus2(z).astype(BF16), negu_ref[...])
            ws.append(jnp.exp2(z + tail + c_ref[hd, r0:r1]).astype(BF16))
            c_ref[hd, r0:r1] += tail[:, :1]
        acc_ref[r0:r1] += _dot(jnp.concatenate(ws, axis=1), v_heads(start))

    def alive(r0, r1):
        return (jnp.max(c_ref[:, r0:r1]) > SB_DEAD_LOG2).astype(jnp.int32)

    def run_blocks(j_first, j_last, r0, r1):
        def cond(carry):
            j, live = carry
            return jnp.logical_and(j >= j_last, live > 0)

        def body(carry):
            j, _ = carry
            block(kstart(j), r0, r1)
            return j - 1, alive(r0, r1)

        lax.while_loop(cond, body, (j_first, alive(r0, r1)))

    @pl.when(alive(0, qb) > 0)
    def _():
        for s in range(1, nsub):
            run_blocks(n_clear + s - depth, jnp.maximum(n_clear - depth + 1, 0), s * sq, (s + 1) * sq)
        run_blocks(n_clear - depth, 0, 0, qb)

    o_ref[...] = acc_ref[...].astype(o_ref.dtype)


def _sb_call(q, k, v, negu, qb, q_pos0):
    b, nq, w = q.shape
    nk = k.shape[1]
    kb = negu.shape[0]
    assert q_pos0 % kb == 0 and (qb % kb == 0 or nq == qb < kb) and nk % kb == 0 and nk >= q_pos0 + nq
    return pl.pallas_call(
        functools.partial(_sb_kernel, qb=qb, kb=kb, q_pos0=q_pos0),
        grid=(b, w // LANES, nq // qb),
        in_specs=[pl.BlockSpec((None, qb, LANES), lambda bi, p, i: (bi, i, p)),
                  pl.BlockSpec((None, nk, LANES), lambda bi, p, i: (bi, 0, p)),
                  pl.BlockSpec((None, nk, LANES), lambda bi, p, i: (bi, 0, p)),
                  _full_spec(negu.shape)],
        out_specs=pl.BlockSpec((None, qb, LANES), lambda bi, p, i: (bi, i, p)),
        out_shape=jax.ShapeDtypeStruct((b, nq, w), BF16),
        scratch_shapes=[pltpu.VMEM((2, qb, LANES), BF16), pltpu.VMEM((2, qb, 1), F32),
                        pltpu.VMEM((qb, LANES), F32)],
        compiler_params=_params(("parallel", "parallel", "parallel")),
        name="stick_breaking",
    )(q, k, v, negu)


def _ret_kernel(sdec_ref, q_ref, k_ref, v_ref, gt_ref, s0_ref, dec_ref, qdec_ref, kdec_ref,
                o_ref, sfin_ref, st_ref):
    c = pl.program_id(0)

    @pl.when(c == 0)
    def _():
        st_ref[...] = s0_ref[...]

    units = [(b, h) for b in range(q_ref.shape[0]) for h in range(RET_HEADS)]
    qk = lambda h: slice(h * RET_QK_DIM, (h + 1) * RET_QK_DIM)
    vw = lambda h: slice(h * RET_V_DIM, (h + 1) * RET_V_DIM)
    scores = [_dot_nt(q_ref[b, :, qk(h)], k_ref[b, :, qk(h)]) for b, h in units]
    cross = [_dot(q_ref[b, :, qk(h)], st_ref[b, h].astype(BF16)) for b, h in units]
    kw = [(k_ref[b, :, qk(h)].astype(F32) * kdec_ref[h]).astype(BF16) for b, h in units]
    grow = [_dot_tn(kw[u], v_ref[b, :, vw(h)]) for u, (b, h) in enumerate(units)]
    decayed = [(scores[u] * dec_ref[h]).astype(BF16) for u, (b, h) in enumerate(units)]
    inner = [_dot(decayed[u], v_ref[b, :, vw(h)]) for u, (b, h) in enumerate(units)]
    for u, (b, h) in enumerate(units):
        st_ref[b, h] = sdec_ref[h] * st_ref[b, h] + grow[u]
        o = inner[u] + cross[u] * qdec_ref[h]
        o = o * lax.rsqrt(jnp.mean(o * o, axis=-1, keepdims=True) + RMS_EPS)
        o_ref[b, :, vw(h)] = (o * gt_ref[b, :, vw(h)]).astype(o_ref.dtype)

    @pl.when(c == pl.num_programs(0) - 1)
    def _():
        sfin_ref[...] = st_ref[...]


def _ret_call(q, k, v, gate, state0, cs):
    b, n, _ = q.shape
    log_gamma = jnp.log1p(-jnp.exp2(-5.0 - jnp.arange(RET_HEADS, dtype=F32)))
    pos = jnp.arange(cs, dtype=F32)
    diff = pos[:, None] - pos[None, :]
    dec = jnp.where(diff >= 0, jnp.exp(log_gamma[:, None, None] * jnp.maximum(diff, 0.0)), 0.0)
    qdec = jnp.exp(log_gamma[:, None] * (pos + 1.0))[..., None]
    kdec = jnp.exp(log_gamma[:, None] * (cs - 1.0 - pos))[..., None]
    sdec = jnp.exp(log_gamma * cs)
    seq = lambda w: pl.BlockSpec((b, cs, w), lambda c: (0, c, 0))
    st = pl.BlockSpec((b, RET_HEADS, RET_QK_DIM, RET_V_DIM), lambda c: (0, 0, 0, 0))
    return pl.pallas_call(
        _ret_kernel,
        grid=(n // cs,),
        in_specs=[pl.BlockSpec(memory_space=pltpu.SMEM), seq(RET_QK_W), seq(RET_QK_W), seq(RET_V_W),
                  seq(RET_V_W), st, _full_spec(dec.shape), _full_spec(qdec.shape), _full_spec(kdec.shape)],
        out_specs=[seq(RET_V_W), st],
        out_shape=[jax.ShapeDtypeStruct((b, n, RET_V_W), BF16),
                   jax.ShapeDtypeStruct((b, RET_HEADS, RET_QK_DIM, RET_V_DIM), F32)],
        scratch_shapes=[pltpu.VMEM((b, RET_HEADS, RET_QK_DIM, RET_V_DIM), F32)],
        compiler_params=_params(("arbitrary",)),
        name="retention",
    )(sdec, q, k, v, gate, state0, dec, qdec, kdec)


def _head_norm_rope(t, bd, gain, cos2, sin2):
    wide = bd.shape[0]
    parts = []
    for c in range(t.shape[1] // wide):
        sq = t[:, c * wide:(c + 1) * wide]
        sq = sq * sq
        hi = sq.astype(BF16)
        lo = (sq - hi.astype(F32)).astype(BF16)
        parts.append(_dot(hi, bd) + _dot(lo, bd))
    ms = jnp.concatenate(parts, axis=1) * (1.0 / SWA_HEAD_DIM)
    y = t * lax.rsqrt(ms + RMS_EPS) * gain
    w = t.shape[1]
    reps = w // LANES
    quarter = SWA_HEAD_DIM // 2
    first_half = (lax.broadcasted_iota(jnp.int32, t.shape, 1) % SWA_HEAD_DIM) < quarter
    partner = jnp.where(first_half, pltpu.roll(y, w - quarter, axis=1), pltpu.roll(y, quarter, axis=1))
    return y * jnp.concatenate([cos2] * reps, axis=1) + partner * jnp.concatenate([sin2] * reps, axis=1)


def _odd_proj_kernel(x_ref, g_ref, w_ref, bd_ref, gain_ref, cos_ref, sin_ref, q_ref, k_ref, v_ref):
    h = _rms(x_ref[...], g_ref[...]).astype(BF16)
    qk_w = SWA_Q_W + SWA_KV_W
    qk = _dot(h, w_ref[:, :qk_w])
    v_ref[...] = _dot(h, w_ref[:, qk_w:])
    qk = _head_norm_rope(qk, bd_ref[...], gain_ref[...], cos_ref[...], sin_ref[...])
    q_ref[...] = (qk[:, :SWA_Q_W] * (LOG2E * SWA_HEAD_DIM ** -0.5)).astype(BF16)
    k_ref[...] = qk[:, SWA_Q_W:]


def _odd_proj_call(x, g, w_in, gain, cos2, sin2, tm):
    t, d = x.shape
    n_tab = cos2.shape[0] // tm
    head = jnp.arange(MXU_TILE) // SWA_HEAD_DIM
    bd = (head[:, None] == head[None, :]).astype(BF16)
    row = lambda w: pl.BlockSpec((tm, w), lambda i: (i, 0))
    tab = pl.BlockSpec((tm, LANES), lambda i: (i % n_tab, 0))
    ins = [g, w_in, bd, gain]
    return pl.pallas_call(
        _odd_proj_kernel,
        grid=(t // tm,),
        in_specs=[row(d)] + [_full_spec(a.shape) for a in ins] + [tab, tab],
        out_specs=[row(SWA_Q_W), row(SWA_KV_W), row(SWA_KV_W)],
        out_shape=[jax.ShapeDtypeStruct((t, SWA_Q_W), BF16), jax.ShapeDtypeStruct((t, SWA_KV_W), F32),
                   jax.ShapeDtypeStruct((t, SWA_KV_W), F32)],
        compiler_params=_params(("parallel",)),
        name="odd_proj",
    )(x, *ins, cos2, sin2)


def _swa_kernel(sink_ref, q_ref, kp_ref, kc_ref, vp_ref, vc_ref, o_ref, *, rows, first_prev_valid):
    i = pl.program_id(1)
    nch = rows // CHUNK
    wk = WINDOW + CHUNK
    nk = WINDOW + rows
    grp = SWA_HEADS // SWA_KV_HEADS
    gw = grp * CHUNK
    key_low = lax.broadcasted_iota(jnp.int32, (nk, LANES), 1) < SWA_HEAD_DIM
    low = lax.broadcasted_iota(jnp.int32, (CHUNK, LANES), 1) < SWA_HEAD_DIM
    if not first_prev_valid:
        krow = lax.broadcasted_iota(jnp.int32, (nk, LANES), 0)
        klane = lax.broadcasted_iota(jnp.int32, (nk, LANES), 1)
        k_bias = jnp.where((krow < WINDOW) & (klane == 0) & (i == 0), SWA_MASKED, 0.0).astype(BF16)
        q_one = jnp.where(lax.broadcasted_iota(jnp.int32, (gw, LANES), 1) == 0, 1.0, 0.0).astype(BF16)

    s, vj_t = [], []
    for j in range(SWA_KV_HEADS):
        pair, odd = divmod(j, 2)
        lanes = slice(pair * LANES, (pair + 1) * LANES)
        k_pair = jnp.concatenate([kp_ref[:, lanes], kc_ref[:, lanes]], axis=0)
        own = jnp.where(key_low, 0.0, k_pair) if odd else jnp.where(key_low, k_pair, 0.0)
        k_both = (own + pltpu.roll(own, SWA_HEAD_DIM, axis=1)).astype(BF16)
        v_t = jnp.concatenate([vp_ref[:, lanes], vc_ref[:, lanes]], axis=0).T
        vj_t.append(v_t[odd * SWA_HEAD_DIM:(odd + 1) * SWA_HEAD_DIM].astype(BF16))
        if not first_prev_valid:
            k_both = jnp.concatenate([k_both, k_bias], axis=1)
        for c in range(nch):
            q_four = []
            for qp in (2 * j, 2 * j + 1):
                q_pair = q_ref[c * CHUNK:(c + 1) * CHUNK, qp * LANES:(qp + 1) * LANES]
                zero = jnp.zeros_like(q_pair)
                q_four += [jnp.where(low, q_pair, zero), jnp.where(low, zero, q_pair)]
            q_four = jnp.concatenate(q_four, axis=0)
            if not first_prev_valid:
                q_four = jnp.concatenate([q_four, q_one], axis=1)
            s.append(_dot_nt(k_both[c * CHUNK:c * CHUNK + wk], q_four))
    s = jnp.concatenate(s, axis=1)
    sink = jnp.concatenate([jnp.full((1, CHUNK), sink_ref[j * grp + g] * LOG2E, F32)
                            for j in range(SWA_KV_HEADS) for _ in range(nch) for g in range(grp)], axis=1)
    m = jnp.maximum(jnp.max(s, axis=0, keepdims=True), sink)
    p = jnp.exp2(s - m)
    inv = 1.0 / (jnp.sum(p, axis=0, keepdims=True) + jnp.exp2(sink - m))
    p = p.astype(BF16)
    o_t = []
    for j in range(SWA_KV_HEADS):
        for c in range(nch):
            cols = slice((j * nch + c) * gw, (j * nch + c + 1) * gw)
            o_t.append(_dot(vj_t[j][:, c * CHUNK:c * CHUNK + wk], p[:, cols]) * inv[:, cols])
    for c0 in range(0, nch, 2):
        pieces = []
        for j in range(SWA_KV_HEADS):
            for t in range(grp // 2):
                tiles = [o_t[j * nch + c][:, t * LANES:(t + 1) * LANES] for c in range(c0, min(c0 + 2, nch))]
                if len(tiles) == 2:
                    pieces += [jnp.where(low, tiles[0], pltpu.roll(tiles[1], CHUNK, axis=1)),
                               jnp.where(low, pltpu.roll(tiles[0], CHUNK, axis=1), tiles[1])]
                else:
                    pieces += [tiles[0], pltpu.roll(tiles[0], CHUNK, axis=1)]
        out = jnp.concatenate(pieces, axis=0).T
        r1 = min((c0 + 2) * CHUNK, rows)
        o_ref[c0 * CHUNK:r1] = out[:r1 - c0 * CHUNK].astype(o_ref.dtype)


def _swa_call(q, k_prev, k_cur, v_prev, v_cur, sinks, rows, same_array):
    b, n, _ = q.shape
    per = rows // WINDOW if same_array else 0
    cur = lambda w: pl.BlockSpec((None, rows, w), lambda bi, i: (bi, i, 0))
    prev = pl.BlockSpec((None, WINDOW, SWA_KV_W), lambda bi, i: (bi, jnp.maximum(i * per - 1, 0), 0))
    return pl.pallas_call(
        functools.partial(_swa_kernel, rows=rows, first_prev_valid=not same_array),
        grid=(b, n // rows),
        in_specs=[pl.BlockSpec(memory_space=pltpu.SMEM), cur(SWA_Q_W), prev, cur(SWA_KV_W), prev, cur(SWA_KV_W)],
        out_specs=cur(SWA_Q_W),
        out_shape=jax.ShapeDtypeStruct((b, n, SWA_Q_W), BF16),
        compiler_params=_params(("parallel", "parallel")),
        name="swa",
    )(sinks, q, k_prev, k_cur, v_prev, v_cur)


def _rope_tables(pos0, n, head_dim):
    half = head_dim // 2
    lane = jnp.arange(LANES)
    inv = jnp.power(ROPE_THETA, -(lane % half).astype(F32) / half)
    sign = jnp.where(lane % head_dim < half, -1.0, 1.0)
    fine = min(n, LANES)
    ang_c = (pos0 + fine * jnp.arange(n // fine)).astype(F32)[:, None] * inv[None, :]
    ang_f = jnp.arange(fine, dtype=F32)[:, None] * inv[None, :]
    cos_c, sin_c = jnp.cos(ang_c)[:, None, :], jnp.sin(ang_c)[:, None, :]
    cos_f, sin_f = jnp.cos(ang_f)[None, :, :], jnp.sin(ang_f)[None, :, :]
    return ((cos_c * cos_f - sin_c * sin_f).reshape(n, LANES),
            ((sin_c * cos_f + cos_c * sin_f) * sign).reshape(n, LANES))


def _prep_weights(norm_g, ffn_w_gate, ffn_w_up, ffn_w_down, even_w_in, even_w_out, odd_w_in, odd_w_out,
                  odd_q_norm, odd_k_norm):
    depth = norm_g.shape[0]
    prm = {"g": norm_g[:, :, None, :], "even": [], "odd": [], "depth": depth,
           "ffn": (ffn_w_gate.astype(BF16), ffn_w_up.astype(BF16), ffn_w_down.astype(BF16))}
    for i in range(even_w_in.shape[0]):
        prm["even"].append((even_w_in[i].astype(BF16), even_w_out[i].astype(BF16)))
    for i in range(odd_w_in.shape[0]):
        gain = jnp.concatenate([jnp.tile(odd_q_norm[i], SWA_HEADS), jnp.tile(odd_k_norm[i], SWA_KV_HEADS)])
        prm["odd"].append((odd_w_in[i].astype(BF16), odd_w_out[i].astype(BF16), gain[None, :]))
    return prm


def _trunk(x, pos0, prm, sinks, caches, tm, sb_qb, sb_kb, ret_cs, swa_rows):
    b, n, d = x.shape
    t = b * n
    xf = x.reshape(t, d)
    depth = prm["depth"]
    tab_rows = n if n % tm == 0 else t
    tile_tab = lambda tb: tb if tab_rows == n else jnp.tile(tb, (b, 1))
    cos_r, sin_r = [tile_tab(tb) for tb in _rope_tables(pos0, n, RET_QK_DIM)]
    cos_s, sin_s = [tile_tab(tb) for tb in _rope_tables(pos0, n, SWA_HEAD_DIM)]
    idx = jnp.arange(sb_kb)
    negu = jnp.where(idx[:, None] >= idx[None, :], -1.0, 0.0).astype(BF16)
    sb_k, sb_v, ret, swa_k, swa_v = [], [], [], [], []
    mix, w_mix = [], None
    for layer in range(depth):
        i = layer // 2
        g = prm["g"][layer]
        xf = _ffn_call(xf, mix, w_mix, g[0], prm["ffn"], layer, 0, tm)
        if layer % 2 == 0:
            w_in, w_mix = prm["even"][i]
            qa, ka, kab, va, vab, qr, kr, vr, gt = _even_proj_call(xf, g[1], w_in, cos_r, sin_r, tm)
            sb_k.append(ka.reshape(b, n, SB_HEADS, SB_HEAD_DIM))
            sb_v.append(va.reshape(b, n, SB_HEADS, SB_HEAD_DIM))
            r3 = lambda a: a.reshape(b, n, a.shape[-1])
            if caches is None:
                k_all, v_all, q_pos0 = r3(kab), r3(vab), 0
                state0 = jnp.zeros((b, RET_HEADS, RET_QK_DIM, RET_V_DIM), F32)
            else:
                past = caches["sb_k"].shape[2]
                padded = -(-(past + n) // sb_kb) * sb_kb
                cat = lambda cache, new: jnp.pad(
                    jnp.concatenate([cache.reshape(b, past, SB_W).astype(BF16), r3(new)], axis=1),
                    ((0, 0), (0, padded - past - n), (0, 0)))
                k_all, v_all, q_pos0 = cat(caches["sb_k"][i], kab), cat(caches["sb_v"][i], vab), past
                state0 = caches["ret"][i]
            o_sb = _sb_call(r3(qa), k_all, v_all, negu, sb_qb, q_pos0)
            o_r, st = _ret_call(r3(qr), r3(kr), r3(vr), r3(gt), state0, ret_cs)
            ret.append(st)
            mix = [o_sb.reshape(t, SB_W), o_r.reshape(t, RET_V_W)]
        else:
            w_in, w_mix, gain = prm["odd"][i]
            q, k, v = _odd_proj_call(xf, g[1], w_in, gain, cos_s, sin_s, tm)
            r3 = lambda a: a.reshape(b, n, a.shape[-1])
            k3, v3 = r3(k), r3(v)
            if caches is None:
                o = _swa_call(r3(q), k3, k3, v3, v3, sinks[i], swa_rows, True)
                keep = min(WINDOW, n)
                k_rows, v_rows = k3[:, n - keep:], v3[:, n - keep:]
            else:
                kc = caches["swa_k"][i].reshape(b, -1, SWA_KV_W)
                vc = caches["swa_v"][i].reshape(b, -1, SWA_KV_W)
                o = _swa_call(r3(q), kc, k3, vc, v3, sinks[i], swa_rows, False)
                k_rows, v_rows = k3, v3
            swa_k.append(k_rows.reshape(b, -1, SWA_KV_HEADS, SWA_HEAD_DIM))
            swa_v.append(v_rows.reshape(b, -1, SWA_KV_HEADS, SWA_HEAD_DIM))
            mix = [o.reshape(t, SWA_Q_W)]
        xf = _ffn_call(xf, mix, w_mix, g[2], prm["ffn"], layer, 1, tm)
        mix, w_mix = [], None
    stack = lambda parts: parts[0][None] if len(parts) == 1 else jnp.stack(parts)
    return (xf.reshape(b, n, d), stack(sb_k), stack(sb_v), stack(ret), stack(swa_k), stack(swa_v))


def kernel(x_prompt, x_sample, cache_sb_k, cache_sb_v, state_ret, cache_swa_k, cache_swa_v, norm_g, ffn_w_gate,
           ffn_w_up, ffn_w_down, even_w_in, even_w_out, odd_w_in, odd_w_out, odd_q_norm, odd_k_norm, odd_sinks):
    prm = _prep_weights(norm_g, ffn_w_gate, ffn_w_up, ffn_w_down, even_w_in, even_w_out, odd_w_in, odd_w_out,
                        odd_q_norm, odd_k_norm)
    past = cache_sb_k.shape[2]
    n_s = x_sample.shape[1]
    y_p, sb_k_p, sb_v_p, ret_p, swa_k_p, swa_v_p = _trunk(
        x_prompt, 0, prm, odd_sinks, None, tm=512, sb_qb=2048, sb_kb=128, ret_cs=256, swa_rows=1024)
    caches = {"sb_k": cache_sb_k, "sb_v": cache_sb_v, "ret": state_ret, "swa_k": cache_swa_k, "swa_v": cache_swa_v}
    y_s, sb_k_s, sb_v_s, ret_s, swa_k_s, swa_v_s = _trunk(
        x_sample, past, prm, odd_sinks, caches, tm=x_sample.shape[0] * n_s, sb_qb=n_s, sb_kb=128,
        ret_cs=n_s, swa_rows=n_s)
    return (y_p, y_s, sb_k_p, sb_v_p, ret_p, swa_k_p, swa_v_p, sb_k_s, sb_v_s, ret_s, swa_k_s, swa_v_s)
```

```python
import functools
import math

import jax
import jax.numpy as jnp
from jax import lax
from jax.experimental import pallas as pl
from jax.experimental.pallas import tpu as pltpu

F32 = jnp.float32
BF16 = jnp.bfloat16

RMS_EPS = 1e-6
ROPE_THETA = 10000.0
CHUNK = 64
SB_HEADS = 8
SB_HEAD_DIM = 64
RET_HEADS = 4
RET_QK_DIM = 128
RET_V_DIM = 256
SWA_HEADS = 16
SWA_KV_HEADS = 4
SWA_HEAD_DIM = 64
WINDOW = 128

SB_W = SB_HEADS * SB_HEAD_DIM
RET_QK_W = RET_HEADS * RET_QK_DIM
RET_V_W = RET_HEADS * RET_V_DIM
SWA_Q_W = SWA_HEADS * SWA_HEAD_DIM
SWA_KV_W = SWA_KV_HEADS * SWA_HEAD_DIM
EVEN_CUTS = (0, SB_W, 2 * SB_W, 3 * SB_W, 3 * SB_W + RET_QK_W, 3 * SB_W + 2 * RET_QK_W,
             3 * SB_W + 2 * RET_QK_W + RET_V_W, 3 * SB_W + 2 * RET_QK_W + 2 * RET_V_W)

LANES = 128
MXU_TILE = 256
FF_CHUNK = MXU_TILE
VMEM_LIMIT = 56 * 1024 * 1024
LOG2E = math.log2(math.e)
SOFTPLUS2_LINEAR = 40.0
SWA_MASKED = -1e30
SB_STATIC_BLOCKS = 3
SB_DEAD_LOG2 = -170.0


def _dot(a, b):
    return jnp.dot(a, b, preferred_element_type=F32)


def _dot_nt(a, b):
    return lax.dot_general(a, b, (((1,), (1,)), ((), ())), preferred_element_type=F32)


def _dot_tn(a, b):
    return lax.dot_general(a, b, (((0,), (0,)), ((), ())), preferred_element_type=F32)


def _rms(x, g):
    return x * lax.rsqrt(jnp.mean(x * x, axis=-1, keepdims=True) + RMS_EPS) * g


def _full_spec(shape):
    nd = len(shape)
    return pl.BlockSpec(shape, lambda *_: (0,) * nd, pipeline_mode=pl.Buffered(1))


def _params(sem):
    return pltpu.CompilerParams(dimension_semantics=sem, vmem_limit_bytes=VMEM_LIMIT)


def _ffn_kernel(*refs, n_mix):
    x_ref = refs[0]
    mix = refs[1:1 + n_mix]
    wmix_ref = refs[1 + n_mix] if n_mix else None
    g_ref, wg_ref, wu_ref, wd_ref, o_ref, h_ref, act_ref = refs[1 + n_mix + bool(n_mix):]
    x = x_ref[...]
    r0 = 0
    for a_ref in mix:
        x = x + _dot(a_ref[...], wmix_ref[r0:r0 + a_ref.shape[1]])
        r0 += a_ref.shape[1]
    h_ref[...] = _rms(x, g_ref[...]).astype(BF16)
    for c in range(wg_ref.shape[1] // FF_CHUNK):
        cols = slice(c * FF_CHUNK, (c + 1) * FF_CHUNK)
        gate = _dot(h_ref[...], wg_ref[:, cols])
        up = _dot(h_ref[...], wu_ref[:, cols])
        act_ref[:, cols] = (gate * jax.nn.sigmoid(gate) * up).astype(BF16)
    o_ref[...] = x + 0.5 * _dot(act_ref[...], wd_ref[...])


def _ffn_call(x, mix, w_mix, g, ffn_w, layer, slot, tm):
    wg, wu, wd = ffn_w
    stack_spec = lambda w: pl.BlockSpec((None, None) + w.shape[2:], lambda i: (layer, slot, 0, 0),
                                        pipeline_mode=pl.Buffered(1))
    t, d = x.shape
    row = lambda w: pl.BlockSpec((tm, w), lambda i: (i, 0))
    in_specs = [row(d)]
    args = [x]
    for a in mix:
        in_specs.append(row(a.shape[1]))
        args.append(a)
    if mix:
        in_specs.append(_full_spec(w_mix.shape))
        args.append(w_mix)
    in_specs += [_full_spec(g.shape), stack_spec(wg), stack_spec(wu), stack_spec(wd)]
    args += [g, wg, wu, wd]
    return pl.pallas_call(
        functools.partial(_ffn_kernel, n_mix=len(mix)),
        grid=(t // tm,),
        in_specs=in_specs,
        out_specs=row(d),
        out_shape=jax.ShapeDtypeStruct((t, d), F32),
        scratch_shapes=[pltpu.VMEM((tm, d), BF16), pltpu.VMEM((tm, wg.shape[-1]), BF16)],
        compiler_params=_params(("parallel",)),
        name="ffn_mix%d" % len(mix),
    )(*args)


def _rope128(x, cos2, sin2):
    parts = []
    for hd in range(x.shape[1] // LANES):
        sl = x[:, hd * LANES:(hd + 1) * LANES]
        parts.append(sl * cos2 + pltpu.roll(sl, LANES // 2, axis=1) * sin2)
    return jnp.concatenate(parts, axis=1)


def _even_proj_kernel(x_ref, g_ref, w_ref, cos_ref, sin_ref,
                      qa_ref, ka_ref, kab_ref, va_ref, vab_ref, qr_ref, kr_ref, vr_ref, gt_ref):
    h = _rms(x_ref[...], g_ref[...]).astype(BF16)
    proj = lambda s: _dot(h, w_ref[:, EVEN_CUTS[s]:EVEN_CUTS[s + 1]])
    qa_ref[...] = (proj(0) * (LOG2E * SB_HEAD_DIM ** -0.5)).astype(BF16)
    ka = proj(1)
    ka_ref[...] = ka.reshape(ka_ref.shape)
    kab_ref[...] = ka.astype(BF16)
    va = proj(2)
    va_ref[...] = va.reshape(va_ref.shape)
    vab_ref[...] = va.astype(BF16)
    cos2 = cos_ref[...]
    sin2 = sin_ref[...]
    qr_ref[...] = _rope128(proj(3), cos2, sin2).astype(BF16)
    kr_ref[...] = (_rope128(proj(4), cos2, sin2) * (RET_QK_DIM ** -0.5)).astype(BF16)
    vr_ref[...] = proj(5).astype(BF16)
    gate = proj(6)
    gt_ref[...] = gate * jax.nn.sigmoid(gate)


def _even_proj_call(x, g, w_in, cos2, sin2, tm):
    t, d = x.shape
    n_tab = cos2.shape[0] // tm
    row = lambda w: pl.BlockSpec((tm, w), lambda i: (i, 0))
    tab = pl.BlockSpec((tm, LANES), lambda i: (i % n_tab, 0))
    per_head = (SB_HEADS, SB_HEAD_DIM)
    outs = [((SB_W,), BF16), (per_head, F32), ((SB_W,), BF16), (per_head, F32), ((SB_W,), BF16),
            ((RET_QK_W,), BF16), ((RET_QK_W,), BF16), ((RET_V_W,), BF16), ((RET_V_W,), F32)]
    return pl.pallas_call(
        _even_proj_kernel,
        grid=(t // tm,),
        in_specs=[row(d), _full_spec(g.shape), _full_spec(w_in.shape), tab, tab],
        out_specs=[pl.BlockSpec((tm,) + tail, lambda i, nd=len(tail): (i,) + (0,) * nd) for tail, _ in outs],
        out_shape=[jax.ShapeDtypeStruct((t,) + tail, dt) for tail, dt in outs],
        compiler_params=_params(("parallel",)),
        name="even_proj",
    )(x, g, w_in, cos2, sin2)


def _sb_kernel(q_ref, k_ref, v_ref, negu_ref, o_ref, qh_ref, c_ref, acc_ref, *, qb, kb, q_pos0):
    i = pl.program_id(2)
    n_clear = (q_pos0 + i * qb) // kb
    sq = min(qb, kb)
    nsub = qb // sq
    rows = lambda s: slice(s * sq, (s + 1) * sq)
    kstart = lambda j: pl.multiple_of(j * kb, kb)
    q = q_ref[...]
    first = lax.broadcasted_iota(jnp.int32, (qb, LANES), 1) < SB_HEAD_DIM
    qh_ref[0] = jnp.where(first, q, jnp.zeros_like(q))
    qh_ref[1] = jnp.where(first, jnp.zeros_like(q), q)

    def scores(hd, rws, start):
        return _dot_nt(qh_ref[hd, rws], k_ref[pl.ds(start, kb), :])

    def softplus2(z):
        return jnp.where(z > SOFTPLUS2_LINEAR, z, jnp.log2(1.0 + jnp.exp2(z)))

    def v_heads(start):
        vs = v_ref[pl.ds(start, kb), :]
        vfirst = lax.broadcasted_iota(jnp.int32, (kb, LANES), 1) < SB_HEAD_DIM
        return jnp.concatenate([jnp.where(vfirst, vs, jnp.zeros_like(vs)),
                                jnp.where(vfirst, jnp.zeros_like(vs), vs)], axis=0)

    depth = SB_STATIC_BLOCKS
    row = lax.broadcasted_iota(jnp.int32, (qb, kb), 0)
    below = lax.broadcasted_iota(jnp.int32, (qb, kb), 1) < (row & (sq - 1))
    kblock = lambda s, d: kstart(jnp.maximum(n_clear + s - d, 0))

    def apply_masks(t):
        parts = [jnp.where(below, t[:qb], 0.0)]
        for d in range(1, depth):
            for s in range(min(d, nsub)):
                exists = jnp.broadcast_to(n_clear + s - d >= 0, (sq, kb))
                parts.append(jnp.where(exists, t[d * qb + s * sq:d * qb + (s + 1) * sq], 0.0))
            if nsub > d:
                parts.append(t[d * qb + d * sq:(d + 1) * qb])
        return jnp.concatenate(parts, axis=0)

    ws = []
    for hd in range(2):
        z = jnp.concatenate([scores(hd, rows(s), kblock(s, d)) for d in range(depth) for s in range(nsub)], axis=0)
        sp = apply_masks(softplus2(z))
        tail = _dot(sp.astype(BF16), negu_ref[...])
        seen = [jnp.zeros((qb, 1), F32)]
        for d in range(depth):
            seen.append(seen[-1] + tail[d * qb:(d + 1) * qb, :1])
        c_ref[hd] = seen[depth]
        ws.append(apply_masks(jnp.exp2(z + tail + jnp.concatenate(seen[:depth], axis=0))).astype(BF16))
    w = jnp.concatenate(ws, axis=1)
    for s in range(nsub):
        acc = None
        for d in range(depth):
            part = _dot(w[d * qb + s * sq:d * qb + (s + 1) * sq], v_heads(kblock(s, d)))
            acc = part if acc is None else acc + part
        acc_ref[rows(s)] = acc

    def block(start, r0, r1):
        ws = []
        for hd in range(2):
            z = scores(hd, slice(r0, r1), start)
            tail = _dot(softplus2(z).astype(BF16), negu_ref[...])
            ws.append(jnp.exp2(z + tail + c_ref[hd, r0:r1]).astype(BF16))
            c_ref[hd, r0:r1] += tail[:, :1]
        acc_ref[r0:r1] += _dot(jnp.concatenate(ws, axis=1), v_heads(start))

    def alive(r0, r1):
        return (jnp.max(c_ref[:, r0:r1]) > SB_DEAD_LOG2).astype(jnp.int32)

    def run_blocks(j_first, j_last, r0, r1):
        def cond(carry):
            j, live = carry
            return jnp.logical_and(j >= j_last, live > 0)

        def body(carry):
            j, _ = carry
            block(kstart(j), r0, r1)
            return j - 1, alive(r0, r1)

        lax.while_loop(cond, body, (j_first, alive(r0, r1)))

    @pl.when(alive(0, qb) > 0)
    def _():
        for s in range(1, nsub):
            run_blocks(n_clear + s - depth, jnp.maximum(n_clear - depth + 1, 0), s * sq, (s + 1) * sq)
        run_blocks(n_clear - depth, 0, 0, qb)

    o_ref[...] = acc_ref[...].astype(o_ref.dtype)


def _sb_call(q, k, v, negu, qb, q_pos0):
    b, nq, w = q.shape
    nk = k.shape[1]
    kb = negu.shape[0]
    assert q_pos0 % kb == 0 and (qb % kb == 0 or nq == qb < kb) and nk % kb == 0 and nk >= q_pos0 + nq
    return pl.pallas_call(
        functools.partial(_sb_kernel, qb=qb, kb=kb, q_pos0=q_pos0),
        grid=(b, w // LANES, nq // qb),
        in_specs=[pl.BlockSpec((None, qb, LANES), lambda bi, p, i: (bi, i, p)),
                  pl.BlockSpec((None, nk, LANES), lambda bi, p, i: (bi, 0, p)),
                  pl.BlockSpec((None, nk, LANES), lambda bi, p, i: (bi, 0, p)),
                  _full_spec(negu.shape)],
        out_specs=pl.BlockSpec((None, qb, LANES), lambda bi, p, i: (bi, i, p)),
        out_shape=jax.ShapeDtypeStruct((b, nq, w), BF16),
        scratch_shapes=[pltpu.VMEM((2, qb, LANES), BF16), pltpu.VMEM((2, qb, 1), F32),
                        pltpu.VMEM((qb, LANES), F32)],
        compiler_params=_params(("parallel", "parallel", "parallel")),
        name="stick_breaking",
    )(q, k, v, negu)


def _ret_kernel(sdec_ref, q_ref, k_ref, v_ref, gt_ref, s0_ref, dec_ref, qdec_ref, kdec_ref,
                o_ref, sfin_ref, st_ref):
    c = pl.program_id(0)

    @pl.when(c == 0)
    def _():
        st_ref[...] = s0_ref[...]

    units = [(b, h) for b in range(q_ref.shape[0]) for h in range(RET_HEADS)]
    qk = lambda h: slice(h * RET_QK_DIM, (h + 1) * RET_QK_DIM)
    vw = lambda h: slice(h * RET_V_DIM, (h + 1) * RET_V_DIM)
    cs = dec_ref.shape[1]
    for part in range(q_ref.shape[1] // cs):
        r = slice(part * cs, (part + 1) * cs)
        scores = [_dot_nt(q_ref[b, r, qk(h)], k_ref[b, r, qk(h)]) for b, h in units]
        cross = [_dot(q_ref[b, r, qk(h)], st_ref[b, h].astype(BF16)) for b, h in units]
        kw = [(k_ref[b, r, qk(h)].astype(F32) * kdec_ref[h]).astype(BF16) for b, h in units]
        grow = [_dot_tn(kw[u], v_ref[b, r, vw(h)]) for u, (b, h) in enumerate(units)]
        decayed = [(scores[u] * dec_ref[h]).astype(BF16) for u, (b, h) in enumerate(units)]
        inner = [_dot(decayed[u], v_ref[b, r, vw(h)]) for u, (b, h) in enumerate(units)]
        for u, (b, h) in enumerate(units):
            st_ref[b, h] = sdec_ref[h] * st_ref[b, h] + grow[u]
            o = inner[u] + cross[u] * qdec_ref[h]
            o = o * lax.rsqrt(jnp.mean(o * o, axis=-1, keepdims=True) + RMS_EPS)
            o_ref[b, r, vw(h)] = (o * gt_ref[b, r, vw(h)]).astype(o_ref.dtype)

    @pl.when(c == pl.num_programs(0) - 1)
    def _():
        sfin_ref[...] = st_ref[...]


def _ret_call(q, k, v, gate, state0, cs, per):
    b, n, _ = q.shape
    log_gamma = jnp.log1p(-jnp.exp2(-5.0 - jnp.arange(RET_HEADS, dtype=F32)))
    pos = jnp.arange(cs, dtype=F32)
    diff = pos[:, None] - pos[None, :]
    dec = jnp.where(diff >= 0, jnp.exp(log_gamma[:, None, None] * jnp.maximum(diff, 0.0)), 0.0)
    qdec = jnp.exp(log_gamma[:, None] * (pos + 1.0))[..., None]
    kdec = jnp.exp(log_gamma[:, None] * (cs - 1.0 - pos))[..., None]
    sdec = jnp.exp(log_gamma * cs)
    seq = lambda w: pl.BlockSpec((b, cs * per, w), lambda c: (0, c, 0))
    st = pl.BlockSpec((b, RET_HEADS, RET_QK_DIM, RET_V_DIM), lambda c: (0, 0, 0, 0))
    return pl.pallas_call(
        _ret_kernel,
        grid=(n // (cs * per),),
        in_specs=[pl.BlockSpec(memory_space=pltpu.SMEM), seq(RET_QK_W), seq(RET_QK_W), seq(RET_V_W),
                  seq(RET_V_W), st, _full_spec(dec.shape), _full_spec(qdec.shape), _full_spec(kdec.shape)],
        out_specs=[seq(RET_V_W), st],
        out_shape=[jax.ShapeDtypeStruct((b, n, RET_V_W), BF16),
                   jax.ShapeDtypeStruct((b, RET_HEADS, RET_QK_DIM, RET_V_DIM), F32)],
        scratch_shapes=[pltpu.VMEM((b, RET_HEADS, RET_QK_DIM, RET_V_DIM), F32)],
        compiler_params=_params(("arbitrary",)),
        name="retention",
    )(sdec, q, k, v, gate, state0, dec, qdec, kdec)


def _head_norm_rope(t, bd, gain, cos2, sin2):
    wide = bd.shape[0]
    parts = []
    for c in range(t.shape[1] // wide):
        sq = t[:, c * wide:(c + 1) * wide]
        sq = sq * sq
        hi = sq.astype(BF16)
        lo = (sq - hi.astype(F32)).astype(BF16)
        parts.append(_dot(hi, bd) + _dot(lo, bd))
    ms = jnp.concatenate(parts, axis=1) * (1.0 / SWA_HEAD_DIM)
    y = t * lax.rsqrt(ms + RMS_EPS) * gain
    w = t.shape[1]
    reps = w // LANES
    quarter = SWA_HEAD_DIM // 2
    first_half = (lax.broadcasted_iota(jnp.int32, t.shape, 1) % SWA_HEAD_DIM) < quarter
    partner = jnp.where(first_half, pltpu.roll(y, w - quarter, axis=1), pltpu.roll(y, quarter, axis=1))
    return y * jnp.concatenate([cos2] * reps, axis=1) + partner * jnp.concatenate([sin2] * reps, axis=1)


def _odd_proj_kernel(x_ref, g_ref, w_ref, bd_ref, gain_ref, cos_ref, sin_ref, q_ref, k_ref, v_ref):
    h = _rms(x_ref[...], g_ref[...]).astype(BF16)
    qk_w = SWA_Q_W + SWA_KV_W
    qk = _dot(h, w_ref[:, :qk_w])
    v_ref[...] = _dot(h, w_ref[:, qk_w:])
    qk = _head_norm_rope(qk, bd_ref[...], gain_ref[...], cos_ref[...], sin_ref[...])
    q_ref[...] = (qk[:, :SWA_Q_W] * (LOG2E * SWA_HEAD_DIM ** -0.5)).astype(BF16)
    k_ref[...] = qk[:, SWA_Q_W:]


def _odd_proj_call(x, g, w_in, gain, cos2, sin2, tm):
    t, d = x.shape
    n_tab = cos2.shape[0] // tm
    head = jnp.arange(MXU_TILE) // SWA_HEAD_DIM
    bd = (head[:, None] == head[None, :]).astype(BF16)
    row = lambda w: pl.BlockSpec((tm, w), lambda i: (i, 0))
    tab = pl.BlockSpec((tm, LANES), lambda i: (i % n_tab, 0))
    ins = [g, w_in, bd, gain]
    return pl.pallas_call(
        _odd_proj_kernel,
        grid=(t // tm,),
        in_specs=[row(d)] + [_full_spec(a.shape) for a in ins] + [tab, tab],
        out_specs=[row(SWA_Q_W), row(SWA_KV_W), row(SWA_KV_W)],
        out_shape=[jax.ShapeDtypeStruct((t, SWA_Q_W), BF16), jax.ShapeDtypeStruct((t, SWA_KV_W), F32),
                   jax.ShapeDtypeStruct((t, SWA_KV_W), F32)],
        compiler_params=_params(("parallel",)),
        name="odd_proj",
    )(x, *ins, cos2, sin2)


def _swa_kernel(sink_ref, q_ref, kp_ref, kc_ref, vp_ref, vc_ref, o_ref, *, rows, first_prev_valid):
    i = pl.program_id(1)
    nch = rows // CHUNK
    wk = WINDOW + CHUNK
    nk = WINDOW + rows
    grp = SWA_HEADS // SWA_KV_HEADS
    gw = grp * CHUNK
    key_low = lax.broadcasted_iota(jnp.int32, (nk, LANES), 1) < SWA_HEAD_DIM
    low = lax.broadcasted_iota(jnp.int32, (CHUNK, LANES), 1) < SWA_HEAD_DIM
    if not first_prev_valid:
        krow = lax.broadcasted_iota(jnp.int32, (nk, LANES), 0)
        klane = lax.broadcasted_iota(jnp.int32, (nk, LANES), 1)
        k_bias = jnp.where((krow < WINDOW) & (klane == 0) & (i == 0), SWA_MASKED, 0.0).astype(BF16)
        q_one = jnp.where(lax.broadcasted_iota(jnp.int32, (gw, LANES), 1) == 0, 1.0, 0.0).astype(BF16)

    s, vj_t = [], []
    for j in range(SWA_KV_HEADS):
        pair, odd = divmod(j, 2)
        lanes = slice(pair * LANES, (pair + 1) * LANES)
        k_pair = jnp.concatenate([kp_ref[:, lanes], kc_ref[:, lanes]], axis=0)
        own = jnp.where(key_low, 0.0, k_pair) if odd else jnp.where(key_low, k_pair, 0.0)
        k_both = (own + pltpu.roll(own, SWA_HEAD_DIM, axis=1)).astype(BF16)
        v_t = jnp.concatenate([vp_ref[:, lanes], vc_ref[:, lanes]], axis=0).T
        vj_t.append(v_t[odd * SWA_HEAD_DIM:(odd + 1) * SWA_HEAD_DIM].astype(BF16))
        if not first_prev_valid:
            k_both = jnp.concatenate([k_both, k_bias], axis=1)
        for c in range(nch):
            q_four = []
            for qp in (2 * j, 2 * j + 1):
                q_pair = q_ref[c * CHUNK:(c + 1) * CHUNK, qp * LANES:(qp + 1) * LANES]
                zero = jnp.zeros_like(q_pair)
                q_four += [jnp.where(low, q_pair, zero), jnp.where(low, zero, q_pair)]
            q_four = jnp.concatenate(q_four, axis=0)
            if not first_prev_valid:
                q_four = jnp.concatenate([q_four, q_one], axis=1)
            s.append(_dot_nt(k_both[c * CHUNK:c * CHUNK + wk], q_four))
    s = jnp.concatenate(s, axis=1)
    sink = jnp.concatenate([jnp.full((1, CHUNK), sink_ref[j * grp + g] * LOG2E, F32)
                            for j in range(SWA_KV_HEADS) for _ in range(nch) for g in range(grp)], axis=1)
    m = jnp.maximum(jnp.max(s, axis=0, keepdims=True), sink)
    p = jnp.exp2(s - m)
    inv = 1.0 / (jnp.sum(p, axis=0, keepdims=True) + jnp.exp2(sink - m))
    p = p.astype(BF16)
    o_t = []
    for j in range(SWA_KV_HEADS):
        for c in range(nch):
            cols = slice((j * nch + c) * gw, (j * nch + c + 1) * gw)
            o_t.append(_dot(vj_t[j][:, c * CHUNK:c * CHUNK + wk], p[:, cols]) * inv[:, cols])
    for c0 in range(0, nch, 2):
        pieces = []
        for j in range(SWA_KV_HEADS):
            for t in range(grp // 2):
                tiles = [o_t[j * nch + c][:, t * LANES:(t + 1) * LANES] for c in range(c0, min(c0 + 2, nch))]
                if len(tiles) == 2:
                    pieces += [jnp.where(low, tiles[0], pltpu.roll(tiles[1], CHUNK, axis=1)),
                               jnp.where(low, pltpu.roll(tiles[0], CHUNK, axis=1), tiles[1])]
                else:
                    pieces += [tiles[0], pltpu.roll(tiles[0], CHUNK, axis=1)]
        out = jnp.concatenate(pieces, axis=0).T
        r1 = min((c0 + 2) * CHUNK, rows)
        o_ref[c0 * CHUNK:r1] = out[:r1 - c0 * CHUNK].astype(o_ref.dtype)


def _swa_call(q, k_prev, k_cur, v_prev, v_cur, sinks, rows, same_array):
    b, n, _ = q.shape
    per = rows // WINDOW if same_array else 0
    cur = lambda w: pl.BlockSpec((None, rows, w), lambda bi, i: (bi, i, 0))
    prev = pl.BlockSpec((None, WINDOW, SWA_KV_W), lambda bi, i: (bi, jnp.maximum(i * per - 1, 0), 0))
    return pl.pallas_call(
        functools.partial(_swa_kernel, rows=rows, first_prev_valid=not same_array),
        grid=(b, n // rows),
        in_specs=[pl.BlockSpec(memory_space=pltpu.SMEM), cur(SWA_Q_W), prev, cur(SWA_KV_W), prev, cur(SWA_KV_W)],
        out_specs=cur(SWA_Q_W),
        out_shape=jax.ShapeDtypeStruct((b, n, SWA_Q_W), BF16),
        compiler_params=_params(("parallel", "parallel")),
        name="swa",
    )(sinks, q, k_prev, k_cur, v_prev, v_cur)


def _rope_tables(pos0, n, head_dim):
    half = head_dim // 2
    lane = jnp.arange(LANES)
    inv = jnp.power(ROPE_THETA, -(lane % half).astype(F32) / half)
    sign = jnp.where(lane % head_dim < half, -1.0, 1.0)
    fine = min(n, LANES)
    ang_c = (pos0 + fine * jnp.arange(n // fine)).astype(F32)[:, None] * inv[None, :]
    ang_f = jnp.arange(fine, dtype=F32)[:, None] * inv[None, :]
    cos_c, sin_c = jnp.cos(ang_c)[:, None, :], jnp.sin(ang_c)[:, None, :]
    cos_f, sin_f = jnp.cos(ang_f)[None, :, :], jnp.sin(ang_f)[None, :, :]
    return ((cos_c * cos_f - sin_c * sin_f).reshape(n, LANES),
            ((sin_c * cos_f + cos_c * sin_f) * sign).reshape(n, LANES))


def _prep_weights(norm_g, ffn_w_gate, ffn_w_up, ffn_w_down, even_w_in, even_w_out, odd_w_in, odd_w_out,
                  odd_q_norm, odd_k_norm):
    depth = norm_g.shape[0]
    prm = {"g": norm_g[:, :, None, :], "even": [], "odd": [], "depth": depth,
           "ffn": (ffn_w_gate.astype(BF16), ffn_w_up.astype(BF16), ffn_w_down.astype(BF16))}
    for i in range(even_w_in.shape[0]):
        prm["even"].append((even_w_in[i].astype(BF16), even_w_out[i].astype(BF16)))
    for i in range(odd_w_in.shape[0]):
        gain = jnp.concatenate([jnp.tile(odd_q_norm[i], SWA_HEADS), jnp.tile(odd_k_norm[i], SWA_KV_HEADS)])
        prm["odd"].append((odd_w_in[i].astype(BF16), odd_w_out[i].astype(BF16), gain[None, :]))
    return prm


def _trunk(x, pos0, prm, sinks, caches, tm, tm_ffn, sb_qb, sb_kb, ret_cs, ret_per, swa_rows):
    b, n, d = x.shape
    t = b * n
    xf = x.reshape(t, d)
    depth = prm["depth"]
    tab_rows = n if n % tm == 0 else t
    tile_tab = lambda tb: tb if tab_rows == n else jnp.tile(tb, (b, 1))
    cos_r, sin_r = [tile_tab(tb) for tb in _rope_tables(pos0, n, RET_QK_DIM)]
    cos_s, sin_s = [tile_tab(tb) for tb in _rope_tables(pos0, n, SWA_HEAD_DIM)]
    idx = jnp.arange(sb_kb)
    negu = jnp.where(idx[:, None] >= idx[None, :], -1.0, 0.0).astype(BF16)
    sb_k, sb_v, ret, swa_k, swa_v = [], [], [], [], []
    mix, w_mix = [], None
    for layer in range(depth):
        i = layer // 2
        g = prm["g"][layer]
        xf = _ffn_call(xf, mix, w_mix, g[0], prm["ffn"], layer, 0, tm_ffn)
        if layer % 2 == 0:
            w_in, w_mix = prm["even"][i]
            qa, ka, kab, va, vab, qr, kr, vr, gt = _even_proj_call(xf, g[1], w_in, cos_r, sin_r, tm)
            sb_k.append(ka.reshape(b, n, SB_HEADS, SB_HEAD_DIM))
            sb_v.append(va.reshape(b, n, SB_HEADS, SB_HEAD_DIM))
            r3 = lambda a: a.reshape(b, n, a.shape[-1])
            if caches is None:
                k_all, v_all, q_pos0 = r3(kab), r3(vab), 0
                state0 = jnp.zeros((b, RET_HEADS, RET_QK_DIM, RET_V_DIM), F32)
            else:
                past = caches["sb_k"].shape[2]
                padded = -(-(past + n) // sb_kb) * sb_kb
                cat = lambda cache, new: jnp.pad(
                    jnp.concatenate([cache.reshape(b, past, SB_W).astype(BF16), r3(new)], axis=1),
                    ((0, 0), (0, padded - past - n), (0, 0)))
                k_all, v_all, q_pos0 = cat(caches["sb_k"][i], kab), cat(caches["sb_v"][i], vab), past
                state0 = caches["ret"][i]
            o_sb = _sb_call(r3(qa), k_all, v_all, negu, sb_qb, q_pos0)
            o_r, st = _ret_call(r3(qr), r3(kr), r3(vr), r3(gt), state0, ret_cs, ret_per)
            ret.append(st)
            mix = [o_sb.reshape(t, SB_W), o_r.reshape(t, RET_V_W)]
        else:
            w_in, w_mix, gain = prm["odd"][i]
            q, k, v = _odd_proj_call(xf, g[1], w_in, gain, cos_s, sin_s, tm)
            r3 = lambda a: a.reshape(b, n, a.shape[-1])
            k3, v3 = r3(k), r3(v)
            if caches is None:
                o = _swa_call(r3(q), k3, k3, v3, v3, sinks[i], swa_rows, True)
                keep = min(WINDOW, n)
                k_rows, v_rows = k3[:, n - keep:], v3[:, n - keep:]
            else:
                kc = caches["swa_k"][i].reshape(b, -1, SWA_KV_W)
                vc = caches["swa_v"][i].reshape(b, -1, SWA_KV_W)
                o = _swa_call(r3(q), kc, k3, vc, v3, sinks[i], swa_rows, False)
                k_rows, v_rows = k3, v3
            swa_k.append(k_rows.reshape(b, -1, SWA_KV_HEADS, SWA_HEAD_DIM))
            swa_v.append(v_rows.reshape(b, -1, SWA_KV_HEADS, SWA_HEAD_DIM))
            mix = [o.reshape(t, SWA_Q_W)]
        xf = _ffn_call(xf, mix, w_mix, g[2], prm["ffn"], layer, 1, tm_ffn)
        mix, w_mix = [], None
    stack = lambda parts: parts[0][None] if len(parts) == 1 else jnp.stack(parts)
    return (xf.reshape(b, n, d), stack(sb_k), stack(sb_v), stack(ret), stack(swa_k), stack(swa_v))


def kernel(x_prompt, x_sample, cache_sb_k, cache_sb_v, state_ret, cache_swa_k, cache_swa_v, norm_g, ffn_w_gate,
           ffn_w_up, ffn_w_down, even_w_in, even_w_out, odd_w_in, odd_w_out, odd_q_norm, odd_k_norm, odd_sinks):
    prm = _prep_weights(norm_g, ffn_w_gate, ffn_w_up, ffn_w_down, even_w_in, even_w_out, odd_w_in, odd_w_out,
                        odd_q_norm, odd_k_norm)
    past = cache_sb_k.shape[2]
    n_s = x_sample.shape[1]
    y_p, sb_k_p, sb_v_p, ret_p, swa_k_p, swa_v_p = _trunk(
        x_prompt, 0, prm, odd_sinks, None, tm=512, tm_ffn=1024, sb_qb=2048, sb_kb=128, ret_cs=256, ret_per=2, swa_rows=1024)
    caches = {"sb_k": cache_sb_k, "sb_v": cache_sb_v, "ret": state_ret, "swa_k": cache_swa_k, "swa_v": cache_swa_v}
    y_s, sb_k_s, sb_v_s, ret_s, swa_k_s, swa_v_s = _trunk(
        x_sample, past, prm, odd_sinks, caches, tm=x_sample.shape[0] * n_s, tm_ffn=x_sample.shape[0] * n_s, sb_qb=n_s, sb_kb=128,
        ret_cs=n_s, ret_per=1, swa_rows=n_s)
    return (y_p, y_s, sb_k_p, sb_v_p, ret_p, swa_k_p, swa_v_p, sb_k_s, sb_v_s, ret_s, swa_k_s, swa_v_s)
```

```python
import functools
import math

import jax
import jax.numpy as jnp
from jax import lax
from jax.experimental import pallas as pl
from jax.experimental.pallas import tpu as pltpu

F32 = jnp.float32
BF16 = jnp.bfloat16

RMS_EPS = 1e-6
ROPE_THETA = 10000.0
CHUNK = 64
SB_HEADS = 8
SB_HEAD_DIM = 64
RET_HEADS = 4
RET_QK_DIM = 128
RET_V_DIM = 256
SWA_HEADS = 16
SWA_KV_HEADS = 4
SWA_HEAD_DIM = 64
WINDOW = 128

SB_W = SB_HEADS * SB_HEAD_DIM
RET_QK_W = RET_HEADS * RET_QK_DIM
RET_V_W = RET_HEADS * RET_V_DIM
SWA_Q_W = SWA_HEADS * SWA_HEAD_DIM
SWA_KV_W = SWA_KV_HEADS * SWA_HEAD_DIM
EVEN_CUTS = (0, SB_W, 2 * SB_W, 3 * SB_W, 3 * SB_W + RET_QK_W, 3 * SB_W + 2 * RET_QK_W,
             3 * SB_W + 2 * RET_QK_W + RET_V_W, 3 * SB_W + 2 * RET_QK_W + 2 * RET_V_W)

LANES = 128
MXU_TILE = 256
FF_CHUNK = MXU_TILE
VMEM_LIMIT = 56 * 1024 * 1024
LOG2E = math.log2(math.e)
SOFTPLUS2_LINEAR = 40.0
SWA_MASKED = -1e30
SB_STATIC_BLOCKS = 3
SB_DEAD_LOG2 = -170.0


def _dot(a, b):
    return jnp.dot(a, b, preferred_element_type=F32)


def _dot_nt(a, b):
    return lax.dot_general(a, b, (((1,), (1,)), ((), ())), preferred_element_type=F32)


def _dot_tn(a, b):
    return lax.dot_general(a, b, (((0,), (0,)), ((), ())), preferred_element_type=F32)


def _rms(x, g):
    return x * lax.rsqrt(jnp.mean(x * x, axis=-1, keepdims=True) + RMS_EPS) * g


def _full_spec(shape):
    nd = len(shape)
    return pl.BlockSpec(shape, lambda *_: (0,) * nd, pipeline_mode=pl.Buffered(1))


def _params(sem):
    return pltpu.CompilerParams(dimension_semantics=sem, vmem_limit_bytes=VMEM_LIMIT)


def _ffn_kernel(*refs, n_mix):
    x_ref = refs[0]
    mix = refs[1:1 + n_mix]
    wmix_ref = refs[1 + n_mix] if n_mix else None
    g_ref, wg_ref, wu_ref, wd_ref, o_ref, h_ref, act_ref = refs[1 + n_mix + bool(n_mix):]
    x = x_ref[...]
    r0 = 0
    for a_ref in mix:
        x = x + _dot(a_ref[...], wmix_ref[r0:r0 + a_ref.shape[1]])
        r0 += a_ref.shape[1]
    h_ref[...] = _rms(x, g_ref[...]).astype(BF16)
    for c in range(wg_ref.shape[1] // FF_CHUNK):
        cols = slice(c * FF_CHUNK, (c + 1) * FF_CHUNK)
        gate = _dot(h_ref[...], wg_ref[:, cols])
        up = _dot(h_ref[...], wu_ref[:, cols])
        act_ref[:, cols] = (gate * jax.nn.sigmoid(gate) * up).astype(BF16)
    o_ref[...] = x + 0.5 * _dot(act_ref[...], wd_ref[...])


def _ffn_call(x, mix, w_mix, g, ffn_w, layer, slot, tm):
    wg, wu, wd = ffn_w
    stack_spec = lambda w: pl.BlockSpec((None, None) + w.shape[2:], lambda i: (layer, slot, 0, 0),
                                        pipeline_mode=pl.Buffered(1))
    t, d = x.shape
    row = lambda w: pl.BlockSpec((tm, w), lambda i: (i, 0))
    in_specs = [row(d)]
    args = [x]
    for a in mix:
        in_specs.append(row(a.shape[1]))
        args.append(a)
    if mix:
        in_specs.append(_full_spec(w_mix.shape))
        args.append(w_mix)
    in_specs += [_full_spec(g.shape), stack_spec(wg), stack_spec(wu), stack_spec(wd)]
    args += [g, wg, wu, wd]
    return pl.pallas_call(
        functools.partial(_ffn_kernel, n_mix=len(mix)),
        grid=(t // tm,),
        in_specs=in_specs,
        out_specs=row(d),
        out_shape=jax.ShapeDtypeStruct((t, d), F32),
        scratch_shapes=[pltpu.VMEM((tm, d), BF16), pltpu.VMEM((tm, wg.shape[-1]), BF16)],
        compiler_params=_params(("parallel",)),
        name="ffn_mix%d" % len(mix),
    )(*args)


def _rope128(x, cos2, sin2):
    parts = []
    for hd in range(x.shape[1] // LANES):
        sl = x[:, hd * LANES:(hd + 1) * LANES]
        parts.append(sl * cos2 + pltpu.roll(sl, LANES // 2, axis=1) * sin2)
    return jnp.concatenate(parts, axis=1)


def _even_proj_kernel(x_ref, g_ref, w_ref, cos_ref, sin_ref,
                      qa_ref, ka_ref, kab_ref, va_ref, vab_ref, qr_ref, kr_ref, vr_ref, gt_ref):
    h = _rms(x_ref[...], g_ref[...]).astype(BF16)
    proj = lambda s: _dot(h, w_ref[:, EVEN_CUTS[s]:EVEN_CUTS[s + 1]])
    qa_ref[...] = (proj(0) * (LOG2E * SB_HEAD_DIM ** -0.5)).astype(BF16)
    ka = proj(1)
    ka_ref[...] = ka.reshape(ka_ref.shape)
    kab_ref[...] = ka.astype(BF16)
    va = proj(2)
    va_ref[...] = va.reshape(va_ref.shape)
    vab_ref[...] = va.astype(BF16)
    cos2 = cos_ref[...]
    sin2 = sin_ref[...]
    qr_ref[...] = _rope128(proj(3), cos2, sin2).astype(BF16)
    kr_ref[...] = (_rope128(proj(4), cos2, sin2) * (RET_QK_DIM ** -0.5)).astype(BF16)
    vr_ref[...] = proj(5).astype(BF16)
    gate = proj(6)
    gt_ref[...] = gate * jax.nn.sigmoid(gate)


def _even_proj_call(x, g, w_in, cos2, sin2, tm):
    t, d = x.shape
    n_tab = cos2.shape[0] // tm
    row = lambda w: pl.BlockSpec((tm, w), lambda i: (i, 0))
    tab = pl.BlockSpec((tm, LANES), lambda i: (i % n_tab, 0))
    per_head = (SB_HEADS, SB_HEAD_DIM)
    outs = [((SB_W,), BF16), (per_head, F32), ((SB_W,), BF16), (per_head, F32), ((SB_W,), BF16),
            ((RET_QK_W,), BF16), ((RET_QK_W,), BF16), ((RET_V_W,), BF16), ((RET_V_W,), F32)]
    return pl.pallas_call(
        _even_proj_kernel,
        grid=(t // tm,),
        in_specs=[row(d), _full_spec(g.shape), _full_spec(w_in.shape), tab, tab],
        out_specs=[pl.BlockSpec((tm,) + tail, lambda i, nd=len(tail): (i,) + (0,) * nd) for tail, _ in outs],
        out_shape=[jax.ShapeDtypeStruct((t,) + tail, dt) for tail, dt in outs],
        compiler_params=_params(("parallel",)),
        name="even_proj",
    )(x, g, w_in, cos2, sin2)


def _sb_kernel(q_ref, k_ref, v_ref, negu_ref, o_ref, qh_ref, c_ref, acc_ref, *, qb, kb, q_pos0):
    i = pl.program_id(2)
    n_clear = (q_pos0 + i * qb) // kb
    sq = min(qb, kb)
    nsub = qb // sq
    rows = lambda s: slice(s * sq, (s + 1) * sq)
    kstart = lambda j: pl.multiple_of(j * kb, kb)
    q = q_ref[...]
    first = lax.broadcasted_iota(jnp.int32, (qb, LANES), 1) < SB_HEAD_DIM
    qh_ref[0] = jnp.where(first, q, jnp.zeros_like(q))
    qh_ref[1] = jnp.where(first, jnp.zeros_like(q), q)

    def scores(hd, rws, start):
        return _dot_nt(qh_ref[hd, rws], k_ref[pl.ds(start, kb), :])

    def softplus2(z):
        return jnp.where(z > SOFTPLUS2_LINEAR, z, jnp.log2(1.0 + jnp.exp2(z)))

    def v_heads(start):
        vs = v_ref[pl.ds(start, kb), :]
        vfirst = lax.broadcasted_iota(jnp.int32, (kb, LANES), 1) < SB_HEAD_DIM
        return jnp.concatenate([jnp.where(vfirst, vs, jnp.zeros_like(vs)),
                                jnp.where(vfirst, jnp.zeros_like(vs), vs)], axis=0)

    depth = SB_STATIC_BLOCKS
    row = lax.broadcasted_iota(jnp.int32, (qb, kb), 0)
    below = lax.broadcasted_iota(jnp.int32, (qb, kb), 1) < (row & (sq - 1))
    kblock = lambda s, d: kstart(jnp.maximum(n_clear + s - d, 0))

    def apply_masks(t):
        parts = [jnp.where(below, t[:qb], 0.0)]
        for d in range(1, depth):
            for s in range(min(d, nsub)):
                exists = jnp.broadcast_to(n_clear + s - d >= 0, (sq, kb))
                parts.append(jnp.where(exists, t[d * qb + s * sq:d * qb + (s + 1) * sq], 0.0))
            if nsub > d:
                parts.append(t[d * qb + d * sq:(d + 1) * qb])
        return jnp.concatenate(parts, axis=0)

    ws = []
    for hd in range(2):
        z = jnp.concatenate([scores(hd, rows(s), kblock(s, d)) for d in range(depth) for s in range(nsub)], axis=0)
        sp = apply_masks(softplus2(z))
        tail = _dot(sp.astype(BF16), negu_ref[...])
        seen = [jnp.zeros((qb, 1), F32)]
        for d in range(depth):
            seen.append(seen[-1] + tail[d * qb:(d + 1) * qb, :1])
        c_ref[hd] = seen[depth]
        ws.append(apply_masks(jnp.exp2(z + tail + jnp.concatenate(seen[:depth], axis=0))).astype(BF16))
    w = jnp.concatenate(ws, axis=1)
    for s in range(nsub):
        acc = None
        for d in range(depth):
            part = _dot(w[d * qb + s * sq:d * qb + (s + 1) * sq], v_heads(kblock(s, d)))
            acc = part if acc is None else acc + part
        acc_ref[rows(s)] = acc

    def block(start, r0, r1):
        ws = []
        for hd in range(2):
            z = scores(hd, slice(r0, r1), start)
            tail = _dot(softplus2(z).astype(BF16), negu_ref[...])
            ws.append(jnp.exp2(z + tail + c_ref[hd, r0:r1]).astype(BF16))
            c_ref[hd, r0:r1] += tail[:, :1]
        acc_ref[r0:r1] += _dot(jnp.concatenate(ws, axis=1), v_heads(start))

    def alive(r0, r1):
        return (jnp.max(c_ref[:, r0:r1]) > SB_DEAD_LOG2).astype(jnp.int32)

    def run_blocks(j_first, j_last, r0, r1):
        def cond(carry):
            j, live = carry
            return jnp.logical_and(j >= j_last, live > 0)

        def body(carry):
            j, _ = carry
            block(kstart(j), r0, r1)
            return j - 1, alive(r0, r1)

        lax.while_loop(cond, body, (j_first, alive(r0, r1)))

    @pl.when(alive(0, qb) > 0)
    def _():
        for s in range(1, nsub):
            run_blocks(n_clear + s - depth, jnp.maximum(n_clear - depth + 1, 0), s * sq, (s + 1) * sq)
        run_blocks(n_clear - depth, 0, 0, qb)

    o_ref[...] = acc_ref[...].astype(o_ref.dtype)


def _sb_call(q, k, v, negu, qb, q_pos0):
    b, nq, w = q.shape
    nk = k.shape[1]
    kb = negu.shape[0]
    assert q_pos0 % kb == 0 and (qb % kb == 0 or nq == qb < kb) and nk % kb == 0 and nk >= q_pos0 + nq
    return pl.pallas_call(
        functools.partial(_sb_kernel, qb=qb, kb=kb, q_pos0=q_pos0),
        grid=(b, w // LANES, nq // qb),
        in_specs=[pl.BlockSpec((None, qb, LANES), lambda bi, p, i: (bi, i, p)),
                  pl.BlockSpec((None, nk, LANES), lambda bi, p, i: (bi, 0, p)),
                  pl.BlockSpec((None, nk, LANES), lambda bi, p, i: (bi, 0, p)),
                  _full_spec(negu.shape)],
        out_specs=pl.BlockSpec((None, qb, LANES), lambda bi, p, i: (bi, i, p)),
        out_shape=jax.ShapeDtypeStruct((b, nq, w), BF16),
        scratch_shapes=[pltpu.VMEM((2, qb, LANES), BF16), pltpu.VMEM((2, qb, 1), F32),
                        pltpu.VMEM((qb, LANES), F32)],
        compiler_params=_params(("parallel", "parallel", "parallel")),
        name="stick_breaking",
    )(q, k, v, negu)


def _ret_kernel(sdec_ref, q_ref, k_ref, v_ref, gt_ref, s0_ref, dec_ref, qdec_ref, kdec_ref,
                o_ref, sfin_ref, st_ref):
    c = pl.program_id(0)

    @pl.when(c == 0)
    def _():
        st_ref[...] = s0_ref[...]

    units = [(b, h) for b in range(q_ref.shape[0]) for h in range(RET_HEADS)]
    qk = lambda h: slice(h * RET_QK_DIM, (h + 1) * RET_QK_DIM)
    vw = lambda h: slice(h * RET_V_DIM, (h + 1) * RET_V_DIM)
    cs = dec_ref.shape[1]
    for part in range(q_ref.shape[1] // cs):
        r = slice(part * cs, (part + 1) * cs)
        scores = [_dot_nt(q_ref[b, r, qk(h)], k_ref[b, r, qk(h)]) for b, h in units]
        cross = [_dot(q_ref[b, r, qk(h)], st_ref[b, h].astype(BF16)) for b, h in units]
        kw = [(k_ref[b, r, qk(h)].astype(F32) * kdec_ref[h]).astype(BF16) for b, h in units]
        grow = [_dot_tn(kw[u], v_ref[b, r, vw(h)]) for u, (b, h) in enumerate(units)]
        decayed = [(scores[u] * dec_ref[h]).astype(BF16) for u, (b, h) in enumerate(units)]
        inner = [_dot(decayed[u], v_ref[b, r, vw(h)]) for u, (b, h) in enumerate(units)]
        for u, (b, h) in enumerate(units):
            st_ref[b, h] = sdec_ref[h] * st_ref[b, h] + grow[u]
            o = inner[u] + cross[u] * qdec_ref[h]
            o = o * lax.rsqrt(jnp.mean(o * o, axis=-1, keepdims=True) + RMS_EPS)
            o_ref[b, r, vw(h)] = (o * gt_ref[b, r, vw(h)]).astype(o_ref.dtype)

    @pl.when(c == pl.num_programs(0) - 1)
    def _():
        sfin_ref[...] = st_ref[...]


def _ret_call(q, k, v, gate, state0, cs, per):
    b, n, _ = q.shape
    log_gamma = jnp.log1p(-jnp.exp2(-5.0 - jnp.arange(RET_HEADS, dtype=F32)))
    pos = jnp.arange(cs, dtype=F32)
    diff = pos[:, None] - pos[None, :]
    dec = jnp.where(diff >= 0, jnp.exp(log_gamma[:, None, None] * jnp.maximum(diff, 0.0)), 0.0)
    qdec = jnp.exp(log_gamma[:, None] * (pos + 1.0))[..., None]
    kdec = jnp.exp(log_gamma[:, None] * (cs - 1.0 - pos))[..., None]
    sdec = jnp.exp(log_gamma * cs)
    seq = lambda w: pl.BlockSpec((b, cs * per, w), lambda c: (0, c, 0))
    st = pl.BlockSpec((b, RET_HEADS, RET_QK_DIM, RET_V_DIM), lambda c: (0, 0, 0, 0))
    return pl.pallas_call(
        _ret_kernel,
        grid=(n // (cs * per),),
        in_specs=[pl.BlockSpec(memory_space=pltpu.SMEM), seq(RET_QK_W), seq(RET_QK_W), seq(RET_V_W),
                  seq(RET_V_W), st, _full_spec(dec.shape), _full_spec(qdec.shape), _full_spec(kdec.shape)],
        out_specs=[seq(RET_V_W), st],
        out_shape=[jax.ShapeDtypeStruct((b, n, RET_V_W), BF16),
                   jax.ShapeDtypeStruct((b, RET_HEADS, RET_QK_DIM, RET_V_DIM), F32)],
        scratch_shapes=[pltpu.VMEM((b, RET_HEADS, RET_QK_DIM, RET_V_DIM), F32)],
        compiler_params=_params(("arbitrary",)),
        name="retention",
    )(sdec, q, k, v, gate, state0, dec, qdec, kdec)


def _head_norm_rope(t, bd, gain, cos2, sin2):
    wide = bd.shape[0]
    parts = []
    for c in range(t.shape[1] // wide):
        sq = t[:, c * wide:(c + 1) * wide]
        sq = sq * sq
        hi = sq.astype(BF16)
        lo = (sq - hi.astype(F32)).astype(BF16)
        parts.append(_dot(hi, bd) + _dot(lo, bd))
    ms = jnp.concatenate(parts, axis=1) * (1.0 / SWA_HEAD_DIM)
    y = t * lax.rsqrt(ms + RMS_EPS) * gain
    w = t.shape[1]
    reps = w // LANES
    quarter = SWA_HEAD_DIM // 2
    first_half = (lax.broadcasted_iota(jnp.int32, t.shape, 1) % SWA_HEAD_DIM) < quarter
    partner = jnp.where(first_half, pltpu.roll(y, w - quarter, axis=1), pltpu.roll(y, quarter, axis=1))
    return y * jnp.concatenate([cos2] * reps, axis=1) + partner * jnp.concatenate([sin2] * reps, axis=1)


def _odd_proj_kernel(x_ref, g_ref, w_ref, bd_ref, gain_ref, cos_ref, sin_ref, q_ref, k_ref, v_ref):
    h = _rms(x_ref[...], g_ref[...]).astype(BF16)
    qk_w = SWA_Q_W + SWA_KV_W
    qk = _dot(h, w_ref[:, :qk_w])
    v_ref[...] = _dot(h, w_ref[:, qk_w:])
    qk = _head_norm_rope(qk, bd_ref[...], gain_ref[...], cos_ref[...], sin_ref[...])
    q_ref[...] = (qk[:, :SWA_Q_W] * (LOG2E * SWA_HEAD_DIM ** -0.5)).astype(BF16)
    k_ref[...] = qk[:, SWA_Q_W:]


def _odd_proj_call(x, g, w_in, gain, cos2, sin2, tm):
    t, d = x.shape
    n_tab = cos2.shape[0] // tm
    head = jnp.arange(MXU_TILE) // SWA_HEAD_DIM
    bd = (head[:, None] == head[None, :]).astype(BF16)
    row = lambda w: pl.BlockSpec((tm, w), lambda i: (i, 0))
    tab = pl.BlockSpec((tm, LANES), lambda i: (i % n_tab, 0))
    ins = [g, w_in, bd, gain]
    return pl.pallas_call(
        _odd_proj_kernel,
        grid=(t // tm,),
        in_specs=[row(d)] + [_full_spec(a.shape) for a in ins] + [tab, tab],
        out_specs=[row(SWA_Q_W), row(SWA_KV_W), row(SWA_KV_W)],
        out_shape=[jax.ShapeDtypeStruct((t, SWA_Q_W), BF16), jax.ShapeDtypeStruct((t, SWA_KV_W), F32),
                   jax.ShapeDtypeStruct((t, SWA_KV_W), F32)],
        compiler_params=_params(("parallel",)),
        name="odd_proj",
    )(x, *ins, cos2, sin2)


def _swa_kernel(sink_ref, q_ref, kp_ref, kc_ref, vp_ref, vc_ref, o_ref, *, rows, first_prev_valid):
    i = pl.program_id(1)
    nch = rows // CHUNK
    wk = WINDOW + CHUNK
    nk = WINDOW + rows
    grp = SWA_HEADS // SWA_KV_HEADS
    gw = grp * CHUNK
    key_low = lax.broadcasted_iota(jnp.int32, (nk, LANES), 1) < SWA_HEAD_DIM
    low = lax.broadcasted_iota(jnp.int32, (CHUNK, LANES), 1) < SWA_HEAD_DIM
    if not first_prev_valid:
        krow = lax.broadcasted_iota(jnp.int32, (nk, LANES), 0)
        klane = lax.broadcasted_iota(jnp.int32, (nk, LANES), 1)
        k_bias = jnp.where((krow < WINDOW) & (klane == 0) & (i == 0), SWA_MASKED, 0.0).astype(BF16)
        q_one = jnp.where(lax.broadcasted_iota(jnp.int32, (gw, LANES), 1) == 0, 1.0, 0.0).astype(BF16)

    s, vj_t = [], []
    for j in range(SWA_KV_HEADS):
        pair, odd = divmod(j, 2)
        lanes = slice(pair * LANES, (pair + 1) * LANES)
        k_pair = jnp.concatenate([kp_ref[:, lanes], kc_ref[:, lanes]], axis=0)
        own = jnp.where(key_low, 0.0, k_pair) if odd else jnp.where(key_low, k_pair, 0.0)
        k_both = (own + pltpu.roll(own, SWA_HEAD_DIM, axis=1)).astype(BF16)
        v_t = jnp.concatenate([vp_ref[:, lanes], vc_ref[:, lanes]], axis=0).T
        vj_t.append(v_t[odd * SWA_HEAD_DIM:(odd + 1) * SWA_HEAD_DIM].astype(BF16))
        if not first_prev_valid:
            k_both = jnp.concatenate([k_both, k_bias], axis=1)
        for c in range(nch):
            q_four = []
            for qp in (2 * j, 2 * j + 1):
                q_pair = q_ref[c * CHUNK:(c + 1) * CHUNK, qp * LANES:(qp + 1) * LANES]
                zero = jnp.zeros_like(q_pair)
                q_four += [jnp.where(low, q_pair, zero), jnp.where(low, zero, q_pair)]
            q_four = jnp.concatenate(q_four, axis=0)
            if not first_prev_valid:
                q_four = jnp.concatenate([q_four, q_one], axis=1)
            s.append(_dot_nt(k_both[c * CHUNK:c * CHUNK + wk], q_four))
    s = jnp.concatenate(s, axis=1)
    sink = jnp.concatenate([jnp.full((1, CHUNK), sink_ref[j * grp + g] * LOG2E, F32)
                            for j in range(SWA_KV_HEADS) for _ in range(nch) for g in range(grp)], axis=1)
    m = jnp.maximum(jnp.max(s, axis=0, keepdims=True), sink)
    p = jnp.exp2(s - m)
    inv = 1.0 / (jnp.sum(p, axis=0, keepdims=True) + jnp.exp2(sink - m))
    p = p.astype(BF16)
    o_t = []
    for j in range(SWA_KV_HEADS):
        for c in range(nch):
            cols = slice((j * nch + c) * gw, (j * nch + c + 1) * gw)
            o_t.append(_dot(vj_t[j][:, c * CHUNK:c * CHUNK + wk], p[:, cols]) * inv[:, cols])
    for c0 in range(0, nch, 2):
        pieces = []
        for j in range(SWA_KV_HEADS):
            for t in range(grp // 2):
                tiles = [o_t[j * nch + c][:, t * LANES:(t + 1) * LANES] for c in range(c0, min(c0 + 2, nch))]
                if len(tiles) == 2:
                    pieces += [jnp.where(low, tiles[0], pltpu.roll(tiles[1], CHUNK, axis=1)),
                               jnp.where(low, pltpu.roll(tiles[0], CHUNK, axis=1), tiles[1])]
                else:
                    pieces += [tiles[0], pltpu.roll(tiles[0], CHUNK, axis=1)]
        out = jnp.concatenate(pieces, axis=0).T
        r1 = min((c0 + 2) * CHUNK, rows)
        o_ref[c0 * CHUNK:r1] = out[:r1 - c0 * CHUNK].astype(o_ref.dtype)


def _swa_call(q, k_prev, k_cur, v_prev, v_cur, sinks, rows, same_array):
    b, n, _ = q.shape
    per = rows // WINDOW if same_array else 0
    cur = lambda w: pl.BlockSpec((None, rows, w), lambda bi, i: (bi, i, 0))
    prev = pl.BlockSpec((None, WINDOW, SWA_KV_W), lambda bi, i: (bi, jnp.maximum(i * per - 1, 0), 0))
    return pl.pallas_call(
        functools.partial(_swa_kernel, rows=rows, first_prev_valid=not same_array),
        grid=(b, n // rows),
        in_specs=[pl.BlockSpec(memory_space=pltpu.SMEM), cur(SWA_Q_W), prev, cur(SWA_KV_W), prev, cur(SWA_KV_W)],
        out_specs=cur(SWA_Q_W),
        out_shape=jax.ShapeDtypeStruct((b, n, SWA_Q_W), BF16),
        compiler_params=_params(("parallel", "parallel")),
        name="swa",
    )(sinks, q, k_prev, k_cur, v_prev, v_cur)


def _rope_tables(pos0, n, head_dim):
    half = head_dim // 2
    lane = jnp.arange(LANES)
    inv = jnp.power(ROPE_THETA, -(lane % half).astype(F32) / half)
    sign = jnp.where(lane % head_dim < half, -1.0, 1.0)
    fine = min(n, LANES)
    ang_c = (pos0 + fine * jnp.arange(n // fine)).astype(F32)[:, None] * inv[None, :]
    ang_f = jnp.arange(fine, dtype=F32)[:, None] * inv[None, :]
    cos_c, sin_c = jnp.cos(ang_c)[:, None, :], jnp.sin(ang_c)[:, None, :]
    cos_f, sin_f = jnp.cos(ang_f)[None, :, :], jnp.sin(ang_f)[None, :, :]
    return ((cos_c * cos_f - sin_c * sin_f).reshape(n, LANES),
            ((sin_c * cos_f + cos_c * sin_f) * sign).reshape(n, LANES))


def _prep_weights(norm_g, ffn_w_gate, ffn_w_up, ffn_w_down, even_w_in, even_w_out, odd_w_in, odd_w_out,
                  odd_q_norm, odd_k_norm):
    depth = norm_g.shape[0]
    prm = {"g": norm_g[:, :, None, :], "even": [], "odd": [], "depth": depth,
           "ffn": (ffn_w_gate.astype(BF16), ffn_w_up.astype(BF16), ffn_w_down.astype(BF16))}
    for i in range(even_w_in.shape[0]):
        prm["even"].append((even_w_in[i].astype(BF16), even_w_out[i].astype(BF16)))
    for i in range(odd_w_in.shape[0]):
        gain = jnp.concatenate([jnp.tile(odd_q_norm[i], SWA_HEADS), jnp.tile(odd_k_norm[i], SWA_KV_HEADS)])
        prm["odd"].append((odd_w_in[i].astype(BF16), odd_w_out[i].astype(BF16), gain[None, :]))
    return prm


def _trunk(x, pos0, prm, sinks, caches, tm, tm_ffn, tm_odd, sb_qb, sb_kb, ret_cs, ret_per, swa_rows):
    b, n, d = x.shape
    t = b * n
    xf = x.reshape(t, d)
    depth = prm["depth"]
    tab_rows = n if n % tm == 0 else t
    tile_tab = lambda tb: tb if tab_rows == n else jnp.tile(tb, (b, 1))
    cos_r, sin_r = [tile_tab(tb) for tb in _rope_tables(pos0, n, RET_QK_DIM)]
    cos_s, sin_s = [tile_tab(tb) for tb in _rope_tables(pos0, n, SWA_HEAD_DIM)]
    idx = jnp.arange(sb_kb)
    negu = jnp.where(idx[:, None] >= idx[None, :], -1.0, 0.0).astype(BF16)
    sb_k, sb_v, ret, swa_k, swa_v = [], [], [], [], []
    mix, w_mix = [], None
    for layer in range(depth):
        i = layer // 2
        g = prm["g"][layer]
        xf = _ffn_call(xf, mix, w_mix, g[0], prm["ffn"], layer, 0, tm_ffn)
        if layer % 2 == 0:
            w_in, w_mix = prm["even"][i]
            qa, ka, kab, va, vab, qr, kr, vr, gt = _even_proj_call(xf, g[1], w_in, cos_r, sin_r, tm)
            sb_k.append(ka.reshape(b, n, SB_HEADS, SB_HEAD_DIM))
            sb_v.append(va.reshape(b, n, SB_HEADS, SB_HEAD_DIM))
            r3 = lambda a: a.reshape(b, n, a.shape[-1])
            if caches is None:
                k_all, v_all, q_pos0 = r3(kab), r3(vab), 0
                state0 = jnp.zeros((b, RET_HEADS, RET_QK_DIM, RET_V_DIM), F32)
            else:
                past = caches["sb_k"].shape[2]
                padded = -(-(past + n) // sb_kb) * sb_kb
                cat = lambda cache, new: jnp.pad(
                    jnp.concatenate([cache.reshape(b, past, SB_W).astype(BF16), r3(new)], axis=1),
                    ((0, 0), (0, padded - past - n), (0, 0)))
                k_all, v_all, q_pos0 = cat(caches["sb_k"][i], kab), cat(caches["sb_v"][i], vab), past
                state0 = caches["ret"][i]
            o_sb = _sb_call(r3(qa), k_all, v_all, negu, sb_qb, q_pos0)
            o_r, st = _ret_call(r3(qr), r3(kr), r3(vr), r3(gt), state0, ret_cs, ret_per)
            ret.append(st)
            mix = [o_sb.reshape(t, SB_W), o_r.reshape(t, RET_V_W)]
        else:
            w_in, w_mix, gain = prm["odd"][i]
            q, k, v = _odd_proj_call(xf, g[1], w_in, gain, cos_s, sin_s, tm_odd)
            r3 = lambda a: a.reshape(b, n, a.shape[-1])
            k3, v3 = r3(k), r3(v)
            if caches is None:
                o = _swa_call(r3(q), k3, k3, v3, v3, sinks[i], swa_rows, True)
                keep = min(WINDOW, n)
                k_rows, v_rows = k3[:, n - keep:], v3[:, n - keep:]
            else:
                kc = caches["swa_k"][i].reshape(b, -1, SWA_KV_W)
                vc = caches["swa_v"][i].reshape(b, -1, SWA_KV_W)
                o = _swa_call(r3(q), kc, k3, vc, v3, sinks[i], swa_rows, False)
                k_rows, v_rows = k3, v3
            swa_k.append(k_rows.reshape(b, -1, SWA_KV_HEADS, SWA_HEAD_DIM))
            swa_v.append(v_rows.reshape(b, -1, SWA_KV_HEADS, SWA_HEAD_DIM))
            mix = [o.reshape(t, SWA_Q_W)]
        xf = _ffn_call(xf, mix, w_mix, g[2], prm["ffn"], layer, 1, tm_ffn)
        mix, w_mix = [], None
    stack = lambda parts: parts[0][None] if len(parts) == 1 else jnp.stack(parts)
    return (xf.reshape(b, n, d), stack(sb_k), stack(sb_v), stack(ret), stack(swa_k), stack(swa_v))


def kernel(x_prompt, x_sample, cache_sb_k, cache_sb_v, state_ret, cache_swa_k, cache_swa_v, norm_g, ffn_w_gate,
           ffn_w_up, ffn_w_down, even_w_in, even_w_out, odd_w_in, odd_w_out, odd_q_norm, odd_k_norm, odd_sinks):
    prm = _prep_weights(norm_g, ffn_w_gate, ffn_w_up, ffn_w_down, even_w_in, even_w_out, odd_w_in, odd_w_out,
                        odd_q_norm, odd_k_norm)
    past = cache_sb_k.shape[2]
    n_s = x_sample.shape[1]
    y_p, sb_k_p, sb_v_p, ret_p, swa_k_p, swa_v_p = _trunk(
        x_prompt, 0, prm, odd_sinks, None, tm=512, tm_ffn=1024, tm_odd=1024, sb_qb=4096, sb_kb=128, ret_cs=256, ret_per=2, swa_rows=1024)
    caches = {"sb_k": cache_sb_k, "sb_v": cache_sb_v, "ret": state_ret, "swa_k": cache_swa_k, "swa_v": cache_swa_v}
    y_s, sb_k_s, sb_v_s, ret_s, swa_k_s, swa_v_s = _trunk(
        x_sample, past, prm, odd_sinks, caches, tm=x_sample.shape[0] * n_s, tm_ffn=x_sample.shape[0] * n_s, tm_odd=x_sample.shape[0] * n_s, sb_qb=n_s, sb_kb=128,
        ret_cs=n_s, ret_per=1, swa_rows=n_s)
    return (y_p, y_s, sb_k_p, sb_v_p, ret_p, swa_k_p, swa_v_p, sb_k_s, sb_v_s, ret_s, swa_k_s, swa_v_s)
```

```python
import functools
import math

import jax
import jax.numpy as jnp
from jax import lax
from jax.experimental import pallas as pl
from jax.experimental.pallas import tpu as pltpu

F32 = jnp.float32
BF16 = jnp.bfloat16

RMS_EPS = 1e-6
ROPE_THETA = 10000.0
CHUNK = 64
SB_HEADS = 8
SB_HEAD_DIM = 64
RET_HEADS = 4
RET_QK_DIM = 128
RET_V_DIM = 256
SWA_HEADS = 16
SWA_KV_HEADS = 4
SWA_HEAD_DIM = 64
WINDOW = 128

SB_W = SB_HEADS * SB_HEAD_DIM
RET_QK_W = RET_HEADS * RET_QK_DIM
RET_V_W = RET_HEADS * RET_V_DIM
SWA_Q_W = SWA_HEADS * SWA_HEAD_DIM
SWA_KV_W = SWA_KV_HEADS * SWA_HEAD_DIM
EVEN_CUTS = (0, SB_W, 2 * SB_W, 3 * SB_W, 3 * SB_W + RET_QK_W, 3 * SB_W + 2 * RET_QK_W,
             3 * SB_W + 2 * RET_QK_W + RET_V_W, 3 * SB_W + 2 * RET_QK_W + 2 * RET_V_W)

LANES = 128
MXU_TILE = 256
FF_CHUNK = MXU_TILE
VMEM_LIMIT = 56 * 1024 * 1024
LOG2E = math.log2(math.e)
SOFTPLUS2_LINEAR = 40.0
SWA_MASKED = -1e30
SB_STATIC_BLOCKS = 3
SB_DEAD_LOG2 = -170.0


def _dot(a, b):
    return jnp.dot(a, b, preferred_element_type=F32)


def _dot_nt(a, b):
    return lax.dot_general(a, b, (((1,), (1,)), ((), ())), preferred_element_type=F32)


def _dot_tn(a, b):
    return lax.dot_general(a, b, (((0,), (0,)), ((), ())), preferred_element_type=F32)


def _rms(x, g):
    return x * lax.rsqrt(jnp.mean(x * x, axis=-1, keepdims=True) + RMS_EPS) * g


def _full_spec(shape):
    nd = len(shape)
    return pl.BlockSpec(shape, lambda *_: (0,) * nd, pipeline_mode=pl.Buffered(1))


def _params(sem):
    return pltpu.CompilerParams(dimension_semantics=sem, vmem_limit_bytes=VMEM_LIMIT)


def _ffn_kernel(*refs, n_mix):
    x_ref = refs[0]
    mix = refs[1:1 + n_mix]
    wmix_ref = refs[1 + n_mix] if n_mix else None
    g_ref, wg_ref, wu_ref, wd_ref, o_ref, h_ref, act_ref = refs[1 + n_mix + bool(n_mix):]
    x = x_ref[...]
    r0 = 0
    for a_ref in mix:
        x = x + _dot(a_ref[...], wmix_ref[r0:r0 + a_ref.shape[1]])
        r0 += a_ref.shape[1]
    h_ref[...] = _rms(x, g_ref[...]).astype(BF16)
    for c in range(wg_ref.shape[1] // FF_CHUNK):
        cols = slice(c * FF_CHUNK, (c + 1) * FF_CHUNK)
        gate = _dot(h_ref[...], wg_ref[:, cols])
        up = _dot(h_ref[...], wu_ref[:, cols])
        act_ref[:, cols] = (gate * jax.nn.sigmoid(gate) * up).astype(BF16)
    o_ref[...] = x + 0.5 * _dot(act_ref[...], wd_ref[...])


def _ffn_call(x, mix, w_mix, g, ffn_w, layer, slot, tm):
    wg, wu, wd = ffn_w
    stack_spec = lambda w: pl.BlockSpec((None, None) + w.shape[2:], lambda i: (layer, slot, 0, 0),
                                        pipeline_mode=pl.Buffered(1))
    t, d = x.shape
    row = lambda w: pl.BlockSpec((tm, w), lambda i: (i, 0))
    in_specs = [row(d)]
    args = [x]
    for a in mix:
        in_specs.append(row(a.shape[1]))
        args.append(a)
    if mix:
        in_specs.append(_full_spec(w_mix.shape))
        args.append(w_mix)
    in_specs += [_full_spec(g.shape), stack_spec(wg), stack_spec(wu), stack_spec(wd)]
    args += [g, wg, wu, wd]
    return pl.pallas_call(
        functools.partial(_ffn_kernel, n_mix=len(mix)),
        grid=(t // tm,),
        in_specs=in_specs,
        out_specs=row(d),
        out_shape=jax.ShapeDtypeStruct((t, d), F32),
        scratch_shapes=[pltpu.VMEM((tm, d), BF16), pltpu.VMEM((tm, wg.shape[-1]), BF16)],
        compiler_params=_params(("parallel",)),
        name="ffn_mix%d" % len(mix),
    )(*args)


def _rope128(x, cos2, sin2):
    parts = []
    for hd in range(x.shape[1] // LANES):
        sl = x[:, hd * LANES:(hd + 1) * LANES]
        parts.append(sl * cos2 + pltpu.roll(sl, LANES // 2, axis=1) * sin2)
    return jnp.concatenate(parts, axis=1)


def _even_proj_kernel(x_ref, g_ref, w_ref, cos_ref, sin_ref,
                      qa_ref, ka_ref, kab_ref, va_ref, vab_ref, qr_ref, kr_ref, vr_ref, gt_ref):
    h = _rms(x_ref[...], g_ref[...]).astype(BF16)
    proj = lambda s: _dot(h, w_ref[:, EVEN_CUTS[s]:EVEN_CUTS[s + 1]])
    qa_ref[...] = (proj(0) * (LOG2E * SB_HEAD_DIM ** -0.5)).astype(BF16)
    ka = proj(1)
    ka_ref[...] = ka.reshape(ka_ref.shape)
    kab_ref[...] = ka.astype(BF16)
    va = proj(2)
    va_ref[...] = va.reshape(va_ref.shape)
    vab_ref[...] = va.astype(BF16)
    cos2 = cos_ref[...]
    sin2 = sin_ref[...]
    qr_ref[...] = _rope128(proj(3), cos2, sin2).astype(BF16)
    kr_ref[...] = (_rope128(proj(4), cos2, sin2) * (RET_QK_DIM ** -0.5)).astype(BF16)
    vr_ref[...] = proj(5).astype(BF16)
    gate = proj(6)
    gt_ref[...] = gate * jax.nn.sigmoid(gate)


def _even_proj_call(x, g, w_in, cos2, sin2, tm):
    t, d = x.shape
    n_tab = cos2.shape[0] // tm
    row = lambda w: pl.BlockSpec((tm, w), lambda i: (i, 0))
    tab = pl.BlockSpec((tm, LANES), lambda i: (i % n_tab, 0))
    per_head = (SB_HEADS, SB_HEAD_DIM)
    outs = [((SB_W,), BF16), (per_head, F32), ((SB_W,), BF16), (per_head, F32), ((SB_W,), BF16),
            ((RET_QK_W,), BF16), ((RET_QK_W,), BF16), ((RET_V_W,), BF16), ((RET_V_W,), F32)]
    return pl.pallas_call(
        _even_proj_kernel,
        grid=(t // tm,),
        in_specs=[row(d), _full_spec(g.shape), _full_spec(w_in.shape), tab, tab],
        out_specs=[pl.BlockSpec((tm,) + tail, lambda i, nd=len(tail): (i,) + (0,) * nd) for tail, _ in outs],
        out_shape=[jax.ShapeDtypeStruct((t,) + tail, dt) for tail, dt in outs],
        compiler_params=_params(("parallel",)),
        name="even_proj",
    )(x, g, w_in, cos2, sin2)


def _sb_kernel(q_ref, k_ref, v_ref, negu_ref, o_ref, qh_ref, c_ref, acc_ref, *, qb, kb, q_pos0):
    i = pl.program_id(2)
    n_clear = (q_pos0 + i * qb) // kb
    sq = min(qb, kb)
    nsub = qb // sq
    rows = lambda s: slice(s * sq, (s + 1) * sq)
    kstart = lambda j: pl.multiple_of(j * kb, kb)
    q = q_ref[...]
    first = lax.broadcasted_iota(jnp.int32, (qb, LANES), 1) < SB_HEAD_DIM
    qh_ref[0] = jnp.where(first, q, jnp.zeros_like(q))
    qh_ref[1] = jnp.where(first, jnp.zeros_like(q), q)

    def scores(hd, rws, start):
        return _dot_nt(qh_ref[hd, rws], k_ref[pl.ds(start, kb), :])

    def softplus2(z):
        return jnp.where(z > SOFTPLUS2_LINEAR, z, jnp.log2(1.0 + jnp.exp2(z)))

    def v_heads(start):
        vs = v_ref[pl.ds(start, kb), :]
        vfirst = lax.broadcasted_iota(jnp.int32, (kb, LANES), 1) < SB_HEAD_DIM
        return jnp.concatenate([jnp.where(vfirst, vs, jnp.zeros_like(vs)),
                                jnp.where(vfirst, jnp.zeros_like(vs), vs)], axis=0)

    depth = SB_STATIC_BLOCKS
    row = lax.broadcasted_iota(jnp.int32, (qb, kb), 0)
    below = lax.broadcasted_iota(jnp.int32, (qb, kb), 1) < (row & (sq - 1))
    kblock = lambda s, d: kstart(jnp.maximum(n_clear + s - d, 0))

    def apply_masks(t):
        parts = [jnp.where(below, t[:qb], 0.0)]
        for d in range(1, depth):
            for s in range(min(d, nsub)):
                exists = jnp.broadcast_to(n_clear + s - d >= 0, (sq, kb))
                parts.append(jnp.where(exists, t[d * qb + s * sq:d * qb + (s + 1) * sq], 0.0))
            if nsub > d:
                parts.append(t[d * qb + d * sq:(d + 1) * qb])
        return jnp.concatenate(parts, axis=0)

    ws = []
    for hd in range(2):
        z = jnp.concatenate([scores(hd, rows(s), kblock(s, d)) for d in range(depth) for s in range(nsub)], axis=0)
        sp = apply_masks(softplus2(z))
        tail = _dot(sp.astype(BF16), negu_ref[...])
        seen = [jnp.zeros((qb, 1), F32)]
        for d in range(depth):
            seen.append(seen[-1] + tail[d * qb:(d + 1) * qb, :1])
        c_ref[hd] = seen[depth]
        ws.append(apply_masks(jnp.exp2(z + tail + jnp.concatenate(seen[:depth], axis=0))).astype(BF16))
    w = jnp.concatenate(ws, axis=1)
    for s in range(nsub):
        acc = None
        for d in range(depth):
            part = _dot(w[d * qb + s * sq:d * qb + (s + 1) * sq], v_heads(kblock(s, d)))
            acc = part if acc is None else acc + part
        acc_ref[rows(s)] = acc

    def block(start, r0, r1):
        ws = []
        for hd in range(2):
            z = scores(hd, slice(r0, r1), start)
            tail = _dot(softplus2(z).astype(BF16), negu_ref[...])
            ws.append(jnp.exp2(z + tail + c_ref[hd, r0:r1]).astype(BF16))
            c_ref[hd, r0:r1] += tail[:, :1]
        acc_ref[r0:r1] += _dot(jnp.concatenate(ws, axis=1), v_heads(start))

    def alive(r0, r1):
        return (jnp.max(c_ref[:, r0:r1]) > SB_DEAD_LOG2).astype(jnp.int32)

    def run_blocks(j_first, j_last, r0, r1):
        def cond(carry):
            j, live = carry
            return jnp.logical_and(j >= j_last, live > 0)

        def body(carry):
            j, _ = carry
            block(kstart(j), r0, r1)
            return j - 1, alive(r0, r1)

        lax.while_loop(cond, body, (j_first, alive(r0, r1)))

    @pl.when(alive(0, qb) > 0)
    def _():
        for s in range(1, nsub):
            run_blocks(n_clear + s - depth, jnp.maximum(n_clear - depth + 1, 0), s * sq, (s + 1) * sq)
        run_blocks(n_clear - depth, 0, 0, qb)

    o_ref[...] = acc_ref[...].astype(o_ref.dtype)


def _sb_call(q, k, v, negu, qb, q_pos0):
    b, nq, w = q.shape
    nk = k.shape[1]
    kb = negu.shape[0]
    assert q_pos0 % kb == 0 and (qb % kb == 0 or nq == qb < kb) and nk % kb == 0 and nk >= q_pos0 + nq
    return pl.pallas_call(
        functools.partial(_sb_kernel, qb=qb, kb=kb, q_pos0=q_pos0),
        grid=(b, w // LANES, nq // qb),
        in_specs=[pl.BlockSpec((None, qb, LANES), lambda bi, p, i: (bi, i, p)),
                  pl.BlockSpec((None, nk, LANES), lambda bi, p, i: (bi, 0, p)),
                  pl.BlockSpec((None, nk, LANES), lambda bi, p, i: (bi, 0, p)),
                  _full_spec(negu.shape)],
        out_specs=pl.BlockSpec((None, qb, LANES), lambda bi, p, i: (bi, i, p)),
        out_shape=jax.ShapeDtypeStruct((b, nq, w), BF16),
        scratch_shapes=[pltpu.VMEM((2, qb, LANES), BF16), pltpu.VMEM((2, qb, 1), F32),
                        pltpu.VMEM((qb, LANES), F32)],
        compiler_params=_params(("parallel", "parallel", "parallel")),
        name="stick_breaking",
    )(q, k, v, negu)


def _ret_kernel(sdec_ref, q_ref, k_ref, v_ref, gt_ref, s0_ref, dec_ref, qdec_ref, kdec_ref,
                o_ref, sfin_ref, st_ref):
    c = pl.program_id(0)

    @pl.when(c == 0)
    def _():
        st_ref[...] = s0_ref[...]

    units = [(b, h) for b in range(q_ref.shape[0]) for h in range(RET_HEADS)]
    qk = lambda h: slice(h * RET_QK_DIM, (h + 1) * RET_QK_DIM)
    vw = lambda h: slice(h * RET_V_DIM, (h + 1) * RET_V_DIM)
    cs = dec_ref.shape[1]
    for part in range(q_ref.shape[1] // cs):
        r = slice(part * cs, (part + 1) * cs)
        scores = [_dot_nt(q_ref[b, r, qk(h)], k_ref[b, r, qk(h)]) for b, h in units]
        cross = [_dot(q_ref[b, r, qk(h)], st_ref[b, h].astype(BF16)) for b, h in units]
        kw = [(k_ref[b, r, qk(h)].astype(F32) * kdec_ref[h]).astype(BF16) for b, h in units]
        grow = [_dot_tn(kw[u], v_ref[b, r, vw(h)]) for u, (b, h) in enumerate(units)]
        decayed = [(scores[u] * dec_ref[h]).astype(BF16) for u, (b, h) in enumerate(units)]
        inner = [_dot(decayed[u], v_ref[b, r, vw(h)]) for u, (b, h) in enumerate(units)]
        for u, (b, h) in enumerate(units):
            st_ref[b, h] = sdec_ref[h] * st_ref[b, h] + grow[u]
            o = inner[u] + cross[u] * qdec_ref[h]
            o = o * lax.rsqrt(jnp.mean(o * o, axis=-1, keepdims=True) + RMS_EPS)
            o_ref[b, r, vw(h)] = (o * gt_ref[b, r, vw(h)]).astype(o_ref.dtype)

    @pl.when(c == pl.num_programs(0) - 1)
    def _():
        sfin_ref[...] = st_ref[...]


def _ret_call(q, k, v, gate, state0, cs, per):
    b, n, _ = q.shape
    log_gamma = jnp.log1p(-jnp.exp2(-5.0 - jnp.arange(RET_HEADS, dtype=F32)))
    pos = jnp.arange(cs, dtype=F32)
    diff = pos[:, None] - pos[None, :]
    dec = jnp.where(diff >= 0, jnp.exp(log_gamma[:, None, None] * jnp.maximum(diff, 0.0)), 0.0)
    qdec = jnp.exp(log_gamma[:, None] * (pos + 1.0))[..., None]
    kdec = jnp.exp(log_gamma[:, None] * (cs - 1.0 - pos))[..., None]
    sdec = jnp.exp(log_gamma * cs)
    seq = lambda w: pl.BlockSpec((b, cs * per, w), lambda c: (0, c, 0))
    st = pl.BlockSpec((b, RET_HEADS, RET_QK_DIM, RET_V_DIM), lambda c: (0, 0, 0, 0))
    return pl.pallas_call(
        _ret_kernel,
        grid=(n // (cs * per),),
        in_specs=[pl.BlockSpec(memory_space=pltpu.SMEM), seq(RET_QK_W), seq(RET_QK_W), seq(RET_V_W),
                  seq(RET_V_W), st, _full_spec(dec.shape), _full_spec(qdec.shape), _full_spec(kdec.shape)],
        out_specs=[seq(RET_V_W), st],
        out_shape=[jax.ShapeDtypeStruct((b, n, RET_V_W), BF16),
                   jax.ShapeDtypeStruct((b, RET_HEADS, RET_QK_DIM, RET_V_DIM), F32)],
        scratch_shapes=[pltpu.VMEM((b, RET_HEADS, RET_QK_DIM, RET_V_DIM), F32)],
        compiler_params=_params(("arbitrary",)),
        name="retention",
    )(sdec, q, k, v, gate, state0, dec, qdec, kdec)


def _head_norm_rope(t, bd, gain, cos2, sin2):
    wide = bd.shape[0]
    parts = []
    for c in range(t.shape[1] // wide):
        sq = t[:, c * wide:(c + 1) * wide]
        sq = sq * sq
        hi = sq.astype(BF16)
        lo = (sq - hi.astype(F32)).astype(BF16)
        parts.append(_dot(hi, bd) + _dot(lo, bd))
    ms = jnp.concatenate(parts, axis=1) * (1.0 / SWA_HEAD_DIM)
    y = t * lax.rsqrt(ms + RMS_EPS) * gain
    w = t.shape[1]
    reps = w // LANES
    quarter = SWA_HEAD_DIM // 2
    first_half = (lax.broadcasted_iota(jnp.int32, t.shape, 1) % SWA_HEAD_DIM) < quarter
    partner = jnp.where(first_half, pltpu.roll(y, w - quarter, axis=1), pltpu.roll(y, quarter, axis=1))
    return y * jnp.concatenate([cos2] * reps, axis=1) + partner * jnp.concatenate([sin2] * reps, axis=1)


def _odd_proj_kernel(x_ref, g_ref, w_ref, bd_ref, gain_ref, cos_ref, sin_ref, q_ref, k_ref, v_ref):
    h = _rms(x_ref[...], g_ref[...]).astype(BF16)
    qk_w = SWA_Q_W + SWA_KV_W
    qk = _dot(h, w_ref[:, :qk_w])
    v_ref[...] = _dot(h, w_ref[:, qk_w:])
    qk = _head_norm_rope(qk, bd_ref[...], gain_ref[...], cos_ref[...], sin_ref[...])
    q_ref[...] = (qk[:, :SWA_Q_W] * (LOG2E * SWA_HEAD_DIM ** -0.5)).astype(BF16)
    k_ref[...] = qk[:, SWA_Q_W:]


def _odd_proj_call(x, g, w_in, gain, cos2, sin2, tm):
    t, d = x.shape
    n_tab = cos2.shape[0] // tm
    head = jnp.arange(MXU_TILE) // SWA_HEAD_DIM
    bd = (head[:, None] == head[None, :]).astype(BF16)
    row = lambda w: pl.BlockSpec((tm, w), lambda i: (i, 0))
    tab = pl.BlockSpec((tm, LANES), lambda i: (i % n_tab, 0))
    ins = [g, w_in, bd, gain]
    return pl.pallas_call(
        _odd_proj_kernel,
        grid=(t // tm,),
        in_specs=[row(d)] + [_full_spec(a.shape) for a in ins] + [tab, tab],
        out_specs=[row(SWA_Q_W), row(SWA_KV_W), row(SWA_KV_W)],
        out_shape=[jax.ShapeDtypeStruct((t, SWA_Q_W), BF16), jax.ShapeDtypeStruct((t, SWA_KV_W), F32),
                   jax.ShapeDtypeStruct((t, SWA_KV_W), F32)],
        compiler_params=_params(("parallel",)),
        name="odd_proj",
    )(x, *ins, cos2, sin2)


def _swa_kernel(sink_ref, q_ref, kp_ref, kc_ref, vp_ref, vc_ref, o_ref, *, rows, first_prev_valid):
    i = pl.program_id(1)
    nch = rows // CHUNK
    wk = WINDOW + CHUNK
    nk = WINDOW + rows
    grp = SWA_HEADS // SWA_KV_HEADS
    gw = grp * CHUNK
    key_low = lax.broadcasted_iota(jnp.int32, (nk, LANES), 1) < SWA_HEAD_DIM
    low = lax.broadcasted_iota(jnp.int32, (CHUNK, LANES), 1) < SWA_HEAD_DIM
    if not first_prev_valid:
        krow = lax.broadcasted_iota(jnp.int32, (nk, LANES), 0)
        klane = lax.broadcasted_iota(jnp.int32, (nk, LANES), 1)
        k_bias = jnp.where((krow < WINDOW) & (klane == 0) & (i == 0), SWA_MASKED, 0.0).astype(BF16)
        q_one = jnp.where(lax.broadcasted_iota(jnp.int32, (gw, LANES), 1) == 0, 1.0, 0.0).astype(BF16)

    s, vj_t = [], []
    for j in range(SWA_KV_HEADS):
        pair, odd = divmod(j, 2)
        lanes = slice(pair * LANES, (pair + 1) * LANES)
        k_pair = jnp.concatenate([kp_ref[:, lanes], kc_ref[:, lanes]], axis=0)
        own = jnp.where(key_low, 0.0, k_pair) if odd else jnp.where(key_low, k_pair, 0.0)
        k_both = (own + pltpu.roll(own, SWA_HEAD_DIM, axis=1)).astype(BF16)
        v_t = jnp.concatenate([vp_ref[:, lanes], vc_ref[:, lanes]], axis=0).T
        vj_t.append(v_t[odd * SWA_HEAD_DIM:(odd + 1) * SWA_HEAD_DIM].astype(BF16))
        if not first_prev_valid:
            k_both = jnp.concatenate([k_both, k_bias], axis=1)
        for c in range(nch):
            q_four = []
            for qp in (2 * j, 2 * j + 1):
                q_pair = q_ref[c * CHUNK:(c + 1) * CHUNK, qp * LANES:(qp + 1) * LANES]
                zero = jnp.zeros_like(q_pair)
                q_four += [jnp.where(low, q_pair, zero), jnp.where(low, zero, q_pair)]
            q_four = jnp.concatenate(q_four, axis=0)
            if not first_prev_valid:
                q_four = jnp.concatenate([q_four, q_one], axis=1)
            s.append(_dot_nt(k_both[c * CHUNK:c * CHUNK + wk], q_four))
    s = jnp.concatenate(s, axis=1)
    sink = jnp.concatenate([jnp.full((1, CHUNK), sink_ref[j * grp + g] * LOG2E, F32)
                            for j in range(SWA_KV_HEADS) for _ in range(nch) for g in range(grp)], axis=1)
    m = jnp.maximum(jnp.max(s, axis=0, keepdims=True), sink)
    p = jnp.exp2(s - m)
    inv = 1.0 / (jnp.sum(p, axis=0, keepdims=True) + jnp.exp2(sink - m))
    p = p.astype(BF16)
    o_t = []
    for j in range(SWA_KV_HEADS):
        for c in range(nch):
            cols = slice((j * nch + c) * gw, (j * nch + c + 1) * gw)
            o_t.append(_dot(vj_t[j][:, c * CHUNK:c * CHUNK + wk], p[:, cols]) * inv[:, cols])
    for c0 in range(0, nch, 2):
        pieces = []
        for j in range(SWA_KV_HEADS):
            for t in range(grp // 2):
                tiles = [o_t[j * nch + c][:, t * LANES:(t + 1) * LANES] for c in range(c0, min(c0 + 2, nch))]
                if len(tiles) == 2:
                    pieces += [jnp.where(low, tiles[0], pltpu.roll(tiles[1], CHUNK, axis=1)),
                               jnp.where(low, pltpu.roll(tiles[0], CHUNK, axis=1), tiles[1])]
                else:
                    pieces += [tiles[0], pltpu.roll(tiles[0], CHUNK, axis=1)]
        out = jnp.concatenate(pieces, axis=0).T
        r1 = min((c0 + 2) * CHUNK, rows)
        o_ref[c0 * CHUNK:r1] = out[:r1 - c0 * CHUNK].astype(o_ref.dtype)


def _swa_call(q, k_prev, k_cur, v_prev, v_cur, sinks, rows, same_array):
    b, n, _ = q.shape
    per = rows // WINDOW if same_array else 0
    cur = lambda w: pl.BlockSpec((None, rows, w), lambda bi, i: (bi, i, 0))
    prev = pl.BlockSpec((None, WINDOW, SWA_KV_W), lambda bi, i: (bi, jnp.maximum(i * per - 1, 0), 0))
    return pl.pallas_call(
        functools.partial(_swa_kernel, rows=rows, first_prev_valid=not same_array),
        grid=(b, n // rows),
        in_specs=[pl.BlockSpec(memory_space=pltpu.SMEM), cur(SWA_Q_W), prev, cur(SWA_KV_W), prev, cur(SWA_KV_W)],
        out_specs=cur(SWA_Q_W),
        out_shape=jax.ShapeDtypeStruct((b, n, SWA_Q_W), BF16),
        compiler_params=_params(("parallel", "parallel")),
        name="swa",
    )(sinks, q, k_prev, k_cur, v_prev, v_cur)


def _rope_tables(pos0, n, head_dim):
    half = head_dim // 2
    lane = jnp.arange(LANES)
    inv = jnp.power(ROPE_THETA, -(lane % half).astype(F32) / half)
    sign = jnp.where(lane % head_dim < half, -1.0, 1.0)
    fine = min(n, LANES)
    ang_c = (pos0 + fine * jnp.arange(n // fine)).astype(F32)[:, None] * inv[None, :]
    ang_f = jnp.arange(fine, dtype=F32)[:, None] * inv[None, :]
    cos_c, sin_c = jnp.cos(ang_c)[:, None, :], jnp.sin(ang_c)[:, None, :]
    cos_f, sin_f = jnp.cos(ang_f)[None, :, :], jnp.sin(ang_f)[None, :, :]
    return ((cos_c * cos_f - sin_c * sin_f).reshape(n, LANES),
            ((sin_c * cos_f + cos_c * sin_f) * sign).reshape(n, LANES))


def _prep_weights(norm_g, ffn_w_gate, ffn_w_up, ffn_w_down, even_w_in, even_w_out, odd_w_in, odd_w_out,
                  odd_q_norm, odd_k_norm):
    depth = norm_g.shape[0]
    prm = {"g": norm_g[:, :, None, :], "even": [], "odd": [], "depth": depth,
           "ffn": (ffn_w_gate.astype(BF16), ffn_w_up.astype(BF16), ffn_w_down.astype(BF16))}
    for i in range(even_w_in.shape[0]):
        prm["even"].append((even_w_in[i].astype(BF16), even_w_out[i].astype(BF16)))
    for i in range(odd_w_in.shape[0]):
        gain = jnp.concatenate([jnp.tile(odd_q_norm[i], SWA_HEADS), jnp.tile(odd_k_norm[i], SWA_KV_HEADS)])
        prm["odd"].append((odd_w_in[i].astype(BF16), odd_w_out[i].astype(BF16), gain[None, :]))
    return prm


def _trunk(x, pos0, prm, sinks, caches, tm, tm_ffn, tm_odd, sb_qb, sb_kb, ret_cs, ret_per, swa_rows):
    b, n, d = x.shape
    t = b * n
    xf = x.reshape(t, d)
    depth = prm["depth"]
    tab_rows = n if n % tm == 0 else t
    tile_tab = lambda tb: tb if tab_rows == n else jnp.tile(tb, (b, 1))
    cos_r, sin_r = [tile_tab(tb) for tb in _rope_tables(pos0, n, RET_QK_DIM)]
    cos_s, sin_s = [tile_tab(tb) for tb in _rope_tables(pos0, n, SWA_HEAD_DIM)]
    idx = jnp.arange(sb_kb)
    negu = jnp.where(idx[:, None] >= idx[None, :], -1.0, 0.0).astype(BF16)
    sb_k, sb_v, ret, swa_k, swa_v = [], [], [], [], []
    mix, w_mix = [], None
    for layer in range(depth):
        i = layer // 2
        g = prm["g"][layer]
        xf = _ffn_call(xf, mix, w_mix, g[0], prm["ffn"], layer, 0, tm_ffn)
        if layer % 2 == 0:
            w_in, w_mix = prm["even"][i]
            qa, ka, kab, va, vab, qr, kr, vr, gt = _even_proj_call(xf, g[1], w_in, cos_r, sin_r, tm)
            sb_k.append(ka.reshape(b, n, SB_HEADS, SB_HEAD_DIM))
            sb_v.append(va.reshape(b, n, SB_HEADS, SB_HEAD_DIM))
            r3 = lambda a: a.reshape(b, n, a.shape[-1])
            if caches is None:
                k_all, v_all, q_pos0 = r3(kab), r3(vab), 0
                state0 = jnp.zeros((b, RET_HEADS, RET_QK_DIM, RET_V_DIM), F32)
            else:
                past = caches["sb_k"].shape[2]
                padded = -(-(past + n) // sb_kb) * sb_kb
                cat = lambda cache, new: jnp.pad(
                    jnp.concatenate([cache.reshape(b, past, SB_W).astype(BF16), r3(new)], axis=1),
                    ((0, 0), (0, padded - past - n), (0, 0)))
                k_all, v_all, q_pos0 = cat(caches["sb_k"][i], kab), cat(caches["sb_v"][i], vab), past
                state0 = caches["ret"][i]
            o_sb = _sb_call(r3(qa), k_all, v_all, negu, sb_qb, q_pos0)
            o_r, st = _ret_call(r3(qr), r3(kr), r3(vr), r3(gt), state0, ret_cs, ret_per)
            ret.append(st)
            mix = [o_sb.reshape(t, SB_W), o_r.reshape(t, RET_V_W)]
        else:
            w_in, w_mix, gain = prm["odd"][i]
            q, k, v = _odd_proj_call(xf, g[1], w_in, gain, cos_s, sin_s, tm_odd)
            r3 = lambda a: a.reshape(b, n, a.shape[-1])
            k3, v3 = r3(k), r3(v)
            if caches is None:
                o = _swa_call(r3(q), k3, k3, v3, v3, sinks[i], swa_rows, True)
                keep = min(WINDOW, n)
                k_rows, v_rows = k3[:, n - keep:], v3[:, n - keep:]
            else:
                kc = caches["swa_k"][i].reshape(b, -1, SWA_KV_W)
                vc = caches["swa_v"][i].reshape(b, -1, SWA_KV_W)
                o = _swa_call(r3(q), kc, k3, vc, v3, sinks[i], swa_rows, False)
                k_rows, v_rows = k3, v3
            swa_k.append(k_rows.reshape(b, -1, SWA_KV_HEADS, SWA_HEAD_DIM))
            swa_v.append(v_rows.reshape(b, -1, SWA_KV_HEADS, SWA_HEAD_DIM))
            mix = [o.reshape(t, SWA_Q_W)]
        xf = _ffn_call(xf, mix, w_mix, g[2], prm["ffn"], layer, 1, tm_ffn)
        mix, w_mix = [], None
    stack = lambda parts: parts[0][None] if len(parts) == 1 else jnp.stack(parts)
    return (xf.reshape(b, n, d), stack(sb_k), stack(sb_v), stack(ret), stack(swa_k), stack(swa_v))


def kernel(x_prompt, x_sample, cache_sb_k, cache_sb_v, state_ret, cache_swa_k, cache_swa_v, norm_g, ffn_w_gate,
           ffn_w_up, ffn_w_down, even_w_in, even_w_out, odd_w_in, odd_w_out, odd_q_norm, odd_k_norm, odd_sinks):
    prm = _prep_weights(norm_g, ffn_w_gate, ffn_w_up, ffn_w_down, even_w_in, even_w_out, odd_w_in, odd_w_out,
                        odd_q_norm, odd_k_norm)
    past = cache_sb_k.shape[2]
    n_s = x_sample.shape[1]
    y_p, sb_k_p, sb_v_p, ret_p, swa_k_p, swa_v_p = _trunk(
        x_prompt, 0, prm, odd_sinks, None, tm=512, tm_ffn=1024, tm_odd=1024, sb_qb=2048, sb_kb=128, ret_cs=256, ret_per=2, swa_rows=1024)
    caches = {"sb_k": cache_sb_k, "sb_v": cache_sb_v, "ret": state_ret, "swa_k": cache_swa_k, "swa_v": cache_swa_v}
    y_s, sb_k_s, sb_v_s, ret_s, swa_k_s, swa_v_s = _trunk(
        x_sample, past, prm, odd_sinks, caches, tm=x_sample.shape[0] * n_s, tm_ffn=x_sample.shape[0] * n_s, tm_odd=x_sample.shape[0] * n_s, sb_qb=n_s, sb_kb=128,
        ret_cs=n_s, ret_per=1, swa_rows=n_s)
    return (y_p, y_s, sb_k_p, sb_v_p, ret_p, swa_k_p, swa_v_p, sb_k_s, sb_v_s, ret_s, swa_k_s, swa_v_s)
```

```python
import functools
import math

import jax
import jax.numpy as jnp
from jax import lax
from jax.experimental import pallas as pl
from jax.experimental.pallas import tpu as pltpu

F32 = jnp.float32
BF16 = jnp.bfloat16

RMS_EPS = 1e-6
ROPE_THETA = 10000.0
CHUNK = 64
SB_HEADS = 8
SB_HEAD_DIM = 64
RET_HEADS = 4
RET_QK_DIM = 128
RET_V_DIM = 256
SWA_HEADS = 16
SWA_KV_HEADS = 4
SWA_HEAD_DIM = 64
WINDOW = 128

SB_W = SB_HEADS * SB_HEAD_DIM
RET_QK_W = RET_HEADS * RET_QK_DIM
RET_V_W = RET_HEADS * RET_V_DIM
SWA_Q_W = SWA_HEADS * SWA_HEAD_DIM
SWA_KV_W = SWA_KV_HEADS * SWA_HEAD_DIM
EVEN_CUTS = (0, SB_W, 2 * SB_W, 3 * SB_W, 3 * SB_W + RET_QK_W, 3 * SB_W + 2 * RET_QK_W,
             3 * SB_W + 2 * RET_QK_W + RET_V_W, 3 * SB_W + 2 * RET_QK_W + 2 * RET_V_W)

LANES = 128
MXU_TILE = 256
FF_CHUNK = MXU_TILE
VMEM_LIMIT = 56 * 1024 * 1024
LOG2E = math.log2(math.e)
SOFTPLUS2_LINEAR = 40.0
SWA_MASKED = -1e30
SB_STATIC_BLOCKS = 3
SB_DEAD_LOG2 = -170.0


def _dot(a, b):
    return jnp.dot(a, b, preferred_element_type=F32)


def _dot_nt(a, b):
    return lax.dot_general(a, b, (((1,), (1,)), ((), ())), preferred_element_type=F32)


def _dot_tn(a, b):
    return lax.dot_general(a, b, (((0,), (0,)), ((), ())), preferred_element_type=F32)


def _rms(x, g):
    return x * lax.rsqrt(jnp.mean(x * x, axis=-1, keepdims=True) + RMS_EPS) * g


def _full_spec(shape):
    nd = len(shape)
    return pl.BlockSpec(shape, lambda *_: (0,) * nd, pipeline_mode=pl.Buffered(1))


def _params(sem):
    return pltpu.CompilerParams(dimension_semantics=sem, vmem_limit_bytes=VMEM_LIMIT)


def _ffn_kernel(*refs, n_mix):
    x_ref = refs[0]
    mix = refs[1:1 + n_mix]
    wmix_ref = refs[1 + n_mix] if n_mix else None
    g_ref, wg_ref, wu_ref, wd_ref, o_ref, h_ref, act_ref = refs[1 + n_mix + bool(n_mix):]
    x = x_ref[...]
    r0 = 0
    for a_ref in mix:
        x = x + _dot(a_ref[...], wmix_ref[r0:r0 + a_ref.shape[1]])
        r0 += a_ref.shape[1]
    h_ref[...] = _rms(x, g_ref[...]).astype(BF16)
    for c in range(wg_ref.shape[1] // FF_CHUNK):
        cols = slice(c * FF_CHUNK, (c + 1) * FF_CHUNK)
        gate = _dot(h_ref[...], wg_ref[:, cols])
        up = _dot(h_ref[...], wu_ref[:, cols])
        act_ref[:, cols] = (gate * jax.nn.sigmoid(gate) * up).astype(BF16)
    o_ref[...] = x + 0.5 * _dot(act_ref[...], wd_ref[...])


def _ffn_stream_kernel(*refs, n_mix):
    x_ref = refs[0]
    mix = refs[1:1 + n_mix]
    wmix_ref = refs[1 + n_mix] if n_mix else None
    g_ref, wg_ref, wu_ref, wd_ref, o_ref, h_ref, res_ref, acc_ref = refs[1 + n_mix + bool(n_mix):]
    c = pl.program_id(0)

    @pl.when(c == 0)
    def _():
        x = x_ref[...]
        r0 = 0
        for a_ref in mix:
            x = x + _dot(a_ref[...], wmix_ref[r0:r0 + a_ref.shape[1]])
            r0 += a_ref.shape[1]
        res_ref[...] = x
        h_ref[...] = _rms(x, g_ref[...]).astype(BF16)
        acc_ref[...] = jnp.zeros_like(acc_ref)

    gate = _dot(h_ref[...], wg_ref[...])
    up = _dot(h_ref[...], wu_ref[...])
    acc_ref[...] += _dot((gate * jax.nn.sigmoid(gate) * up).astype(BF16), wd_ref[...])

    @pl.when(c == pl.num_programs(0) - 1)
    def _():
        o_ref[...] = res_ref[...] + 0.5 * acc_ref[...]


def _ffn_stream_call(x, mix, w_mix, g, ffn_w, layer, slot):
    wg, wu, wd = ffn_w
    t, d = x.shape
    whole = lambda a: pl.BlockSpec(a.shape, lambda c: (0,) * a.ndim)
    in_specs = [whole(x)] + [whole(a) for a in mix] + ([whole(w_mix)] if mix else [])
    in_specs += [whole(g),
                 pl.BlockSpec((None, None, d, FF_CHUNK), lambda c: (layer, slot, 0, c)),
                 pl.BlockSpec((None, None, d, FF_CHUNK), lambda c: (layer, slot, 0, c)),
                 pl.BlockSpec((None, None, FF_CHUNK, d), lambda c: (layer, slot, c, 0))]
    args = [x] + list(mix) + ([w_mix] if mix else []) + [g, wg, wu, wd]
    return pl.pallas_call(
        functools.partial(_ffn_stream_kernel, n_mix=len(mix)),
        grid=(wg.shape[-1] // FF_CHUNK,),
        in_specs=in_specs,
        out_specs=whole(x),
        out_shape=jax.ShapeDtypeStruct((t, d), F32),
        scratch_shapes=[pltpu.VMEM((t, d), BF16), pltpu.VMEM((t, d), F32), pltpu.VMEM((t, d), F32)],
        compiler_params=_params(("arbitrary",)),
        name="ffn_stream_mix%d" % len(mix),
    )(*args)


def _ffn_call(x, mix, w_mix, g, ffn_w, layer, slot, tm):
    if x.shape[0] == tm:
        return _ffn_stream_call(x, mix, w_mix, g, ffn_w, layer, slot)
    wg, wu, wd = ffn_w
    stack_spec = lambda w: pl.BlockSpec((None, None) + w.shape[2:], lambda i: (layer, slot, 0, 0),
                                        pipeline_mode=pl.Buffered(1))
    t, d = x.shape
    row = lambda w: pl.BlockSpec((tm, w), lambda i: (i, 0))
    in_specs = [row(d)]
    args = [x]
    for a in mix:
        in_specs.append(row(a.shape[1]))
        args.append(a)
    if mix:
        in_specs.append(_full_spec(w_mix.shape))
        args.append(w_mix)
    in_specs += [_full_spec(g.shape), stack_spec(wg), stack_spec(wu), stack_spec(wd)]
    args += [g, wg, wu, wd]
    return pl.pallas_call(
        functools.partial(_ffn_kernel, n_mix=len(mix)),
        grid=(t // tm,),
        in_specs=in_specs,
        out_specs=row(d),
        out_shape=jax.ShapeDtypeStruct((t, d), F32),
        scratch_shapes=[pltpu.VMEM((tm, d), BF16), pltpu.VMEM((tm, wg.shape[-1]), BF16)],
        compiler_params=_params(("parallel",)),
        name="ffn_mix%d" % len(mix),
    )(*args)


def _rope128(x, cos2, sin2):
    parts = []
    for hd in range(x.shape[1] // LANES):
        sl = x[:, hd * LANES:(hd + 1) * LANES]
        parts.append(sl * cos2 + pltpu.roll(sl, LANES // 2, axis=1) * sin2)
    return jnp.concatenate(parts, axis=1)


def _even_proj_kernel(x_ref, g_ref, w_ref, cos_ref, sin_ref,
                      qa_ref, ka_ref, kab_ref, va_ref, vab_ref, qr_ref, kr_ref, vr_ref, gt_ref):
    h = _rms(x_ref[...], g_ref[...]).astype(BF16)
    proj = lambda s: _dot(h, w_ref[:, EVEN_CUTS[s]:EVEN_CUTS[s + 1]])
    qa_ref[...] = (proj(0) * (LOG2E * SB_HEAD_DIM ** -0.5)).astype(BF16)
    ka = proj(1)
    ka_ref[...] = ka.reshape(ka_ref.shape)
    kab_ref[...] = ka.astype(BF16)
    va = proj(2)
    va_ref[...] = va.reshape(va_ref.shape)
    vab_ref[...] = va.astype(BF16)
    cos2 = cos_ref[...]
    sin2 = sin_ref[...]
    qr_ref[...] = _rope128(proj(3), cos2, sin2).astype(BF16)
    kr_ref[...] = (_rope128(proj(4), cos2, sin2) * (RET_QK_DIM ** -0.5)).astype(BF16)
    vr_ref[...] = proj(5).astype(BF16)
    gate = proj(6)
    gt_ref[...] = gate * jax.nn.sigmoid(gate)


def _even_proj_call(x, g, w_in, cos2, sin2, tm):
    t, d = x.shape
    n_tab = cos2.shape[0] // tm
    row = lambda w: pl.BlockSpec((tm, w), lambda i: (i, 0))
    tab = pl.BlockSpec((tm, LANES), lambda i: (i % n_tab, 0))
    per_head = (SB_HEADS, SB_HEAD_DIM)
    outs = [((SB_W,), BF16), (per_head, F32), ((SB_W,), BF16), (per_head, F32), ((SB_W,), BF16),
            ((RET_QK_W,), BF16), ((RET_QK_W,), BF16), ((RET_V_W,), BF16), ((RET_V_W,), F32)]
    return pl.pallas_call(
        _even_proj_kernel,
        grid=(t // tm,),
        in_specs=[row(d), _full_spec(g.shape), _full_spec(w_in.shape), tab, tab],
        out_specs=[pl.BlockSpec((tm,) + tail, lambda i, nd=len(tail): (i,) + (0,) * nd) for tail, _ in outs],
        out_shape=[jax.ShapeDtypeStruct((t,) + tail, dt) for tail, dt in outs],
        compiler_params=_params(("parallel",)),
        name="even_proj",
    )(x, g, w_in, cos2, sin2)


def _sb_kernel(q_ref, k_ref, v_ref, negu_ref, o_ref, qh_ref, c_ref, acc_ref, *, qb, kb, q_pos0):
    i = pl.program_id(2)
    n_clear = (q_pos0 + i * qb) // kb
    sq = min(qb, kb)
    nsub = qb // sq
    rows = lambda s: slice(s * sq, (s + 1) * sq)
    kstart = lambda j: pl.multiple_of(j * kb, kb)
    q = q_ref[...]
    first = lax.broadcasted_iota(jnp.int32, (qb, LANES), 1) < SB_HEAD_DIM
    qh_ref[0] = jnp.where(first, q, jnp.zeros_like(q))
    qh_ref[1] = jnp.where(first, jnp.zeros_like(q), q)

    def scores(hd, rws, start):
        return _dot_nt(qh_ref[hd, rws], k_ref[pl.ds(start, kb), :])

    def softplus2(z):
        return jnp.where(z > SOFTPLUS2_LINEAR, z, jnp.log2(1.0 + jnp.exp2(z)))

    def v_heads(start):
        vs = v_ref[pl.ds(start, kb), :]
        vfirst = lax.broadcasted_iota(jnp.int32, (kb, LANES), 1) < SB_HEAD_DIM
        return jnp.concatenate([jnp.where(vfirst, vs, jnp.zeros_like(vs)),
                                jnp.where(vfirst, jnp.zeros_like(vs), vs)], axis=0)

    depth = SB_STATIC_BLOCKS
    row = lax.broadcasted_iota(jnp.int32, (qb, kb), 0)
    below = lax.broadcasted_iota(jnp.int32, (qb, kb), 1) < (row & (sq - 1))
    kblock = lambda s, d: kstart(jnp.maximum(n_clear + s - d, 0))

    def apply_masks(t):
        parts = [jnp.where(below, t[:qb], 0.0)]
        for d in range(1, depth):
            for s in range(min(d, nsub)):
                exists = jnp.broadcast_to(n_clear + s - d >= 0, (sq, kb))
                parts.append(jnp.where(exists, t[d * qb + s * sq:d * qb + (s + 1) * sq], 0.0))
            if nsub > d:
                parts.append(t[d * qb + d * sq:(d + 1) * qb])
        return jnp.concatenate(parts, axis=0)

    ws = []
    for hd in range(2):
        z = jnp.concatenate([scores(hd, rows(s), kblock(s, d)) for d in range(depth) for s in range(nsub)], axis=0)
        sp = apply_masks(softplus2(z))
        tail = _dot(sp.astype(BF16), negu_ref[...])
        seen = [jnp.zeros((qb, 1), F32)]
        for d in range(depth):
            seen.append(seen[-1] + tail[d * qb:(d + 1) * qb, :1])
        c_ref[hd] = seen[depth]
        ws.append(apply_masks(jnp.exp2(z + tail + jnp.concatenate(seen[:depth], axis=0))).astype(BF16))
    w = jnp.concatenate(ws, axis=1)
    for s in range(nsub):
        acc = None
        for d in range(depth):
            part = _dot(w[d * qb + s * sq:d * qb + (s + 1) * sq], v_heads(kblock(s, d)))
            acc = part if acc is None else acc + part
        acc_ref[rows(s)] = acc

    def block(start, r0, r1):
        ws = []
        for hd in range(2):
            z = scores(hd, slice(r0, r1), start)
            tail = _dot(softplus2(z).astype(BF16), negu_ref[...])
            ws.append(jnp.exp2(z + tail + c_ref[hd, r0:r1]).astype(BF16))
            c_ref[hd, r0:r1] += tail[:, :1]
        acc_ref[r0:r1] += _dot(jnp.concatenate(ws, axis=1), v_heads(start))

    def alive(r0, r1):
        return (jnp.max(c_ref[:, r0:r1]) > SB_DEAD_LOG2).astype(jnp.int32)

    def run_blocks(j_first, j_last, r0, r1):
        def cond(carry):
            j, live = carry
            return jnp.logical_and(j >= j_last, live > 0)

        def body(carry):
            j, _ = carry
            block(kstart(j), r0, r1)
            return j - 1, alive(r0, r1)

        lax.while_loop(cond, body, (j_first, alive(r0, r1)))

    @pl.when(alive(0, qb) > 0)
    def _():
        for s in range(1, nsub):
            run_blocks(n_clear + s - depth, jnp.maximum(n_clear - depth + 1, 0), s * sq, (s + 1) * sq)
        run_blocks(n_clear - depth, 0, 0, qb)

    o_ref[...] = acc_ref[...].astype(o_ref.dtype)


def _sb_call(q, k, v, negu, qb, q_pos0):
    b, nq, w = q.shape
    nk = k.shape[1]
    kb = negu.shape[0]
    assert q_pos0 % kb == 0 and (qb % kb == 0 or nq == qb < kb) and nk % kb == 0 and nk >= q_pos0 + nq
    return pl.pallas_call(
        functools.partial(_sb_kernel, qb=qb, kb=kb, q_pos0=q_pos0),
        grid=(b, w // LANES, nq // qb),
        in_specs=[pl.BlockSpec((None, qb, LANES), lambda bi, p, i: (bi, i, p)),
                  pl.BlockSpec((None, nk, LANES), lambda bi, p, i: (bi, 0, p)),
                  pl.BlockSpec((None, nk, LANES), lambda bi, p, i: (bi, 0, p)),
                  _full_spec(negu.shape)],
        out_specs=pl.BlockSpec((None, qb, LANES), lambda bi, p, i: (bi, i, p)),
        out_shape=jax.ShapeDtypeStruct((b, nq, w), BF16),
        scratch_shapes=[pltpu.VMEM((2, qb, LANES), BF16), pltpu.VMEM((2, qb, 1), F32),
                        pltpu.VMEM((qb, LANES), F32)],
        compiler_params=_params(("parallel", "parallel", "parallel")),
        name="stick_breaking",
    )(q, k, v, negu)


def _ret_kernel(sdec_ref, q_ref, k_ref, v_ref, gt_ref, s0_ref, dec_ref, qdec_ref, kdec_ref,
                o_ref, sfin_ref, st_ref):
    c = pl.program_id(0)

    @pl.when(c == 0)
    def _():
        st_ref[...] = s0_ref[...]

    units = [(b, h) for b in range(q_ref.shape[0]) for h in range(RET_HEADS)]
    qk = lambda h: slice(h * RET_QK_DIM, (h + 1) * RET_QK_DIM)
    vw = lambda h: slice(h * RET_V_DIM, (h + 1) * RET_V_DIM)
    cs = dec_ref.shape[1]
    for part in range(q_ref.shape[1] // cs):
        r = slice(part * cs, (part + 1) * cs)
        scores = [_dot_nt(q_ref[b, r, qk(h)], k_ref[b, r, qk(h)]) for b, h in units]
        cross = [_dot(q_ref[b, r, qk(h)], st_ref[b, h].astype(BF16)) for b, h in units]
        kw = [(k_ref[b, r, qk(h)].astype(F32) * kdec_ref[h]).astype(BF16) for b, h in units]
        grow = [_dot_tn(kw[u], v_ref[b, r, vw(h)]) for u, (b, h) in enumerate(units)]
        decayed = [(scores[u] * dec_ref[h]).astype(BF16) for u, (b, h) in enumerate(units)]
        inner = [_dot(decayed[u], v_ref[b, r, vw(h)]) for u, (b, h) in enumerate(units)]
        for u, (b, h) in enumerate(units):
            st_ref[b, h] = sdec_ref[h] * st_ref[b, h] + grow[u]
            o = inner[u] + cross[u] * qdec_ref[h]
            o = o * lax.rsqrt(jnp.mean(o * o, axis=-1, keepdims=True) + RMS_EPS)
            o_ref[b, r, vw(h)] = (o * gt_ref[b, r, vw(h)]).astype(o_ref.dtype)

    @pl.when(c == pl.num_programs(0) - 1)
    def _():
        sfin_ref[...] = st_ref[...]


def _ret_call(q, k, v, gate, state0, cs, per):
    b, n, _ = q.shape
    log_gamma = jnp.log1p(-jnp.exp2(-5.0 - jnp.arange(RET_HEADS, dtype=F32)))
    pos = jnp.arange(cs, dtype=F32)
    diff = pos[:, None] - pos[None, :]
    dec = jnp.where(diff >= 0, jnp.exp(log_gamma[:, None, None] * jnp.maximum(diff, 0.0)), 0.0)
    qdec = jnp.exp(log_gamma[:, None] * (pos + 1.0))[..., None]
    kdec = jnp.exp(log_gamma[:, None] * (cs - 1.0 - pos))[..., None]
    sdec = jnp.exp(log_gamma * cs)
    seq = lambda w: pl.BlockSpec((b, cs * per, w), lambda c: (0, c, 0))
    st = pl.BlockSpec((b, RET_HEADS, RET_QK_DIM, RET_V_DIM), lambda c: (0, 0, 0, 0))
    return pl.pallas_call(
        _ret_kernel,
        grid=(n // (cs * per),),
        in_specs=[pl.BlockSpec(memory_space=pltpu.SMEM), seq(RET_QK_W), seq(RET_QK_W), seq(RET_V_W),
                  seq(RET_V_W), st, _full_spec(dec.shape), _full_spec(qdec.shape), _full_spec(kdec.shape)],
        out_specs=[seq(RET_V_W), st],
        out_shape=[jax.ShapeDtypeStruct((b, n, RET_V_W), BF16),
                   jax.ShapeDtypeStruct((b, RET_HEADS, RET_QK_DIM, RET_V_DIM), F32)],
        scratch_shapes=[pltpu.VMEM((b, RET_HEADS, RET_QK_DIM, RET_V_DIM), F32)],
        compiler_params=_params(("arbitrary",)),
        name="retention",
    )(sdec, q, k, v, gate, state0, dec, qdec, kdec)


def _head_norm_rope(t, bd, gain, cos2, sin2):
    wide = bd.shape[0]
    parts = []
    for c in range(t.shape[1] // wide):
        sq = t[:, c * wide:(c + 1) * wide]
        sq = sq * sq
        hi = sq.astype(BF16)
        lo = (sq - hi.astype(F32)).astype(BF16)
        parts.append(_dot(hi, bd) + _dot(lo, bd))
    ms = jnp.concatenate(parts, axis=1) * (1.0 / SWA_HEAD_DIM)
    y = t * lax.rsqrt(ms + RMS_EPS) * gain
    w = t.shape[1]
    reps = w // LANES
    quarter = SWA_HEAD_DIM // 2
    first_half = (lax.broadcasted_iota(jnp.int32, t.shape, 1) % SWA_HEAD_DIM) < quarter
    partner = jnp.where(first_half, pltpu.roll(y, w - quarter, axis=1), pltpu.roll(y, quarter, axis=1))
    return y * jnp.concatenate([cos2] * reps, axis=1) + partner * jnp.concatenate([sin2] * reps, axis=1)


def _odd_proj_kernel(x_ref, g_ref, w_ref, bd_ref, gain_ref, cos_ref, sin_ref, q_ref, k_ref, v_ref):
    h = _rms(x_ref[...], g_ref[...]).astype(BF16)
    qk_w = SWA_Q_W + SWA_KV_W
    qk = _dot(h, w_ref[:, :qk_w])
    v_ref[...] = _dot(h, w_ref[:, qk_w:])
    qk = _head_norm_rope(qk, bd_ref[...], gain_ref[...], cos_ref[...], sin_ref[...])
    q_ref[...] = (qk[:, :SWA_Q_W] * (LOG2E * SWA_HEAD_DIM ** -0.5)).astype(BF16)
    k_ref[...] = qk[:, SWA_Q_W:]


def _odd_proj_call(x, g, w_in, gain, cos2, sin2, tm):
    t, d = x.shape
    n_tab = cos2.shape[0] // tm
    head = jnp.arange(MXU_TILE) // SWA_HEAD_DIM
    bd = (head[:, None] == head[None, :]).astype(BF16)
    row = lambda w: pl.BlockSpec((tm, w), lambda i: (i, 0))
    tab = pl.BlockSpec((tm, LANES), lambda i: (i % n_tab, 0))
    ins = [g, w_in, bd, gain]
    return pl.pallas_call(
        _odd_proj_kernel,
        grid=(t // tm,),
        in_specs=[row(d)] + [_full_spec(a.shape) for a in ins] + [tab, tab],
        out_specs=[row(SWA_Q_W), row(SWA_KV_W), row(SWA_KV_W)],
        out_shape=[jax.ShapeDtypeStruct((t, SWA_Q_W), BF16), jax.ShapeDtypeStruct((t, SWA_KV_W), F32),
                   jax.ShapeDtypeStruct((t, SWA_KV_W), F32)],
        compiler_params=_params(("parallel",)),
        name="odd_proj",
    )(x, *ins, cos2, sin2)


def _swa_kernel(sink_ref, q_ref, kp_ref, kc_ref, vp_ref, vc_ref, o_ref, *, rows, first_prev_valid):
    i = pl.program_id(1)
    nch = rows // CHUNK
    wk = WINDOW + CHUNK
    nk = WINDOW + rows
    grp = SWA_HEADS // SWA_KV_HEADS
    gw = grp * CHUNK
    key_low = lax.broadcasted_iota(jnp.int32, (nk, LANES), 1) < SWA_HEAD_DIM
    low = lax.broadcasted_iota(jnp.int32, (CHUNK, LANES), 1) < SWA_HEAD_DIM
    if not first_prev_valid:
        krow = lax.broadcasted_iota(jnp.int32, (nk, LANES), 0)
        klane = lax.broadcasted_iota(jnp.int32, (nk, LANES), 1)
        k_bias = jnp.where((krow < WINDOW) & (klane == 0) & (i == 0), SWA_MASKED, 0.0).astype(BF16)
        q_one = jnp.where(lax.broadcasted_iota(jnp.int32, (gw, LANES), 1) == 0, 1.0, 0.0).astype(BF16)

    s, vj_t = [], []
    for j in range(SWA_KV_HEADS):
        pair, odd = divmod(j, 2)
        lanes = slice(pair * LANES, (pair + 1) * LANES)
        k_pair = jnp.concatenate([kp_ref[:, lanes], kc_ref[:, lanes]], axis=0)
        own = jnp.where(key_low, 0.0, k_pair) if odd else jnp.where(key_low, k_pair, 0.0)
        k_both = (own + pltpu.roll(own, SWA_HEAD_DIM, axis=1)).astype(BF16)
        v_t = jnp.concatenate([vp_ref[:, lanes], vc_ref[:, lanes]], axis=0).T
        vj_t.append(v_t[odd * SWA_HEAD_DIM:(odd + 1) * SWA_HEAD_DIM].astype(BF16))
        if not first_prev_valid:
            k_both = jnp.concatenate([k_both, k_bias], axis=1)
        for c in range(nch):
            q_four = []
            for qp in (2 * j, 2 * j + 1):
                q_pair = q_ref[c * CHUNK:(c + 1) * CHUNK, qp * LANES:(qp + 1) * LANES]
                zero = jnp.zeros_like(q_pair)
                q_four += [jnp.where(low, q_pair, zero), jnp.where(low, zero, q_pair)]
            q_four = jnp.concatenate(q_four, axis=0)
            if not first_prev_valid:
                q_four = jnp.concatenate([q_four, q_one], axis=1)
            s.append(_dot_nt(k_both[c * CHUNK:c * CHUNK + wk], q_four))
    s = jnp.concatenate(s, axis=1)
    sink = jnp.concatenate([jnp.full((1, CHUNK), sink_ref[j * grp + g] * LOG2E, F32)
                            for j in range(SWA_KV_HEADS) for _ in range(nch) for g in range(grp)], axis=1)
    m = jnp.maximum(jnp.max(s, axis=0, keepdims=True), sink)
    p = jnp.exp2(s - m)
    inv = 1.0 / (jnp.sum(p, axis=0, keepdims=True) + jnp.exp2(sink - m))
    p = p.astype(BF16)
    o_t = []
    for j in range(SWA_KV_HEADS):
        for c in range(nch):
            cols = slice((j * nch + c) * gw, (j * nch + c + 1) * gw)
            o_t.append(_dot(vj_t[j][:, c * CHUNK:c * CHUNK + wk], p[:, cols]) * inv[:, cols])
    for c0 in range(0, nch, 2):
        pieces = []
        for j in range(SWA_KV_HEADS):
            for t in range(grp // 2):
                tiles = [o_t[j * nch + c][:, t * LANES:(t + 1) * LANES] for c in range(c0, min(c0 + 2, nch))]
                if len(tiles) == 2:
                    pieces += [jnp.where(low, tiles[0], pltpu.roll(tiles[1], CHUNK, axis=1)),
                               jnp.where(low, pltpu.roll(tiles[0], CHUNK, axis=1), tiles[1])]
                else:
                    pieces += [tiles[0], pltpu.roll(tiles[0], CHUNK, axis=1)]
        out = jnp.concatenate(pieces, axis=0).T
        r1 = min((c0 + 2) * CHUNK, rows)
        o_ref[c0 * CHUNK:r1] = out[:r1 - c0 * CHUNK].astype(o_ref.dtype)


def _swa_call(q, k_prev, k_cur, v_prev, v_cur, sinks, rows, same_array):
    b, n, _ = q.shape
    per = rows // WINDOW if same_array else 0
    cur = lambda w: pl.BlockSpec((None, rows, w), lambda bi, i: (bi, i, 0))
    prev = pl.BlockSpec((None, WINDOW, SWA_KV_W), lambda bi, i: (bi, jnp.maximum(i * per - 1, 0), 0))
    return pl.pallas_call(
        functools.partial(_swa_kernel, rows=rows, first_prev_valid=not same_array),
        grid=(b, n // rows),
        in_specs=[pl.BlockSpec(memory_space=pltpu.SMEM), cur(SWA_Q_W), prev, cur(SWA_KV_W), prev, cur(SWA_KV_W)],
        out_specs=cur(SWA_Q_W),
        out_shape=jax.ShapeDtypeStruct((b, n, SWA_Q_W), BF16),
        compiler_params=_params(("parallel", "parallel")),
        name="swa",
    )(sinks, q, k_prev, k_cur, v_prev, v_cur)


def _rope_tables(pos0, n, head_dim):
    half = head_dim // 2
    lane = jnp.arange(LANES)
    inv = jnp.power(ROPE_THETA, -(lane % half).astype(F32) / half)
    sign = jnp.where(lane % head_dim < half, -1.0, 1.0)
    fine = min(n, LANES)
    ang_c = (pos0 + fine * jnp.arange(n // fine)).astype(F32)[:, None] * inv[None, :]
    ang_f = jnp.arange(fine, dtype=F32)[:, None] * inv[None, :]
    cos_c, sin_c = jnp.cos(ang_c)[:, None, :], jnp.sin(ang_c)[:, None, :]
    cos_f, sin_f = jnp.cos(ang_f)[None, :, :], jnp.sin(ang_f)[None, :, :]
    return ((cos_c * cos_f - sin_c * sin_f).reshape(n, LANES),
            ((sin_c * cos_f + cos_c * sin_f) * sign).reshape(n, LANES))


def _prep_weights(norm_g, ffn_w_gate, ffn_w_up, ffn_w_down, even_w_in, even_w_out, odd_w_in, odd_w_out,
                  odd_q_norm, odd_k_norm):
    depth = norm_g.shape[0]
    prm = {"g": norm_g[:, :, None, :], "even": [], "odd": [], "depth": depth,
           "ffn": (ffn_w_gate.astype(BF16), ffn_w_up.astype(BF16), ffn_w_down.astype(BF16))}
    for i in range(even_w_in.shape[0]):
        prm["even"].append((even_w_in[i].astype(BF16), even_w_out[i].astype(BF16)))
    for i in range(odd_w_in.shape[0]):
        gain = jnp.concatenate([jnp.tile(odd_q_norm[i], SWA_HEADS), jnp.tile(odd_k_norm[i], SWA_KV_HEADS)])
        prm["odd"].append((odd_w_in[i].astype(BF16), odd_w_out[i].astype(BF16), gain[None, :]))
    return prm


def _trunk(x, pos0, prm, sinks, caches, tm, tm_ffn, tm_odd, sb_qb, sb_kb, ret_cs, ret_per, swa_rows):
    b, n, d = x.shape
    t = b * n
    xf = x.reshape(t, d)
    depth = prm["depth"]
    tab_rows = n if n % tm == 0 else t
    tile_tab = lambda tb: tb if tab_rows == n else jnp.tile(tb, (b, 1))
    cos_r, sin_r = [tile_tab(tb) for tb in _rope_tables(pos0, n, RET_QK_DIM)]
    cos_s, sin_s = [tile_tab(tb) for tb in _rope_tables(pos0, n, SWA_HEAD_DIM)]
    idx = jnp.arange(sb_kb)
    negu = jnp.where(idx[:, None] >= idx[None, :], -1.0, 0.0).astype(BF16)
    sb_k, sb_v, ret, swa_k, swa_v = [], [], [], [], []
    mix, w_mix = [], None
    for layer in range(depth):
        i = layer // 2
        g = prm["g"][layer]
        xf = _ffn_call(xf, mix, w_mix, g[0], prm["ffn"], layer, 0, tm_ffn)
        if layer % 2 == 0:
            w_in, w_mix = prm["even"][i]
            qa, ka, kab, va, vab, qr, kr, vr, gt = _even_proj_call(xf, g[1], w_in, cos_r, sin_r, tm)
            sb_k.append(ka.reshape(b, n, SB_HEADS, SB_HEAD_DIM))
            sb_v.append(va.reshape(b, n, SB_HEADS, SB_HEAD_DIM))
            r3 = lambda a: a.reshape(b, n, a.shape[-1])
            if caches is None:
                k_all, v_all, q_pos0 = r3(kab), r3(vab), 0
                state0 = jnp.zeros((b, RET_HEADS, RET_QK_DIM, RET_V_DIM), F32)
            else:
                past = caches["sb_k"].shape[2]
                padded = -(-(past + n) // sb_kb) * sb_kb
                cat = lambda cache, new: jnp.pad(
                    jnp.concatenate([cache.reshape(b, past, SB_W).astype(BF16), r3(new)], axis=1),
                    ((0, 0), (0, padded - past - n), (0, 0)))
                k_all, v_all, q_pos0 = cat(caches["sb_k"][i], kab), cat(caches["sb_v"][i], vab), past
                state0 = caches["ret"][i]
            o_sb = _sb_call(r3(qa), k_all, v_all, negu, sb_qb, q_pos0)
            o_r, st = _ret_call(r3(qr), r3(kr), r3(vr), r3(gt), state0, ret_cs, ret_per)
            ret.append(st)
            mix = [o_sb.reshape(t, SB_W), o_r.reshape(t, RET_V_W)]
        else:
            w_in, w_mix, gain = prm["odd"][i]
            q, k, v = _odd_proj_call(xf, g[1], w_in, gain, cos_s, sin_s, tm_odd)
            r3 = lambda a: a.reshape(b, n, a.shape[-1])
            k3, v3 = r3(k), r3(v)
            if caches is None:
                o = _swa_call(r3(q), k3, k3, v3, v3, sinks[i], swa_rows, True)
                keep = min(WINDOW, n)
                k_rows, v_rows = k3[:, n - keep:], v3[:, n - keep:]
            else:
                kc = caches["swa_k"][i].reshape(b, -1, SWA_KV_W)
                vc = caches["swa_v"][i].reshape(b, -1, SWA_KV_W)
                o = _swa_call(r3(q), kc, k3, vc, v3, sinks[i], swa_rows, False)
                k_rows, v_rows = k3, v3
            swa_k.append(k_rows.reshape(b, -1, SWA_KV_HEADS, SWA_HEAD_DIM))
            swa_v.append(v_rows.reshape(b, -1, SWA_KV_HEADS, SWA_HEAD_DIM))
            mix = [o.reshape(t, SWA_Q_W)]
        xf = _ffn_call(xf, mix, w_mix, g[2], prm["ffn"], layer, 1, tm_ffn)
        mix, w_mix = [], None
    stack = lambda parts: parts[0][None] if len(parts) == 1 else jnp.stack(parts)
    return (xf.reshape(b, n, d), stack(sb_k), stack(sb_v), stack(ret), stack(swa_k), stack(swa_v))


def kernel(x_prompt, x_sample, cache_sb_k, cache_sb_v, state_ret, cache_swa_k, cache_swa_v, norm_g, ffn_w_gate,
           ffn_w_up, ffn_w_down, even_w_in, even_w_out, odd_w_in, odd_w_out, odd_q_norm, odd_k_norm, odd_sinks):
    prm = _prep_weights(norm_g, ffn_w_gate, ffn_w_up, ffn_w_down, even_w_in, even_w_out, odd_w_in, odd_w_out,
                        odd_q_norm, odd_k_norm)
    past = cache_sb_k.shape[2]
    n_s = x_sample.shape[1]
    y_p, sb_k_p, sb_v_p, ret_p, swa_k_p, swa_v_p = _trunk(
        x_prompt, 0, prm, odd_sinks, None, tm=512, tm_ffn=1024, tm_odd=1024, sb_qb=2048, sb_kb=128, ret_cs=256, ret_per=2, swa_rows=1024)
    caches = {"sb_k": cache_sb_k, "sb_v": cache_sb_v, "ret": state_ret, "swa_k": cache_swa_k, "swa_v": cache_swa_v}
    y_s, sb_k_s, sb_v_s, ret_s, swa_k_s, swa_v_s = _trunk(
        x_sample, past, prm, odd_sinks, caches, tm=x_sample.shape[0] * n_s, tm_ffn=x_sample.shape[0] * n_s, tm_odd=x_sample.shape[0] * n_s, sb_qb=n_s, sb_kb=128,
        ret_cs=n_s, ret_per=1, swa_rows=n_s)
    return (y_p, y_s, sb_k_p, sb_v_p, ret_p, swa_k_p, swa_v_p, sb_k_s, sb_v_s, ret_s, swa_k_s, swa_v_s)
```
